```python
import math
import jax
import jax.numpy as jnp
from jax import lax
import numpy as np


D_MODEL = 1024
BATCH = 8
SEQ = 4096
DEPTH = 1

D_MIX = D_MODEL
D_LRU = D_MIX // 2
LRU_BLOCKS = 8
LRU_BLOCK = D_LRU // LRU_BLOCKS
LRU_C = 8.0
CONV_W = 4
CONV_LEFT = 2
N_ATT_HEADS = 4
D_ATT = D_MIX - D_LRU
HEAD_DV = D_ATT // N_ATT_HEADS
HEAD_DK = HEAD_DV // 2
QK_W = N_ATT_HEADS * 2 * HEAD_DK
D_IN_PROJ = 2 * D_LRU + 2 * QK_W + D_ATT
SPLITS = [D_LRU, 2 * D_LRU, 2 * D_LRU + QK_W, 2 * D_LRU + 2 * QK_W]
N_BUCKETS = 32
MAX_DISTANCE = 128
Q_BLOCK = 128
N_EXPERTS = 16
EC_FACTOR = 2
D_FF_EXPERT = D_MODEL
EPS = 1e-6

kernel_name = 'hybrid_rglru_diffattn_ecmoe_block'


def rms_norm(x, g):
    xf = x.astype(jnp.float32)
    y = xf * lax.rsqrt(jnp.mean(xf * xf, axis=-1, keepdims=True) + EPS)
    return (y * g.astype(jnp.float32)).astype(x.dtype)


def t5_buckets(rel):
    half = N_BUCKETS // 2
    max_exact = half // 2
    ret = jnp.where(rel > 0, half, 0)
    n = jnp.abs(rel)
    nf = jnp.maximum(n, 1).astype(jnp.float32)
    large = max_exact + (jnp.log(nf / max_exact) / math.log(MAX_DISTANCE / max_exact)
                         * (half - max_exact)).astype(jnp.int32)
    large = jnp.minimum(large, half - 1)
    return ret + jnp.where(n < max_exact, n, large)


def rg_lru(xc, w_a, b_a, w_x, b_x, lam, reverse):
    B, S, _ = xc.shape
    xb = xc.reshape(B, S, LRU_BLOCKS, LRU_BLOCK)
    r = jax.nn.sigmoid((jnp.einsum('bsnc,ncd->bsnd', xb, w_a).reshape(B, S, D_LRU) + b_a).astype(jnp.float32))
    i = jax.nn.sigmoid((jnp.einsum('bsnc,ncd->bsnd', xb, w_x).reshape(B, S, D_LRU) + b_x).astype(jnp.float32))
    log_a = LRU_C * r * jax.nn.log_sigmoid(lam.astype(jnp.float32))
    a = jnp.exp(log_a)
    u = jnp.sqrt(-jnp.expm1(2.0 * log_a)) * (i * xc.astype(jnp.float32))

    def combine(left, right):
        a1, b1 = left
        a2, b2 = right
        return a1 * a2, a2 * b1 + b2

    _, h = lax.associative_scan(combine, (a, u), axis=1, reverse=reverse)
    return h.astype(xc.dtype)


def diff_attention(q, k, v, g_q, g_k, lam, g_o, bias_table, lam_init):
    B, S = q.shape[0], q.shape[1]
    q = rms_norm(q, g_q) * (HEAD_DK ** -0.5)
    k = rms_norm(k, g_k)
    nb = S // Q_BLOCK
    qb = q.reshape(B, nb, Q_BLOCK, N_ATT_HEADS, 2, HEAD_DK).transpose(1, 0, 2, 3, 4, 5)
    kpos = jnp.arange(S, dtype=jnp.int32)

    def block(args):
        qi, start = args
        qpos = start + jnp.arange(Q_BLOCK, dtype=jnp.int32)
        bias = bias_table[t5_buckets(kpos[None, :] - qpos[:, None])].astype(jnp.float32)
        bias = bias.transpose(2, 0, 1)
        s = jnp.einsum('bqhjd,bkhjd->bhjqk', qi, k).astype(jnp.float32) + bias[None, :, None]
        p = jax.nn.softmax(s, axis=-1)
        w = (p[:, :, 0] - lam * p[:, :, 1]).astype(v.dtype)
        return jnp.einsum('bhqk,bkhd->bqhd', w, v)

    o = lax.map(block, (qb, jnp.arange(nb, dtype=jnp.int32) * Q_BLOCK))
    o = o.transpose(1, 0, 2, 3, 4).reshape(B, S, N_ATT_HEADS, HEAD_DV)
    o = rms_norm(o, g_o) * (1.0 - lam_init)
    return o.reshape(B, S, D_ATT)


def expert_choice_ffn(h, w_router, w1, w3, w2):
    B, S, D = h.shape
    cap = max(1, EC_FACTOR * S // N_EXPERTS)
    aff = jax.nn.softmax(jnp.dot(h, w_router).astype(jnp.float32), axis=-1)
    g, idx = lax.top_k(aff.transpose(0, 2, 1), cap)
    xe = jax.vmap(lambda hb, ib: hb[ib])(h, idx)
    a = jnp.einsum('becd,edf->becf', xe, w1)
    b = jnp.einsum('becd,edf->becf', xe, w3)
    y = jnp.einsum('becf,efd->becd', jax.nn.silu(a) * b, w2)
    y = y * g[..., None].astype(y.dtype)
    return jax.vmap(lambda ib, yb: jnp.zeros((S, D), yb.dtype).at[ib.reshape(-1)].add(yb.reshape(-1, D)))(idx, y)


def setup_inputs(seed: int = 0) -> dict:
    key = jax.random.key(seed)
    ks = jax.random.split(key, 26)
    f32 = jnp.float32

    def nrm(k, shape, fan_in):
        return jax.random.normal(k, shape, f32) * (fan_in ** -0.5)

    def gain(k, shape):
        return 1.0 + 0.02 * jax.random.normal(k, shape, f32)

    a0 = jax.random.uniform(ks[12], (DEPTH, 2, D_LRU), f32, minval=0.9, maxval=0.999)
    return {
        'x': jax.random.normal(ks[0], (BATCH, SEQ, D_MODEL), f32),
        'c': jax.random.normal(ks[1], (BATCH, D_MODEL), f32),
        'w_mod': nrm(ks[2], (DEPTH, D_MODEL, 6 * D_MODEL), D_MODEL),
        'b_mod': 0.02 * jax.random.normal(ks[3], (DEPTH, 6 * D_MODEL), f32),
        'g_norm1': gain(ks[4], (DEPTH, D_MODEL)),
        'w_in': nrm(ks[5], (DEPTH, D_MODEL, D_IN_PROJ), D_MODEL),
        'conv_w': nrm(ks[6], (DEPTH, CONV_W, 1, D_LRU), CONV_W),
        'conv_b': 0.02 * jax.random.normal(ks[7], (DEPTH, D_LRU), f32),
        'lru_w_a': nrm(ks[8], (DEPTH, 2, LRU_BLOCKS, LRU_BLOCK, LRU_BLOCK), LRU_BLOCK),
        'lru_b_a': 0.1 * jax.random.normal(ks[9], (DEPTH, 2, D_LRU), f32),
        'lru_w_x': nrm(ks[10], (DEPTH, 2, LRU_BLOCKS, LRU_BLOCK, LRU_BLOCK), LRU_BLOCK),
        'lru_b_x': 0.1 * jax.random.normal(ks[11], (DEPTH, 2, D_LRU), f32),
        'lru_lambda': jnp.log(a0) - jnp.log1p(-a0),
        'g_q': gain(ks[13], (DEPTH, HEAD_DK)),
        'g_k': gain(ks[14], (DEPTH, HEAD_DK)),
        'lambda_qk': 0.1 * jax.random.normal(ks[15], (DEPTH, 4, HEAD_DK), f32),
        'g_attn_out': gain(ks[16], (DEPTH, HEAD_DV)),
        'rel_bias': 0.5 * jax.random.normal(ks[17], (N_BUCKETS, N_ATT_HEADS), f32),
        'w_out': nrm(ks[18], (DEPTH, D_MIX, D_MODEL), D_MIX),
        'g_norm2': gain(ks[19], (DEPTH, D_MODEL)),
        'w_router': nrm(ks[20], (DEPTH, D_MODEL, N_EXPERTS), D_MODEL),
        'w1': nrm(ks[21], (DEPTH, N_EXPERTS, D_MODEL, D_FF_EXPERT), D_MODEL),
        'w3': nrm(ks[22], (DEPTH, N_EXPERTS, D_MODEL, D_FF_EXPERT), D_MODEL),
        'w2': nrm(ks[23], (DEPTH, N_EXPERTS, D_FF_EXPERT, D_MODEL), D_FF_EXPERT),
    }


def reference(x, c, w_mod, b_mod, g_norm1, w_in, conv_w, conv_b, lru_w_a, lru_b_a, lru_w_x, lru_b_x,
              lru_lambda, g_q, g_k, lambda_qk, g_attn_out, rel_bias, w_out, g_norm2, w_router, w1, w3, w2):
    B, S, _ = x.shape
    for l in range(DEPTH):
        mod = jnp.dot(jax.nn.silu(c), w_mod[l]) + b_mod[l]
        shift1, scale1, gate1, shift2, scale2, gate2 = jnp.split(mod[:, None, :], 6, axis=-1)

        h = rms_norm(x, g_norm1[l]) * (1.0 + scale1) + shift1
        proj = jnp.dot(h, w_in[l])
        x_lru, z_lru, q, k, v = jnp.split(proj, SPLITS, axis=-1)

        xc = lax.conv_general_dilated(x_lru, conv_w[l], (1,), [(CONV_LEFT, CONV_W - 1 - CONV_LEFT)],
                                      dimension_numbers=('NWC', 'WIO', 'NWC'),
                                      feature_group_count=D_LRU) + conv_b[l]
        h_fwd = rg_lru(xc, lru_w_a[l, 0], lru_b_a[l, 0], lru_w_x[l, 0], lru_b_x[l, 0], lru_lambda[l, 0], False)
        h_bwd = rg_lru(xc, lru_w_a[l, 1], lru_b_a[l, 1], lru_w_x[l, 1], lru_b_x[l, 1], lru_lambda[l, 1], True)
        y_lru = (h_fwd + h_bwd) * jax.nn.gelu(z_lru)

        lam_init = 0.8 - 0.6 * math.exp(-0.3 * l)
        lq = lambda_qk[l].astype(jnp.float32)
        lam = jnp.exp(jnp.sum(lq[0] * lq[1])) - jnp.exp(jnp.sum(lq[2] * lq[3])) + lam_init
        y_att = diff_attention(q.reshape(B, S, N_ATT_HEADS, 2, HEAD_DK),
                               k.reshape(B, S, N_ATT_HEADS, 2, HEAD_DK),
                               v.reshape(B, S, N_ATT_HEADS, HEAD_DV),
                               g_q[l], g_k[l], lam, g_attn_out[l], rel_bias, lam_init)

        mix = jnp.dot(jnp.concatenate([y_lru, y_att], axis=-1), w_out[l])
        x = x + gate1 * mix

        h2 = rms_norm(x, g_norm2[l]) * (1.0 + scale2) + shift2
        x = x + gate2 * expert_choice_ffn(h2, w_router[l], w1[l], w3[l], w2[l])
    return x
```

```python
import functools
import math

import jax
import jax.numpy as jnp
from jax import lax
from jax.experimental import pallas as pl
from jax.experimental.pallas import tpu as pltpu

F32 = jnp.float32
BF16 = jnp.bfloat16
HIGHEST = lax.Precision.HIGHEST

D_MODEL = 1024
D_LRU = 512
LRU_BLOCK = 64
LRU_C = 8.0
CONV_W = 4
N_HEADS = 4
HEAD_DV = 128
HEAD_DK = 64
D_ATT = N_HEADS * HEAD_DV
N_BUCKETS = 32
N_EXPERTS = 16
EC_FACTOR = 2
EPS = 1e-6
LAM_INIT = 0.8 - 0.6 * math.exp(-0.3 * 0)

LRU_HALF = 256
ATT_TILE = 256
N_TOK_CHUNKS = 8
N_CAP_CHUNKS = 4
VMEM_LIMIT = 56 * 1024 * 1024

NT_DIMS = (((1,), (1,)), ((), ()))
TN_DIMS = (((0,), (0,)), ((), ()))


def _sigmoid(x):
    return 1.0 / (1.0 + jnp.exp(-x))


def _params(sem, vmem=None):
    return pltpu.CompilerParams(dimension_semantics=sem, vmem_limit_bytes=vmem)


def _mod_kernel(c_ref, w_ref, b_ref, o_ref):
    c = c_ref[...]
    o_ref[...] = jnp.dot(c * _sigmoid(c), w_ref[...], precision=HIGHEST,
                         preferred_element_type=F32) + b_ref[...]


def _modulation(c, w_mod, b_mod):
    bsz, d = c.shape
    n = w_mod.shape[1]
    return pl.pallas_call(
        _mod_kernel,
        grid=(n // d,),
        in_specs=[pl.BlockSpec((bsz, d), lambda j: (0, 0)),
                  pl.BlockSpec((d, d), lambda j: (0, j)),
                  pl.BlockSpec((1, d), lambda j: (0, j))],
        out_specs=pl.BlockSpec((bsz, d), lambda j: (0, j)),
        out_shape=jax.ShapeDtypeStruct((bsz, n), F32),
        compiler_params=_params(("arbitrary",)),
        name="adaln_mod",
    )(c, w_mod, b_mod.reshape(1, n))


def _bias_kernel(tab_ref, o_ref):
    h = pl.program_id(0)
    t = o_ref.shape[-1]
    a = lax.broadcasted_iota(jnp.int32, (t, t), 0)
    b = lax.broadcasted_iota(jnp.int32, (t, t), 1)
    half = N_BUCKETS // 2
    max_exact = half // 2
    for d in range(5):
        if d == 0:
            o_ref[0, d] = jnp.full((t, t), tab_ref[half - 1, h], F32)
        elif d == 4:
            o_ref[0, d] = jnp.full((t, t), tab_ref[N_BUCKETS - 1, h], F32)
        else:
            rel = (d - 2) * t + b - a
            n = jnp.abs(rel)
            n2 = n * n
            large = jnp.full((t, t), max_exact, jnp.int32)
            for k in range(1, half - max_exact):
                large = large + jnp.where(n2 >= (max_exact * max_exact) * (2 ** k), 1, 0)
            idx = jnp.where(n < max_exact, n, large) + jnp.where(rel > 0, half, 0)
            val = jnp.zeros((t, t), F32)
            for j in range(N_BUCKETS):
                val = jnp.where(idx == j, tab_ref[j, h], val)
            o_ref[0, d] = val


def _bias_tiles(rel_bias, t):
    return pl.pallas_call(
        _bias_kernel,
        grid=(N_HEADS,),
        in_specs=[pl.BlockSpec(memory_space=pltpu.SMEM)],
        out_specs=pl.BlockSpec((1, 5, t, t), lambda h: (h, 0, 0, 0)),
        out_shape=jax.ShapeDtypeStruct((N_HEADS, 5, t, t), F32),
        compiler_params=_params(("arbitrary",)),
        name="t5_bias_tiles",
    )(rel_bias)


def _inproj_kernel(x_ref, sc_ref, sh_ref, g1_ref, w_ref, mseg_ref, gq_ref, gk_ref,
                   xl_ref, gz_ref, q_ref, k_ref, v_ref):
    x = x_ref[0]
    ms = jnp.mean(x * x, axis=-1, keepdims=True)
    h = (x * lax.rsqrt(ms + EPS) * g1_ref[...]) * (1.0 + sc_ref[0]) + sh_ref[0]
    hb = h.astype(BF16)

    def proj(lo, width):
        return jnp.dot(hb, w_ref[:, lo:lo + width], preferred_element_type=F32)

    def qk_norm(t, g):
        t2 = t * t
        hi = t2.astype(BF16)
        lo = (t2 - hi.astype(F32)).astype(BF16)
        ss = (jnp.dot(hi, mseg_ref[...], preferred_element_type=F32)
              + jnp.dot(lo, mseg_ref[...], preferred_element_type=F32))
        return t * lax.rsqrt(ss * (1.0 / HEAD_DK) + EPS) * g

    xl_ref[0] = proj(0, D_LRU)
    z = proj(D_LRU, D_LRU)
    cdf = 0.5 * (1.0 + jnp.tanh(math.sqrt(2.0 / math.pi) * (z + 0.044715 * (z * z * z))))
    gz_ref[0] = (z * cdf).astype(BF16)
    q_ref[0] = (qk_norm(proj(2 * D_LRU, D_ATT), gq_ref[...]) * (HEAD_DK ** -0.5)).astype(BF16)
    k_ref[0] = qk_norm(proj(2 * D_LRU + D_ATT, D_ATT), gk_ref[...]).astype(BF16)
    v_ref[0] = proj(2 * D_LRU + 2 * D_ATT, D_ATT).astype(BF16)


def _in_projection(x, scale1, shift1, g_norm1, w_in, g_q, g_k, tm):
    bsz, s, d = x.shape
    n = w_in.shape[1]
    seg = jnp.arange(D_ATT, dtype=jnp.int32) // HEAD_DK
    mseg = (seg[:, None] == seg[None, :]).astype(BF16)
    n_sub = D_ATT // HEAD_DK
    row = lambda b, i: (b, i, 0)
    vec = lambda b, i: (b, 0, 0)
    full = lambda b, i: (0, 0)
    out_block = pl.BlockSpec((1, tm, D_LRU), row)
    return pl.pallas_call(
        _inproj_kernel,
        grid=(bsz, s // tm),
        in_specs=[pl.BlockSpec((1, tm, d), row),
                  pl.BlockSpec((1, 1, d), vec),
                  pl.BlockSpec((1, 1, d), vec),
                  pl.BlockSpec((1, d), full),
                  pl.BlockSpec((d, n), full),
                  pl.BlockSpec((D_ATT, D_ATT), full),
                  pl.BlockSpec((1, D_ATT), full),
                  pl.BlockSpec((1, D_ATT), full)],
        out_specs=[out_block] * 5,
        out_shape=[jax.ShapeDtypeStruct((bsz, s, D_LRU), F32)]
                  + [jax.ShapeDtypeStruct((bsz, s, D_LRU), BF16)] * 4,
        compiler_params=_params(("parallel", "parallel"), VMEM_LIMIT),
        name="norm1_in_proj",
    )(x, scale1, shift1, g_norm1.reshape(1, d), w_in.astype(BF16), mseg,
      jnp.tile(g_q, n_sub).reshape(1, D_ATT), jnp.tile(g_k, n_sub).reshape(1, D_ATT))


def _lru_kernel(x_ref, gz_ref, cw_ref, cb_ref, wa_ref, wx_ref, ba_ref, bx_ref, lam_ref, y_ref,
                xpad, a_f, u_f, a_b, u_b, *, tc):
    s = x_ref.shape[1]
    c = x_ref.shape[2]
    n_chunks = s // tc
    zeros8 = jnp.zeros((8, c), F32)
    xpad[0:8, :] = zeros8
    xpad[s + 8:s + 16, :] = zeros8

    def fill(ci, carry):
        t0 = pl.multiple_of(ci * tc, tc)
        xpad[pl.ds(t0 + 8, tc), :] = x_ref[0, pl.ds(t0, tc), :]
        return carry

    lax.fori_loop(0, n_chunks, fill, 0)

    cw = cw_ref[...]
    cb = cb_ref[...]
    decay = []
    for d in range(2):
        lam = lam_ref[d]
        softplus_neg = jnp.maximum(-lam, 0.0) + jnp.log(1.0 + jnp.exp(-jnp.abs(lam)))
        decay.append(-LRU_C * softplus_neg)
    a_scr = (a_f, a_b)
    u_scr = (u_f, u_b)

    def gates(ci, carry):
        t0 = pl.multiple_of(ci * tc, tc)
        xw = xpad[pl.ds(t0, tc + 16), :]
        xc = (cw[0:1] * pltpu.roll(xw, 2, 0)[8:8 + tc]
              + cw[1:2] * pltpu.roll(xw, 1, 0)[8:8 + tc]
              + cw[2:3] * xw[8:8 + tc]
              + cw[3:4] * pltpu.roll(xw, tc + 15, 0)[8:8 + tc]) + cb
        xcb = xc.astype(BF16)
        for d in range(2):
            r = _sigmoid(jnp.dot(xcb, wa_ref[d], preferred_element_type=F32) + ba_ref[d])
            i = _sigmoid(jnp.dot(xcb, wx_ref[d], preferred_element_type=F32) + bx_ref[d])
            a = jnp.exp(r * decay[d])
            a_scr[d][pl.ds(t0, tc), :] = a
            u_scr[d][pl.ds(t0, tc), :] = jnp.sqrt(1.0 - a * a) * (i * xc)
        return carry

    lax.fori_loop(0, n_chunks, gates, 0)

    def step(t, carry):
        hf, hb = carry
        hf = a_f[pl.ds(t, 1), :] * hf + u_f[pl.ds(t, 1), :]
        u_f[pl.ds(t, 1), :] = hf
        tb = s - 1 - t
        hb = a_b[pl.ds(tb, 1), :] * hb + u_b[pl.ds(tb, 1), :]
        u_b[pl.ds(tb, 1), :] = hb
        return hf, hb

    zero_row = jnp.zeros((1, c), F32)
    lax.fori_loop(0, s, step, (zero_row, zero_row), unroll=8)

    def emit(ci, carry):
        t0 = pl.multiple_of(ci * tc, tc)
        hsum = u_f[pl.ds(t0, tc), :] + u_b[pl.ds(t0, tc), :]
        y_ref[0, pl.ds(t0, tc), :] = (hsum * gz_ref[0, pl.ds(t0, tc), :].astype(F32)).astype(BF16)
        return carry

    lax.fori_loop(0, n_chunks, emit, 0)


def _block_diag(w, half):
    n_dir, n_blocks, blk, _ = w.shape
    per = half // blk
    n_half = n_blocks // per
    w = w.reshape(n_dir, n_half, per, blk, blk)
    eye = jnp.eye(per, dtype=w.dtype)
    out = w[:, :, :, :, None, :] * eye[None, None, :, None, :, None]
    return out.reshape(n_dir, n_half, half, half)


def _rg_lru(x_lru, gz, conv_w, conv_b, w_a, b_a, w_x, b_x, lam, tc):
    bsz, s, c = x_lru.shape
    half = LRU_HALF
    n_half = c // half
    wa = _block_diag(w_a, half).astype(BF16)
    wx = _block_diag(w_x, half).astype(BF16)
    seq = lambda b, p: (b, 0, p)
    chan = lambda b, p: (0, p)
    dirchan = lambda b, p: (0, 0, p)
    blk = lambda b, p: (0, p, 0, 0)
    return pl.pallas_call(
        functools.partial(_lru_kernel, tc=tc),
        grid=(bsz, n_half),
        in_specs=[pl.BlockSpec((1, s, half), seq),
                  pl.BlockSpec((1, s, half), seq),
                  pl.BlockSpec((CONV_W, half), chan),
                  pl.BlockSpec((1, half), chan),
                  pl.BlockSpec((2, None, half, half), blk),
                  pl.BlockSpec((2, None, half, half), blk),
                  pl.BlockSpec((2, 1, half), dirchan),
                  pl.BlockSpec((2, 1, half), dirchan),
                  pl.BlockSpec((2, 1, half), dirchan)],
        out_specs=pl.BlockSpec((1, s, half), seq),
        out_shape=jax.ShapeDtypeStruct((bsz, s, c), BF16),
        scratch_shapes=[pltpu.VMEM((s + 16, half), F32)] + [pltpu.VMEM((s, half), F32)] * 4,
        compiler_params=_params(("parallel", "parallel"), VMEM_LIMIT),
        name="rg_lru",
    )(x_lru, gz, conv_w.reshape(CONV_W, c), conv_b.reshape(1, c), wa, wx,
      b_a.reshape(2, 1, c), b_x.reshape(2, 1, c), lam.reshape(2, 1, c))


def _attn_kernel(q_ref, k_ref, v_ref, bias_ref, lq_ref, go_ref, o_ref, s1_scr, s2_scr):
    i = pl.program_id(2)
    t = q_ref.shape[1]
    n_k = k_ref.shape[1] // t
    q = q_ref[0]
    lane = lax.broadcasted_iota(jnp.int32, q.shape, 1)
    zero = jnp.zeros_like(q)
    q_sub = (jnp.where(lane < HEAD_DK, q, zero), jnp.where(lane >= HEAD_DK, q, zero))
    s_scr = (s1_scr, s2_scr)

    m = [jnp.full((t, 128), -jnp.inf, F32) for _ in range(2)]
    for j in range(n_k):
        kj = k_ref[0, j * t:(j + 1) * t, :]
        bj = bias_ref[0, jnp.clip(j - i, -2, 2) + 2]
        for u in range(2):
            sc = lax.dot_general(q_sub[u], kj, NT_DIMS, preferred_element_type=F32) + bj
            s_scr[u][j] = sc
            for w in range(t // 128):
                m[u] = jnp.maximum(m[u], sc[:, w * 128:(w + 1) * 128])

    outs = []
    for u in range(2):
        mr = jnp.max(m[u], axis=-1, keepdims=True)
        l = jnp.zeros((t, 128), F32)
        acc = jnp.zeros((t, HEAD_DV), F32)
        for j in range(n_k):
            p = jnp.exp(s_scr[u][j] - mr)
            for w in range(t // 128):
                l = l + p[:, w * 128:(w + 1) * 128]
            acc = acc + jnp.dot(p.astype(BF16), v_ref[0, j * t:(j + 1) * t, :],
                                preferred_element_type=F32)
        outs.append(acc / jnp.sum(l, axis=-1, keepdims=True))

    lq = lq_ref[...]
    lam = (jnp.exp(jnp.sum(lq[0:1] * lq[1:2], axis=-1, keepdims=True))
           - jnp.exp(jnp.sum(lq[2:3] * lq[3:4], axis=-1, keepdims=True)) + LAM_INIT)
    o = outs[0] - lam * outs[1]
    ms = jnp.mean(o * o, axis=-1, keepdims=True)
    o_ref[0] = ((o * lax.rsqrt(ms + EPS) * go_ref[...]) * (1.0 - LAM_INIT)).astype(BF16)


def _diff_attention(qn, kn, v, bias_tiles, lambda_qk, g_o):
    bsz, s, _ = qn.shape
    t = bias_tiles.shape[-1]
    n_k = s // t
    return pl.pallas_call(
        _attn_kernel,
        grid=(bsz, N_HEADS, s // t),
        in_specs=[pl.BlockSpec((1, t, HEAD_DV), lambda b, h, i: (b, i, h)),
                  pl.BlockSpec((1, s, HEAD_DV), lambda b, h, i: (b, 0, h)),
                  pl.BlockSpec((1, s, HEAD_DV), lambda b, h, i: (b, 0, h)),
                  pl.BlockSpec((1, 5, t, t), lambda b, h, i: (h, 0, 0, 0)),
                  pl.BlockSpec((4, HEAD_DK), lambda b, h, i: (0, 0)),
                  pl.BlockSpec((1, HEAD_DV), lambda b, h, i: (0, 0))],
        out_specs=pl.BlockSpec((1, t, HEAD_DV), lambda b, h, i: (b, i, h)),
        out_shape=jax.ShapeDtypeStruct((bsz, s, D_ATT), BF16),
        scratch_shapes=[pltpu.VMEM((n_k, t, t), F32)] * 2,
        compiler_params=_params(("parallel", "parallel", "parallel"), VMEM_LIMIT),
        name="diff_attention",
    )(qn, kn, v, bias_tiles, lambda_qk, g_o.reshape(1, HEAD_DV))


def _outproj_kernel(yl_ref, ya_ref, x_ref, gate_ref, sc_ref, sh_ref, g2_ref, wo_ref, wr_ref,
                    x1_ref, h2_ref, aff_ref):
    mix = (jnp.dot(yl_ref[0], wo_ref[0:D_LRU, :], preferred_element_type=F32)
           + jnp.dot(ya_ref[0], wo_ref[D_LRU:D_LRU + D_ATT, :], preferred_element_type=F32))
    x1 = x_ref[0] + gate_ref[0] * mix
    x1_ref[0] = x1
    ms = jnp.mean(x1 * x1, axis=-1, keepdims=True)
    h2 = (x1 * lax.rsqrt(ms + EPS) * g2_ref[...]) * (1.0 + sc_ref[0]) + sh_ref[0]
    h2_ref[0] = h2.astype(BF16)
    logits = lax.dot_general(wr_ref[...], h2, NT_DIMS, precision=HIGHEST,
                             preferred_element_type=F32)
    ex = jnp.exp(logits - jnp.max(logits, axis=0, keepdims=True))
    aff_ref[0] = ex / jnp.sum(ex, axis=0, keepdims=True)


def _out_projection(y_lru, y_att, x, gate1, scale2, shift2, g_norm2, w_out, w_router, tm):
    bsz, s, d = x.shape
    row = lambda b, i: (b, i, 0)
    vec = lambda b, i: (b, 0, 0)
    full = lambda b, i: (0, 0)
    return pl.pallas_call(
        _outproj_kernel,
        grid=(bsz, s // tm),
        in_specs=[pl.BlockSpec((1, tm, D_LRU), row),
                  pl.BlockSpec((1, tm, D_ATT), row),
                  pl.BlockSpec((1, tm, d), row),
                  pl.BlockSpec((1, 1, d), vec),
                  pl.BlockSpec((1, 1, d), vec),
                  pl.BlockSpec((1, 1, d), vec),
                  pl.BlockSpec((1, d), full),
                  pl.BlockSpec((D_LRU + D_ATT, d), full),
                  pl.BlockSpec((N_EXPERTS, d), full)],
        out_specs=[pl.BlockSpec((1, tm, d), row),
                   pl.BlockSpec((1, tm, d), row),
                   pl.BlockSpec((1, N_EXPERTS, tm), lambda b, i: (b, 0, i))],
        out_shape=[jax.ShapeDtypeStruct((bsz, s, d), F32),
                   jax.ShapeDtypeStruct((bsz, s, d), BF16),
                   jax.ShapeDtypeStruct((bsz, N_EXPERTS, s), F32)],
        compiler_params=_params(("parallel", "parallel"), VMEM_LIMIT),
        name="out_proj_norm2_router",
    )(y_lru, y_att, x, gate1, scale2, shift2, g_norm2.reshape(1, d), w_out.astype(BF16),
      w_router.T)


def _route_kernel(aff_ref, pos_ref, cnt_ref, *, cap, n_tok_chunks):
    aff = aff_ref[0]
    n_e, s = aff.shape
    bits = lax.bitcast_convert_type(aff, jnp.int32)
    capf = float(cap)

    def count(mask):
        return jnp.sum(jnp.where(mask, 1.0, 0.0), axis=-1, keepdims=True)

    tau = jnp.zeros((n_e, 1), jnp.int32)
    for bit in range(30, -1, -1):
        cand = tau | (1 << bit)
        tau = jnp.where(count(bits >= cand) >= capf, cand, tau)
    gt = bits > tau
    eq = bits == tau
    need = capf - count(gt)

    blk = 256
    r = lax.broadcasted_iota(jnp.int32, (blk, blk), 0)
    cidx = lax.broadcasted_iota(jnp.int32, (blk, blk), 1)
    upper = jnp.where(r < cidx, 1.0, 0.0).astype(BF16)

    def prefix_blocks(mask):
        off = jnp.zeros((n_e, 1), F32)
        pieces, offs = [], []
        for k in range(s // blk):
            mb = jnp.where(mask[:, k * blk:(k + 1) * blk], 1.0, 0.0)
            offs.append(off)
            pieces.append(jnp.dot(mb.astype(BF16), upper, preferred_element_type=F32) + off)
            off = off + jnp.sum(mb, axis=-1, keepdims=True)
        return pieces, offs

    eq_rank, _ = prefix_blocks(eq)
    sel_blocks = []
    for k in range(s // blk):
        sl = slice(k * blk, (k + 1) * blk)
        sel_blocks.append(jnp.logical_or(gt[:, sl], jnp.logical_and(eq[:, sl], eq_rank[k] < need)))
    sel = jnp.concatenate(sel_blocks, axis=1)
    slot, offs = prefix_blocks(sel)
    for k in range(s // blk):
        pos_ref[0, :, k * blk:(k + 1) * blk] = jnp.where(sel_blocks[k], slot[k], -1.0).astype(jnp.int32)

    lane = lax.broadcasted_iota(jnp.int32, (n_e, 128), 1)
    cnt = jnp.zeros((n_e, 128), F32)
    per = (s // n_tok_chunks) // blk
    for j in range(n_tok_chunks):
        cnt = jnp.where(lane == j, offs[j * per], cnt)
    cnt_ref[0] = cnt.astype(jnp.int32)


def _routing(aff, cap):
    bsz, n_e, s = aff.shape
    return pl.pallas_call(
        functools.partial(_route_kernel, cap=cap, n_tok_chunks=N_TOK_CHUNKS),
        grid=(bsz,),
        in_specs=[pl.BlockSpec((1, n_e, s), lambda b: (b, 0, 0))],
        out_specs=[pl.BlockSpec((1, n_e, s), lambda b: (b, 0, 0)),
                   pl.BlockSpec((1, n_e, 128), lambda b: (b, 0, 0))],
        out_shape=[jax.ShapeDtypeStruct((bsz, n_e, s), jnp.int32),
                   jax.ShapeDtypeStruct((bsz, n_e, 128), jnp.int32)],
        compiler_params=_params(("parallel",)),
        name="expert_choice_routing",
    )(aff)


def _moe_kernel(cnt_ref, pos_ref, aff_ref, h2_ref, w1_ref, w3_ref, w2_ref, out_ref,
                xe_scr, g_scr, y_scr, *, cap):
    b = pl.program_id(0)
    e = pl.program_id(1)
    n_e = pl.num_programs(1)
    s = h2_ref.shape[1]
    tchunk = s // N_TOK_CHUNKS
    cchunk = cap // N_CAP_CHUNKS

    @pl.when(e == 0)
    def _():
        out_ref[...] = jnp.zeros_like(out_ref)

    xe_scr[...] = jnp.zeros_like(xe_scr)
    g_scr[...] = jnp.zeros_like(g_scr)
    base = (b * n_e + e) * N_TOK_CHUNKS
    slot_iota = lax.broadcasted_iota(jnp.int32, (cchunk, tchunk), 0)

    def blocks():
        for j in range(N_TOK_CHUNKS):
            c0 = cnt_ref[base + j]
            c1 = cnt_ref[base + j + 1] if j + 1 < N_TOK_CHUNKS else cap
            for i in range(N_CAP_CHUNKS):
                yield j, i, jnp.logical_and(c0 < cchunk * (i + 1), c1 > cchunk * i)

    def one_hot(j, i):
        posj = pos_ref[0, 0, :, j * tchunk:(j + 1) * tchunk]
        return posj == slot_iota + cchunk * i

    for j, i, hit in blocks():
        @pl.when(hit)
        def _(j=j, i=i):
            sel = one_hot(j, i)
            rows = slice(i * cchunk, (i + 1) * cchunk)
            toks = slice(j * tchunk, (j + 1) * tchunk)
            xe_scr[rows, :] += jnp.dot(jnp.where(sel, 1.0, 0.0).astype(BF16), h2_ref[0, toks, :],
                                       preferred_element_type=F32)
            g_scr[rows, :] += jnp.sum(jnp.where(sel, aff_ref[0, 0, :, toks], 0.0), axis=-1, keepdims=True)

    xe = xe_scr[...].astype(BF16)
    a = jnp.dot(xe, w1_ref[0], preferred_element_type=F32)
    gate = jnp.dot(xe, w3_ref[0], preferred_element_type=F32)
    hmid = ((a * _sigmoid(a)) * gate).astype(BF16)
    y = jnp.dot(hmid, w2_ref[0], preferred_element_type=F32) * g_scr[...]
    y_scr[...] = y.astype(BF16)

    for j, i, hit in blocks():
        @pl.when(hit)
        def _(j=j, i=i):
            sel = one_hot(j, i)
            rows = slice(i * cchunk, (i + 1) * cchunk)
            toks = slice(j * tchunk, (j + 1) * tchunk)
            out_ref[0, toks, :] += lax.dot_general(jnp.where(sel, 1.0, 0.0).astype(BF16), y_scr[rows, :],
                                                   TN_DIMS, preferred_element_type=F32)


def _moe(cnt, pos, aff, h2, w1, w3, w2, cap):
    bsz, s, d = h2.shape
    n_e = w1.shape[0]
    f = w1.shape[2]
    tok_row = lambda b, e, c: (b, e, 0, 0)
    wspec = lambda b, e, c: (e, 0, 0)
    resident = lambda b, e, c: (b, 0, 0)
    grid_spec = pltpu.PrefetchScalarGridSpec(
        num_scalar_prefetch=1,
        grid=(bsz, n_e),
        in_specs=[pl.BlockSpec((1, 1, 1, s), tok_row),
                  pl.BlockSpec((1, 1, 1, s), tok_row),
                  pl.BlockSpec((1, s, d), resident, pipeline_mode=pl.Buffered(1)),
                  pl.BlockSpec((1, d, f), wspec),
                  pl.BlockSpec((1, d, f), wspec),
                  pl.BlockSpec((1, f, d), wspec)],
        out_specs=pl.BlockSpec((1, s, d), resident, pipeline_mode=pl.Buffered(1)),
        scratch_shapes=[pltpu.VMEM((cap, d), F32), pltpu.VMEM((cap, 1), F32), pltpu.VMEM((cap, d), BF16)],
    )
    return pl.pallas_call(
        functools.partial(_moe_kernel, cap=cap),
        grid_spec=grid_spec,
        out_shape=jax.ShapeDtypeStruct((bsz, s, d), F32),
        compiler_params=_params(("arbitrary", "arbitrary"), VMEM_LIMIT),
        name="expert_choice_ffn",
    )(cnt.reshape(-1), pos.reshape(bsz, n_e, 1, s), aff.reshape(bsz, n_e, 1, s), h2,
      w1.astype(BF16), w3.astype(BF16), w2.astype(BF16))


def _residual_kernel(x_ref, gate_ref, y_ref, o_ref):
    o_ref[0] = x_ref[0] + gate_ref[0] * y_ref[0]


def _gated_residual(x1, gate2, y, tm):
    bsz, s, d = x1.shape
    row = lambda b, i: (b, i, 0)
    return pl.pallas_call(
        _residual_kernel,
        grid=(bsz, s // tm),
        in_specs=[pl.BlockSpec((1, tm, d), row),
                  pl.BlockSpec((1, 1, d), lambda b, i: (b, 0, 0)),
                  pl.BlockSpec((1, tm, d), row)],
        out_specs=pl.BlockSpec((1, tm, d), row),
        out_shape=jax.ShapeDtypeStruct((bsz, s, d), F32),
        compiler_params=_params(("parallel", "parallel")),
        name="gated_residual",
    )(x1, gate2, y)


def kernel(x, c, w_mod, b_mod, g_norm1, w_in, conv_w, conv_b, lru_w_a, lru_b_a, lru_w_x, lru_b_x,
           lru_lambda, g_q, g_k, lambda_qk, g_attn_out, rel_bias, w_out, g_norm2, w_router, w1, w3, w2):
    bsz, s, d = x.shape
    depth = w_mod.shape[0]
    cap = max(1, EC_FACTOR * s // N_EXPERTS)
    tm = min(512, s)
    bias_tiles = _bias_tiles(rel_bias, min(ATT_TILE, s))
    for l in range(depth):
        mod = _modulation(c, w_mod[l], b_mod[l])
        shift1, scale1, gate1, shift2, scale2, gate2 = [m.reshape(bsz, 1, d) for m in jnp.split(mod, 6, axis=-1)]
        x_lru, gz, qn, kn, v = _in_projection(x, scale1, shift1, g_norm1[l], w_in[l], g_q[l], g_k[l], tm)
        y_lru = _rg_lru(x_lru, gz, conv_w[l], conv_b[l], lru_w_a[l], lru_b_a[l], lru_w_x[l], lru_b_x[l],
                        lru_lambda[l], tm)
        y_att = _diff_attention(qn, kn, v, bias_tiles, lambda_qk[l], g_attn_out[l])
        x1, h2, aff = _out_projection(y_lru, y_att, x, gate1, scale2, shift2, g_norm2[l], w_out[l],
                                      w_router[l], tm)
        pos, cnt = _routing(aff, cap)
        y = _moe(cnt[:, :, :N_TOK_CHUNKS], pos, aff, h2, w1[l], w3[l], w2[l], cap)
        x = _gated_residual(x1, gate2, y, tm)
    return x
```

```python
import functools
import math

import jax
import jax.numpy as jnp
from jax import lax
from jax.experimental import pallas as pl
from jax.experimental.pallas import tpu as pltpu

F32 = jnp.float32
BF16 = jnp.bfloat16
HIGHEST = lax.Precision.HIGHEST

D_MODEL = 1024
D_LRU = 512
LRU_BLOCK = 64
LRU_C = 8.0
CONV_W = 4
N_HEADS = 4
HEAD_DV = 128
HEAD_DK = 64
D_ATT = N_HEADS * HEAD_DV
N_BUCKETS = 32
N_EXPERTS = 16
EC_FACTOR = 2
EPS = 1e-6
LOG2E = math.log2(math.e)

LRU_HALF = 256
ATT_TILE = 256
N_TOK_CHUNKS = 8
N_CAP_CHUNKS = 4
VMEM_LIMIT = 56 * 1024 * 1024

NT_DIMS = (((1,), (1,)), ((), ()))
TN_DIMS = (((0,), (0,)), ((), ()))


def _sigmoid(x):
    return 1.0 / (1.0 + jnp.exp(-x))


def _params(sem, vmem=None, flags=None):
    return pltpu.CompilerParams(dimension_semantics=sem, vmem_limit_bytes=vmem, flags=flags)


def _mod_kernel(c_ref, w_ref, b_ref, o_ref):
    c = c_ref[...]
    o_ref[...] = jnp.dot(c * _sigmoid(c), w_ref[...], precision=HIGHEST,
                         preferred_element_type=F32) + b_ref[...]


def _modulation(c, w_mod, b_mod):
    bsz, d = c.shape
    n = w_mod.shape[1]
    return pl.pallas_call(
        _mod_kernel,
        grid=(n // d,),
        in_specs=[pl.BlockSpec((bsz, d), lambda j: (0, 0)),
                  pl.BlockSpec((d, d), lambda j: (0, j)),
                  pl.BlockSpec((1, d), lambda j: (0, j))],
        out_specs=pl.BlockSpec((bsz, d), lambda j: (0, j)),
        out_shape=jax.ShapeDtypeStruct((bsz, n), F32),
        compiler_params=_params(("arbitrary",)),
        name="adaln_mod",
    )(c, w_mod, b_mod.reshape(1, n))


def _bias_kernel(tab_ref, o_ref):
    h = pl.program_id(0)
    t = o_ref.shape[-1]
    key = lax.broadcasted_iota(jnp.int32, (t, t), 0)
    qry = lax.broadcasted_iota(jnp.int32, (t, t), 1)
    half = N_BUCKETS // 2
    max_exact = half // 2
    for d in range(5):
        if d == 0:
            o_ref[0, d] = jnp.full((t, t), tab_ref[half - 1, h] * LOG2E, F32)
        elif d == 4:
            o_ref[0, d] = jnp.full((t, t), tab_ref[N_BUCKETS - 1, h] * LOG2E, F32)
        else:
            rel = (d - 2) * t + key - qry
            n = jnp.abs(rel)
            n2 = n * n
            large = jnp.full((t, t), max_exact, jnp.int32)
            for k in range(1, half - max_exact):
                large = large + jnp.where(n2 >= (max_exact * max_exact) * (2 ** k), 1, 0)
            idx = jnp.where(n < max_exact, n, large) + jnp.where(rel > 0, half, 0)
            val = jnp.zeros((t, t), F32)
            for j in range(N_BUCKETS):
                val = jnp.where(idx == j, tab_ref[j, h] * LOG2E, val)
            o_ref[0, d] = val


def _bias_tiles(rel_bias, t):
    return pl.pallas_call(
        _bias_kernel,
        grid=(N_HEADS,),
        in_specs=[pl.BlockSpec(memory_space=pltpu.SMEM)],
        out_specs=pl.BlockSpec((1, 5, t, t), lambda h: (h, 0, 0, 0)),
        out_shape=jax.ShapeDtypeStruct((N_HEADS, 5, t, t), F32),
        compiler_params=_params(("arbitrary",)),
        name="t5_bias_tiles",
    )(rel_bias)


def _inproj_kernel(x_ref, sc_ref, sh_ref, g1_ref, w_ref, wvt_ref, mseg_ref, gq_ref, gk_ref,
                   xl_ref, gz_ref, q_ref, k_ref, vt_ref):
    x = x_ref[0]
    ms = jnp.mean(x * x, axis=-1, keepdims=True)
    h = (x * lax.rsqrt(ms + EPS) * g1_ref[...]) * (1.0 + sc_ref[0]) + sh_ref[0]
    hb = h.astype(BF16)

    def proj(lo, width):
        return jnp.dot(hb, w_ref[:, lo:lo + width], preferred_element_type=F32)

    def qk_norm(t, g):
        t2 = t * t
        hi = t2.astype(BF16)
        lo = (t2 - hi.astype(F32)).astype(BF16)
        ss = (jnp.dot(hi, mseg_ref[...], preferred_element_type=F32)
              + jnp.dot(lo, mseg_ref[...], preferred_element_type=F32))
        return t * lax.rsqrt(ss * (1.0 / HEAD_DK) + EPS) * g

    xl_ref[0] = proj(0, D_LRU)
    z = proj(D_LRU, D_LRU)
    cdf = 0.5 * (1.0 + jnp.tanh(math.sqrt(2.0 / math.pi) * (z + 0.044715 * (z * z * z))))
    gz_ref[0] = (z * cdf).astype(BF16)
    q_ref[0] = (qk_norm(proj(2 * D_LRU, D_ATT), gq_ref[...]) * (HEAD_DK ** -0.5 * LOG2E)).astype(BF16)
    k_ref[0] = qk_norm(proj(2 * D_LRU + D_ATT, D_ATT), gk_ref[...]).astype(BF16)
    vt_ref[0] = lax.dot_general(wvt_ref[...], hb, NT_DIMS,
                                preferred_element_type=F32).astype(BF16)


def _in_projection(x, scale1, shift1, g_norm1, w_in, g_q, g_k, tm):
    bsz, s, d = x.shape
    n = w_in.shape[1] - D_ATT
    w_main = w_in[:, :n].astype(BF16)
    w_vt = w_in[:, n:].T.astype(BF16)
    seg = jnp.arange(D_ATT, dtype=jnp.int32) // HEAD_DK
    mseg = (seg[:, None] == seg[None, :]).astype(BF16)
    n_sub = D_ATT // HEAD_DK
    row = lambda b, i: (b, i, 0)
    vec = lambda b, i: (b, 0, 0)
    full = lambda b, i: (0, 0)
    out_block = pl.BlockSpec((1, tm, D_LRU), row)
    return pl.pallas_call(
        _inproj_kernel,
        grid=(bsz, s // tm),
        in_specs=[pl.BlockSpec((1, tm, d), row),
                  pl.BlockSpec((1, 1, d), vec),
                  pl.BlockSpec((1, 1, d), vec),
                  pl.BlockSpec((1, d), full),
                  pl.BlockSpec((d, n), full),
                  pl.BlockSpec((D_ATT, d), full),
                  pl.BlockSpec((D_ATT, D_ATT), full),
                  pl.BlockSpec((1, D_ATT), full),
                  pl.BlockSpec((1, D_ATT), full)],
        out_specs=[out_block] * 4 + [pl.BlockSpec((1, D_ATT, tm), lambda b, i: (b, 0, i))],
        out_shape=[jax.ShapeDtypeStruct((bsz, s, D_LRU), F32)]
                  + [jax.ShapeDtypeStruct((bsz, s, D_LRU), BF16)] * 3
                  + [jax.ShapeDtypeStruct((bsz, D_ATT, s), BF16)],
        compiler_params=_params(("parallel", "parallel"), VMEM_LIMIT),
        name="norm1_in_proj",
    )(x, scale1, shift1, g_norm1.reshape(1, d), w_main, w_vt, mseg,
      jnp.tile(g_q, n_sub).reshape(1, D_ATT), jnp.tile(g_k, n_sub).reshape(1, D_ATT))


def _lru_kernel(x_ref, gz_ref, cw_ref, cb_ref, wa_ref, wx_ref, ba_ref, bx_ref, lam_ref, y_ref,
                xpad, a_f, u_f, a_b, u_b, *, tc):
    s = x_ref.shape[1]
    c = x_ref.shape[2]
    n_chunks = s // tc
    zeros8 = jnp.zeros((8, c), F32)
    xpad[0:8, :] = zeros8
    xpad[s + 8:s + 16, :] = zeros8

    def fill(ci, carry):
        t0 = pl.multiple_of(ci * tc, tc)
        xpad[pl.ds(t0 + 8, tc), :] = x_ref[0, pl.ds(t0, tc), :]
        return carry

    lax.fori_loop(0, n_chunks, fill, 0)

    cw = cw_ref[...]
    cb = cb_ref[...]
    decay = []
    for d in range(2):
        lam = lam_ref[d]
        softplus_neg = jnp.maximum(-lam, 0.0) + jnp.log(1.0 + jnp.exp(-jnp.abs(lam)))
        decay.append(-LRU_C * softplus_neg)
    a_scr = (a_f, a_b)
    u_scr = (u_f, u_b)

    def gates(ci, carry):
        t0 = pl.multiple_of(ci * tc, tc)
        xw = xpad[pl.ds(t0, tc + 16), :]
        xc = (cw[0:1] * pltpu.roll(xw, 2, 0)[8:8 + tc]
              + cw[1:2] * pltpu.roll(xw, 1, 0)[8:8 + tc]
              + cw[2:3] * xw[8:8 + tc]
              + cw[3:4] * pltpu.roll(xw, tc + 15, 0)[8:8 + tc]) + cb
        xcb = xc.astype(BF16)
        for d in range(2):
            r = _sigmoid(jnp.dot(xcb, wa_ref[d], preferred_element_type=F32) + ba_ref[d])
            i = _sigmoid(jnp.dot(xcb, wx_ref[d], preferred_element_type=F32) + bx_ref[d])
            a = jnp.exp(r * decay[d])
            a_scr[d][pl.ds(t0, tc), :] = a
            u_scr[d][pl.ds(t0, tc), :] = jnp.sqrt(1.0 - a * a) * (i * xc)
        return carry

    lax.fori_loop(0, n_chunks, gates, 0)

    def step(t, carry):
        hf, hb = carry
        hf = a_f[pl.ds(t, 1), :] * hf + u_f[pl.ds(t, 1), :]
        u_f[pl.ds(t, 1), :] = hf
        tb = s - 1 - t
        hb = a_b[pl.ds(tb, 1), :] * hb + u_b[pl.ds(tb, 1), :]
        u_b[pl.ds(tb, 1), :] = hb
        return hf, hb

    zero_row = jnp.zeros((1, c), F32)
    lax.fori_loop(0, s, step, (zero_row, zero_row), unroll=8)

    def emit(ci, carry):
        t0 = pl.multiple_of(ci * tc, tc)
        hsum = u_f[pl.ds(t0, tc), :] + u_b[pl.ds(t0, tc), :]
        y_ref[0, pl.ds(t0, tc), :] = (hsum * gz_ref[0, pl.ds(t0, tc), :].astype(F32)).astype(BF16)
        return carry

    lax.fori_loop(0, n_chunks, emit, 0)


def _block_diag(w, half):
    n_dir, n_blocks, blk, _ = w.shape
    per = half // blk
    n_half = n_blocks // per
    w = w.reshape(n_dir, n_half, per, blk, blk)
    eye = jnp.eye(per, dtype=w.dtype)
    out = w[:, :, :, :, None, :] * eye[None, None, :, None, :, None]
    return out.reshape(n_dir, n_half, half, half)


def _rg_lru(x_lru, gz, conv_w, conv_b, w_a, b_a, w_x, b_x, lam, tc):
    bsz, s, c = x_lru.shape
    half = LRU_HALF
    n_half = c // half
    wa = _block_diag(w_a, half).astype(BF16)
    wx = _block_diag(w_x, half).astype(BF16)
    seq = lambda b, p: (b, 0, p)
    chan = lambda b, p: (0, p)
    dirchan = lambda b, p: (0, 0, p)
    blk = lambda b, p: (0, p, 0, 0)
    return pl.pallas_call(
        functools.partial(_lru_kernel, tc=tc),
        grid=(bsz, n_half),
        in_specs=[pl.BlockSpec((1, s, half), seq),
                  pl.BlockSpec((1, s, half), seq),
                  pl.BlockSpec((CONV_W, half), chan),
                  pl.BlockSpec((1, half), chan),
                  pl.BlockSpec((2, None, half, half), blk),
                  pl.BlockSpec((2, None, half, half), blk),
                  pl.BlockSpec((2, 1, half), dirchan),
                  pl.BlockSpec((2, 1, half), dirchan),
                  pl.BlockSpec((2, 1, half), dirchan)],
        out_specs=pl.BlockSpec((1, s, half), seq),
        out_shape=jax.ShapeDtypeStruct((bsz, s, c), BF16),
        scratch_shapes=[pltpu.VMEM((s + 16, half), F32)] + [pltpu.VMEM((s, half), F32)] * 4,
        compiler_params=_params(("parallel", "parallel"), VMEM_LIMIT),
        name="rg_lru",
    )(x_lru, gz, conv_w.reshape(CONV_W, c), conv_b.reshape(1, c), wa, wx,
      b_a.reshape(2, 1, c), b_x.reshape(2, 1, c), lam.reshape(2, 1, c))


def _attn_kernel(tab_ref, q_ref, k_ref, vt_ref, bias_ref, lq_ref, go_ref, o_ref, s_scr, p_scr, o_scr, *, lam_init):
    h = pl.program_id(1)
    i = pl.program_id(2)
    t = q_ref.shape[1]
    n_k = k_ref.shape[1] // t
    chunk = 4
    n_c = n_k // chunk
    q = q_ref[0]
    lane = lax.broadcasted_iota(jnp.int32, q.shape, 1)
    zero = jnp.zeros_like(q)
    q_sub = (jnp.where(lane < HEAD_DK, q, zero), jnp.where(lane >= HEAD_DK, q, zero))
    c_left = tab_ref[N_BUCKETS // 2 - 1, h] * LOG2E
    c_right = tab_ref[N_BUCKETS - 1, h] * LOG2E

    def fold(x, op):
        parts = [x[r * 8:(r + 1) * 8, :] for r in range(t // 8)]
        acc = parts[:2]
        for r in range(2, len(parts)):
            acc[r % 2] = op(acc[r % 2], parts[r])
        return op(acc[0], acc[1])

    def far_bias(j):
        return jnp.where(j < i, c_left, c_right)

    def is_far(j):
        return jnp.abs(j - i) >= 2

    m_acc = [jnp.full((8, t), -jnp.inf, F32) for _ in range(2)]
    for c in range(n_c):
        rows = slice(c * chunk * t, (c + 1) * chunk * t)
        for u in range(2):
            sc_all = lax.dot_general(k_ref[0, rows, :], q_sub[u], NT_DIMS, preferred_element_type=F32)
            for jj in range(chunk):
                j = c * chunk + jj
                sc = sc_all[jj * t:(jj + 1) * t, :]
                s_scr[u, j] = sc
                m_acc[u] = jnp.maximum(m_acc[u], fold(sc, jnp.maximum)
                                       + jnp.where(is_far(j), far_bias(j), -jnp.inf))

    for dj in (-1, 0, 1):
        j = i + dj
        valid = jnp.logical_and(j >= 0, j < n_k)
        jc = jnp.clip(j, 0, n_k - 1)
        bias = jnp.where(valid, bias_ref[0, dj + 2], 0.0)
        for u in range(2):
            sc = s_scr[u, jc] + bias
            s_scr[u, jc] = sc
            m_acc[u] = jnp.maximum(m_acc[u], fold(sc, jnp.maximum) + jnp.where(valid, 0.0, -jnp.inf))

    outs = []
    for u in range(2):
        m = jnp.max(m_acc[u], axis=0, keepdims=True)
        l_acc = jnp.zeros((8, t), F32)
        for j in range(n_k):
            p = jnp.exp2(s_scr[u, j] - (m - jnp.where(is_far(j), far_bias(j), 0.0)))
            p_scr[u, j * t:(j + 1) * t, :] = p.astype(BF16)
            l_acc = l_acc + fold(p, jnp.add)
            if (j + 1) % chunk == 0:
                rows = slice((j + 1 - chunk) * t, (j + 1) * t)
                part = jnp.dot(vt_ref[0, :, rows], p_scr[u, rows, :], preferred_element_type=F32)
                if j + 1 == chunk:
                    o_scr[u] = part
                else:
                    o_scr[u] += part
        outs.append(o_scr[u] / jnp.sum(l_acc, axis=0, keepdims=True))

    lq = lq_ref[...]
    lam = (jnp.exp(jnp.sum(lq[0:1] * lq[1:2], axis=-1, keepdims=True))
           - jnp.exp(jnp.sum(lq[2:3] * lq[3:4], axis=-1, keepdims=True)) + lam_init)
    o = (outs[0] - lam * outs[1]).T
    ms = jnp.mean(o * o, axis=-1, keepdims=True)
    o_ref[0] = ((o * lax.rsqrt(ms + EPS) * go_ref[...]) * (1.0 - lam_init)).astype(BF16)


def _diff_attention(qn, kn, vt, rel_bias, bias_tiles, lambda_qk, g_o, lam_init):
    bsz, s, _ = qn.shape
    t = bias_tiles.shape[-1]
    n_k = s // t
    return pl.pallas_call(
        functools.partial(_attn_kernel, lam_init=lam_init),
        grid=(bsz, N_HEADS, s // t),
        in_specs=[pl.BlockSpec(memory_space=pltpu.SMEM),
                  pl.BlockSpec((1, t, HEAD_DV), lambda b, h, i: (b, i, h)),
                  pl.BlockSpec((1, s, HEAD_DV), lambda b, h, i: (b, 0, h)),
                  pl.BlockSpec((1, HEAD_DV, s), lambda b, h, i: (b, h, 0)),
                  pl.BlockSpec((1, 5, t, t), lambda b, h, i: (h, 0, 0, 0)),
                  pl.BlockSpec((4, HEAD_DK), lambda b, h, i: (0, 0)),
                  pl.BlockSpec((1, HEAD_DV), lambda b, h, i: (0, 0))],
        out_specs=pl.BlockSpec((1, t, HEAD_DV), lambda b, h, i: (b, i, h)),
        out_shape=jax.ShapeDtypeStruct((bsz, s, D_ATT), BF16),
        scratch_shapes=[pltpu.VMEM((2, n_k, t, t), F32), pltpu.VMEM((2, s, t), BF16),
                        pltpu.VMEM((2, HEAD_DV, t), F32)],
        compiler_params=_params(("parallel", "parallel", "parallel"), VMEM_LIMIT),
        name="diff_attention",
    )(rel_bias, qn, kn, vt, bias_tiles, lambda_qk, g_o.reshape(1, HEAD_DV))


def _outproj_kernel(yl_ref, ya_ref, x_ref, gate_ref, sc_ref, sh_ref, g2_ref, wo_ref, wr_ref,
                    x1_ref, h2_ref, aff_ref):
    mix = (jnp.dot(yl_ref[0], wo_ref[0:D_LRU, :], preferred_element_type=F32)
           + jnp.dot(ya_ref[0], wo_ref[D_LRU:D_LRU + D_ATT, :], preferred_element_type=F32))
    x1 = x_ref[0] + gate_ref[0] * mix
    x1_ref[0] = x1
    ms = jnp.mean(x1 * x1, axis=-1, keepdims=True)
    h2 = (x1 * lax.rsqrt(ms + EPS) * g2_ref[...]) * (1.0 + sc_ref[0]) + sh_ref[0]
    h2_ref[0] = h2.astype(BF16)
    logits = lax.dot_general(wr_ref[...], h2, NT_DIMS, precision=HIGHEST,
                             preferred_element_type=F32)
    ex = jnp.exp(logits - jnp.max(logits, axis=0, keepdims=True))
    aff_ref[0] = ex / jnp.sum(ex, axis=0, keepdims=True)


def _out_projection(y_lru, y_att, x, gate1, scale2, shift2, g_norm2, w_out, w_router, tm):
    bsz, s, d = x.shape
    row = lambda b, i: (b, i, 0)
    vec = lambda b, i: (b, 0, 0)
    full = lambda b, i: (0, 0)
    return pl.pallas_call(
        _outproj_kernel,
        grid=(bsz, s // tm),
        in_specs=[pl.BlockSpec((1, tm, D_LRU), row),
                  pl.BlockSpec((1, tm, D_ATT), row),
                  pl.BlockSpec((1, tm, d), row),
                  pl.BlockSpec((1, 1, d), vec),
                  pl.BlockSpec((1, 1, d), vec),
                  pl.BlockSpec((1, 1, d), vec),
                  pl.BlockSpec((1, d), full),
                  pl.BlockSpec((D_LRU + D_ATT, d), full),
                  pl.BlockSpec((N_EXPERTS, d), full)],
        out_specs=[pl.BlockSpec((1, tm, d), row),
                   pl.BlockSpec((1, tm, d), row),
                   pl.BlockSpec((1, N_EXPERTS, tm), lambda b, i: (b, 0, i))],
        out_shape=[jax.ShapeDtypeStruct((bsz, s, d), F32),
                   jax.ShapeDtypeStruct((bsz, s, d), BF16),
                   jax.ShapeDtypeStruct((bsz, N_EXPERTS, s), F32)],
        compiler_params=_params(("parallel", "parallel"), VMEM_LIMIT),
        name="out_proj_norm2_router",
    )(y_lru, y_att, x, gate1, scale2, shift2, g_norm2.reshape(1, d), w_out.astype(BF16),
      w_router.T)


def _route_kernel(aff_ref, pos_ref, cnt_ref, *, cap, n_tok_chunks):
    aff = aff_ref[0]
    n_e, s = aff.shape
    bits = lax.bitcast_convert_type(aff, jnp.int32)
    capf = float(cap)

    def count(mask):
        return jnp.sum(jnp.where(mask, 1.0, 0.0), axis=-1, keepdims=True)

    tau = jnp.zeros((n_e, 1), jnp.int32)
    for bit in range(30, -1, -1):
        cand = tau | (1 << bit)
        tau = jnp.where(count(bits >= cand) >= capf, cand, tau)
    gt = bits > tau
    eq = bits == tau
    need = capf - count(gt)

    blk = 256
    r = lax.broadcasted_iota(jnp.int32, (blk, blk), 0)
    cidx = lax.broadcasted_iota(jnp.int32, (blk, blk), 1)
    upper = jnp.where(r < cidx, 1.0, 0.0).astype(BF16)

    def prefix_blocks(mask):
        off = jnp.zeros((n_e, 1), F32)
        pieces, offs = [], []
        for k in range(s // blk):
            mb = jnp.where(mask[:, k * blk:(k + 1) * blk], 1.0, 0.0)
            offs.append(off)
            pieces.append(jnp.dot(mb.astype(BF16), upper, preferred_element_type=F32) + off)
            off = off + jnp.sum(mb, axis=-1, keepdims=True)
        return pieces, offs

    eq_rank, _ = prefix_blocks(eq)
    sel_blocks = []
    for k in range(s // blk):
        sl = slice(k * blk, (k + 1) * blk)
        sel_blocks.append(jnp.logical_or(gt[:, sl], jnp.logical_and(eq[:, sl], eq_rank[k] < need)))
    sel = jnp.concatenate(sel_blocks, axis=1)
    slot, offs = prefix_blocks(sel)
    for k in range(s // blk):
        pos_ref[0, :, k * blk:(k + 1) * blk] = jnp.where(sel_blocks[k], slot[k], -1.0).astype(jnp.int32)

    lane = lax.broadcasted_iota(jnp.int32, (n_e, 128), 1)
    cnt = jnp.zeros((n_e, 128), F32)
    per = (s // n_tok_chunks) // blk
    for j in range(n_tok_chunks):
        cnt = jnp.where(lane == j, offs[j * per], cnt)
    cnt_ref[0] = cnt.astype(jnp.int32)


def _routing(aff, cap):
    bsz, n_e, s = aff.shape
    return pl.pallas_call(
        functools.partial(_route_kernel, cap=cap, n_tok_chunks=N_TOK_CHUNKS),
        grid=(bsz,),
        in_specs=[pl.BlockSpec((1, n_e, s), lambda b: (b, 0, 0))],
        out_specs=[pl.BlockSpec((1, n_e, s), lambda b: (b, 0, 0)),
                   pl.BlockSpec((1, n_e, 128), lambda b: (b, 0, 0))],
        out_shape=[jax.ShapeDtypeStruct((bsz, n_e, s), jnp.int32),
                   jax.ShapeDtypeStruct((bsz, n_e, 128), jnp.int32)],
        compiler_params=_params(("parallel",)),
        name="expert_choice_routing",
    )(aff)


def _moe_kernel(cnt_ref, pos_ref, aff_ref, h2_ref, w1_ref, w3_ref, w2_ref, out_ref,
                xe_scr, g_scr, y_scr, *, cap):
    b = pl.program_id(0)
    e = pl.program_id(1)
    n_e = pl.num_programs(1)
    s = h2_ref.shape[1]
    tchunk = s // N_TOK_CHUNKS
    cchunk = cap // N_CAP_CHUNKS

    @pl.when(e == 0)
    def _():
        out_ref[...] = jnp.zeros_like(out_ref)

    xe_scr[...] = jnp.zeros_like(xe_scr)
    g_scr[...] = jnp.zeros_like(g_scr)
    base = (b * n_e + e) * N_TOK_CHUNKS
    slot_iota = lax.broadcasted_iota(jnp.int32, (cchunk, tchunk), 0)

    def blocks():
        for j in range(N_TOK_CHUNKS):
            c0 = cnt_ref[base + j]
            c1 = cnt_ref[base + j + 1] if j + 1 < N_TOK_CHUNKS else cap
            for i in range(N_CAP_CHUNKS):
                yield j, i, jnp.logical_and(c0 < cchunk * (i + 1), c1 > cchunk * i)

    def one_hot(j, i):
        posj = pos_ref[0, 0, :, j * tchunk:(j + 1) * tchunk]
        return posj == slot_iota + cchunk * i

    for j, i, hit in blocks():
        @pl.when(hit)
        def _(j=j, i=i):
            sel = one_hot(j, i)
            rows = slice(i * cchunk, (i + 1) * cchunk)
            toks = slice(j * tchunk, (j + 1) * tchunk)
            xe_scr[rows, :] += jnp.dot(jnp.where(sel, 1.0, 0.0).astype(BF16), h2_ref[0, toks, :],
                                       preferred_element_type=F32)
            g_scr[rows, :] += jnp.sum(jnp.where(sel, aff_ref[0, 0, :, toks], 0.0), axis=-1, keepdims=True)

    xe = xe_scr[...].astype(BF16)
    a = jnp.dot(xe, w1_ref[0], preferred_element_type=F32)
    gate = jnp.dot(xe, w3_ref[0], preferred_element_type=F32)
    hmid = ((a * _sigmoid(a)) * gate).astype(BF16)
    y = jnp.dot(hmid, w2_ref[0], preferred_element_type=F32) * g_scr[...]
    y_scr[...] = y.astype(BF16)

    for j, i, hit in blocks():
        @pl.when(hit)
        def _(j=j, i=i):
            sel = one_hot(j, i)
            rows = slice(i * cchunk, (i + 1) * cchunk)
            toks = slice(j * tchunk, (j + 1) * tchunk)
            out_ref[0, toks, :] += lax.dot_general(jnp.where(sel, 1.0, 0.0).astype(BF16), y_scr[rows, :],
                                                   TN_DIMS, preferred_element_type=F32)


def _moe(cnt, pos, aff, h2, w1, w3, w2, cap):
    bsz, s, d = h2.shape
    n_e = w1.shape[0]
    f = w1.shape[2]
    tok_row = lambda b, e, c: (b, e, 0, 0)
    wspec = lambda b, e, c: (e, 0, 0)
    resident = lambda b, e, c: (b, 0, 0)
    grid_spec = pltpu.PrefetchScalarGridSpec(
        num_scalar_prefetch=1,
        grid=(bsz, n_e),
        in_specs=[pl.BlockSpec((1, 1, 1, s), tok_row),
                  pl.BlockSpec((1, 1, 1, s), tok_row),
                  pl.BlockSpec((1, s, d), resident, pipeline_mode=pl.Buffered(1)),
                  pl.BlockSpec((1, d, f), wspec),
                  pl.BlockSpec((1, d, f), wspec),
                  pl.BlockSpec((1, f, d), wspec)],
        out_specs=pl.BlockSpec((1, s, d), resident, pipeline_mode=pl.Buffered(1)),
        scratch_shapes=[pltpu.VMEM((cap, d), F32), pltpu.VMEM((cap, 1), F32), pltpu.VMEM((cap, d), BF16)],
    )
    return pl.pallas_call(
        functools.partial(_moe_kernel, cap=cap),
        grid_spec=grid_spec,
        out_shape=jax.ShapeDtypeStruct((bsz, s, d), F32),
        compiler_params=_params(("arbitrary", "arbitrary"), VMEM_LIMIT),
        name="expert_choice_ffn",
    )(cnt.reshape(-1), pos.reshape(bsz, n_e, 1, s), aff.reshape(bsz, n_e, 1, s), h2,
      w1.astype(BF16), w3.astype(BF16), w2.astype(BF16))


def _residual_kernel(x_ref, gate_ref, y_ref, o_ref):
    o_ref[0] = x_ref[0] + gate_ref[0] * y_ref[0]


def _gated_residual(x1, gate2, y, tm):
    bsz, s, d = x1.shape
    row = lambda b, i: (b, i, 0)
    return pl.pallas_call(
        _residual_kernel,
        grid=(bsz, s // tm),
        in_specs=[pl.BlockSpec((1, tm, d), row),
                  pl.BlockSpec((1, 1, d), lambda b, i: (b, 0, 0)),
                  pl.BlockSpec((1, tm, d), row)],
        out_specs=pl.BlockSpec((1, tm, d), row),
        out_shape=jax.ShapeDtypeStruct((bsz, s, d), F32),
        compiler_params=_params(("parallel", "parallel")),
        name="gated_residual",
    )(x1, gate2, y)


def kernel(x, c, w_mod, b_mod, g_norm1, w_in, conv_w, conv_b, lru_w_a, lru_b_a, lru_w_x, lru_b_x,
           lru_lambda, g_q, g_k, lambda_qk, g_attn_out, rel_bias, w_out, g_norm2, w_router, w1, w3, w2):
    bsz, s, d = x.shape
    depth = w_mod.shape[0]
    cap = max(1, EC_FACTOR * s // N_EXPERTS)
    tm = min(512, s)
    bias_tiles = _bias_tiles(rel_bias, min(ATT_TILE, s))
    for l in range(depth):
        mod = _modulation(c, w_mod[l], b_mod[l])
        shift1, scale1, gate1, shift2, scale2, gate2 = [m.reshape(bsz, 1, d) for m in jnp.split(mod, 6, axis=-1)]
        x_lru, gz, qn, kn, vt = _in_projection(x, scale1, shift1, g_norm1[l], w_in[l], g_q[l], g_k[l], tm)
        y_lru = _rg_lru(x_lru, gz, conv_w[l], conv_b[l], lru_w_a[l], lru_b_a[l], lru_w_x[l], lru_b_x[l],
                        lru_lambda[l], tm)
        lam_init = 0.8 - 0.6 * math.exp(-0.3 * l)
        y_att = _diff_attention(qn, kn, vt, rel_bias, bias_tiles, lambda_qk[l], g_attn_out[l], lam_init)
        x1, h2, aff = _out_projection(y_lru, y_att, x, gate1, scale2, shift2, g_norm2[l], w_out[l],
                                      w_router[l], tm)
        pos, cnt = _routing(aff, cap)
        y = _moe(cnt[:, :, :N_TOK_CHUNKS], pos, aff, h2, w1[l], w3[l], w2[l], cap)
        x = _gated_residual(x1, gate2, y, tm)
    return x
```

```python
import functools
import math

import jax
import jax.numpy as jnp
from jax import lax
from jax.experimental import pallas as pl
from jax.experimental.pallas import tpu as pltpu

F32 = jnp.float32
BF16 = jnp.bfloat16
HIGHEST = lax.Precision.HIGHEST

D_MODEL = 1024
D_LRU = 512
LRU_BLOCK = 64
LRU_C = 8.0
CONV_W = 4
N_HEADS = 4
HEAD_DV = 128
HEAD_DK = 64
D_ATT = N_HEADS * HEAD_DV
N_BUCKETS = 32
N_EXPERTS = 16
EC_FACTOR = 2
EPS = 1e-6
LOG2E = math.log2(math.e)

LRU_HALF = 256
SCAN_LANES = 8
ATT_TILE = 256
N_TOK_CHUNKS = 8
N_CAP_CHUNKS = 4
N_SCATTER_CHUNKS = 2
VMEM_LIMIT = 56 * 1024 * 1024

NT_DIMS = (((1,), (1,)), ((), ()))
TN_DIMS = (((0,), (0,)), ((), ()))


def _sigmoid(x):
    return 1.0 / (1.0 + jnp.exp(-x))


def _params(sem, vmem=None, flags=None):
    return pltpu.CompilerParams(dimension_semantics=sem, vmem_limit_bytes=vmem, flags=flags)


def _mod_kernel(c_ref, w_ref, b_ref, o_ref):
    c = c_ref[...]
    o_ref[...] = jnp.dot(c * _sigmoid(c), w_ref[...], precision=HIGHEST,
                         preferred_element_type=F32) + b_ref[...]


def _modulation(c, w_mod, b_mod):
    bsz, d = c.shape
    n = w_mod.shape[1]
    return pl.pallas_call(
        _mod_kernel,
        grid=(n // d,),
        in_specs=[pl.BlockSpec((bsz, d), lambda j: (0, 0)),
                  pl.BlockSpec((d, d), lambda j: (0, j)),
                  pl.BlockSpec((1, d), lambda j: (0, j))],
        out_specs=pl.BlockSpec((bsz, d), lambda j: (0, j)),
        out_shape=jax.ShapeDtypeStruct((bsz, n), F32),
        compiler_params=_params(("arbitrary",)),
        name="adaln_mod",
    )(c, w_mod, b_mod.reshape(1, n))


def _bias_kernel(tab_ref, o_ref):
    h = pl.program_id(0)
    t = o_ref.shape[-1]
    key = lax.broadcasted_iota(jnp.int32, (t, t), 0)
    qry = lax.broadcasted_iota(jnp.int32, (t, t), 1)
    half = N_BUCKETS // 2
    max_exact = half // 2
    for d in range(5):
        if d == 0:
            o_ref[0, d] = jnp.full((t, t), tab_ref[half - 1, h] * LOG2E, F32)
        elif d == 4:
            o_ref[0, d] = jnp.full((t, t), tab_ref[N_BUCKETS - 1, h] * LOG2E, F32)
        else:
            rel = (d - 2) * t + key - qry
            n = jnp.abs(rel)
            n2 = n * n
            large = jnp.full((t, t), max_exact, jnp.int32)
            for k in range(1, half - max_exact):
                large = large + jnp.where(n2 >= (max_exact * max_exact) * (2 ** k), 1, 0)
            idx = jnp.where(n < max_exact, n, large) + jnp.where(rel > 0, half, 0)
            val = jnp.zeros((t, t), F32)
            for j in range(N_BUCKETS):
                val = jnp.where(idx == j, tab_ref[j, h] * LOG2E, val)
            o_ref[0, d] = val


def _bias_tiles(rel_bias, t):
    return pl.pallas_call(
        _bias_kernel,
        grid=(N_HEADS,),
        in_specs=[pl.BlockSpec(memory_space=pltpu.SMEM)],
        out_specs=pl.BlockSpec((1, 5, t, t), lambda h: (h, 0, 0, 0)),
        out_shape=jax.ShapeDtypeStruct((N_HEADS, 5, t, t), F32),
        compiler_params=_params(("arbitrary",)),
        name="t5_bias_tiles",
    )(rel_bias)


def _inproj_kernel(x_ref, sc_ref, sh_ref, g1_ref, w_ref, wvt_ref, mseg_ref, gq_ref, gk_ref,
                   xl_ref, gz_ref, q_ref, k_ref, vt_ref):
    x = x_ref[0]
    ms = jnp.mean(x * x, axis=-1, keepdims=True)
    h = (x * lax.rsqrt(ms + EPS) * g1_ref[...]) * (1.0 + sc_ref[0]) + sh_ref[0]
    hb = h.astype(BF16)

    def proj(lo, width):
        return jnp.dot(hb, w_ref[:, lo:lo + width], preferred_element_type=F32)

    def qk_norm(t, g):
        t2 = t * t
        hi = t2.astype(BF16)
        lo = (t2 - hi.astype(F32)).astype(BF16)
        ss = (jnp.dot(hi, mseg_ref[...], preferred_element_type=F32)
              + jnp.dot(lo, mseg_ref[...], preferred_element_type=F32))
        return t * lax.rsqrt(ss * (1.0 / HEAD_DK) + EPS) * g

    xl_ref[0] = proj(0, D_LRU)
    z = proj(D_LRU, D_LRU)
    cdf = 0.5 * (1.0 + jnp.tanh(math.sqrt(2.0 / math.pi) * (z + 0.044715 * (z * z * z))))
    gz_ref[0] = (z * cdf).astype(BF16)
    q_ref[0] = (qk_norm(proj(2 * D_LRU, D_ATT), gq_ref[...]) * (HEAD_DK ** -0.5 * LOG2E)).astype(BF16)
    k_ref[0] = qk_norm(proj(2 * D_LRU + D_ATT, D_ATT), gk_ref[...]).astype(BF16)
    vt_ref[0] = lax.dot_general(wvt_ref[...], hb, NT_DIMS,
                                preferred_element_type=F32).astype(BF16)


def _in_projection(x, scale1, shift1, g_norm1, w_in, g_q, g_k, tm):
    bsz, s, d = x.shape
    n = w_in.shape[1] - D_ATT
    w_main = w_in[:, :n].astype(BF16)
    w_vt = w_in[:, n:].T.astype(BF16)
    seg = jnp.arange(D_ATT, dtype=jnp.int32) // HEAD_DK
    mseg = (seg[:, None] == seg[None, :]).astype(BF16)
    n_sub = D_ATT // HEAD_DK
    row = lambda b, i: (b, i, 0)
    vec = lambda b, i: (b, 0, 0)
    full = lambda b, i: (0, 0)
    out_block = pl.BlockSpec((1, tm, D_LRU), row)
    return pl.pallas_call(
        _inproj_kernel,
        grid=(bsz, s // tm),
        in_specs=[pl.BlockSpec((1, tm, d), row),
                  pl.BlockSpec((1, 1, d), vec),
                  pl.BlockSpec((1, 1, d), vec),
                  pl.BlockSpec((1, d), full),
                  pl.BlockSpec((d, n), full),
                  pl.BlockSpec((D_ATT, d), full),
                  pl.BlockSpec((D_ATT, D_ATT), full),
                  pl.BlockSpec((1, D_ATT), full),
                  pl.BlockSpec((1, D_ATT), full)],
        out_specs=[out_block] * 4 + [pl.BlockSpec((1, D_ATT, tm), lambda b, i: (b, 0, i))],
        out_shape=[jax.ShapeDtypeStruct((bsz, s, D_LRU), F32)]
                  + [jax.ShapeDtypeStruct((bsz, s, D_LRU), BF16)] * 3
                  + [jax.ShapeDtypeStruct((bsz, D_ATT, s), BF16)],
        compiler_params=_params(("parallel", "parallel"), VMEM_LIMIT),
        name="norm1_in_proj",
    )(x, scale1, shift1, g_norm1.reshape(1, d), w_main, w_vt, mseg,
      jnp.tile(g_q, n_sub).reshape(1, D_ATT), jnp.tile(g_k, n_sub).reshape(1, D_ATT))


def _lru_kernel(x_ref, gz_ref, cw_ref, cb_ref, wa_ref, wx_ref, ba_ref, bx_ref, lam_ref, y_ref,
                xpad, a_f, u_f, a_b, u_b, *, tc):
    s = x_ref.shape[1]
    c = x_ref.shape[2]
    n_chunks = s // tc
    n_slab = c // 128
    pitch = a_f.shape[1] // n_chunks
    zeros8 = jnp.zeros((8, c), F32)
    xpad[0:8, :] = zeros8
    xpad[s + 8:s + 16, :] = zeros8

    def fill(ci, carry):
        t0 = pl.multiple_of(ci * tc, tc)
        xpad[pl.ds(t0 + 8, tc), :] = x_ref[0, pl.ds(t0, tc), :]
        return carry

    lax.fori_loop(0, n_chunks, fill, 0)

    cw = cw_ref[...]
    cb = cb_ref[...]
    decay = []
    for d in range(2):
        lam = lam_ref[d]
        softplus_neg = jnp.maximum(-lam, 0.0) + jnp.log(1.0 + jnp.exp(-jnp.abs(lam)))
        decay.append(-LRU_C * softplus_neg)
    a_scr = (a_f, a_b)
    u_scr = (u_f, u_b)

    def gates(ci, carry):
        t0 = pl.multiple_of(ci * tc, tc)
        xw = xpad[pl.ds(t0, tc + 16), :]
        xc = (cw[0:1] * pltpu.roll(xw, 2, 0)[8:8 + tc]
              + cw[1:2] * pltpu.roll(xw, 1, 0)[8:8 + tc]
              + cw[2:3] * xw[8:8 + tc]
              + cw[3:4] * pltpu.roll(xw, tc + 15, 0)[8:8 + tc]) + cb
        xcb = xc.astype(BF16)
        for d in range(2):
            r = _sigmoid(jnp.dot(xcb, wa_ref[d], preferred_element_type=F32) + ba_ref[d])
            i = _sigmoid(jnp.dot(xcb, wx_ref[d], preferred_element_type=F32) + bx_ref[d])
            a = jnp.exp(r * decay[d])
            u = jnp.sqrt(1.0 - a * a) * (i * xc)
            r0 = pl.multiple_of(ci * pitch, 8)
            for sl in range(n_slab):
                a_scr[d][sl, pl.ds(r0, tc), :] = a[:, sl * 128:(sl + 1) * 128]
                u_scr[d][sl, pl.ds(r0, tc), :] = u[:, sl * 128:(sl + 1) * 128]
        return carry

    lax.fori_loop(0, n_chunks, gates, 0)

    def rows(tt):
        return pl.ds(tt, n_chunks, stride=pitch)

    def step(tt, carry):
        out = []
        for d in range(2):
            t_loc = tt if d == 0 else tc - 1 - tt
            for sl in range(n_slab):
                h, p = carry[len(out)]
                a8 = a_scr[d][sl, rows(t_loc), :]
                h = a8 * h + u_scr[d][sl, rows(t_loc), :]
                p = a8 * p
                u_scr[d][sl, rows(t_loc), :] = h
                a_scr[d][sl, rows(t_loc), :] = p
                out.append((h, p))
        return tuple(out)

    init = (jnp.zeros((n_chunks, 128), F32), jnp.ones((n_chunks, 128), F32))
    lax.fori_loop(0, tc, step, (init,) * (2 * n_slab), unroll=4)

    for sl in range(n_slab):
        lanes = slice(sl * 128, (sl + 1) * 128)
        h_end, p_end = u_f[sl, rows(tc - 1), :], a_f[sl, rows(tc - 1), :]
        h_beg, p_beg = u_b[sl, rows(0), :], a_b[sl, rows(0), :]
        carry_f = [jnp.zeros((1, 128), F32)]
        for r in range(1, n_chunks):
            carry_f.append(p_end[r - 1:r] * carry_f[-1] + h_end[r - 1:r])
        carry_b = [jnp.zeros((1, 128), F32)]
        for r in range(n_chunks - 2, -1, -1):
            carry_b.insert(0, p_beg[r + 1:r + 2] * carry_b[0] + h_beg[r + 1:r + 2])
        for r in range(n_chunks):
            blk = slice(r * pitch, r * pitch + tc)
            hsum = (u_f[sl, blk, :] + a_f[sl, blk, :] * carry_f[r]
                    + u_b[sl, blk, :] + a_b[sl, blk, :] * carry_b[r])
            y_ref[0, r * tc:(r + 1) * tc, lanes] = (hsum * gz_ref[0, r * tc:(r + 1) * tc, lanes].astype(F32)).astype(BF16)


def _block_diag(w, half):
    n_dir, n_blocks, blk, _ = w.shape
    per = half // blk
    n_half = n_blocks // per
    w = w.reshape(n_dir, n_half, per, blk, blk)
    eye = jnp.eye(per, dtype=w.dtype)
    out = w[:, :, :, :, None, :] * eye[None, None, :, None, :, None]
    return out.reshape(n_dir, n_half, half, half)


def _rg_lru(x_lru, gz, conv_w, conv_b, w_a, b_a, w_x, b_x, lam):
    bsz, s, c = x_lru.shape
    half = LRU_HALF
    tc = s // SCAN_LANES
    pitch = tc + 8 if (tc // 8) % 2 == 0 else tc + 16
    n_half = c // half
    wa = _block_diag(w_a, half).astype(BF16)
    wx = _block_diag(w_x, half).astype(BF16)
    seq = lambda b, p: (b, 0, p)
    chan = lambda b, p: (0, p)
    dirchan = lambda b, p: (0, 0, p)
    blk = lambda b, p: (0, p, 0, 0)
    return pl.pallas_call(
        functools.partial(_lru_kernel, tc=tc),
        grid=(bsz, n_half),
        in_specs=[pl.BlockSpec((1, s, half), seq),
                  pl.BlockSpec((1, s, half), seq),
                  pl.BlockSpec((CONV_W, half), chan),
                  pl.BlockSpec((1, half), chan),
                  pl.BlockSpec((2, None, half, half), blk),
                  pl.BlockSpec((2, None, half, half), blk),
                  pl.BlockSpec((2, 1, half), dirchan),
                  pl.BlockSpec((2, 1, half), dirchan),
                  pl.BlockSpec((2, 1, half), dirchan)],
        out_specs=pl.BlockSpec((1, s, half), seq),
        out_shape=jax.ShapeDtypeStruct((bsz, s, c), BF16),
        scratch_shapes=[pltpu.VMEM((s + 16, half), F32)]
                       + [pltpu.VMEM((half // 128, SCAN_LANES * pitch, 128), F32)] * 4,
        compiler_params=_params(("parallel", "parallel"), VMEM_LIMIT),
        name="rg_lru",
    )(x_lru, gz, conv_w.reshape(CONV_W, c), conv_b.reshape(1, c), wa, wx,
      b_a.reshape(2, 1, c), b_x.reshape(2, 1, c), lam.reshape(2, 1, c))


def _attn_kernel(tab_ref, q_ref, k_ref, vt_ref, bias_ref, lq_ref, go_ref, o_ref, s_scr, p_scr, o_scr, *, lam_init):
    h = pl.program_id(1)
    i = pl.program_id(2)
    t = q_ref.shape[1]
    n_k = k_ref.shape[1] // t
    chunk = 4
    n_c = n_k // chunk
    q = q_ref[0]
    lane = lax.broadcasted_iota(jnp.int32, q.shape, 1)
    zero = jnp.zeros_like(q)
    q_sub = (jnp.where(lane < HEAD_DK, q, zero), jnp.where(lane >= HEAD_DK, q, zero))
    c_left = tab_ref[N_BUCKETS // 2 - 1, h] * LOG2E
    c_right = tab_ref[N_BUCKETS - 1, h] * LOG2E

    def fold(x, op):
        parts = [x[r * 8:(r + 1) * 8, :] for r in range(t // 8)]
        acc = parts[:2]
        for r in range(2, len(parts)):
            acc[r % 2] = op(acc[r % 2], parts[r])
        return op(acc[0], acc[1])

    def far_bias(j):
        return jnp.where(j < i, c_left, c_right)

    def is_far(j):
        return jnp.abs(j - i) >= 2

    m_acc = [jnp.full((8, t), -jnp.inf, F32) for _ in range(2)]
    for c in range(n_c):
        rows = slice(c * chunk * t, (c + 1) * chunk * t)
        for u in range(2):
            sc_all = lax.dot_general(k_ref[0, rows, :], q_sub[u], NT_DIMS, preferred_element_type=F32)
            for jj in range(chunk):
                j = c * chunk + jj
                sc = sc_all[jj * t:(jj + 1) * t, :]
                s_scr[u, j] = sc
                m_acc[u] = jnp.maximum(m_acc[u], fold(sc, jnp.maximum)
                                       + jnp.where(is_far(j), far_bias(j), -jnp.inf))

    for dj in (-1, 0, 1):
        j = i + dj
        valid = jnp.logical_and(j >= 0, j < n_k)
        jc = jnp.clip(j, 0, n_k - 1)
        bias = jnp.where(valid, bias_ref[0, dj + 2], 0.0)
        for u in range(2):
            sc = s_scr[u, jc] + bias
            s_scr[u, jc] = sc
            m_acc[u] = jnp.maximum(m_acc[u], fold(sc, jnp.maximum) + jnp.where(valid, 0.0, -jnp.inf))

    outs = []
    for u in range(2):
        m = jnp.max(m_acc[u], axis=0, keepdims=True)
        l_acc = jnp.zeros((8, t), F32)
        for j in range(n_k):
            p = jnp.exp2(s_scr[u, j] - (m - jnp.where(is_far(j), far_bias(j), 0.0)))
            p_scr[u, j * t:(j + 1) * t, :] = p.astype(BF16)
            l_acc = l_acc + fold(p, jnp.add)
            if (j + 1) % chunk == 0:
                rows = slice((j + 1 - chunk) * t, (j + 1) * t)
                part = jnp.dot(vt_ref[0, :, rows], p_scr[u, rows, :], preferred_element_type=F32)
                if j + 1 == chunk:
                    o_scr[u] = part
                else:
                    o_scr[u] += part
        outs.append(o_scr[u] / jnp.sum(l_acc, axis=0, keepdims=True))

    lq = lq_ref[...]
    lam = (jnp.exp(jnp.sum(lq[0:1] * lq[1:2], axis=-1, keepdims=True))
           - jnp.exp(jnp.sum(lq[2:3] * lq[3:4], axis=-1, keepdims=True)) + lam_init)
    o = (outs[0] - lam * outs[1]).T
    ms = jnp.mean(o * o, axis=-1, keepdims=True)
    o_ref[0] = ((o * lax.rsqrt(ms + EPS) * go_ref[...]) * (1.0 - lam_init)).astype(BF16)


def _diff_attention(qn, kn, vt, rel_bias, bias_tiles, lambda_qk, g_o, lam_init):
    bsz, s, _ = qn.shape
    t = bias_tiles.shape[-1]
    n_k = s // t
    return pl.pallas_call(
        functools.partial(_attn_kernel, lam_init=lam_init),
        grid=(bsz, N_HEADS, s // t),
        in_specs=[pl.BlockSpec(memory_space=pltpu.SMEM),
                  pl.BlockSpec((1, t, HEAD_DV), lambda b, h, i: (b, i, h)),
                  pl.BlockSpec((1, s, HEAD_DV), lambda b, h, i: (b, 0, h)),
                  pl.BlockSpec((1, HEAD_DV, s), lambda b, h, i: (b, h, 0)),
                  pl.BlockSpec((1, 5, t, t), lambda b, h, i: (h, 0, 0, 0)),
                  pl.BlockSpec((4, HEAD_DK), lambda b, h, i: (0, 0)),
                  pl.BlockSpec((1, HEAD_DV), lambda b, h, i: (0, 0))],
        out_specs=pl.BlockSpec((1, t, HEAD_DV), lambda b, h, i: (b, i, h)),
        out_shape=jax.ShapeDtypeStruct((bsz, s, D_ATT), BF16),
        scratch_shapes=[pltpu.VMEM((2, n_k, t, t), F32), pltpu.VMEM((2, s, t), BF16),
                        pltpu.VMEM((2, HEAD_DV, t), F32)],
        compiler_params=_params(("parallel", "parallel", "parallel"), VMEM_LIMIT),
        name="diff_attention",
    )(rel_bias, qn, kn, vt, bias_tiles, lambda_qk, g_o.reshape(1, HEAD_DV))


def _outproj_kernel(yl_ref, ya_ref, x_ref, gate_ref, sc_ref, sh_ref, g2_ref, wo_ref, wr_ref,
                    x1_ref, h2_ref, aff_ref):
    mix = (jnp.dot(yl_ref[0], wo_ref[0:D_LRU, :], preferred_element_type=F32)
           + jnp.dot(ya_ref[0], wo_ref[D_LRU:D_LRU + D_ATT, :], preferred_element_type=F32))
    x1 = x_ref[0] + gate_ref[0] * mix
    x1_ref[0] = x1
    ms = jnp.mean(x1 * x1, axis=-1, keepdims=True)
    h2 = (x1 * lax.rsqrt(ms + EPS) * g2_ref[...]) * (1.0 + sc_ref[0]) + sh_ref[0]
    h2_ref[0] = h2.astype(BF16)
    logits = lax.dot_general(wr_ref[...], h2, NT_DIMS, precision=HIGHEST,
                             preferred_element_type=F32)
    ex = jnp.exp(logits - jnp.max(logits, axis=0, keepdims=True))
    aff_ref[0] = ex / jnp.sum(ex, axis=0, keepdims=True)


def _out_projection(y_lru, y_att, x, gate1, scale2, shift2, g_norm2, w_out, w_router, tm):
    bsz, s, d = x.shape
    row = lambda b, i: (b, i, 0)
    vec = lambda b, i: (b, 0, 0)
    full = lambda b, i: (0, 0)
    return pl.pallas_call(
        _outproj_kernel,
        grid=(bsz, s // tm),
        in_specs=[pl.BlockSpec((1, tm, D_LRU), row),
                  pl.BlockSpec((1, tm, D_ATT), row),
                  pl.BlockSpec((1, tm, d), row),
                  pl.BlockSpec((1, 1, d), vec),
                  pl.BlockSpec((1, 1, d), vec),
                  pl.BlockSpec((1, 1, d), vec),
                  pl.BlockSpec((1, d), full),
                  pl.BlockSpec((D_LRU + D_ATT, d), full),
                  pl.BlockSpec((N_EXPERTS, d), full)],
        out_specs=[pl.BlockSpec((1, tm, d), row),
                   pl.BlockSpec((1, tm, d), row),
                   pl.BlockSpec((1, N_EXPERTS, tm), lambda b, i: (b, 0, i))],
        out_shape=[jax.ShapeDtypeStruct((bsz, s, d), F32),
                   jax.ShapeDtypeStruct((bsz, s, d), BF16),
                   jax.ShapeDtypeStruct((bsz, N_EXPERTS, s), F32)],
        compiler_params=_params(("parallel", "parallel"), VMEM_LIMIT),
        name="out_proj_norm2_router",
    )(y_lru, y_att, x, gate1, scale2, shift2, g_norm2.reshape(1, d), w_out.astype(BF16),
      w_router.T)


def _route_kernel(aff_ref, pos_ref, cnt_ref, *, cap, n_tok_chunks):
    aff = aff_ref[0]
    n_e, s = aff.shape
    bits = lax.bitcast_convert_type(aff, jnp.int32)
    capf = float(cap)

    def count(mask):
        return jnp.sum(jnp.where(mask, 1.0, 0.0), axis=-1, keepdims=True)

    tau = jnp.zeros((n_e, 1), jnp.int32)
    for bit in range(30, -1, -1):
        cand = tau | (1 << bit)
        tau = jnp.where(count(bits >= cand) >= capf, cand, tau)
    gt = bits > tau
    eq = bits == tau
    need = capf - count(gt)

    blk = 256
    r = lax.broadcasted_iota(jnp.int32, (blk, blk), 0)
    cidx = lax.broadcasted_iota(jnp.int32, (blk, blk), 1)
    upper = jnp.where(r < cidx, 1.0, 0.0).astype(BF16)

    def prefix_blocks(mask):
        off = jnp.zeros((n_e, 1), F32)
        pieces, offs = [], []
        for k in range(s // blk):
            mb = jnp.where(mask[:, k * blk:(k + 1) * blk], 1.0, 0.0)
            offs.append(off)
            pieces.append(jnp.dot(mb.astype(BF16), upper, preferred_element_type=F32) + off)
            off = off + jnp.sum(mb, axis=-1, keepdims=True)
        return pieces, offs

    eq_rank, _ = prefix_blocks(eq)
    sel_blocks = []
    for k in range(s // blk):
        sl = slice(k * blk, (k + 1) * blk)
        sel_blocks.append(jnp.logical_or(gt[:, sl], jnp.logical_and(eq[:, sl], eq_rank[k] < need)))
    sel = jnp.concatenate(sel_blocks, axis=1)
    slot, offs = prefix_blocks(sel)
    for k in range(s // blk):
        pos_ref[0, :, k * blk:(k + 1) * blk] = jnp.where(sel_blocks[k], slot[k], -1.0).astype(jnp.int32)

    lane = lax.broadcasted_iota(jnp.int32, (n_e, 128), 1)
    cnt = jnp.zeros((n_e, 128), F32)
    per = (s // n_tok_chunks) // blk
    for j in range(n_tok_chunks):
        cnt = jnp.where(lane == j, offs[j * per], cnt)
    cnt_ref[0] = cnt.astype(jnp.int32)


def _routing(aff, cap):
    bsz, n_e, s = aff.shape
    return pl.pallas_call(
        functools.partial(_route_kernel, cap=cap, n_tok_chunks=N_TOK_CHUNKS),
        grid=(bsz,),
        in_specs=[pl.BlockSpec((1, n_e, s), lambda b: (b, 0, 0))],
        out_specs=[pl.BlockSpec((1, n_e, s), lambda b: (b, 0, 0)),
                   pl.BlockSpec((1, n_e, 128), lambda b: (b, 0, 0))],
        out_shape=[jax.ShapeDtypeStruct((bsz, n_e, s), jnp.int32),
                   jax.ShapeDtypeStruct((bsz, n_e, 128), jnp.int32)],
        compiler_params=_params(("parallel",)),
        name="expert_choice_routing",
    )(aff)


def _moe_kernel(cnt_ref, pos_ref, aff_ref, h2_ref, w1_ref, w3_ref, w2_ref, out_ref,
                xe_scr, g_scr, y_scr, *, cap):
    b = pl.program_id(0)
    e = pl.program_id(1)
    n_e = pl.num_programs(1)
    s = h2_ref.shape[1]
    tchunk = s // N_TOK_CHUNKS

    @pl.when(e == 0)
    def _():
        out_ref[...] = jnp.zeros_like(out_ref)

    xe_scr[...] = jnp.zeros_like(xe_scr)
    g_scr[...] = jnp.zeros_like(g_scr)
    base = (b * n_e + e) * N_TOK_CHUNKS

    def walk(cchunk, visit):
        n_cap = cap // cchunk
        slot_iota = lax.broadcasted_iota(jnp.int32, (cchunk, tchunk), 0)
        j = jnp.int32(0)
        i = jnp.int32(0)
        for _ in range(N_TOK_CHUNKS + n_cap - 1):
            jc = jnp.minimum(j, N_TOK_CHUNKS - 1)
            ic = jnp.minimum(i, n_cap - 1)
            c_lo = cnt_ref[base + jc]
            c_hi = jnp.where(jc + 1 < N_TOK_CHUNKS, cnt_ref[base + jnp.minimum(jc + 1, N_TOK_CHUNKS - 1)], cap)
            s_hi = (ic + 1) * cchunk
            valid = jnp.logical_and(jnp.maximum(c_lo, ic * cchunk) < jnp.minimum(c_hi, s_hi),
                                    jnp.logical_and(j < N_TOK_CHUNKS, i < n_cap))
            slot0 = jnp.where(valid, ic * cchunk, -2 * cap)
            sel = pos_ref[0, 0, pl.ds(jc, 1), :] == slot_iota + slot0
            visit(sel, jc, pl.ds(pl.multiple_of(ic * cchunk, cchunk), cchunk),
                  pl.ds(pl.multiple_of(jc * tchunk, tchunk), tchunk))
            j = j + jnp.where(c_hi <= s_hi, 1, 0)
            i = i + jnp.where(s_hi <= c_hi, 1, 0)

    def gather(sel, jc, rows, toks):
        xe_scr[rows, :] += jnp.dot(jnp.where(sel, 1.0, 0.0).astype(BF16), h2_ref[0, toks, :],
                                   preferred_element_type=F32)
        g_scr[rows, :] += jnp.sum(jnp.where(sel, aff_ref[0, 0, pl.ds(jc, 1), :], 0.0), axis=-1, keepdims=True)

    walk(cap // N_CAP_CHUNKS, gather)

    xe = xe_scr[...].astype(BF16)
    a = jnp.dot(xe, w1_ref[0], preferred_element_type=F32)
    gate = jnp.dot(xe, w3_ref[0], preferred_element_type=F32)
    hmid = ((a * _sigmoid(a)) * gate).astype(BF16)
    y = jnp.dot(hmid, w2_ref[0], preferred_element_type=F32) * g_scr[...]
    y_scr[...] = y.astype(BF16)

    def scatter(sel, jc, rows, toks):
        out_ref[0, toks, :] += lax.dot_general(jnp.where(sel, 1.0, 0.0).astype(BF16), y_scr[rows, :],
                                               TN_DIMS, preferred_element_type=F32)

    walk(cap // N_SCATTER_CHUNKS, scatter)


def _moe(cnt, pos, aff, h2, w1, w3, w2, cap):
    bsz, s, d = h2.shape
    n_e = w1.shape[0]
    f = w1.shape[2]
    tok_row = lambda b, e, c: (b, e, 0, 0)
    wspec = lambda b, e, c: (e, 0, 0)
    resident = lambda b, e, c: (b, 0, 0)
    grid_spec = pltpu.PrefetchScalarGridSpec(
        num_scalar_prefetch=1,
        grid=(bsz, n_e),
        in_specs=[pl.BlockSpec((1, 1, N_TOK_CHUNKS, s // N_TOK_CHUNKS), tok_row),
                  pl.BlockSpec((1, 1, N_TOK_CHUNKS, s // N_TOK_CHUNKS), tok_row),
                  pl.BlockSpec((1, s, d), resident, pipeline_mode=pl.Buffered(1)),
                  pl.BlockSpec((1, d, f), wspec),
                  pl.BlockSpec((1, d, f), wspec),
                  pl.BlockSpec((1, f, d), wspec)],
        out_specs=pl.BlockSpec((1, s, d), resident, pipeline_mode=pl.Buffered(1)),
        scratch_shapes=[pltpu.VMEM((cap, d), F32), pltpu.VMEM((cap, 1), F32), pltpu.VMEM((cap, d), BF16)],
    )
    return pl.pallas_call(
        functools.partial(_moe_kernel, cap=cap),
        grid_spec=grid_spec,
        out_shape=jax.ShapeDtypeStruct((bsz, s, d), F32),
        compiler_params=_params(("arbitrary", "arbitrary"), VMEM_LIMIT),
        name="expert_choice_ffn",
    )(cnt.reshape(-1), pos.reshape(bsz, n_e, N_TOK_CHUNKS, -1), aff.reshape(bsz, n_e, N_TOK_CHUNKS, -1), h2,
      w1.astype(BF16), w3.astype(BF16), w2.astype(BF16))


def _residual_kernel(x_ref, gate_ref, y_ref, o_ref):
    o_ref[0] = x_ref[0] + gate_ref[0] * y_ref[0]


def _gated_residual(x1, gate2, y, tm):
    bsz, s, d = x1.shape
    row = lambda b, i: (b, i, 0)
    return pl.pallas_call(
        _residual_kernel,
        grid=(bsz, s // tm),
        in_specs=[pl.BlockSpec((1, tm, d), row),
                  pl.BlockSpec((1, 1, d), lambda b, i: (b, 0, 0)),
                  pl.BlockSpec((1, tm, d), row)],
        out_specs=pl.BlockSpec((1, tm, d), row),
        out_shape=jax.ShapeDtypeStruct((bsz, s, d), F32),
        compiler_params=_params(("parallel", "parallel")),
        name="gated_residual",
    )(x1, gate2, y)


def kernel(x, c, w_mod, b_mod, g_norm1, w_in, conv_w, conv_b, lru_w_a, lru_b_a, lru_w_x, lru_b_x,
           lru_lambda, g_q, g_k, lambda_qk, g_attn_out, rel_bias, w_out, g_norm2, w_router, w1, w3, w2):
    bsz, s, d = x.shape
    depth = w_mod.shape[0]
    cap = max(1, EC_FACTOR * s // N_EXPERTS)
    tm = min(512, s)
    bias_tiles = _bias_tiles(rel_bias, min(ATT_TILE, s))
    for l in range(depth):
        mod = _modulation(c, w_mod[l], b_mod[l])
        shift1, scale1, gate1, shift2, scale2, gate2 = [m.reshape(bsz, 1, d) for m in jnp.split(mod, 6, axis=-1)]
        x_lru, gz, qn, kn, vt = _in_projection(x, scale1, shift1, g_norm1[l], w_in[l], g_q[l], g_k[l], tm)
        y_lru = _rg_lru(x_lru, gz, conv_w[l], conv_b[l], lru_w_a[l], lru_b_a[l], lru_w_x[l], lru_b_x[l],
                        lru_lambda[l])
        lam_init = 0.8 - 0.6 * math.exp(-0.3 * l)
        y_att = _diff_attention(qn, kn, vt, rel_bias, bias_tiles, lambda_qk[l], g_attn_out[l], lam_init)
        x1, h2, aff = _out_projection(y_lru, y_att, x, gate1, scale2, shift2, g_norm2[l], w_out[l],
                                      w_router[l], tm)
        pos, cnt = _routing(aff, cap)
        y = _moe(cnt[:, :, :N_TOK_CHUNKS], pos, aff, h2, w1[l], w3[l], w2[l], cap)
        x = _gated_residual(x1, gate2, y, tm)
    return x
```

```python
import functools
import math

import jax
import jax.numpy as jnp
from jax import lax
from jax.experimental import pallas as pl
from jax.experimental.pallas import tpu as pltpu

F32 = jnp.float32
BF16 = jnp.bfloat16
HIGHEST = lax.Precision.HIGHEST

D_MODEL = 1024
D_LRU = 512
LRU_BLOCK = 64
LRU_C = 8.0
CONV_W = 4
N_HEADS = 4
HEAD_DV = 128
HEAD_DK = 64
D_ATT = N_HEADS * HEAD_DV
N_BUCKETS = 32
N_EXPERTS = 16
EC_FACTOR = 2
EPS = 1e-6
LOG2E = math.log2(math.e)

LRU_HALF = 256
SCAN_LANES = 8
ATT_TILE = 256
N_TOK_CHUNKS = 8
N_CAP_CHUNKS = 4
N_SCATTER_CHUNKS = 2
VMEM_LIMIT = 56 * 1024 * 1024

NT_DIMS = (((1,), (1,)), ((), ()))
TN_DIMS = (((0,), (0,)), ((), ()))


def _sigmoid(x):
    return 1.0 / (1.0 + jnp.exp(-x))


def _params(sem, vmem=None, flags=None):
    return pltpu.CompilerParams(dimension_semantics=sem, vmem_limit_bytes=vmem, flags=flags)


def _mod_kernel(c_ref, w_ref, b_ref, o_ref):
    c = c_ref[...]
    o_ref[...] = jnp.dot(c * _sigmoid(c), w_ref[...], precision=HIGHEST,
                         preferred_element_type=F32) + b_ref[...]


def _modulation(c, w_mod, b_mod):
    bsz, d = c.shape
    n = w_mod.shape[1]
    return pl.pallas_call(
        _mod_kernel,
        grid=(n // d,),
        in_specs=[pl.BlockSpec((bsz, d), lambda j: (0, 0)),
                  pl.BlockSpec((d, d), lambda j: (0, j)),
                  pl.BlockSpec((1, d), lambda j: (0, j))],
        out_specs=pl.BlockSpec((bsz, d), lambda j: (0, j)),
        out_shape=jax.ShapeDtypeStruct((bsz, n), F32),
        compiler_params=_params(("arbitrary",)),
        name="adaln_mod",
    )(c, w_mod, b_mod.reshape(1, n))


def _bias_kernel(tab_ref, o_ref):
    h = pl.program_id(0)
    t = o_ref.shape[-1]
    key = lax.broadcasted_iota(jnp.int32, (t, t), 0)
    qry = lax.broadcasted_iota(jnp.int32, (t, t), 1)
    half = N_BUCKETS // 2
    max_exact = half // 2
    for d in range(5):
        if d == 0:
            o_ref[0, d] = jnp.full((t, t), tab_ref[half - 1, h] * LOG2E, F32)
        elif d == 4:
            o_ref[0, d] = jnp.full((t, t), tab_ref[N_BUCKETS - 1, h] * LOG2E, F32)
        else:
            rel = (d - 2) * t + key - qry
            n = jnp.abs(rel)
            n2 = n * n
            large = jnp.full((t, t), max_exact, jnp.int32)
            for k in range(1, half - max_exact):
                large = large + jnp.where(n2 >= (max_exact * max_exact) * (2 ** k), 1, 0)
            idx = jnp.where(n < max_exact, n, large) + jnp.where(rel > 0, half, 0)
            val = jnp.zeros((t, t), F32)
            for j in range(N_BUCKETS):
                val = jnp.where(idx == j, tab_ref[j, h] * LOG2E, val)
            o_ref[0, d] = val


def _bias_tiles(rel_bias, t):
    return pl.pallas_call(
        _bias_kernel,
        grid=(N_HEADS,),
        in_specs=[pl.BlockSpec(memory_space=pltpu.SMEM)],
        out_specs=pl.BlockSpec((1, 5, t, t), lambda h: (h, 0, 0, 0)),
        out_shape=jax.ShapeDtypeStruct((N_HEADS, 5, t, t), F32),
        compiler_params=_params(("arbitrary",)),
        name="t5_bias_tiles",
    )(rel_bias)


def _inproj_kernel(x_ref, sc_ref, sh_ref, g1_ref, w_ref, wvt_ref, mseg_ref, gq_ref, gk_ref,
                   xl_ref, gz_ref, q_ref, k_ref, vt_ref):
    x = x_ref[0]
    ms = jnp.mean(x * x, axis=-1, keepdims=True)
    h = (x * lax.rsqrt(ms + EPS) * g1_ref[...]) * (1.0 + sc_ref[0]) + sh_ref[0]
    hb = h.astype(BF16)

    def proj(lo, width):
        return jnp.dot(hb, w_ref[:, lo:lo + width], preferred_element_type=F32)

    def qk_norm(t, g):
        t2 = t * t
        hi = t2.astype(BF16)
        lo = (t2 - hi.astype(F32)).astype(BF16)
        ss = (jnp.dot(hi, mseg_ref[...], preferred_element_type=F32)
              + jnp.dot(lo, mseg_ref[...], preferred_element_type=F32))
        return t * lax.rsqrt(ss * (1.0 / HEAD_DK) + EPS) * g

    xl_ref[0] = proj(0, D_LRU)
    z = proj(D_LRU, D_LRU)
    cdf = 0.5 * (1.0 + jnp.tanh(math.sqrt(2.0 / math.pi) * (z + 0.044715 * (z * z * z))))
    gz_ref[0] = (z * cdf).astype(BF16)
    q_ref[0] = (qk_norm(proj(2 * D_LRU, D_ATT), gq_ref[...]) * (HEAD_DK ** -0.5 * LOG2E)).astype(BF16)
    k_ref[0] = qk_norm(proj(2 * D_LRU + D_ATT, D_ATT), gk_ref[...]).astype(BF16)
    vt_ref[0] = lax.dot_general(wvt_ref[...], hb, NT_DIMS,
                                preferred_element_type=F32).astype(BF16)


def _in_projection(x, scale1, shift1, g_norm1, w_in, g_q, g_k, tm):
    bsz, s, d = x.shape
    n = w_in.shape[1] - D_ATT
    w_main = w_in[:, :n].astype(BF16)
    w_vt = w_in[:, n:].T.astype(BF16)
    seg = jnp.arange(D_ATT, dtype=jnp.int32) // HEAD_DK
    mseg = (seg[:, None] == seg[None, :]).astype(BF16)
    n_sub = D_ATT // HEAD_DK
    row = lambda b, i: (b, i, 0)
    vec = lambda b, i: (b, 0, 0)
    full = lambda b, i: (0, 0)
    out_block = pl.BlockSpec((1, tm, D_LRU), row)
    return pl.pallas_call(
        _inproj_kernel,
        grid=(bsz, s // tm),
        in_specs=[pl.BlockSpec((1, tm, d), row),
                  pl.BlockSpec((1, 1, d), vec),
                  pl.BlockSpec((1, 1, d), vec),
                  pl.BlockSpec((1, d), full),
                  pl.BlockSpec((d, n), full),
                  pl.BlockSpec((D_ATT, d), full),
                  pl.BlockSpec((D_ATT, D_ATT), full),
                  pl.BlockSpec((1, D_ATT), full),
                  pl.BlockSpec((1, D_ATT), full)],
        out_specs=[out_block] * 4 + [pl.BlockSpec((1, D_ATT, tm), lambda b, i: (b, 0, i))],
        out_shape=[jax.ShapeDtypeStruct((bsz, s, D_LRU), F32)]
                  + [jax.ShapeDtypeStruct((bsz, s, D_LRU), BF16)] * 3
                  + [jax.ShapeDtypeStruct((bsz, D_ATT, s), BF16)],
        compiler_params=_params(("parallel", "parallel"), VMEM_LIMIT),
        name="norm1_in_proj",
    )(x, scale1, shift1, g_norm1.reshape(1, d), w_main, w_vt, mseg,
      jnp.tile(g_q, n_sub).reshape(1, D_ATT), jnp.tile(g_k, n_sub).reshape(1, D_ATT))


def _lru_kernel(x_ref, gz_ref, cw_ref, cb_ref, wa_ref, wx_ref, ba_ref, bx_ref, lam_ref, y_ref,
                xpad, a_f, u_f, a_b, u_b, *, tc):
    s = x_ref.shape[1]
    c = x_ref.shape[2]
    n_chunks = s // tc
    n_slab = c // 128
    pitch = a_f.shape[1] // n_chunks
    zeros8 = jnp.zeros((8, c), F32)
    xpad[0:8, :] = zeros8
    xpad[s + 8:s + 16, :] = zeros8

    def fill(ci, carry):
        t0 = pl.multiple_of(ci * tc, tc)
        xpad[pl.ds(t0 + 8, tc), :] = x_ref[0, pl.ds(t0, tc), :]
        return carry

    lax.fori_loop(0, n_chunks, fill, 0)

    cw = cw_ref[...]
    cb = cb_ref[...]
    decay = []
    for d in range(2):
        lam = lam_ref[d]
        softplus_neg = jnp.maximum(-lam, 0.0) + jnp.log(1.0 + jnp.exp(-jnp.abs(lam)))
        decay.append(-LRU_C * softplus_neg)
    a_scr = (a_f, a_b)
    u_scr = (u_f, u_b)

    def gates(ci, carry):
        t0 = pl.multiple_of(ci * tc, tc)
        xw = xpad[pl.ds(t0, tc + 16), :]
        xc = (cw[0:1] * pltpu.roll(xw, 2, 0)[8:8 + tc]
              + cw[1:2] * pltpu.roll(xw, 1, 0)[8:8 + tc]
              + cw[2:3] * xw[8:8 + tc]
              + cw[3:4] * pltpu.roll(xw, tc + 15, 0)[8:8 + tc]) + cb
        xcb = xc.astype(BF16)
        for d in range(2):
            r = _sigmoid(jnp.dot(xcb, wa_ref[d], preferred_element_type=F32) + ba_ref[d])
            i = _sigmoid(jnp.dot(xcb, wx_ref[d], preferred_element_type=F32) + bx_ref[d])
            a = jnp.exp(r * decay[d])
            u = jnp.sqrt(1.0 - a * a) * (i * xc)
            r0 = pl.multiple_of(ci * pitch, 8)
            for sl in range(n_slab):
                a_scr[d][sl, pl.ds(r0, tc), :] = a[:, sl * 128:(sl + 1) * 128]
                u_scr[d][sl, pl.ds(r0, tc), :] = u[:, sl * 128:(sl + 1) * 128]
        return carry

    lax.fori_loop(0, n_chunks, gates, 0)

    def rows(tt):
        return pl.ds(tt, n_chunks, stride=pitch)

    def step(tt, carry):
        out = []
        for d in range(2):
            t_loc = tt if d == 0 else tc - 1 - tt
            for sl in range(n_slab):
                h, p = carry[len(out)]
                a8 = a_scr[d][sl, rows(t_loc), :]
                h = a8 * h + u_scr[d][sl, rows(t_loc), :]
                p = a8 * p
                u_scr[d][sl, rows(t_loc), :] = h
                a_scr[d][sl, rows(t_loc), :] = p
                out.append((h, p))
        return tuple(out)

    init = (jnp.zeros((n_chunks, 128), F32), jnp.ones((n_chunks, 128), F32))
    lax.fori_loop(0, tc, step, (init,) * (2 * n_slab), unroll=4)

    for sl in range(n_slab):
        lanes = slice(sl * 128, (sl + 1) * 128)
        h_end, p_end = u_f[sl, rows(tc - 1), :], a_f[sl, rows(tc - 1), :]
        h_beg, p_beg = u_b[sl, rows(0), :], a_b[sl, rows(0), :]
        carry_f = [jnp.zeros((1, 128), F32)]
        for r in range(1, n_chunks):
            carry_f.append(p_end[r - 1:r] * carry_f[-1] + h_end[r - 1:r])
        carry_b = [jnp.zeros((1, 128), F32)]
        for r in range(n_chunks - 2, -1, -1):
            carry_b.insert(0, p_beg[r + 1:r + 2] * carry_b[0] + h_beg[r + 1:r + 2])
        for r in range(n_chunks):
            blk = slice(r * pitch, r * pitch + tc)
            hsum = (u_f[sl, blk, :] + a_f[sl, blk, :] * carry_f[r]
                    + u_b[sl, blk, :] + a_b[sl, blk, :] * carry_b[r])
            y_ref[0, r * tc:(r + 1) * tc, lanes] = (hsum * gz_ref[0, r * tc:(r + 1) * tc, lanes].astype(F32)).astype(BF16)


def _block_diag(w, half):
    n_dir, n_blocks, blk, _ = w.shape
    per = half // blk
    n_half = n_blocks // per
    w = w.reshape(n_dir, n_half, per, blk, blk)
    eye = jnp.eye(per, dtype=w.dtype)
    out = w[:, :, :, :, None, :] * eye[None, None, :, None, :, None]
    return out.reshape(n_dir, n_half, half, half)


def _rg_lru(x_lru, gz, conv_w, conv_b, w_a, b_a, w_x, b_x, lam):
    bsz, s, c = x_lru.shape
    half = LRU_HALF
    tc = s // SCAN_LANES
    pitch = tc + 8 if (tc // 8) % 2 == 0 else tc + 16
    n_half = c // half
    wa = _block_diag(w_a, half).astype(BF16)
    wx = _block_diag(w_x, half).astype(BF16)
    seq = lambda b, p: (b, 0, p)
    chan = lambda b, p: (0, p)
    dirchan = lambda b, p: (0, 0, p)
    blk = lambda b, p: (0, p, 0, 0)
    return pl.pallas_call(
        functools.partial(_lru_kernel, tc=tc),
        grid=(bsz, n_half),
        in_specs=[pl.BlockSpec((1, s, half), seq),
                  pl.BlockSpec((1, s, half), seq),
                  pl.BlockSpec((CONV_W, half), chan),
                  pl.BlockSpec((1, half), chan),
                  pl.BlockSpec((2, None, half, half), blk),
                  pl.BlockSpec((2, None, half, half), blk),
                  pl.BlockSpec((2, 1, half), dirchan),
                  pl.BlockSpec((2, 1, half), dirchan),
                  pl.BlockSpec((2, 1, half), dirchan)],
        out_specs=pl.BlockSpec((1, s, half), seq),
        out_shape=jax.ShapeDtypeStruct((bsz, s, c), BF16),
        scratch_shapes=[pltpu.VMEM((s + 16, half), F32)]
                       + [pltpu.VMEM((half // 128, SCAN_LANES * pitch, 128), F32)] * 4,
        compiler_params=_params(("parallel", "parallel"), VMEM_LIMIT),
        name="rg_lru",
    )(x_lru, gz, conv_w.reshape(CONV_W, c), conv_b.reshape(1, c), wa, wx,
      b_a.reshape(2, 1, c), b_x.reshape(2, 1, c), lam.reshape(2, 1, c))


def _attn_kernel(tab_ref, q_ref, k_ref, vt_ref, bias_ref, lq_ref, go_ref, o_ref,
                 s_even, s_odd, m_even, m_odd, p_scr, o_scr, *, lam_init, n_q):
    h = pl.program_id(1)
    i = pl.program_id(2)
    t = q_ref.shape[1]
    n_k = k_ref.shape[1] // t
    chunk = 4
    n_c = n_k // chunk
    tile_a = jnp.minimum(i, n_q - 1)
    tile_b = jnp.maximum(i - 1, 0)
    c_left = tab_ref[N_BUCKETS // 2 - 1, h] * LOG2E
    c_right = tab_ref[N_BUCKETS - 1, h] * LOG2E

    @pl.when(i == 0)
    def _():
        s_odd[...] = jnp.zeros_like(s_odd)
        m_odd[...] = jnp.zeros_like(m_odd)

    def fold(x, op):
        parts = [x[r * 8:(r + 1) * 8, :] for r in range(t // 8)]
        acc = parts[:2]
        for r in range(2, len(parts)):
            acc[r % 2] = op(acc[r % 2], parts[r])
        return op(acc[0], acc[1])

    def far_bias(j, tile):
        return jnp.where(j < tile, c_left, c_right)

    def is_far(j, tile):
        return jnp.abs(j - tile) >= 2

    def step(s_cur, m_cur, s_prev, m_prev):
        q = q_ref[0]
        lane = lax.broadcasted_iota(jnp.int32, q.shape, 1)
        zero = jnp.zeros_like(q)
        q_sub = (jnp.where(lane < HEAD_DK, q, zero), jnp.where(lane >= HEAD_DK, q, zero))
        m_fin = [m_prev[u] for u in range(2)]
        for dj in (-1, 0, 1):
            j = tile_b + dj
            valid = jnp.logical_and(j >= 0, j < n_k)
            jc = jnp.clip(j, 0, n_k - 1)
            bias = jnp.where(valid, bias_ref[0, dj + 2], 0.0)
            for u in range(2):
                sc = s_prev[u, jc] + bias
                s_prev[u, jc] = sc
                m_fin[u] = jnp.maximum(m_fin[u], fold(sc, jnp.maximum) + jnp.where(valid, 0.0, -jnp.inf))
        m_b = [jnp.max(m_fin[u], axis=0, keepdims=True) for u in range(2)]
        m_acc = [jnp.full((8, t), -jnp.inf, F32) for _ in range(2)]
        l_acc = [jnp.zeros((8, t), F32) for _ in range(2)]
        for c in range(n_c):
            rows = slice(c * chunk * t, (c + 1) * chunk * t)
            for u in range(2):
                sc_all = lax.dot_general(k_ref[0, rows, :], q_sub[u], NT_DIMS, preferred_element_type=F32)
                for jj in range(chunk):
                    j = c * chunk + jj
                    sc = sc_all[jj * t:(jj + 1) * t, :]
                    s_cur[u, j] = sc
                    m_acc[u] = jnp.maximum(m_acc[u], fold(sc, jnp.maximum)
                                           + jnp.where(is_far(j, tile_a), far_bias(j, tile_a), -jnp.inf))
            for u in range(2):
                for jj in range(chunk):
                    j = c * chunk + jj
                    off = m_b[u] - jnp.where(is_far(j, tile_b), far_bias(j, tile_b), 0.0)
                    p = jnp.exp2(s_prev[u, j] - off)
                    p_scr[u, j * t:(j + 1) * t, :] = p.astype(BF16)
                    l_acc[u] = l_acc[u] + fold(p, jnp.add)
                part = jnp.dot(vt_ref[0, :, rows], p_scr[u, rows, :], preferred_element_type=F32)
                if c == 0:
                    o_scr[u] = part
                else:
                    o_scr[u] += part

        for u in range(2):
            m_cur[u] = m_acc[u]

        outs = [o_scr[u] / jnp.sum(l_acc[u], axis=0, keepdims=True) for u in range(2)]
        lq = lq_ref[...]
        lam = (jnp.exp(jnp.sum(lq[0:1] * lq[1:2], axis=-1, keepdims=True))
               - jnp.exp(jnp.sum(lq[2:3] * lq[3:4], axis=-1, keepdims=True)) + lam_init)
        o = (outs[0] - lam * outs[1]).T
        ms = jnp.mean(o * o, axis=-1, keepdims=True)
        o_ref[0] = ((o * lax.rsqrt(ms + EPS) * go_ref[...]) * (1.0 - lam_init)).astype(BF16)

    @pl.when(i % 2 == 0)
    def _():
        step(s_even, m_even, s_odd, m_odd)

    @pl.when(i % 2 == 1)
    def _():
        step(s_odd, m_odd, s_even, m_even)


def _diff_attention(qn, kn, vt, rel_bias, bias_tiles, lambda_qk, g_o, lam_init):
    bsz, s, _ = qn.shape
    t = bias_tiles.shape[-1]
    n_k = s // t
    n_q = s // t
    scores = pltpu.VMEM((2, n_k, t, t), F32)
    maxima = pltpu.VMEM((2, 8, t), F32)
    return pl.pallas_call(
        functools.partial(_attn_kernel, lam_init=lam_init, n_q=n_q),
        grid=(bsz, N_HEADS, n_q + 1),
        in_specs=[pl.BlockSpec(memory_space=pltpu.SMEM),
                  pl.BlockSpec((1, t, HEAD_DV), lambda b, h, i: (b, jnp.minimum(i, n_q - 1), h)),
                  pl.BlockSpec((1, s, HEAD_DV), lambda b, h, i: (b, 0, h)),
                  pl.BlockSpec((1, HEAD_DV, s), lambda b, h, i: (b, h, 0)),
                  pl.BlockSpec((1, 5, t, t), lambda b, h, i: (h, 0, 0, 0)),
                  pl.BlockSpec((4, HEAD_DK), lambda b, h, i: (0, 0)),
                  pl.BlockSpec((1, HEAD_DV), lambda b, h, i: (0, 0))],
        out_specs=pl.BlockSpec((1, t, HEAD_DV), lambda b, h, i: (b, jnp.maximum(i - 1, 0), h)),
        out_shape=jax.ShapeDtypeStruct((bsz, s, D_ATT), BF16),
        scratch_shapes=[scores, scores, maxima, maxima, pltpu.VMEM((2, s, t), BF16),
                        pltpu.VMEM((2, HEAD_DV, t), F32)],
        compiler_params=_params(("parallel", "parallel", "arbitrary"), VMEM_LIMIT),
        name="diff_attention",
    )(rel_bias, qn, kn, vt, bias_tiles, lambda_qk, g_o.reshape(1, HEAD_DV))


def _outproj_kernel(yl_ref, ya_ref, x_ref, gate_ref, sc_ref, sh_ref, g2_ref, wo_ref, wr_ref,
                    x1_ref, h2_ref, aff_ref):
    mix = (jnp.dot(yl_ref[0], wo_ref[0:D_LRU, :], preferred_element_type=F32)
           + jnp.dot(ya_ref[0], wo_ref[D_LRU:D_LRU + D_ATT, :], preferred_element_type=F32))
    x1 = x_ref[0] + gate_ref[0] * mix
    x1_ref[0] = x1
    ms = jnp.mean(x1 * x1, axis=-1, keepdims=True)
    h2 = (x1 * lax.rsqrt(ms + EPS) * g2_ref[...]) * (1.0 + sc_ref[0]) + sh_ref[0]
    h2_ref[0] = h2.astype(BF16)
    logits = lax.dot_general(wr_ref[...], h2, NT_DIMS, precision=HIGHEST,
                             preferred_element_type=F32)
    ex = jnp.exp(logits - jnp.max(logits, axis=0, keepdims=True))
    aff_ref[0] = ex / jnp.sum(ex, axis=0, keepdims=True)


def _out_projection(y_lru, y_att, x, gate1, scale2, shift2, g_norm2, w_out, w_router, tm):
    bsz, s, d = x.shape
    row = lambda b, i: (b, i, 0)
    vec = lambda b, i: (b, 0, 0)
    full = lambda b, i: (0, 0)
    return pl.pallas_call(
        _outproj_kernel,
        grid=(bsz, s // tm),
        in_specs=[pl.BlockSpec((1, tm, D_LRU), row),
                  pl.BlockSpec((1, tm, D_ATT), row),
                  pl.BlockSpec((1, tm, d), row),
                  pl.BlockSpec((1, 1, d), vec),
                  pl.BlockSpec((1, 1, d), vec),
                  pl.BlockSpec((1, 1, d), vec),
                  pl.BlockSpec((1, d), full),
                  pl.BlockSpec((D_LRU + D_ATT, d), full),
                  pl.BlockSpec((N_EXPERTS, d), full)],
        out_specs=[pl.BlockSpec((1, tm, d), row),
                   pl.BlockSpec((1, tm, d), row),
                   pl.BlockSpec((1, N_EXPERTS, tm), lambda b, i: (b, 0, i))],
        out_shape=[jax.ShapeDtypeStruct((bsz, s, d), F32),
                   jax.ShapeDtypeStruct((bsz, s, d), BF16),
                   jax.ShapeDtypeStruct((bsz, N_EXPERTS, s), F32)],
        compiler_params=_params(("parallel", "parallel"), VMEM_LIMIT),
        name="out_proj_norm2_router",
    )(y_lru, y_att, x, gate1, scale2, shift2, g_norm2.reshape(1, d), w_out.astype(BF16),
      w_router.T)


def _route_kernel(aff_ref, pos_ref, cnt_ref, *, cap, n_tok_chunks):
    aff = aff_ref[0]
    n_e, s = aff.shape
    bits = lax.bitcast_convert_type(aff, jnp.int32)
    capf = float(cap)

    def count(mask):
        return jnp.sum(jnp.where(mask, 1.0, 0.0), axis=-1, keepdims=True)

    tau = jnp.zeros((n_e, 1), jnp.int32)
    for bit in range(30, -1, -1):
        cand = tau | (1 << bit)
        tau = jnp.where(count(bits >= cand) >= capf, cand, tau)
    gt = bits > tau
    eq = bits == tau
    need = capf - count(gt)

    blk = 256
    r = lax.broadcasted_iota(jnp.int32, (blk, blk), 0)
    cidx = lax.broadcasted_iota(jnp.int32, (blk, blk), 1)
    upper = jnp.where(r < cidx, 1.0, 0.0).astype(BF16)

    def prefix_blocks(mask):
        off = jnp.zeros((n_e, 1), F32)
        pieces, offs = [], []
        for k in range(s // blk):
            mb = jnp.where(mask[:, k * blk:(k + 1) * blk], 1.0, 0.0)
            offs.append(off)
            pieces.append(jnp.dot(mb.astype(BF16), upper, preferred_element_type=F32) + off)
            off = off + jnp.sum(mb, axis=-1, keepdims=True)
        return pieces, offs

    eq_rank, _ = prefix_blocks(eq)
    sel_blocks = []
    for k in range(s // blk):
        sl = slice(k * blk, (k + 1) * blk)
        sel_blocks.append(jnp.logical_or(gt[:, sl], jnp.logical_and(eq[:, sl], eq_rank[k] < need)))
    sel = jnp.concatenate(sel_blocks, axis=1)
    slot, offs = prefix_blocks(sel)
    for k in range(s // blk):
        pos_ref[0, :, k * blk:(k + 1) * blk] = jnp.where(sel_blocks[k], slot[k], -1.0).astype(jnp.int32)

    lane = lax.broadcasted_iota(jnp.int32, (n_e, 128), 1)
    cnt = jnp.zeros((n_e, 128), F32)
    per = (s // n_tok_chunks) // blk
    for j in range(n_tok_chunks):
        cnt = jnp.where(lane == j, offs[j * per], cnt)
    cnt_ref[0] = cnt.astype(jnp.int32)


def _routing(aff, cap):
    bsz, n_e, s = aff.shape
    return pl.pallas_call(
        functools.partial(_route_kernel, cap=cap, n_tok_chunks=N_TOK_CHUNKS),
        grid=(bsz,),
        in_specs=[pl.BlockSpec((1, n_e, s), lambda b: (b, 0, 0))],
        out_specs=[pl.BlockSpec((1, n_e, s), lambda b: (b, 0, 0)),
                   pl.BlockSpec((1, n_e, 128), lambda b: (b, 0, 0))],
        out_shape=[jax.ShapeDtypeStruct((bsz, n_e, s), jnp.int32),
                   jax.ShapeDtypeStruct((bsz, n_e, 128), jnp.int32)],
        compiler_params=_params(("parallel",)),
        name="expert_choice_routing",
    )(aff)


def _moe_kernel(cnt_ref, pos_ref, aff_ref, h2_ref, w1_ref, w3_ref, w2_ref, out_ref,
                xe_scr, g_scr, y_scr, *, cap):
    b = pl.program_id(0)
    e = pl.program_id(1)
    n_e = pl.num_programs(1)
    s = h2_ref.shape[1]
    tchunk = s // N_TOK_CHUNKS

    @pl.when(e == 0)
    def _():
        out_ref[...] = jnp.zeros_like(out_ref)

    xe_scr[...] = jnp.zeros_like(xe_scr)
    g_scr[...] = jnp.zeros_like(g_scr)
    base = (b * n_e + e) * N_TOK_CHUNKS

    def walk(cchunk, visit):
        n_cap = cap // cchunk
        slot_iota = lax.broadcasted_iota(jnp.int32, (cchunk, tchunk), 0)
        j = jnp.int32(0)
        i = jnp.int32(0)
        for _ in range(N_TOK_CHUNKS + n_cap - 1):
            jc = jnp.minimum(j, N_TOK_CHUNKS - 1)
            ic = jnp.minimum(i, n_cap - 1)
            c_lo = cnt_ref[base + jc]
            c_hi = jnp.where(jc + 1 < N_TOK_CHUNKS, cnt_ref[base + jnp.minimum(jc + 1, N_TOK_CHUNKS - 1)], cap)
            s_hi = (ic + 1) * cchunk
            valid = jnp.logical_and(jnp.maximum(c_lo, ic * cchunk) < jnp.minimum(c_hi, s_hi),
                                    jnp.logical_and(j < N_TOK_CHUNKS, i < n_cap))
            slot0 = jnp.where(valid, ic * cchunk, -2 * cap)
            sel = pos_ref[0, 0, pl.ds(jc, 1), :] == slot_iota + slot0
            visit(sel, jc, pl.ds(pl.multiple_of(ic * cchunk, cchunk), cchunk),
                  pl.ds(pl.multiple_of(jc * tchunk, tchunk), tchunk))
            j = j + jnp.where(c_hi <= s_hi, 1, 0)
            i = i + jnp.where(s_hi <= c_hi, 1, 0)

    def gather(sel, jc, rows, toks):
        xe_scr[rows, :] += jnp.dot(jnp.where(sel, 1.0, 0.0).astype(BF16), h2_ref[0, toks, :],
                                   preferred_element_type=F32)
        g_scr[rows, :] += jnp.sum(jnp.where(sel, aff_ref[0, 0, pl.ds(jc, 1), :], 0.0), axis=-1, keepdims=True)

    walk(cap // N_CAP_CHUNKS, gather)

    xe = xe_scr[...].astype(BF16)
    a = jnp.dot(xe, w1_ref[0], preferred_element_type=F32)
    gate = jnp.dot(xe, w3_ref[0], preferred_element_type=F32)
    hmid = ((a * _sigmoid(a)) * gate).astype(BF16)
    y = jnp.dot(hmid, w2_ref[0], preferred_element_type=F32) * g_scr[...]
    y_scr[...] = y.astype(BF16)

    def scatter(sel, jc, rows, toks):
        out_ref[0, toks, :] += lax.dot_general(jnp.where(sel, 1.0, 0.0).astype(BF16), y_scr[rows, :],
                                               TN_DIMS, preferred_element_type=F32)

    walk(cap // N_SCATTER_CHUNKS, scatter)


def _moe(cnt, pos, aff, h2, w1, w3, w2, cap):
    bsz, s, d = h2.shape
    n_e = w1.shape[0]
    f = w1.shape[2]
    tok_row = lambda b, e, c: (b, e, 0, 0)
    wspec = lambda b, e, c: (e, 0, 0)
    resident = lambda b, e, c: (b, 0, 0)
    grid_spec = pltpu.PrefetchScalarGridSpec(
        num_scalar_prefetch=1,
        grid=(bsz, n_e),
        in_specs=[pl.BlockSpec((1, 1, N_TOK_CHUNKS, s // N_TOK_CHUNKS), tok_row),
                  pl.BlockSpec((1, 1, N_TOK_CHUNKS, s // N_TOK_CHUNKS), tok_row),
                  pl.BlockSpec((1, s, d), resident, pipeline_mode=pl.Buffered(1)),
                  pl.BlockSpec((1, d, f), wspec),
                  pl.BlockSpec((1, d, f), wspec),
                  pl.BlockSpec((1, f, d), wspec)],
        out_specs=pl.BlockSpec((1, s, d), resident, pipeline_mode=pl.Buffered(1)),
        scratch_shapes=[pltpu.VMEM((cap, d), F32), pltpu.VMEM((cap, 1), F32), pltpu.VMEM((cap, d), BF16)],
    )
    return pl.pallas_call(
        functools.partial(_moe_kernel, cap=cap),
        grid_spec=grid_spec,
        out_shape=jax.ShapeDtypeStruct((bsz, s, d), F32),
        compiler_params=_params(("arbitrary", "arbitrary"), VMEM_LIMIT),
        name="expert_choice_ffn",
    )(cnt.reshape(-1), pos.reshape(bsz, n_e, N_TOK_CHUNKS, -1), aff.reshape(bsz, n_e, N_TOK_CHUNKS, -1), h2,
      w1.astype(BF16), w3.astype(BF16), w2.astype(BF16))


def _residual_kernel(x_ref, gate_ref, y_ref, o_ref):
    o_ref[0] = x_ref[0] + gate_ref[0] * y_ref[0]


def _gated_residual(x1, gate2, y, tm):
    bsz, s, d = x1.shape
    row = lambda b, i: (b, i, 0)
    return pl.pallas_call(
        _residual_kernel,
        grid=(bsz, s // tm),
        in_specs=[pl.BlockSpec((1, tm, d), row),
                  pl.BlockSpec((1, 1, d), lambda b, i: (b, 0, 0)),
                  pl.BlockSpec((1, tm, d), row)],
        out_specs=pl.BlockSpec((1, tm, d), row),
        out_shape=jax.ShapeDtypeStruct((bsz, s, d), F32),
        compiler_params=_params(("parallel", "parallel")),
        name="gated_residual",
    )(x1, gate2, y)


def kernel(x, c, w_mod, b_mod, g_norm1, w_in, conv_w, conv_b, lru_w_a, lru_b_a, lru_w_x, lru_b_x,
           lru_lambda, g_q, g_k, lambda_qk, g_attn_out, rel_bias, w_out, g_norm2, w_router, w1, w3, w2):
    bsz, s, d = x.shape
    depth = w_mod.shape[0]
    cap = max(1, EC_FACTOR * s // N_EXPERTS)
    tm = min(512, s)
    bias_tiles = _bias_tiles(rel_bias, min(ATT_TILE, s))
    for l in range(depth):
        mod = _modulation(c, w_mod[l], b_mod[l])
        shift1, scale1, gate1, shift2, scale2, gate2 = [m.reshape(bsz, 1, d) for m in jnp.split(mod, 6, axis=-1)]
        x_lru, gz, qn, kn, vt = _in_projection(x, scale1, shift1, g_norm1[l], w_in[l], g_q[l], g_k[l], tm)
        y_lru = _rg_lru(x_lru, gz, conv_w[l], conv_b[l], lru_w_a[l], lru_b_a[l], lru_w_x[l], lru_b_x[l],
                        lru_lambda[l])
        lam_init = 0.8 - 0.6 * math.exp(-0.3 * l)
        y_att = _diff_attention(qn, kn, vt, rel_bias, bias_tiles, lambda_qk[l], g_attn_out[l], lam_init)
        x1, h2, aff = _out_projection(y_lru, y_att, x, gate1, scale2, shift2, g_norm2[l], w_out[l],
                                      w_router[l], tm)
        pos, cnt = _routing(aff, cap)
        y = _moe(cnt[:, :, :N_TOK_CHUNKS], pos, aff, h2, w1[l], w3[l], w2[l], cap)
        x = _gated_residual(x1, gate2, y, tm)
    return x
```

```python
import functools
import math

import jax
import jax.numpy as jnp
from jax import lax
from jax.experimental import pallas as pl
from jax.experimental.pallas import tpu as pltpu

F32 = jnp.float32
BF16 = jnp.bfloat16

D_MODEL = 1024
D_LRU = 512
LRU_BLOCK = 64
LRU_C = 8.0
CONV_W = 4
N_HEADS = 4
HEAD_DV = 128
HEAD_DK = 64
D_ATT = N_HEADS * HEAD_DV
N_BUCKETS = 32
N_EXPERTS = 16
EC_FACTOR = 2
EPS = 1e-6
LOG2E = math.log2(math.e)

LRU_HALF = 256
SCAN_LANES = 8
ATT_TILE = 256
N_TOK_CHUNKS = 8
N_CAP_CHUNKS = 4
N_SCATTER_CHUNKS = 2
VMEM_LIMIT = 56 * 1024 * 1024

NT_DIMS = (((1,), (1,)), ((), ()))
TN_DIMS = (((0,), (0,)), ((), ()))


def _sigmoid(x):
    return 1.0 / (1.0 + jnp.exp(-x))


def _params(sem, vmem=None, flags=None):
    return pltpu.CompilerParams(dimension_semantics=sem, vmem_limit_bytes=vmem, flags=flags)


def _mod_kernel(c_ref, w_ref, b_ref, o_ref):
    c = c_ref[...]
    o_ref[...] = jnp.dot((c * _sigmoid(c)).astype(BF16), w_ref[...].astype(BF16),
                         preferred_element_type=F32) + b_ref[...]


def _modulation(c, w_mod, b_mod):
    bsz, d = c.shape
    n = w_mod.shape[1]
    return pl.pallas_call(
        _mod_kernel,
        grid=(n // d,),
        in_specs=[pl.BlockSpec((bsz, d), lambda j: (0, 0)),
                  pl.BlockSpec((d, d), lambda j: (0, j)),
                  pl.BlockSpec((1, d), lambda j: (0, j))],
        out_specs=pl.BlockSpec((bsz, d), lambda j: (0, j)),
        out_shape=jax.ShapeDtypeStruct((bsz, n), F32),
        compiler_params=_params(("arbitrary",)),
        name="adaln_mod",
    )(c, w_mod, b_mod.reshape(1, n))


def _bias_kernel(tab_ref, o_ref):
    h = pl.program_id(0)
    t = o_ref.shape[-1]
    key = lax.broadcasted_iota(jnp.int32, (t, t), 0)
    qry = lax.broadcasted_iota(jnp.int32, (t, t), 1)
    half = N_BUCKETS // 2
    max_exact = half // 2
    for d in range(5):
        if d == 0:
            o_ref[0, d] = jnp.full((t, t), tab_ref[half - 1, h] * LOG2E, F32)
        elif d == 4:
            o_ref[0, d] = jnp.full((t, t), tab_ref[N_BUCKETS - 1, h] * LOG2E, F32)
        else:
            rel = (d - 2) * t + key - qry
            n = jnp.abs(rel)
            n2 = n * n
            large = jnp.full((t, t), max_exact, jnp.int32)
            for k in range(1, half - max_exact):
                large = large + jnp.where(n2 >= (max_exact * max_exact) * (2 ** k), 1, 0)
            idx = jnp.where(n < max_exact, n, large) + jnp.where(rel > 0, half, 0)
            val = jnp.zeros((t, t), F32)
            for j in range(N_BUCKETS):
                val = jnp.where(idx == j, tab_ref[j, h] * LOG2E, val)
            o_ref[0, d] = val


def _bias_tiles(rel_bias, t):
    return pl.pallas_call(
        _bias_kernel,
        grid=(N_HEADS,),
        in_specs=[pl.BlockSpec(memory_space=pltpu.SMEM)],
        out_specs=pl.BlockSpec((1, 5, t, t), lambda h: (h, 0, 0, 0)),
        out_shape=jax.ShapeDtypeStruct((N_HEADS, 5, t, t), F32),
        compiler_params=_params(("arbitrary",)),
        name="t5_bias_tiles",
    )(rel_bias)


def _inproj_kernel(x_ref, sc_ref, sh_ref, g1_ref, w_ref, wvt_ref, mseg_ref, gq_ref, gk_ref,
                   xl_ref, gz_ref, q_ref, k_ref, vt_ref):
    x = x_ref[0]
    ms = jnp.mean(x * x, axis=-1, keepdims=True)
    h = (x * lax.rsqrt(ms + EPS) * g1_ref[...]) * (1.0 + sc_ref[0]) + sh_ref[0]
    hb = h.astype(BF16)

    def proj(lo, width):
        return jnp.dot(hb, w_ref[:, lo:lo + width], preferred_element_type=F32)

    def qk_norm(t, g):
        t2 = t * t
        hi = t2.astype(BF16)
        lo = (t2 - hi.astype(F32)).astype(BF16)
        ss = (jnp.dot(hi, mseg_ref[...], preferred_element_type=F32)
              + jnp.dot(lo, mseg_ref[...], preferred_element_type=F32))
        return t * lax.rsqrt(ss * (1.0 / HEAD_DK) + EPS) * g

    xl_ref[0] = proj(0, D_LRU)
    z = proj(D_LRU, D_LRU)
    cdf = 0.5 * (1.0 + jnp.tanh(math.sqrt(2.0 / math.pi) * (z + 0.044715 * (z * z * z))))
    gz_ref[0] = (z * cdf).astype(BF16)
    q_ref[0] = (qk_norm(proj(2 * D_LRU, D_ATT), gq_ref[...]) * (HEAD_DK ** -0.5 * LOG2E)).astype(BF16)
    k_ref[0] = qk_norm(proj(2 * D_LRU + D_ATT, D_ATT), gk_ref[...]).astype(BF16)
    vt_ref[0] = lax.dot_general(wvt_ref[...], hb, NT_DIMS,
                                preferred_element_type=F32).astype(BF16)


def _in_projection(x, scale1, shift1, g_norm1, w_in, g_q, g_k, tm):
    bsz, s, d = x.shape
    n = w_in.shape[1] - D_ATT
    w_main = w_in[:, :n].astype(BF16)
    w_vt = w_in[:, n:].T.astype(BF16)
    seg = jnp.arange(D_ATT, dtype=jnp.int32) // HEAD_DK
    mseg = (seg[:, None] == seg[None, :]).astype(BF16)
    n_sub = D_ATT // HEAD_DK
    row = lambda b, i: (b, i, 0)
    vec = lambda b, i: (b, 0, 0)
    full = lambda b, i: (0, 0)
    out_block = pl.BlockSpec((1, tm, D_LRU), row)
    return pl.pallas_call(
        _inproj_kernel,
        grid=(bsz, s // tm),
        in_specs=[pl.BlockSpec((1, tm, d), row),
                  pl.BlockSpec((1, 1, d), vec),
                  pl.BlockSpec((1, 1, d), vec),
                  pl.BlockSpec((1, d), full),
                  pl.BlockSpec((d, n), full),
                  pl.BlockSpec((D_ATT, d), full),
                  pl.BlockSpec((D_ATT, D_ATT), full),
                  pl.BlockSpec((1, D_ATT), full),
                  pl.BlockSpec((1, D_ATT), full)],
        out_specs=[out_block] * 4 + [pl.BlockSpec((1, D_ATT, tm), lambda b, i: (b, 0, i))],
        out_shape=[jax.ShapeDtypeStruct((bsz, s, D_LRU), F32)]
                  + [jax.ShapeDtypeStruct((bsz, s, D_LRU), BF16)] * 3
                  + [jax.ShapeDtypeStruct((bsz, D_ATT, s), BF16)],
        compiler_params=_params(("parallel", "parallel"), VMEM_LIMIT),
        name="norm1_in_proj",
    )(x, scale1, shift1, g_norm1.reshape(1, d), w_main, w_vt, mseg,
      jnp.tile(g_q, n_sub).reshape(1, D_ATT), jnp.tile(g_k, n_sub).reshape(1, D_ATT))


def _lru_kernel(x_ref, gz_ref, cw_ref, cb_ref, wa_ref, wx_ref, ba_ref, bx_ref, lam_ref, y_ref,
                xpad, a_f, u_f, a_b, u_b, *, tc):
    s = x_ref.shape[1]
    c = x_ref.shape[2]
    n_chunks = s // tc
    n_slab = c // 128
    pitch = a_f.shape[1] // n_chunks
    zeros8 = jnp.zeros((8, c), F32)
    xpad[0:8, :] = zeros8
    xpad[s + 8:s + 16, :] = zeros8

    def fill(ci, carry):
        t0 = pl.multiple_of(ci * tc, tc)
        xpad[pl.ds(t0 + 8, tc), :] = x_ref[0, pl.ds(t0, tc), :]
        return carry

    lax.fori_loop(0, n_chunks, fill, 0)

    cw = cw_ref[...]
    cb = cb_ref[...]
    decay = []
    for d in range(2):
        lam = lam_ref[d]
        softplus_neg = jnp.maximum(-lam, 0.0) + jnp.log(1.0 + jnp.exp(-jnp.abs(lam)))
        decay.append(-LRU_C * softplus_neg)
    a_scr = (a_f, a_b)
    u_scr = (u_f, u_b)

    def gates(ci, carry):
        t0 = pl.multiple_of(ci * tc, tc)
        xw = xpad[pl.ds(t0, tc + 16), :]
        xc = (cw[0:1] * pltpu.roll(xw, 2, 0)[8:8 + tc]
              + cw[1:2] * pltpu.roll(xw, 1, 0)[8:8 + tc]
              + cw[2:3] * xw[8:8 + tc]
              + cw[3:4] * pltpu.roll(xw, tc + 15, 0)[8:8 + tc]) + cb
        xcb = xc.astype(BF16)
        for d in range(2):
            r = _sigmoid(jnp.dot(xcb, wa_ref[d], preferred_element_type=F32) + ba_ref[d])
            i = _sigmoid(jnp.dot(xcb, wx_ref[d], preferred_element_type=F32) + bx_ref[d])
            a = jnp.exp(r * decay[d])
            u = jnp.sqrt(1.0 - a * a) * (i * xc)
            r0 = pl.multiple_of(ci * pitch, 8)
            for sl in range(n_slab):
                a_scr[d][sl, pl.ds(r0, tc), :] = a[:, sl * 128:(sl + 1) * 128]
                u_scr[d][sl, pl.ds(r0, tc), :] = u[:, sl * 128:(sl + 1) * 128]
        return carry

    lax.fori_loop(0, n_chunks, gates, 0)

    def rows(tt):
        return pl.ds(tt, n_chunks, stride=pitch)

    def step(tt, carry):
        out = []
        for d in range(2):
            t_loc = tt if d == 0 else tc - 1 - tt
            for sl in range(n_slab):
                h, p = carry[len(out)]
                a8 = a_scr[d][sl, rows(t_loc), :]
                h = a8 * h + u_scr[d][sl, rows(t_loc), :]
                p = a8 * p
                u_scr[d][sl, rows(t_loc), :] = h
                a_scr[d][sl, rows(t_loc), :] = p
                out.append((h, p))
        return tuple(out)

    init = (jnp.zeros((n_chunks, 128), F32), jnp.ones((n_chunks, 128), F32))
    lax.fori_loop(0, tc, step, (init,) * (2 * n_slab), unroll=4)

    for sl in range(n_slab):
        lanes = slice(sl * 128, (sl + 1) * 128)
        h_end, p_end = u_f[sl, rows(tc - 1), :], a_f[sl, rows(tc - 1), :]
        h_beg, p_beg = u_b[sl, rows(0), :], a_b[sl, rows(0), :]
        carry_f = [jnp.zeros((1, 128), F32)]
        for r in range(1, n_chunks):
            carry_f.append(p_end[r - 1:r] * carry_f[-1] + h_end[r - 1:r])
        carry_b = [jnp.zeros((1, 128), F32)]
        for r in range(n_chunks - 2, -1, -1):
            carry_b.insert(0, p_beg[r + 1:r + 2] * carry_b[0] + h_beg[r + 1:r + 2])
        for r in range(n_chunks):
            blk = slice(r * pitch, r * pitch + tc)
            hsum = (u_f[sl, blk, :] + a_f[sl, blk, :] * carry_f[r]
                    + u_b[sl, blk, :] + a_b[sl, blk, :] * carry_b[r])
            y_ref[0, r * tc:(r + 1) * tc, lanes] = (hsum * gz_ref[0, r * tc:(r + 1) * tc, lanes].astype(F32)).astype(BF16)


def _block_diag(w, half):
    n_dir, n_blocks, blk, _ = w.shape
    per = half // blk
    n_half = n_blocks // per
    w = w.reshape(n_dir, n_half, per, blk, blk)
    eye = jnp.eye(per, dtype=w.dtype)
    out = w[:, :, :, :, None, :] * eye[None, None, :, None, :, None]
    return out.reshape(n_dir, n_half, half, half)


def _rg_lru(x_lru, gz, conv_w, conv_b, w_a, b_a, w_x, b_x, lam):
    bsz, s, c = x_lru.shape
    half = LRU_HALF
    tc = s // SCAN_LANES
    pitch = tc + 8 if (tc // 8) % 2 == 0 else tc + 16
    n_half = c // half
    wa = _block_diag(w_a, half).astype(BF16)
    wx = _block_diag(w_x, half).astype(BF16)
    seq = lambda b, p: (b, 0, p)
    chan = lambda b, p: (0, p)
    dirchan = lambda b, p: (0, 0, p)
    blk = lambda b, p: (0, p, 0, 0)
    return pl.pallas_call(
        functools.partial(_lru_kernel, tc=tc),
        grid=(bsz, n_half),
        in_specs=[pl.BlockSpec((1, s, half), seq),
                  pl.BlockSpec((1, s, half), seq),
                  pl.BlockSpec((CONV_W, half), chan),
                  pl.BlockSpec((1, half), chan),
                  pl.BlockSpec((2, None, half, half), blk),
                  pl.BlockSpec((2, None, half, half), blk),
                  pl.BlockSpec((2, 1, half), dirchan),
                  pl.BlockSpec((2, 1, half), dirchan),
                  pl.BlockSpec((2, 1, half), dirchan)],
        out_specs=pl.BlockSpec((1, s, half), seq),
        out_shape=jax.ShapeDtypeStruct((bsz, s, c), BF16),
        scratch_shapes=[pltpu.VMEM((s + 16, half), F32)]
                       + [pltpu.VMEM((half // 128, SCAN_LANES * pitch, 128), F32)] * 4,
        compiler_params=_params(("parallel", "parallel"), VMEM_LIMIT),
        name="rg_lru",
    )(x_lru, gz, conv_w.reshape(CONV_W, c), conv_b.reshape(1, c), wa, wx,
      b_a.reshape(2, 1, c), b_x.reshape(2, 1, c), lam.reshape(2, 1, c))


def _attn_kernel(tab_ref, q_ref, k_ref, vt_ref, bias_ref, lq_ref, go_ref, o_ref,
                 s_even, s_odd, m_even, m_odd, p_scr, o_scr, *, lam_init, n_q):
    h = pl.program_id(1)
    i = pl.program_id(2)
    t = q_ref.shape[1]
    n_k = k_ref.shape[1] // t
    chunk = 4
    n_c = n_k // chunk
    tile_a = jnp.minimum(i, n_q - 1)
    tile_b = jnp.maximum(i - 1, 0)
    c_left = tab_ref[N_BUCKETS // 2 - 1, h] * LOG2E
    c_right = tab_ref[N_BUCKETS - 1, h] * LOG2E

    @pl.when(i == 0)
    def _():
        s_odd[...] = jnp.zeros_like(s_odd)
        m_odd[...] = jnp.zeros_like(m_odd)

    def fold(x, op):
        parts = [x[r * 8:(r + 1) * 8, :] for r in range(t // 8)]
        acc = parts[:2]
        for r in range(2, len(parts)):
            acc[r % 2] = op(acc[r % 2], parts[r])
        return op(acc[0], acc[1])

    def far_bias(j, tile):
        return jnp.where(j < tile, c_left, c_right)

    def is_far(j, tile):
        return jnp.abs(j - tile) >= 2

    def step(s_cur, m_cur, s_prev, m_prev):
        q = q_ref[0]
        lane = lax.broadcasted_iota(jnp.int32, q.shape, 1)
        zero = jnp.zeros_like(q)
        q_sub = (jnp.where(lane < HEAD_DK, q, zero), jnp.where(lane >= HEAD_DK, q, zero))
        m_fin = [m_prev[u] for u in range(2)]
        for dj in (-1, 0, 1):
            j = tile_b + dj
            valid = jnp.logical_and(j >= 0, j < n_k)
            jc = jnp.clip(j, 0, n_k - 1)
            bias = jnp.where(valid, bias_ref[0, dj + 2], 0.0)
            for u in range(2):
                sc = s_prev[u, jc] + bias
                s_prev[u, jc] = sc
                m_fin[u] = jnp.maximum(m_fin[u], fold(sc, jnp.maximum) + jnp.where(valid, 0.0, -jnp.inf))
        m_b = [jnp.max(m_fin[u], axis=0, keepdims=True) for u in range(2)]
        m_acc = [jnp.full((8, t), -jnp.inf, F32) for _ in range(2)]
        l_acc = [jnp.zeros((8, t), F32) for _ in range(2)]
        for c in range(n_c):
            rows = slice(c * chunk * t, (c + 1) * chunk * t)
            for u in range(2):
                sc_all = lax.dot_general(k_ref[0, rows, :], q_sub[u], NT_DIMS, preferred_element_type=F32)
                for jj in range(chunk):
                    j = c * chunk + jj
                    sc = sc_all[jj * t:(jj + 1) * t, :]
                    s_cur[u, j] = sc
                    m_acc[u] = jnp.maximum(m_acc[u], fold(sc, jnp.maximum)
                                           + jnp.where(is_far(j, tile_a), far_bias(j, tile_a), -jnp.inf))
            for u in range(2):
                for jj in range(chunk):
                    j = c * chunk + jj
                    off = m_b[u] - jnp.where(is_far(j, tile_b), far_bias(j, tile_b), 0.0)
                    p = jnp.exp2(s_prev[u, j] - off)
                    p_scr[u, j * t:(j + 1) * t, :] = p.astype(BF16)
                    l_acc[u] = l_acc[u] + fold(p, jnp.add)
                part = jnp.dot(vt_ref[0, :, rows], p_scr[u, rows, :], preferred_element_type=F32)
                if c == 0:
                    o_scr[u] = part
                else:
                    o_scr[u] += part

        for u in range(2):
            m_cur[u] = m_acc[u]

        outs = [o_scr[u] / jnp.sum(l_acc[u], axis=0, keepdims=True) for u in range(2)]
        lq = lq_ref[...]
        lam = (jnp.exp(jnp.sum(lq[0:1] * lq[1:2], axis=-1, keepdims=True))
               - jnp.exp(jnp.sum(lq[2:3] * lq[3:4], axis=-1, keepdims=True)) + lam_init)
        o = (outs[0] - lam * outs[1]).T
        ms = jnp.mean(o * o, axis=-1, keepdims=True)
        o_ref[0] = ((o * lax.rsqrt(ms + EPS) * go_ref[...]) * (1.0 - lam_init)).astype(BF16)

    @pl.when(i % 2 == 0)
    def _():
        step(s_even, m_even, s_odd, m_odd)

    @pl.when(i % 2 == 1)
    def _():
        step(s_odd, m_odd, s_even, m_even)


def _diff_attention(qn, kn, vt, rel_bias, bias_tiles, lambda_qk, g_o, lam_init):
    bsz, s, _ = qn.shape
    t = bias_tiles.shape[-1]
    n_k = s // t
    n_q = s // t
    scores = pltpu.VMEM((2, n_k, t, t), F32)
    maxima = pltpu.VMEM((2, 8, t), F32)
    return pl.pallas_call(
        functools.partial(_attn_kernel, lam_init=lam_init, n_q=n_q),
        grid=(bsz, N_HEADS, n_q + 1),
        in_specs=[pl.BlockSpec(memory_space=pltpu.SMEM),
                  pl.BlockSpec((1, t, HEAD_DV), lambda b, h, i: (b, jnp.minimum(i, n_q - 1), h)),
                  pl.BlockSpec((1, s, HEAD_DV), lambda b, h, i: (b, 0, h)),
                  pl.BlockSpec((1, HEAD_DV, s), lambda b, h, i: (b, h, 0)),
                  pl.BlockSpec((1, 5, t, t), lambda b, h, i: (h, 0, 0, 0)),
                  pl.BlockSpec((4, HEAD_DK), lambda b, h, i: (0, 0)),
                  pl.BlockSpec((1, HEAD_DV), lambda b, h, i: (0, 0))],
        out_specs=pl.BlockSpec((1, t, HEAD_DV), lambda b, h, i: (b, jnp.maximum(i - 1, 0), h)),
        out_shape=jax.ShapeDtypeStruct((bsz, s, D_ATT), BF16),
        scratch_shapes=[scores, scores, maxima, maxima, pltpu.VMEM((2, s, t), BF16),
                        pltpu.VMEM((2, HEAD_DV, t), F32)],
        compiler_params=_params(("parallel", "parallel", "arbitrary"), VMEM_LIMIT),
        name="diff_attention",
    )(rel_bias, qn, kn, vt, bias_tiles, lambda_qk, g_o.reshape(1, HEAD_DV))


def _outproj_kernel(yl_ref, ya_ref, x_ref, gate_ref, sc_ref, sh_ref, g2_ref, wo_ref, wr_ref,
                    x1_ref, h2_ref, aff_ref):
    mix = (jnp.dot(yl_ref[0], wo_ref[0:D_LRU, :], preferred_element_type=F32)
           + jnp.dot(ya_ref[0], wo_ref[D_LRU:D_LRU + D_ATT, :], preferred_element_type=F32))
    x1 = x_ref[0] + gate_ref[0] * mix
    x1_ref[0] = x1
    ms = jnp.mean(x1 * x1, axis=-1, keepdims=True)
    h2 = (x1 * lax.rsqrt(ms + EPS) * g2_ref[...]) * (1.0 + sc_ref[0]) + sh_ref[0]
    h2b = h2.astype(BF16)
    h2_ref[0] = h2b
    logits = lax.dot_general(wr_ref[...], h2b, NT_DIMS, preferred_element_type=F32)
    ex = jnp.exp(logits - jnp.max(logits, axis=0, keepdims=True))
    aff_ref[0] = ex / jnp.sum(ex, axis=0, keepdims=True)


def _out_projection(y_lru, y_att, x, gate1, scale2, shift2, g_norm2, w_out, w_router, tm):
    bsz, s, d = x.shape
    row = lambda b, i: (b, i, 0)
    vec = lambda b, i: (b, 0, 0)
    full = lambda b, i: (0, 0)
    return pl.pallas_call(
        _outproj_kernel,
        grid=(bsz, s // tm),
        in_specs=[pl.BlockSpec((1, tm, D_LRU), row),
                  pl.BlockSpec((1, tm, D_ATT), row),
                  pl.BlockSpec((1, tm, d), row),
                  pl.BlockSpec((1, 1, d), vec),
                  pl.BlockSpec((1, 1, d), vec),
                  pl.BlockSpec((1, 1, d), vec),
                  pl.BlockSpec((1, d), full),
                  pl.BlockSpec((D_LRU + D_ATT, d), full),
                  pl.BlockSpec((N_EXPERTS, d), full)],
        out_specs=[pl.BlockSpec((1, tm, d), row),
                   pl.BlockSpec((1, tm, d), row),
                   pl.BlockSpec((1, N_EXPERTS, tm), lambda b, i: (b, 0, i))],
        out_shape=[jax.ShapeDtypeStruct((bsz, s, d), F32),
                   jax.ShapeDtypeStruct((bsz, s, d), BF16),
                   jax.ShapeDtypeStruct((bsz, N_EXPERTS, s), F32)],
        compiler_params=_params(("parallel", "parallel"), VMEM_LIMIT),
        name="out_proj_norm2_router",
    )(y_lru, y_att, x, gate1, scale2, shift2, g_norm2.reshape(1, d), w_out.astype(BF16),
      w_router.T.astype(BF16))


def _route_kernel(aff_ref, pos_ref, cnt_ref, *, cap, n_tok_chunks):
    aff = aff_ref[0]
    n_e, s = aff.shape
    bits = lax.bitcast_convert_type(aff, jnp.int32)
    capf = float(cap)

    def count(mask):
        return jnp.sum(jnp.where(mask, 1.0, 0.0), axis=-1, keepdims=True)

    tau = jnp.zeros((n_e, 1), jnp.int32)
    for bit in range(30, -1, -1):
        cand = tau | (1 << bit)
        tau = jnp.where(count(bits >= cand) >= capf, cand, tau)
    gt = bits > tau
    eq = bits == tau
    need = capf - count(gt)

    blk = 256
    r = lax.broadcasted_iota(jnp.int32, (blk, blk), 0)
    cidx = lax.broadcasted_iota(jnp.int32, (blk, blk), 1)
    upper = jnp.where(r < cidx, 1.0, 0.0).astype(BF16)

    def prefix_blocks(mask):
        off = jnp.zeros((n_e, 1), F32)
        pieces, offs = [], []
        for k in range(s // blk):
            mb = jnp.where(mask[:, k * blk:(k + 1) * blk], 1.0, 0.0)
            offs.append(off)
            pieces.append(jnp.dot(mb.astype(BF16), upper, preferred_element_type=F32) + off)
            off = off + jnp.sum(mb, axis=-1, keepdims=True)
        return pieces, offs

    eq_rank, _ = prefix_blocks(eq)
    sel_blocks = []
    for k in range(s // blk):
        sl = slice(k * blk, (k + 1) * blk)
        sel_blocks.append(jnp.logical_or(gt[:, sl], jnp.logical_and(eq[:, sl], eq_rank[k] < need)))
    sel = jnp.concatenate(sel_blocks, axis=1)
    slot, offs = prefix_blocks(sel)
    for k in range(s // blk):
        pos_ref[0, :, k * blk:(k + 1) * blk] = jnp.where(sel_blocks[k], slot[k], -1.0).astype(jnp.int32)

    lane = lax.broadcasted_iota(jnp.int32, (n_e, 128), 1)
    cnt = jnp.zeros((n_e, 128), F32)
    per = (s // n_tok_chunks) // blk
    for j in range(n_tok_chunks):
        cnt = jnp.where(lane == j, offs[j * per], cnt)
    cnt_ref[0] = cnt.astype(jnp.int32)


def _routing(aff, cap):
    bsz, n_e, s = aff.shape
    return pl.pallas_call(
        functools.partial(_route_kernel, cap=cap, n_tok_chunks=N_TOK_CHUNKS),
        grid=(bsz,),
        in_specs=[pl.BlockSpec((1, n_e, s), lambda b: (b, 0, 0))],
        out_specs=[pl.BlockSpec((1, n_e, s), lambda b: (b, 0, 0)),
                   pl.BlockSpec((1, n_e, 128), lambda b: (b, 0, 0))],
        out_shape=[jax.ShapeDtypeStruct((bsz, n_e, s), jnp.int32),
                   jax.ShapeDtypeStruct((bsz, n_e, 128), jnp.int32)],
        compiler_params=_params(("parallel",)),
        name="expert_choice_routing",
    )(aff)


def _moe_kernel(cnt_ref, pos_ref, aff_ref, h2_ref, w1_ref, w3_ref, w2_ref, gate_ref, x1_hbm, out_ref,
                xe_scr, g_scr, y_scr, x1_sem, *, cap):
    b = pl.program_id(0)
    e = pl.program_id(1)
    n_e = pl.num_programs(1)
    s = h2_ref.shape[1]
    tchunk = s // N_TOK_CHUNKS

    def residual_copy():
        return pltpu.make_async_copy(x1_hbm.at[b], out_ref.at[0], x1_sem.at[0])

    @pl.when(e == 0)
    def _():
        residual_copy().start()

    xe_scr[...] = jnp.zeros_like(xe_scr)
    g_scr[...] = jnp.zeros_like(g_scr)
    base = (b * n_e + e) * N_TOK_CHUNKS

    def walk(cchunk, visit):
        n_cap = cap // cchunk
        slot_iota = lax.broadcasted_iota(jnp.int32, (cchunk, tchunk), 0)
        j = jnp.int32(0)
        i = jnp.int32(0)
        for _ in range(N_TOK_CHUNKS + n_cap - 1):
            jc = jnp.minimum(j, N_TOK_CHUNKS - 1)
            ic = jnp.minimum(i, n_cap - 1)
            c_lo = cnt_ref[base + jc]
            c_hi = jnp.where(jc + 1 < N_TOK_CHUNKS, cnt_ref[base + jnp.minimum(jc + 1, N_TOK_CHUNKS - 1)], cap)
            s_hi = (ic + 1) * cchunk
            valid = jnp.logical_and(jnp.maximum(c_lo, ic * cchunk) < jnp.minimum(c_hi, s_hi),
                                    jnp.logical_and(j < N_TOK_CHUNKS, i < n_cap))
            slot0 = jnp.where(valid, ic * cchunk, -2 * cap)
            sel = pos_ref[0, 0, pl.ds(jc, 1), :] == slot_iota + slot0
            visit(sel, jc, pl.ds(pl.multiple_of(ic * cchunk, cchunk), cchunk),
                  pl.ds(pl.multiple_of(jc * tchunk, tchunk), tchunk))
            j = j + jnp.where(c_hi <= s_hi, 1, 0)
            i = i + jnp.where(s_hi <= c_hi, 1, 0)

    def gather(sel, jc, rows, toks):
        xe_scr[rows, :] += jnp.dot(jnp.where(sel, 1.0, 0.0).astype(BF16), h2_ref[0, toks, :],
                                   preferred_element_type=F32)
        g_scr[rows, :] += jnp.sum(jnp.where(sel, aff_ref[0, 0, pl.ds(jc, 1), :], 0.0), axis=-1, keepdims=True)

    walk(cap // N_CAP_CHUNKS, gather)

    xe = xe_scr[...].astype(BF16)
    a = jnp.dot(xe, w1_ref[0], preferred_element_type=F32)
    gate = jnp.dot(xe, w3_ref[0], preferred_element_type=F32)
    hmid = ((a * _sigmoid(a)) * gate).astype(BF16)
    y = jnp.dot(hmid, w2_ref[0], preferred_element_type=F32) * g_scr[...] * gate_ref[0]
    y_scr[...] = y.astype(BF16)

    @pl.when(e == 0)
    def _():
        residual_copy().wait()

    def scatter(sel, jc, rows, toks):
        out_ref[0, toks, :] += lax.dot_general(jnp.where(sel, 1.0, 0.0).astype(BF16), y_scr[rows, :],
                                               TN_DIMS, preferred_element_type=F32)

    walk(cap // N_SCATTER_CHUNKS, scatter)


def _moe(cnt, pos, aff, h2, w1, w3, w2, gate2, x1, cap):
    bsz, s, d = h2.shape
    n_e = w1.shape[0]
    f = w1.shape[2]
    tok_row = lambda b, e, c: (b, e, 0, 0)
    wspec = lambda b, e, c: (e, 0, 0)
    resident = lambda b, e, c: (b, 0, 0)
    grid_spec = pltpu.PrefetchScalarGridSpec(
        num_scalar_prefetch=1,
        grid=(bsz, n_e),
        in_specs=[pl.BlockSpec((1, 1, N_TOK_CHUNKS, s // N_TOK_CHUNKS), tok_row),
                  pl.BlockSpec((1, 1, N_TOK_CHUNKS, s // N_TOK_CHUNKS), tok_row),
                  pl.BlockSpec((1, s, d), resident, pipeline_mode=pl.Buffered(1)),
                  pl.BlockSpec((1, d, f), wspec),
                  pl.BlockSpec((1, d, f), wspec),
                  pl.BlockSpec((1, f, d), wspec),
                  pl.BlockSpec((1, 1, d), resident),
                  pl.BlockSpec(memory_space=pl.ANY)],
        out_specs=pl.BlockSpec((1, s, d), resident, pipeline_mode=pl.Buffered(1)),
        scratch_shapes=[pltpu.VMEM((cap, d), F32), pltpu.VMEM((cap, 1), F32), pltpu.VMEM((cap, d), BF16),
                        pltpu.SemaphoreType.DMA((1,))],
    )
    return pl.pallas_call(
        functools.partial(_moe_kernel, cap=cap),
        grid_spec=grid_spec,
        out_shape=jax.ShapeDtypeStruct((bsz, s, d), F32),
        compiler_params=_params(("arbitrary", "arbitrary"), VMEM_LIMIT),
        name="expert_choice_ffn",
    )(cnt.reshape(-1), pos.reshape(bsz, n_e, N_TOK_CHUNKS, -1), aff.reshape(bsz, n_e, N_TOK_CHUNKS, -1), h2,
      w1.astype(BF16), w3.astype(BF16), w2.astype(BF16), gate2, x1)


def kernel(x, c, w_mod, b_mod, g_norm1, w_in, conv_w, conv_b, lru_w_a, lru_b_a, lru_w_x, lru_b_x,
           lru_lambda, g_q, g_k, lambda_qk, g_attn_out, rel_bias, w_out, g_norm2, w_router, w1, w3, w2):
    bsz, s, d = x.shape
    depth = w_mod.shape[0]
    cap = max(1, EC_FACTOR * s // N_EXPERTS)
    tm = min(512, s)
    bias_tiles = _bias_tiles(rel_bias, min(ATT_TILE, s))
    for l in range(depth):
        mod = _modulation(c, w_mod[l], b_mod[l])
        shift1, scale1, gate1, shift2, scale2, gate2 = [m.reshape(bsz, 1, d) for m in jnp.split(mod, 6, axis=-1)]
        x_lru, gz, qn, kn, vt = _in_projection(x, scale1, shift1, g_norm1[l], w_in[l], g_q[l], g_k[l], tm)
        y_lru = _rg_lru(x_lru, gz, conv_w[l], conv_b[l], lru_w_a[l], lru_b_a[l], lru_w_x[l], lru_b_x[l],
                        lru_lambda[l])
        lam_init = 0.8 - 0.6 * math.exp(-0.3 * l)
        y_att = _diff_attention(qn, kn, vt, rel_bias, bias_tiles, lambda_qk[l], g_attn_out[l], lam_init)
        x1, h2, aff = _out_projection(y_lru, y_att, x, gate1, scale2, shift2, g_norm2[l], w_out[l],
                                      w_router[l], tm)
        pos, cnt = _routing(aff, cap)
        x = _moe(cnt[:, :, :N_TOK_CHUNKS], pos, aff, h2, w1[l], w3[l], w2[l], gate2, x1, cap)
    return x
```

```python
import functools
import math

import jax
import jax.numpy as jnp
from jax import lax
from jax.experimental import pallas as pl
from jax.experimental.pallas import tpu as pltpu

F32 = jnp.float32
BF16 = jnp.bfloat16

D_MODEL = 1024
D_LRU = 512
LRU_BLOCK = 64
LRU_C = 8.0
CONV_W = 4
N_HEADS = 4
HEAD_DV = 128
HEAD_DK = 64
D_ATT = N_HEADS * HEAD_DV
N_BUCKETS = 32
N_EXPERTS = 16
EC_FACTOR = 2
EPS = 1e-6
LOG2E = math.log2(math.e)

LRU_HALF = 256
SCAN_LANES = 8
ATT_TILE = 256
N_TOK_CHUNKS = 8
N_CAP_CHUNKS = 4
N_SCATTER_CHUNKS = 2
VMEM_LIMIT = 56 * 1024 * 1024

NT_DIMS = (((1,), (1,)), ((), ()))
TN_DIMS = (((0,), (0,)), ((), ()))


def _sigmoid(x):
    return 1.0 / (1.0 + jnp.exp(-x))


def _params(sem, vmem=None, flags=None):
    return pltpu.CompilerParams(dimension_semantics=sem, vmem_limit_bytes=vmem, flags=flags)


def _mod_kernel(c_ref, w_ref, b_ref, o_ref):
    c = c_ref[...]
    o_ref[...] = jnp.dot((c * _sigmoid(c)).astype(BF16), w_ref[...].astype(BF16),
                         preferred_element_type=F32) + b_ref[...]


def _modulation(c, w_mod, b_mod):
    bsz, d = c.shape
    n = w_mod.shape[1]
    return pl.pallas_call(
        _mod_kernel,
        grid=(n // d,),
        in_specs=[pl.BlockSpec((bsz, d), lambda j: (0, 0)),
                  pl.BlockSpec((d, d), lambda j: (0, j)),
                  pl.BlockSpec((1, d), lambda j: (0, j))],
        out_specs=pl.BlockSpec((bsz, d), lambda j: (0, j)),
        out_shape=jax.ShapeDtypeStruct((bsz, n), F32),
        compiler_params=_params(("arbitrary",)),
        name="adaln_mod",
    )(c, w_mod, b_mod.reshape(1, n))


def _bias_kernel(tab_ref, o_ref):
    h = pl.program_id(0)
    t = o_ref.shape[-1]
    key = lax.broadcasted_iota(jnp.int32, (t, t), 0)
    qry = lax.broadcasted_iota(jnp.int32, (t, t), 1)
    half = N_BUCKETS // 2
    max_exact = half // 2
    for d in range(5):
        if d == 0:
            o_ref[0, d] = jnp.full((t, t), tab_ref[half - 1, h] * LOG2E, F32)
        elif d == 4:
            o_ref[0, d] = jnp.full((t, t), tab_ref[N_BUCKETS - 1, h] * LOG2E, F32)
        else:
            rel = (d - 2) * t + key - qry
            n = jnp.abs(rel)
            n2 = n * n
            large = jnp.full((t, t), max_exact, jnp.int32)
            for k in range(1, half - max_exact):
                large = large + jnp.where(n2 >= (max_exact * max_exact) * (2 ** k), 1, 0)
            idx = jnp.where(n < max_exact, n, large) + jnp.where(rel > 0, half, 0)
            val = jnp.zeros((t, t), F32)
            for j in range(N_BUCKETS):
                val = jnp.where(idx == j, tab_ref[j, h] * LOG2E, val)
            o_ref[0, d] = val


def _bias_tiles(rel_bias, t):
    return pl.pallas_call(
        _bias_kernel,
        grid=(N_HEADS,),
        in_specs=[pl.BlockSpec(memory_space=pltpu.SMEM)],
        out_specs=pl.BlockSpec((1, 5, t, t), lambda h: (h, 0, 0, 0)),
        out_shape=jax.ShapeDtypeStruct((N_HEADS, 5, t, t), F32),
        compiler_params=_params(("arbitrary",)),
        name="t5_bias_tiles",
    )(rel_bias)


def _inproj_kernel(x_ref, sc_ref, sh_ref, g1_ref, w_ref, wvt_ref, mseg_ref, gq_ref, gk_ref,
                   xl_ref, gz_ref, q_ref, k_ref, vt_ref):
    x = x_ref[0]
    ms = jnp.mean(x * x, axis=-1, keepdims=True)
    h = (x * lax.rsqrt(ms + EPS) * g1_ref[...]) * (1.0 + sc_ref[0]) + sh_ref[0]
    hb = h.astype(BF16)

    def proj(lo, width):
        return jnp.dot(hb, w_ref[:, lo:lo + width], preferred_element_type=F32)

    def qk_norm(t, g):
        t2 = t * t
        hi = t2.astype(BF16)
        lo = (t2 - hi.astype(F32)).astype(BF16)
        ss = (jnp.dot(hi, mseg_ref[...], preferred_element_type=F32)
              + jnp.dot(lo, mseg_ref[...], preferred_element_type=F32))
        return t * lax.rsqrt(ss * (1.0 / HEAD_DK) + EPS) * g

    xl_ref[0] = proj(0, D_LRU)
    z = proj(D_LRU, D_LRU)
    cdf = 0.5 * (1.0 + jnp.tanh(math.sqrt(2.0 / math.pi) * (z + 0.044715 * (z * z * z))))
    gz_ref[0] = (z * cdf).astype(BF16)
    q_ref[0] = (qk_norm(proj(2 * D_LRU, D_ATT), gq_ref[...]) * (HEAD_DK ** -0.5 * LOG2E)).astype(BF16)
    k_ref[0] = qk_norm(proj(2 * D_LRU + D_ATT, D_ATT), gk_ref[...]).astype(BF16)
    vt_ref[0] = lax.dot_general(wvt_ref[...], hb, NT_DIMS,
                                preferred_element_type=F32).astype(BF16)


def _in_projection(x, scale1, shift1, g_norm1, w_in, g_q, g_k, tm):
    bsz, s, d = x.shape
    n = w_in.shape[1] - D_ATT
    w_main = w_in[:, :n].astype(BF16)
    w_vt = w_in[:, n:].T.astype(BF16)
    seg = jnp.arange(D_ATT, dtype=jnp.int32) // HEAD_DK
    mseg = (seg[:, None] == seg[None, :]).astype(BF16)
    n_sub = D_ATT // HEAD_DK
    row = lambda b, i: (b, i, 0)
    vec = lambda b, i: (b, 0, 0)
    full = lambda b, i: (0, 0)
    out_block = pl.BlockSpec((1, tm, D_LRU), row)
    return pl.pallas_call(
        _inproj_kernel,
        grid=(bsz, s // tm),
        in_specs=[pl.BlockSpec((1, tm, d), row),
                  pl.BlockSpec((1, 1, d), vec),
                  pl.BlockSpec((1, 1, d), vec),
                  pl.BlockSpec((1, d), full),
                  pl.BlockSpec((d, n), full),
                  pl.BlockSpec((D_ATT, d), full),
                  pl.BlockSpec((D_ATT, D_ATT), full),
                  pl.BlockSpec((1, D_ATT), full),
                  pl.BlockSpec((1, D_ATT), full)],
        out_specs=[out_block] * 4 + [pl.BlockSpec((1, D_ATT, tm), lambda b, i: (b, 0, i))],
        out_shape=[jax.ShapeDtypeStruct((bsz, s, D_LRU), F32)]
                  + [jax.ShapeDtypeStruct((bsz, s, D_LRU), BF16)] * 3
                  + [jax.ShapeDtypeStruct((bsz, D_ATT, s), BF16)],
        compiler_params=_params(("parallel", "parallel"), VMEM_LIMIT),
        name="norm1_in_proj",
    )(x, scale1, shift1, g_norm1.reshape(1, d), w_main, w_vt, mseg,
      jnp.tile(g_q, n_sub).reshape(1, D_ATT), jnp.tile(g_k, n_sub).reshape(1, D_ATT))


def _lru_kernel(x_ref, gz_ref, cw_ref, cb_ref, wa_ref, wx_ref, ba_ref, bx_ref, lam_ref, y_ref,
                xpad, a_f, u_f, a_b, u_b, *, tc):
    s = x_ref.shape[1]
    c = x_ref.shape[2]
    n_chunks = s // tc
    n_slab = c // 128
    pitch = a_f.shape[1] // n_chunks
    zeros8 = jnp.zeros((8, c), F32)
    xpad[0:8, :] = zeros8
    xpad[s + 8:s + 16, :] = zeros8

    def fill(ci, carry):
        t0 = pl.multiple_of(ci * tc, tc)
        xpad[pl.ds(t0 + 8, tc), :] = x_ref[0, pl.ds(t0, tc), :]
        return carry

    lax.fori_loop(0, n_chunks, fill, 0)

    cw = cw_ref[...]
    cb = cb_ref[...]
    decay = []
    for d in range(2):
        lam = lam_ref[d]
        softplus_neg = jnp.maximum(-lam, 0.0) + jnp.log(1.0 + jnp.exp(-jnp.abs(lam)))
        decay.append(-LRU_C * softplus_neg)
    a_scr = (a_f, a_b)
    u_scr = (u_f, u_b)

    def gates(ci, carry):
        t0 = pl.multiple_of(ci * tc, tc)
        xw = xpad[pl.ds(t0, tc + 16), :]
        xc = (cw[0:1] * pltpu.roll(xw, 2, 0)[8:8 + tc]
              + cw[1:2] * pltpu.roll(xw, 1, 0)[8:8 + tc]
              + cw[2:3] * xw[8:8 + tc]
              + cw[3:4] * pltpu.roll(xw, tc + 15, 0)[8:8 + tc]) + cb
        xcb = xc.astype(BF16)
        for d in range(2):
            r = _sigmoid(jnp.dot(xcb, wa_ref[d], preferred_element_type=F32) + ba_ref[d])
            i = _sigmoid(jnp.dot(xcb, wx_ref[d], preferred_element_type=F32) + bx_ref[d])
            a = jnp.exp(r * decay[d])
            u = jnp.sqrt(1.0 - a * a) * (i * xc)
            r0 = pl.multiple_of(ci * pitch, 8)
            for sl in range(n_slab):
                a_scr[d][sl, pl.ds(r0, tc), :] = a[:, sl * 128:(sl + 1) * 128]
                u_scr[d][sl, pl.ds(r0, tc), :] = u[:, sl * 128:(sl + 1) * 128]
        return carry

    lax.fori_loop(0, n_chunks, gates, 0)

    def rows(tt):
        return pl.ds(tt, n_chunks, stride=pitch)

    def step(tt, carry):
        out = []
        for d in range(2):
            t_loc = tt if d == 0 else tc - 1 - tt
            for sl in range(n_slab):
                h, p = carry[len(out)]
                a8 = a_scr[d][sl, rows(t_loc), :]
                h = a8 * h + u_scr[d][sl, rows(t_loc), :]
                p = a8 * p
                u_scr[d][sl, rows(t_loc), :] = h
                a_scr[d][sl, rows(t_loc), :] = p
                out.append((h, p))
        return tuple(out)

    init = (jnp.zeros((n_chunks, 128), F32), jnp.ones((n_chunks, 128), F32))
    lax.fori_loop(0, tc, step, (init,) * (2 * n_slab), unroll=4)

    for sl in range(n_slab):
        lanes = slice(sl * 128, (sl + 1) * 128)
        h_end, p_end = u_f[sl, rows(tc - 1), :], a_f[sl, rows(tc - 1), :]
        h_beg, p_beg = u_b[sl, rows(0), :], a_b[sl, rows(0), :]
        carry_f = [jnp.zeros((1, 128), F32)]
        for r in range(1, n_chunks):
            carry_f.append(p_end[r - 1:r] * carry_f[-1] + h_end[r - 1:r])
        carry_b = [jnp.zeros((1, 128), F32)]
        for r in range(n_chunks - 2, -1, -1):
            carry_b.insert(0, p_beg[r + 1:r + 2] * carry_b[0] + h_beg[r + 1:r + 2])
        for r in range(n_chunks):
            blk = slice(r * pitch, r * pitch + tc)
            hsum = (u_f[sl, blk, :] + a_f[sl, blk, :] * carry_f[r]
                    + u_b[sl, blk, :] + a_b[sl, blk, :] * carry_b[r])
            y_ref[0, r * tc:(r + 1) * tc, lanes] = (hsum * gz_ref[0, r * tc:(r + 1) * tc, lanes].astype(F32)).astype(BF16)


def _block_diag(w, half):
    n_dir, n_blocks, blk, _ = w.shape
    per = half // blk
    n_half = n_blocks // per
    w = w.reshape(n_dir, n_half, per, blk, blk)
    eye = jnp.eye(per, dtype=w.dtype)
    out = w[:, :, :, :, None, :] * eye[None, None, :, None, :, None]
    return out.reshape(n_dir, n_half, half, half)


def _rg_lru(x_lru, gz, conv_w, conv_b, w_a, b_a, w_x, b_x, lam):
    bsz, s, c = x_lru.shape
    half = LRU_HALF
    tc = s // SCAN_LANES
    pitch = tc + 8 if (tc // 8) % 2 == 0 else tc + 16
    n_half = c // half
    wa = _block_diag(w_a, half).astype(BF16)
    wx = _block_diag(w_x, half).astype(BF16)
    seq = lambda b, p: (b, 0, p)
    chan = lambda b, p: (0, p)
    dirchan = lambda b, p: (0, 0, p)
    blk = lambda b, p: (0, p, 0, 0)
    return pl.pallas_call(
        functools.partial(_lru_kernel, tc=tc),
        grid=(bsz, n_half),
        in_specs=[pl.BlockSpec((1, s, half), seq),
                  pl.BlockSpec((1, s, half), seq),
                  pl.BlockSpec((CONV_W, half), chan),
                  pl.BlockSpec((1, half), chan),
                  pl.BlockSpec((2, None, half, half), blk),
                  pl.BlockSpec((2, None, half, half), blk),
                  pl.BlockSpec((2, 1, half), dirchan),
                  pl.BlockSpec((2, 1, half), dirchan),
                  pl.BlockSpec((2, 1, half), dirchan)],
        out_specs=pl.BlockSpec((1, s, half), seq),
        out_shape=jax.ShapeDtypeStruct((bsz, s, c), BF16),
        scratch_shapes=[pltpu.VMEM((s + 16, half), F32)]
                       + [pltpu.VMEM((half // 128, SCAN_LANES * pitch, 128), F32)] * 4,
        compiler_params=_params(("parallel", "parallel"), VMEM_LIMIT),
        name="rg_lru",
    )(x_lru, gz, conv_w.reshape(CONV_W, c), conv_b.reshape(1, c), wa, wx,
      b_a.reshape(2, 1, c), b_x.reshape(2, 1, c), lam.reshape(2, 1, c))


def _attn_kernel(tab_ref, q_ref, k_ref, vt_ref, bias_ref, lq_ref, go_ref, o_ref,
                 s_even, s_odd, m_even, m_odd, p_scr, o_scr, *, lam_init):
    h = pl.program_id(1)
    t = bias_ref.shape[-1]
    n_k = k_ref.shape[1] // t
    n_q = q_ref.shape[1] // t
    chunk = 4
    n_c = n_k // chunk
    c_left = tab_ref[N_BUCKETS // 2 - 1, h] * LOG2E
    c_right = tab_ref[N_BUCKETS - 1, h] * LOG2E

    s_odd[...] = jnp.zeros_like(s_odd)
    m_odd[...] = jnp.zeros_like(m_odd)

    def fold(x, op):
        parts = [x[r * 8:(r + 1) * 8, :] for r in range(t // 8)]
        acc = parts[:2]
        for r in range(2, len(parts)):
            acc[r % 2] = op(acc[r % 2], parts[r])
        return op(acc[0], acc[1])

    def far_bias(j, tile):
        return jnp.where(j < tile, c_left, c_right)

    def is_far(j, tile):
        return jnp.abs(j - tile) >= 2

    def step(i, s_cur, m_cur, s_prev, m_prev, scores=True):
        tile_a = i
        tile_b = jnp.maximum(i - 1, 0)
        m_fin = [m_prev[u] for u in range(2)]
        for dj in (-1, 0, 1):
            j = tile_b + dj
            valid = jnp.logical_and(j >= 0, j < n_k)
            jc = jnp.clip(j, 0, n_k - 1)
            bias = jnp.where(valid, bias_ref[0, dj + 2], 0.0)
            for u in range(2):
                sc = s_prev[u, jc] + bias
                s_prev[u, jc] = sc
                m_fin[u] = jnp.maximum(m_fin[u], fold(sc, jnp.maximum) + jnp.where(valid, 0.0, -jnp.inf))
        m_b = [jnp.max(m_fin[u], axis=0, keepdims=True) for u in range(2)]
        m_acc = [jnp.full((8, t), -jnp.inf, F32) for _ in range(2)]
        l_acc = [jnp.zeros((8, t), F32) for _ in range(2)]
        if scores:
            q = q_ref[0, pl.ds(pl.multiple_of(tile_a * t, t), t), :]
            lane = lax.broadcasted_iota(jnp.int32, q.shape, 1)
            zero = jnp.zeros_like(q)
            q_sub = (jnp.where(lane < HEAD_DK, q, zero), jnp.where(lane >= HEAD_DK, q, zero))
        for c in range(n_c):
            rows = slice(c * chunk * t, (c + 1) * chunk * t)
            for u in range(2 if scores else 0):
                sc_all = lax.dot_general(k_ref[0, rows, :], q_sub[u], NT_DIMS, preferred_element_type=F32)
                for jj in range(chunk):
                    j = c * chunk + jj
                    sc = sc_all[jj * t:(jj + 1) * t, :]
                    s_cur[u, j] = sc
                    m_acc[u] = jnp.maximum(m_acc[u], fold(sc, jnp.maximum)
                                           + jnp.where(is_far(j, tile_a), far_bias(j, tile_a), -jnp.inf))
            for u in range(2):
                for jj in range(chunk):
                    j = c * chunk + jj
                    off = m_b[u] - jnp.where(is_far(j, tile_b), far_bias(j, tile_b), 0.0)
                    p = jnp.exp2(s_prev[u, j] - off)
                    p_scr[u, j * t:(j + 1) * t, :] = p.astype(BF16)
                    l_acc[u] = l_acc[u] + fold(p, jnp.add)
                part = jnp.dot(vt_ref[0, :, rows], p_scr[u, rows, :], preferred_element_type=F32)
                if c == 0:
                    o_scr[u] = part
                else:
                    o_scr[u] += part

        for u in range(2 if scores else 0):
            m_cur[u] = m_acc[u]

        outs = [o_scr[u] / jnp.sum(l_acc[u], axis=0, keepdims=True) for u in range(2)]
        lq = lq_ref[...]
        lam = (jnp.exp(jnp.sum(lq[0:1] * lq[1:2], axis=-1, keepdims=True))
               - jnp.exp(jnp.sum(lq[2:3] * lq[3:4], axis=-1, keepdims=True)) + lam_init)
        o = (outs[0] - lam * outs[1]).T
        ms = jnp.mean(o * o, axis=-1, keepdims=True)
        o_ref[0, pl.ds(pl.multiple_of(tile_b * t, t), t), :] = (
            (o * lax.rsqrt(ms + EPS) * go_ref[...]) * (1.0 - lam_init)).astype(BF16)

    def pair(pi, carry):
        @pl.when(pi >= 0)
        def _():
            step(2 * pi, s_even, m_even, s_odd, m_odd)

        @pl.when(pi < n_q)
        def _():
            step(2 * pi + 1, s_odd, m_odd, s_even, m_even)

        return carry

    lax.fori_loop(0, n_q // 2, pair, 0)
    step(n_q, s_even, m_even, s_odd, m_odd, scores=False)


def _diff_attention(qn, kn, vt, rel_bias, bias_tiles, lambda_qk, g_o, lam_init):
    bsz, s, _ = qn.shape
    t = bias_tiles.shape[-1]
    n_k = s // t
    assert (s // t) % 2 == 0, "the query-tile pipeline advances two tiles per loop trip"
    scores = pltpu.VMEM((2, n_k, t, t), F32)
    maxima = pltpu.VMEM((2, 8, t), F32)
    seq = lambda b, h: (b, 0, h)
    return pl.pallas_call(
        functools.partial(_attn_kernel, lam_init=lam_init),
        grid=(bsz, N_HEADS),
        in_specs=[pl.BlockSpec(memory_space=pltpu.SMEM),
                  pl.BlockSpec((1, s, HEAD_DV), seq),
                  pl.BlockSpec((1, s, HEAD_DV), seq),
                  pl.BlockSpec((1, HEAD_DV, s), lambda b, h: (b, h, 0)),
                  pl.BlockSpec((1, 5, t, t), lambda b, h: (h, 0, 0, 0)),
                  pl.BlockSpec((4, HEAD_DK), lambda b, h: (0, 0)),
                  pl.BlockSpec((1, HEAD_DV), lambda b, h: (0, 0))],
        out_specs=pl.BlockSpec((1, s, HEAD_DV), seq),
        out_shape=jax.ShapeDtypeStruct((bsz, s, D_ATT), BF16),
        scratch_shapes=[scores, scores, maxima, maxima, pltpu.VMEM((2, s, t), BF16),
                        pltpu.VMEM((2, HEAD_DV, t), F32)],
        compiler_params=_params(("parallel", "parallel"), VMEM_LIMIT),
        name="diff_attention",
    )(rel_bias, qn, kn, vt, bias_tiles, lambda_qk, g_o.reshape(1, HEAD_DV))


def _outproj_kernel(yl_ref, ya_ref, x_ref, gate_ref, sc_ref, sh_ref, g2_ref, wo_ref, wr_ref,
                    x1_ref, h2_ref, aff_ref):
    mix = (jnp.dot(yl_ref[0], wo_ref[0:D_LRU, :], preferred_element_type=F32)
           + jnp.dot(ya_ref[0], wo_ref[D_LRU:D_LRU + D_ATT, :], preferred_element_type=F32))
    x1 = x_ref[0] + gate_ref[0] * mix
    x1_ref[0] = x1
    ms = jnp.mean(x1 * x1, axis=-1, keepdims=True)
    h2 = (x1 * lax.rsqrt(ms + EPS) * g2_ref[...]) * (1.0 + sc_ref[0]) + sh_ref[0]
    h2b = h2.astype(BF16)
    h2_ref[0] = h2b
    logits = lax.dot_general(wr_ref[...], h2b, NT_DIMS, preferred_element_type=F32)
    ex = jnp.exp(logits - jnp.max(logits, axis=0, keepdims=True))
    aff_ref[0] = ex / jnp.sum(ex, axis=0, keepdims=True)


def _out_projection(y_lru, y_att, x, gate1, scale2, shift2, g_norm2, w_out, w_router, tm):
    bsz, s, d = x.shape
    row = lambda b, i: (b, i, 0)
    vec = lambda b, i: (b, 0, 0)
    full = lambda b, i: (0, 0)
    return pl.pallas_call(
        _outproj_kernel,
        grid=(bsz, s // tm),
        in_specs=[pl.BlockSpec((1, tm, D_LRU), row),
                  pl.BlockSpec((1, tm, D_ATT), row),
                  pl.BlockSpec((1, tm, d), row),
                  pl.BlockSpec((1, 1, d), vec),
                  pl.BlockSpec((1, 1, d), vec),
                  pl.BlockSpec((1, 1, d), vec),
                  pl.BlockSpec((1, d), full),
                  pl.BlockSpec((D_LRU + D_ATT, d), full),
                  pl.BlockSpec((N_EXPERTS, d), full)],
        out_specs=[pl.BlockSpec((1, tm, d), row),
                   pl.BlockSpec((1, tm, d), row),
                   pl.BlockSpec((1, N_EXPERTS, tm), lambda b, i: (b, 0, i))],
        out_shape=[jax.ShapeDtypeStruct((bsz, s, d), F32),
                   jax.ShapeDtypeStruct((bsz, s, d), BF16),
                   jax.ShapeDtypeStruct((bsz, N_EXPERTS, s), F32)],
        compiler_params=_params(("parallel", "parallel"), VMEM_LIMIT),
        name="out_proj_norm2_router",
    )(y_lru, y_att, x, gate1, scale2, shift2, g_norm2.reshape(1, d), w_out.astype(BF16),
      w_router.T.astype(BF16))


def _route_kernel(aff_ref, pos_ref, cnt_ref, *, cap, n_tok_chunks):
    aff = aff_ref[0]
    n_e, s = aff.shape
    bits = lax.bitcast_convert_type(aff, jnp.int32)
    capf = float(cap)

    def count(mask):
        return jnp.sum(jnp.where(mask, 1.0, 0.0), axis=-1, keepdims=True)

    tau = jnp.zeros((n_e, 1), jnp.int32)
    for bit in range(30, -1, -1):
        cand = tau | (1 << bit)
        tau = jnp.where(count(bits >= cand) >= capf, cand, tau)
    gt = bits > tau
    eq = bits == tau
    need = capf - count(gt)

    blk = 256
    r = lax.broadcasted_iota(jnp.int32, (blk, blk), 0)
    cidx = lax.broadcasted_iota(jnp.int32, (blk, blk), 1)
    upper = jnp.where(r < cidx, 1.0, 0.0).astype(BF16)

    def prefix_blocks(mask):
        off = jnp.zeros((n_e, 1), F32)
        pieces, offs = [], []
        for k in range(s // blk):
            mb = jnp.where(mask[:, k * blk:(k + 1) * blk], 1.0, 0.0)
            offs.append(off)
            pieces.append(jnp.dot(mb.astype(BF16), upper, preferred_element_type=F32) + off)
            off = off + jnp.sum(mb, axis=-1, keepdims=True)
        return pieces, offs

    eq_rank, _ = prefix_blocks(eq)
    sel_blocks = []
    for k in range(s // blk):
        sl = slice(k * blk, (k + 1) * blk)
        sel_blocks.append(jnp.logical_or(gt[:, sl], jnp.logical_and(eq[:, sl], eq_rank[k] < need)))
    sel = jnp.concatenate(sel_blocks, axis=1)
    slot, offs = prefix_blocks(sel)
    for k in range(s // blk):
        pos_ref[0, :, k * blk:(k + 1) * blk] = jnp.where(sel_blocks[k], slot[k], -1.0).astype(jnp.int32)

    lane = lax.broadcasted_iota(jnp.int32, (n_e, 128), 1)
    cnt = jnp.zeros((n_e, 128), F32)
    per = (s // n_tok_chunks) // blk
    for j in range(n_tok_chunks):
        cnt = jnp.where(lane == j, offs[j * per], cnt)
    cnt_ref[0] = cnt.astype(jnp.int32)


def _routing(aff, cap):
    bsz, n_e, s = aff.shape
    return pl.pallas_call(
        functools.partial(_route_kernel, cap=cap, n_tok_chunks=N_TOK_CHUNKS),
        grid=(bsz,),
        in_specs=[pl.BlockSpec((1, n_e, s), lambda b: (b, 0, 0))],
        out_specs=[pl.BlockSpec((1, n_e, s), lambda b: (b, 0, 0)),
                   pl.BlockSpec((1, n_e, 128), lambda b: (b, 0, 0))],
        out_shape=[jax.ShapeDtypeStruct((bsz, n_e, s), jnp.int32),
                   jax.ShapeDtypeStruct((bsz, n_e, 128), jnp.int32)],
        compiler_params=_params(("parallel",)),
        name="expert_choice_routing",
    )(aff)


def _moe_kernel(cnt_ref, pos_ref, aff_ref, h2_ref, w1_ref, w3_ref, w2_ref, gate_ref, x1_hbm, out_ref,
                xe_scr, g_scr, y_scr, x1_sem, *, cap):
    b = pl.program_id(0)
    e = pl.program_id(1)
    n_e = pl.num_programs(1)
    s = h2_ref.shape[1]
    tchunk = s // N_TOK_CHUNKS

    def residual_copy():
        return pltpu.make_async_copy(x1_hbm.at[b], out_ref.at[0], x1_sem.at[0])

    @pl.when(e == 0)
    def _():
        residual_copy().start()

    xe_scr[...] = jnp.zeros_like(xe_scr)
    g_scr[...] = jnp.zeros_like(g_scr)
    base = (b * n_e + e) * N_TOK_CHUNKS

    def walk(cchunk, visit):
        n_cap = cap // cchunk
        slot_iota = lax.broadcasted_iota(jnp.int32, (cchunk, tchunk), 0)
        j = jnp.int32(0)
        i = jnp.int32(0)
        for _ in range(N_TOK_CHUNKS + n_cap - 1):
            jc = jnp.minimum(j, N_TOK_CHUNKS - 1)
            ic = jnp.minimum(i, n_cap - 1)
            c_lo = cnt_ref[base + jc]
            c_hi = jnp.where(jc + 1 < N_TOK_CHUNKS, cnt_ref[base + jnp.minimum(jc + 1, N_TOK_CHUNKS - 1)], cap)
            s_hi = (ic + 1) * cchunk
            valid = jnp.logical_and(jnp.maximum(c_lo, ic * cchunk) < jnp.minimum(c_hi, s_hi),
                                    jnp.logical_and(j < N_TOK_CHUNKS, i < n_cap))
            slot0 = jnp.where(valid, ic * cchunk, -2 * cap)
            sel = pos_ref[0, 0, pl.ds(jc, 1), :] == slot_iota + slot0
            visit(sel, jc, pl.ds(pl.multiple_of(ic * cchunk, cchunk), cchunk),
                  pl.ds(pl.multiple_of(jc * tchunk, tchunk), tchunk))
            j = j + jnp.where(c_hi <= s_hi, 1, 0)
            i = i + jnp.where(s_hi <= c_hi, 1, 0)

    def gather(sel, jc, rows, toks):
        xe_scr[rows, :] += jnp.dot(jnp.where(sel, 1.0, 0.0).astype(BF16), h2_ref[0, toks, :],
                                   preferred_element_type=F32)
        g_scr[rows, :] += jnp.sum(jnp.where(sel, aff_ref[0, 0, pl.ds(jc, 1), :], 0.0), axis=-1, keepdims=True)

    walk(cap // N_CAP_CHUNKS, gather)

    xe = xe_scr[...].astype(BF16)
    a = jnp.dot(xe, w1_ref[0], preferred_element_type=F32)
    gate = jnp.dot(xe, w3_ref[0], preferred_element_type=F32)
    hmid = ((a * _sigmoid(a)) * gate).astype(BF16)
    y = jnp.dot(hmid, w2_ref[0], preferred_element_type=F32) * g_scr[...] * gate_ref[0]
    y_scr[...] = y.astype(BF16)

    @pl.when(e == 0)
    def _():
        residual_copy().wait()

    def scatter(sel, jc, rows, toks):
        out_ref[0, toks, :] += lax.dot_general(jnp.where(sel, 1.0, 0.0).astype(BF16), y_scr[rows, :],
                                               TN_DIMS, preferred_element_type=F32)

    walk(cap // N_SCATTER_CHUNKS, scatter)


def _moe(cnt, pos, aff, h2, w1, w3, w2, gate2, x1, cap):
    bsz, s, d = h2.shape
    n_e = w1.shape[0]
    f = w1.shape[2]
    tok_row = lambda b, e, c: (b, e, 0, 0)
    wspec = lambda b, e, c: (e, 0, 0)
    resident = lambda b, e, c: (b, 0, 0)
    grid_spec = pltpu.PrefetchScalarGridSpec(
        num_scalar_prefetch=1,
        grid=(bsz, n_e),
        in_specs=[pl.BlockSpec((1, 1, N_TOK_CHUNKS, s // N_TOK_CHUNKS), tok_row),
                  pl.BlockSpec((1, 1, N_TOK_CHUNKS, s // N_TOK_CHUNKS), tok_row),
                  pl.BlockSpec((1, s, d), resident, pipeline_mode=pl.Buffered(1)),
                  pl.BlockSpec((1, d, f), wspec),
                  pl.BlockSpec((1, d, f), wspec),
                  pl.BlockSpec((1, f, d), wspec),
                  pl.BlockSpec((1, 1, d), resident),
                  pl.BlockSpec(memory_space=pl.ANY)],
        out_specs=pl.BlockSpec((1, s, d), resident, pipeline_mode=pl.Buffered(1)),
        scratch_shapes=[pltpu.VMEM((cap, d), F32), pltpu.VMEM((cap, 1), F32), pltpu.VMEM((cap, d), BF16),
                        pltpu.SemaphoreType.DMA((1,))],
    )
    return pl.pallas_call(
        functools.partial(_moe_kernel, cap=cap),
        grid_spec=grid_spec,
        out_shape=jax.ShapeDtypeStruct((bsz, s, d), F32),
        compiler_params=_params(("arbitrary", "arbitrary"), VMEM_LIMIT),
        name="expert_choice_ffn",
    )(cnt.reshape(-1), pos.reshape(bsz, n_e, N_TOK_CHUNKS, -1), aff.reshape(bsz, n_e, N_TOK_CHUNKS, -1), h2,
      w1.astype(BF16), w3.astype(BF16), w2.astype(BF16), gate2, x1)


def kernel(x, c, w_mod, b_mod, g_norm1, w_in, conv_w, conv_b, lru_w_a, lru_b_a, lru_w_x, lru_b_x,
           lru_lambda, g_q, g_k, lambda_qk, g_attn_out, rel_bias, w_out, g_norm2, w_router, w1, w3, w2):
    bsz, s, d = x.shape
    depth = w_mod.shape[0]
    cap = max(1, EC_FACTOR * s // N_EXPERTS)
    tm = min(512, s)
    bias_tiles = _bias_tiles(rel_bias, min(ATT_TILE, s))
    for l in range(depth):
        mod = _modulation(c, w_mod[l], b_mod[l])
        shift1, scale1, gate1, shift2, scale2, gate2 = [m.reshape(bsz, 1, d) for m in jnp.split(mod, 6, axis=-1)]
        x_lru, gz, qn, kn, vt = _in_projection(x, scale1, shift1, g_norm1[l], w_in[l], g_q[l], g_k[l], tm)
        y_lru = _rg_lru(x_lru, gz, conv_w[l], conv_b[l], lru_w_a[l], lru_b_a[l], lru_w_x[l], lru_b_x[l],
                        lru_lambda[l])
        lam_init = 0.8 - 0.6 * math.exp(-0.3 * l)
        y_att = _diff_attention(qn, kn, vt, rel_bias, bias_tiles, lambda_qk[l], g_attn_out[l], lam_init)
        x1, h2, aff = _out_projection(y_lru, y_att, x, gate1, scale2, shift2, g_norm2[l], w_out[l],
                                      w_router[l], tm)
        pos, cnt = _routing(aff, cap)
        x = _moe(cnt[:, :, :N_TOK_CHUNKS], pos, aff, h2, w1[l], w3[l], w2[l], gate2, x1, cap)
    return x
```

```python
import functools
import math

import jax
import jax.numpy as jnp
from jax import lax
from jax.experimental import pallas as pl
from jax.experimental.pallas import tpu as pltpu

F32 = jnp.float32
BF16 = jnp.bfloat16

D_MODEL = 1024
D_LRU = 512
LRU_BLOCK = 64
LRU_C = 8.0
CONV_W = 4
N_HEADS = 4
HEAD_DV = 128
HEAD_DK = 64
D_ATT = N_HEADS * HEAD_DV
N_BUCKETS = 32
N_EXPERTS = 16
EC_FACTOR = 2
EPS = 1e-6
LOG2E = math.log2(math.e)

LRU_HALF = 256
SCAN_LANES = 8
ATT_TILE = 256
PV_ONES = 16
N_TOK_CHUNKS = 8
N_CAP_CHUNKS = 4
N_SCATTER_CHUNKS = 2
VMEM_LIMIT = 56 * 1024 * 1024

NT_DIMS = (((1,), (1,)), ((), ()))
TN_DIMS = (((0,), (0,)), ((), ()))


def _sigmoid(x):
    return 1.0 / (1.0 + jnp.exp(-x))


def _params(sem, vmem=None, flags=None):
    return pltpu.CompilerParams(dimension_semantics=sem, vmem_limit_bytes=vmem, flags=flags)


def _mod_kernel(c_ref, w_ref, b_ref, o_ref):
    c = c_ref[...]
    o_ref[...] = jnp.dot((c * _sigmoid(c)).astype(BF16), w_ref[...].astype(BF16),
                         preferred_element_type=F32) + b_ref[...]


def _modulation(c, w_mod, b_mod):
    bsz, d = c.shape
    n = w_mod.shape[1]
    return pl.pallas_call(
        _mod_kernel,
        grid=(n // d,),
        in_specs=[pl.BlockSpec((bsz, d), lambda j: (0, 0)),
                  pl.BlockSpec((d, d), lambda j: (0, j)),
                  pl.BlockSpec((1, d), lambda j: (0, j))],
        out_specs=pl.BlockSpec((bsz, d), lambda j: (0, j)),
        out_shape=jax.ShapeDtypeStruct((bsz, n), F32),
        compiler_params=_params(("arbitrary",)),
        name="adaln_mod",
    )(c, w_mod, b_mod.reshape(1, n))


def _bias_kernel(tab_ref, o_ref):
    h = pl.program_id(0)
    t = o_ref.shape[-1]
    key = lax.broadcasted_iota(jnp.int32, (t, t), 0)
    qry = lax.broadcasted_iota(jnp.int32, (t, t), 1)
    half = N_BUCKETS // 2
    max_exact = half // 2
    for d in range(5):
        if d == 0:
            o_ref[0, d] = jnp.full((t, t), tab_ref[half - 1, h] * LOG2E, F32)
        elif d == 4:
            o_ref[0, d] = jnp.full((t, t), tab_ref[N_BUCKETS - 1, h] * LOG2E, F32)
        else:
            rel = (d - 2) * t + key - qry
            n = jnp.abs(rel)
            n2 = n * n
            large = jnp.full((t, t), max_exact, jnp.int32)
            for k in range(1, half - max_exact):
                large = large + jnp.where(n2 >= (max_exact * max_exact) * (2 ** k), 1, 0)
            idx = jnp.where(n < max_exact, n, large) + jnp.where(rel > 0, half, 0)
            val = jnp.zeros((t, t), F32)
            for j in range(N_BUCKETS):
                val = jnp.where(idx == j, tab_ref[j, h] * LOG2E, val)
            o_ref[0, d] = val


def _bias_tiles(rel_bias, t):
    return pl.pallas_call(
        _bias_kernel,
        grid=(N_HEADS,),
        in_specs=[pl.BlockSpec(memory_space=pltpu.SMEM)],
        out_specs=pl.BlockSpec((1, 5, t, t), lambda h: (h, 0, 0, 0)),
        out_shape=jax.ShapeDtypeStruct((N_HEADS, 5, t, t), F32),
        compiler_params=_params(("arbitrary",)),
        name="t5_bias_tiles",
    )(rel_bias)


def _inproj_kernel(x_ref, sc_ref, sh_ref, g1_ref, w_ref, wvt_ref, mseg_ref, gq_ref, gk_ref,
                   xl_ref, gz_ref, q_ref, k_ref, vt_ref):
    x = x_ref[0]
    ms = jnp.mean(x * x, axis=-1, keepdims=True)
    h = (x * lax.rsqrt(ms + EPS) * g1_ref[...]) * (1.0 + sc_ref[0]) + sh_ref[0]
    hb = h.astype(BF16)

    def proj(lo, width):
        return jnp.dot(hb, w_ref[:, lo:lo + width], preferred_element_type=F32)

    def qk_norm(t, g):
        t2 = t * t
        hi = t2.astype(BF16)
        lo = (t2 - hi.astype(F32)).astype(BF16)
        ss = (jnp.dot(hi, mseg_ref[...], preferred_element_type=F32)
              + jnp.dot(lo, mseg_ref[...], preferred_element_type=F32))
        return t * lax.rsqrt(ss * (1.0 / HEAD_DK) + EPS) * g

    xl_ref[0] = proj(0, D_LRU)
    z = proj(D_LRU, D_LRU)
    cdf = 0.5 * (1.0 + jnp.tanh(math.sqrt(2.0 / math.pi) * (z + 0.044715 * (z * z * z))))
    gz_ref[0] = (z * cdf).astype(BF16)
    q_ref[0] = (qk_norm(proj(2 * D_LRU, D_ATT), gq_ref[...]) * (HEAD_DK ** -0.5 * LOG2E)).astype(BF16)
    k_ref[0] = qk_norm(proj(2 * D_LRU + D_ATT, D_ATT), gk_ref[...]).astype(BF16)
    vt_ref[0] = lax.dot_general(wvt_ref[...], hb, NT_DIMS,
                                preferred_element_type=F32).astype(BF16)


def _in_projection(x, scale1, shift1, g_norm1, w_in, g_q, g_k, tm):
    bsz, s, d = x.shape
    n = w_in.shape[1] - D_ATT
    w_main = w_in[:, :n].astype(BF16)
    w_vt = w_in[:, n:].T.astype(BF16)
    seg = jnp.arange(D_ATT, dtype=jnp.int32) // HEAD_DK
    mseg = (seg[:, None] == seg[None, :]).astype(BF16)
    n_sub = D_ATT // HEAD_DK
    row = lambda b, i: (b, i, 0)
    vec = lambda b, i: (b, 0, 0)
    full = lambda b, i: (0, 0)
    out_block = pl.BlockSpec((1, tm, D_LRU), row)
    return pl.pallas_call(
        _inproj_kernel,
        grid=(bsz, s // tm),
        in_specs=[pl.BlockSpec((1, tm, d), row),
                  pl.BlockSpec((1, 1, d), vec),
                  pl.BlockSpec((1, 1, d), vec),
                  pl.BlockSpec((1, d), full),
                  pl.BlockSpec((d, n), full),
                  pl.BlockSpec((D_ATT, d), full),
                  pl.BlockSpec((D_ATT, D_ATT), full),
                  pl.BlockSpec((1, D_ATT), full),
                  pl.BlockSpec((1, D_ATT), full)],
        out_specs=[out_block] * 4 + [pl.BlockSpec((1, D_ATT, tm), lambda b, i: (b, 0, i))],
        out_shape=[jax.ShapeDtypeStruct((bsz, s, D_LRU), F32)]
                  + [jax.ShapeDtypeStruct((bsz, s, D_LRU), BF16)] * 3
                  + [jax.ShapeDtypeStruct((bsz, D_ATT, s), BF16)],
        compiler_params=_params(("parallel", "parallel"), VMEM_LIMIT),
        name="norm1_in_proj",
    )(x, scale1, shift1, g_norm1.reshape(1, d), w_main, w_vt, mseg,
      jnp.tile(g_q, n_sub).reshape(1, D_ATT), jnp.tile(g_k, n_sub).reshape(1, D_ATT))


def _lru_kernel(x_ref, gz_ref, cw_ref, cb_ref, wa_ref, wx_ref, ba_ref, bx_ref, lam_ref, y_ref,
                xpad, a_f, u_f, a_b, u_b, *, tc):
    s = x_ref.shape[1]
    c = x_ref.shape[2]
    n_chunks = s // tc
    n_slab = c // 128
    pitch = a_f.shape[1] // n_chunks
    zeros8 = jnp.zeros((8, c), F32)
    xpad[0:8, :] = zeros8
    xpad[s + 8:s + 16, :] = zeros8

    def fill(ci, carry):
        t0 = pl.multiple_of(ci * tc, tc)
        xpad[pl.ds(t0 + 8, tc), :] = x_ref[0, pl.ds(t0, tc), :]
        return carry

    lax.fori_loop(0, n_chunks, fill, 0)

    cw = cw_ref[...]
    cb = cb_ref[...]
    decay = []
    for d in range(2):
        lam = lam_ref[d]
        softplus_neg = jnp.maximum(-lam, 0.0) + jnp.log(1.0 + jnp.exp(-jnp.abs(lam)))
        decay.append(-LRU_C * softplus_neg)
    a_scr = (a_f, a_b)
    u_scr = (u_f, u_b)

    def gates(ci, carry):
        t0 = pl.multiple_of(ci * tc, tc)
        xw = xpad[pl.ds(t0, tc + 16), :]
        xc = (cw[0:1] * pltpu.roll(xw, 2, 0)[8:8 + tc]
              + cw[1:2] * pltpu.roll(xw, 1, 0)[8:8 + tc]
              + cw[2:3] * xw[8:8 + tc]
              + cw[3:4] * pltpu.roll(xw, tc + 15, 0)[8:8 + tc]) + cb
        xcb = xc.astype(BF16)
        for d in range(2):
            r = _sigmoid(jnp.dot(xcb, wa_ref[d], preferred_element_type=F32) + ba_ref[d])
            i = _sigmoid(jnp.dot(xcb, wx_ref[d], preferred_element_type=F32) + bx_ref[d])
            a = jnp.exp(r * decay[d])
            u = jnp.sqrt(1.0 - a * a) * (i * xc)
            r0 = pl.multiple_of(ci * pitch, 8)
            for sl in range(n_slab):
                a_scr[d][sl, pl.ds(r0, tc), :] = a[:, sl * 128:(sl + 1) * 128]
                u_scr[d][sl, pl.ds(r0, tc), :] = u[:, sl * 128:(sl + 1) * 128]
        return carry

    lax.fori_loop(0, n_chunks, gates, 0)

    def rows(tt):
        return pl.ds(tt, n_chunks, stride=pitch)

    def step(tt, carry):
        out = []
        for d in range(2):
            t_loc = tt if d == 0 else tc - 1 - tt
            for sl in range(n_slab):
                h, p = carry[len(out)]
                a8 = a_scr[d][sl, rows(t_loc), :]
                h = a8 * h + u_scr[d][sl, rows(t_loc), :]
                p = a8 * p
                u_scr[d][sl, rows(t_loc), :] = h
                a_scr[d][sl, rows(t_loc), :] = p
                out.append((h, p))
        return tuple(out)

    init = (jnp.zeros((n_chunks, 128), F32), jnp.ones((n_chunks, 128), F32))
    lax.fori_loop(0, tc, step, (init,) * (2 * n_slab), unroll=4)

    for sl in range(n_slab):
        lanes = slice(sl * 128, (sl + 1) * 128)
        h_end, p_end = u_f[sl, rows(tc - 1), :], a_f[sl, rows(tc - 1), :]
        h_beg, p_beg = u_b[sl, rows(0), :], a_b[sl, rows(0), :]
        carry_f = [jnp.zeros((1, 128), F32)]
        for r in range(1, n_chunks):
            carry_f.append(p_end[r - 1:r] * carry_f[-1] + h_end[r - 1:r])
        carry_b = [jnp.zeros((1, 128), F32)]
        for r in range(n_chunks - 2, -1, -1):
            carry_b.insert(0, p_beg[r + 1:r + 2] * carry_b[0] + h_beg[r + 1:r + 2])
        for r in range(n_chunks):
            blk = slice(r * pitch, r * pitch + tc)
            hsum = (u_f[sl, blk, :] + a_f[sl, blk, :] * carry_f[r]
                    + u_b[sl, blk, :] + a_b[sl, blk, :] * carry_b[r])
            y_ref[0, r * tc:(r + 1) * tc, lanes] = (hsum * gz_ref[0, r * tc:(r + 1) * tc, lanes].astype(F32)).astype(BF16)


def _block_diag(w, half):
    n_dir, n_blocks, blk, _ = w.shape
    per = half // blk
    n_half = n_blocks // per
    w = w.reshape(n_dir, n_half, per, blk, blk)
    eye = jnp.eye(per, dtype=w.dtype)
    out = w[:, :, :, :, None, :] * eye[None, None, :, None, :, None]
    return out.reshape(n_dir, n_half, half, half)


def _rg_lru(x_lru, gz, conv_w, conv_b, w_a, b_a, w_x, b_x, lam):
    bsz, s, c = x_lru.shape
    half = LRU_HALF
    tc = s // SCAN_LANES
    pitch = tc + 8 if (tc // 8) % 2 == 0 else tc + 16
    n_half = c // half
    wa = _block_diag(w_a, half).astype(BF16)
    wx = _block_diag(w_x, half).astype(BF16)
    seq = lambda b, p: (b, 0, p)
    chan = lambda b, p: (0, p)
    dirchan = lambda b, p: (0, 0, p)
    blk = lambda b, p: (0, p, 0, 0)
    return pl.pallas_call(
        functools.partial(_lru_kernel, tc=tc),
        grid=(bsz, n_half),
        in_specs=[pl.BlockSpec((1, s, half), seq),
                  pl.BlockSpec((1, s, half), seq),
                  pl.BlockSpec((CONV_W, half), chan),
                  pl.BlockSpec((1, half), chan),
                  pl.BlockSpec((2, None, half, half), blk),
                  pl.BlockSpec((2, None, half, half), blk),
                  pl.BlockSpec((2, 1, half), dirchan),
                  pl.BlockSpec((2, 1, half), dirchan),
                  pl.BlockSpec((2, 1, half), dirchan)],
        out_specs=pl.BlockSpec((1, s, half), seq),
        out_shape=jax.ShapeDtypeStruct((bsz, s, c), BF16),
        scratch_shapes=[pltpu.VMEM((s + 16, half), F32)]
                       + [pltpu.VMEM((half // 128, SCAN_LANES * pitch, 128), F32)] * 4,
        compiler_params=_params(("parallel", "parallel"), VMEM_LIMIT),
        name="rg_lru",
    )(x_lru, gz, conv_w.reshape(CONV_W, c), conv_b.reshape(1, c), wa, wx,
      b_a.reshape(2, 1, c), b_x.reshape(2, 1, c), lam.reshape(2, 1, c))


def _attn_kernel(tab_ref, q_ref, k_ref, vt_ref, bias_ref, lq_ref, go_ref, o_ref,
                 s_even, s_odd, m_even, m_odd, o_even, o_odd, p_scr, *, lam_init):
    h = pl.program_id(1)
    t = bias_ref.shape[-1]
    n_k = k_ref.shape[1] // t
    n_q = q_ref.shape[1] // t
    chunk = 2
    n_c = n_k // chunk
    c_left = tab_ref[N_BUCKETS // 2 - 1, h] * LOG2E
    c_right = tab_ref[N_BUCKETS - 1, h] * LOG2E

    s_odd[...] = jnp.zeros_like(s_odd)
    m_odd[...] = jnp.zeros_like(m_odd)
    o_odd[...] = jnp.ones_like(o_odd)
    even = dict(s=s_even, m=m_even, o=o_even)
    odd = dict(s=s_odd, m=m_odd, o=o_odd)
    ones_rows = jnp.ones((PV_ONES, chunk * t), BF16)

    def fold(x, op):
        parts = [x[r * 8:(r + 1) * 8, :] for r in range(t // 8)]
        acc = parts[:2]
        for r in range(2, len(parts)):
            acc[r % 2] = op(acc[r % 2], parts[r])
        return op(acc[0], acc[1])

    def far_bias(j, tile):
        return jnp.where(j < tile, c_left, c_right)

    def is_far(j, tile):
        return jnp.abs(j - tile) >= 2

    def finish(prev, tile):
        outs = [prev['o'][u, :HEAD_DV, :] / prev['o'][u, HEAD_DV:HEAD_DV + 1, :] for u in range(2)]
        lq = lq_ref[...]
        lam = (jnp.exp(jnp.sum(lq[0:1] * lq[1:2], axis=-1, keepdims=True))
               - jnp.exp(jnp.sum(lq[2:3] * lq[3:4], axis=-1, keepdims=True)) + lam_init)
        o = (outs[0] - lam * outs[1]).T
        ms = jnp.mean(o * o, axis=-1, keepdims=True)
        o_ref[0, pl.ds(pl.multiple_of(tile * t, t), t), :] = (
            (o * lax.rsqrt(ms + EPS) * go_ref[...]) * (1.0 - lam_init)).astype(BF16)

    def step(i, cur, prev, scores=True, softmax=True):
        tile_a = i
        tile_b = jnp.maximum(i - 1, 0)
        finish(prev, jnp.maximum(i - 2, 0))
        s_cur, s_prev = cur['s'], prev['s']
        if softmax:
            m_fin = [prev['m'][u] for u in range(2)]
            for dj in (-1, 0, 1):
                j = tile_b + dj
                valid = jnp.logical_and(j >= 0, j < n_k)
                jc = jnp.clip(j, 0, n_k - 1)
                bias = jnp.where(valid, bias_ref[0, dj + 2], 0.0)
                for u in range(2):
                    sc = s_prev[u, jc] + bias
                    s_prev[u, jc] = sc
                    m_fin[u] = jnp.maximum(m_fin[u], fold(sc, jnp.maximum) + jnp.where(valid, 0.0, -jnp.inf))
            m_b = [jnp.max(m_fin[u], axis=0, keepdims=True) for u in range(2)]
        m_acc = [jnp.full((8, t), -jnp.inf, F32) for _ in range(2)]
        if scores:
            q = q_ref[0, pl.ds(pl.multiple_of(tile_a * t, t), t), :]
            lane = lax.broadcasted_iota(jnp.int32, q.shape, 1)
            zero = jnp.zeros_like(q)
            q_sub = (jnp.where(lane < HEAD_DK, q, zero), jnp.where(lane >= HEAD_DK, q, zero))
        for c in range(n_c):
            rows = slice(c * chunk * t, (c + 1) * chunk * t)
            for u in range(2 if scores else 0):
                sc_all = lax.dot_general(k_ref[0, rows, :], q_sub[u], NT_DIMS, preferred_element_type=F32)
                for jj in range(chunk):
                    j = c * chunk + jj
                    sc = sc_all[jj * t:(jj + 1) * t, :]
                    s_cur[u, j] = sc
                    m_acc[u] = jnp.maximum(m_acc[u], fold(sc, jnp.maximum)
                                           + jnp.where(is_far(j, tile_a), far_bias(j, tile_a), -jnp.inf))
            for u in range(2 if softmax else 0):
                for jj in range(chunk):
                    j = c * chunk + jj
                    off = m_b[u] - jnp.where(is_far(j, tile_b), far_bias(j, tile_b), 0.0)
                    p = jnp.exp2(s_prev[u, j] - off)
                    p_scr[u, j * t:(j + 1) * t, :] = p.astype(BF16)
                part = jnp.dot(jnp.concatenate([vt_ref[0, :, rows], ones_rows], axis=0), p_scr[u, rows, :],
                               preferred_element_type=F32)
                if c == 0:
                    cur['o'][u] = part
                else:
                    cur['o'][u] += part

        for u in range(2 if scores else 0):
            cur['m'][u] = m_acc[u]

    def pair(pi, carry):
        @pl.when(pi >= 0)
        def _():
            step(2 * pi, even, odd)

        @pl.when(pi < n_q)
        def _():
            step(2 * pi + 1, odd, even)

        return carry

    lax.fori_loop(0, n_q // 2, pair, 0)
    step(n_q, even, odd, scores=False)
    finish(even, n_q - 1)


def _diff_attention(qn, kn, vt, rel_bias, bias_tiles, lambda_qk, g_o, lam_init):
    bsz, s, _ = qn.shape
    t = bias_tiles.shape[-1]
    n_k = s // t
    assert (s // t) % 2 == 0, "the query-tile pipeline advances two tiles per loop trip"
    scores = pltpu.VMEM((2, n_k, t, t), F32)
    maxima = pltpu.VMEM((2, 8, t), F32)
    pv_acc = pltpu.VMEM((2, HEAD_DV + PV_ONES, t), F32)
    seq = lambda b, h: (b, 0, h)
    return pl.pallas_call(
        functools.partial(_attn_kernel, lam_init=lam_init),
        grid=(bsz, N_HEADS),
        in_specs=[pl.BlockSpec(memory_space=pltpu.SMEM),
                  pl.BlockSpec((1, s, HEAD_DV), seq),
                  pl.BlockSpec((1, s, HEAD_DV), seq),
                  pl.BlockSpec((1, HEAD_DV, s), lambda b, h: (b, h, 0)),
                  pl.BlockSpec((1, 5, t, t), lambda b, h: (h, 0, 0, 0)),
                  pl.BlockSpec((4, HEAD_DK), lambda b, h: (0, 0)),
                  pl.BlockSpec((1, HEAD_DV), lambda b, h: (0, 0))],
        out_specs=pl.BlockSpec((1, s, HEAD_DV), seq),
        out_shape=jax.ShapeDtypeStruct((bsz, s, D_ATT), BF16),
        scratch_shapes=[scores, scores, maxima, maxima, pv_acc, pv_acc, pltpu.VMEM((2, s, t), BF16)],
        compiler_params=_params(("parallel", "parallel"), VMEM_LIMIT),
        name="diff_attention",
    )(rel_bias, qn, kn, vt, bias_tiles, lambda_qk, g_o.reshape(1, HEAD_DV))


def _outproj_kernel(yl_ref, ya_ref, x_ref, gate_ref, sc_ref, sh_ref, g2_ref, wo_ref, wr_ref,
                    x1_ref, h2_ref, aff_ref):
    mix = (jnp.dot(yl_ref[0], wo_ref[0:D_LRU, :], preferred_element_type=F32)
           + jnp.dot(ya_ref[0], wo_ref[D_LRU:D_LRU + D_ATT, :], preferred_element_type=F32))
    x1 = x_ref[0] + gate_ref[0] * mix
    x1_ref[0] = x1
    ms = jnp.mean(x1 * x1, axis=-1, keepdims=True)
    h2 = (x1 * lax.rsqrt(ms + EPS) * g2_ref[...]) * (1.0 + sc_ref[0]) + sh_ref[0]
    h2b = h2.astype(BF16)
    h2_ref[0] = h2b
    logits = lax.dot_general(wr_ref[...], h2b, NT_DIMS, preferred_element_type=F32)
    ex = jnp.exp(logits - jnp.max(logits, axis=0, keepdims=True))
    aff_ref[0] = ex / jnp.sum(ex, axis=0, keepdims=True)


def _out_projection(y_lru, y_att, x, gate1, scale2, shift2, g_norm2, w_out, w_router, tm):
    bsz, s, d = x.shape
    row = lambda b, i: (b, i, 0)
    vec = lambda b, i: (b, 0, 0)
    full = lambda b, i: (0, 0)
    return pl.pallas_call(
        _outproj_kernel,
        grid=(bsz, s // tm),
        in_specs=[pl.BlockSpec((1, tm, D_LRU), row),
                  pl.BlockSpec((1, tm, D_ATT), row),
                  pl.BlockSpec((1, tm, d), row),
                  pl.BlockSpec((1, 1, d), vec),
                  pl.BlockSpec((1, 1, d), vec),
                  pl.BlockSpec((1, 1, d), vec),
                  pl.BlockSpec((1, d), full),
                  pl.BlockSpec((D_LRU + D_ATT, d), full),
                  pl.BlockSpec((N_EXPERTS, d), full)],
        out_specs=[pl.BlockSpec((1, tm, d), row),
                   pl.BlockSpec((1, tm, d), row),
                   pl.BlockSpec((1, N_EXPERTS, tm), lambda b, i: (b, 0, i))],
        out_shape=[jax.ShapeDtypeStruct((bsz, s, d), F32),
                   jax.ShapeDtypeStruct((bsz, s, d), BF16),
                   jax.ShapeDtypeStruct((bsz, N_EXPERTS, s), F32)],
        compiler_params=_params(("parallel", "parallel"), VMEM_LIMIT),
        name="out_proj_norm2_router",
    )(y_lru, y_att, x, gate1, scale2, shift2, g_norm2.reshape(1, d), w_out.astype(BF16),
      w_router.T.astype(BF16))


def _route_kernel(aff_ref, pos_ref, cnt_ref, *, cap, n_tok_chunks):
    aff = aff_ref[0]
    n_e, s = aff.shape
    bits = lax.bitcast_convert_type(aff, jnp.int32)
    capf = float(cap)

    def count(mask):
        return jnp.sum(jnp.where(mask, 1.0, 0.0), axis=-1, keepdims=True)

    tau = jnp.zeros((n_e, 1), jnp.int32)
    for bit in range(30, -1, -1):
        cand = tau | (1 << bit)
        tau = jnp.where(count(bits >= cand) >= capf, cand, tau)
    gt = bits > tau
    eq = bits == tau
    need = capf - count(gt)

    blk = 256
    r = lax.broadcasted_iota(jnp.int32, (blk, blk), 0)
    cidx = lax.broadcasted_iota(jnp.int32, (blk, blk), 1)
    upper = jnp.where(r < cidx, 1.0, 0.0).astype(BF16)

    def prefix_blocks(mask):
        off = jnp.zeros((n_e, 1), F32)
        pieces, offs = [], []
        for k in range(s // blk):
            mb = jnp.where(mask[:, k * blk:(k + 1) * blk], 1.0, 0.0)
            offs.append(off)
            pieces.append(jnp.dot(mb.astype(BF16), upper, preferred_element_type=F32) + off)
            off = off + jnp.sum(mb, axis=-1, keepdims=True)
        return pieces, offs

    eq_rank, _ = prefix_blocks(eq)
    sel_blocks = []
    for k in range(s // blk):
        sl = slice(k * blk, (k + 1) * blk)
        sel_blocks.append(jnp.logical_or(gt[:, sl], jnp.logical_and(eq[:, sl], eq_rank[k] < need)))
    sel = jnp.concatenate(sel_blocks, axis=1)
    slot, offs = prefix_blocks(sel)
    for k in range(s // blk):
        pos_ref[0, :, k * blk:(k + 1) * blk] = jnp.where(sel_blocks[k], slot[k], -1.0).astype(jnp.int32)

    lane = lax.broadcasted_iota(jnp.int32, (n_e, 128), 1)
    cnt = jnp.zeros((n_e, 128), F32)
    per = (s // n_tok_chunks) // blk
    for j in range(n_tok_chunks):
        cnt = jnp.where(lane == j, offs[j * per], cnt)
    cnt_ref[0] = cnt.astype(jnp.int32)


def _routing(aff, cap):
    bsz, n_e, s = aff.shape
    return pl.pallas_call(
        functools.partial(_route_kernel, cap=cap, n_tok_chunks=N_TOK_CHUNKS),
        grid=(bsz,),
        in_specs=[pl.BlockSpec((1, n_e, s), lambda b: (b, 0, 0))],
        out_specs=[pl.BlockSpec((1, n_e, s), lambda b: (b, 0, 0)),
                   pl.BlockSpec((1, n_e, 128), lambda b: (b, 0, 0))],
        out_shape=[jax.ShapeDtypeStruct((bsz, n_e, s), jnp.int32),
                   jax.ShapeDtypeStruct((bsz, n_e, 128), jnp.int32)],
        compiler_params=_params(("parallel",)),
        name="expert_choice_routing",
    )(aff)


def _moe_kernel(cnt_ref, pos_ref, aff_ref, h2_ref, w1_ref, w3_ref, w2_ref, gate_ref, x1_hbm, out_ref,
                xe_scr, g_scr, y_scr, x1_sem, *, cap):
    b = pl.program_id(0)
    e = pl.program_id(1)
    n_e = pl.num_programs(1)
    s = h2_ref.shape[1]
    tchunk = s // N_TOK_CHUNKS

    def residual_copy():
        return pltpu.make_async_copy(x1_hbm.at[b], out_ref.at[0], x1_sem.at[0])

    @pl.when(e == 0)
    def _():
        residual_copy().start()

    xe_scr[...] = jnp.zeros_like(xe_scr)
    g_scr[...] = jnp.zeros_like(g_scr)
    base = (b * n_e + e) * N_TOK_CHUNKS

    def walk(cchunk, visit):
        n_cap = cap // cchunk
        slot_iota = lax.broadcasted_iota(jnp.int32, (cchunk, tchunk), 0)
        j = jnp.int32(0)
        i = jnp.int32(0)
        for _ in range(N_TOK_CHUNKS + n_cap - 1):
            jc = jnp.minimum(j, N_TOK_CHUNKS - 1)
            ic = jnp.minimum(i, n_cap - 1)
            c_lo = cnt_ref[base + jc]
            c_hi = jnp.where(jc + 1 < N_TOK_CHUNKS, cnt_ref[base + jnp.minimum(jc + 1, N_TOK_CHUNKS - 1)], cap)
            s_hi = (ic + 1) * cchunk
            valid = jnp.logical_and(jnp.maximum(c_lo, ic * cchunk) < jnp.minimum(c_hi, s_hi),
                                    jnp.logical_and(j < N_TOK_CHUNKS, i < n_cap))
            slot0 = jnp.where(valid, ic * cchunk, -2 * cap)
            sel = pos_ref[0, 0, pl.ds(jc, 1), :] == slot_iota + slot0
            visit(sel, jc, pl.ds(pl.multiple_of(ic * cchunk, cchunk), cchunk),
                  pl.ds(pl.multiple_of(jc * tchunk, tchunk), tchunk))
            j = j + jnp.where(c_hi <= s_hi, 1, 0)
            i = i + jnp.where(s_hi <= c_hi, 1, 0)

    def gather(sel, jc, rows, toks):
        xe_scr[rows, :] += jnp.dot(jnp.where(sel, 1.0, 0.0).astype(BF16), h2_ref[0, toks, :],
                                   preferred_element_type=F32)
        g_scr[rows, :] += jnp.sum(jnp.where(sel, aff_ref[0, 0, pl.ds(jc, 1), :], 0.0), axis=-1, keepdims=True)

    walk(cap // N_CAP_CHUNKS, gather)

    xe = xe_scr[...].astype(BF16)
    a = jnp.dot(xe, w1_ref[0], preferred_element_type=F32)
    gate = jnp.dot(xe, w3_ref[0], preferred_element_type=F32)
    hmid = ((a * _sigmoid(a)) * gate).astype(BF16)
    y = jnp.dot(hmid, w2_ref[0], preferred_element_type=F32) * g_scr[...] * gate_ref[0]
    y_scr[...] = y.astype(BF16)

    @pl.when(e == 0)
    def _():
        residual_copy().wait()

    def scatter(sel, jc, rows, toks):
        out_ref[0, toks, :] += lax.dot_general(jnp.where(sel, 1.0, 0.0).astype(BF16), y_scr[rows, :],
                                               TN_DIMS, preferred_element_type=F32)

    walk(cap // N_SCATTER_CHUNKS, scatter)


def _moe(cnt, pos, aff, h2, w1, w3, w2, gate2, x1, cap):
    bsz, s, d = h2.shape
    n_e = w1.shape[0]
    f = w1.shape[2]
    tok_row = lambda b, e, c: (b, e, 0, 0)
    wspec = lambda b, e, c: (e, 0, 0)
    resident = lambda b, e, c: (b, 0, 0)
    grid_spec = pltpu.PrefetchScalarGridSpec(
        num_scalar_prefetch=1,
        grid=(bsz, n_e),
        in_specs=[pl.BlockSpec((1, 1, N_TOK_CHUNKS, s // N_TOK_CHUNKS), tok_row),
                  pl.BlockSpec((1, 1, N_TOK_CHUNKS, s // N_TOK_CHUNKS), tok_row),
                  pl.BlockSpec((1, s, d), resident, pipeline_mode=pl.Buffered(1)),
                  pl.BlockSpec((1, d, f), wspec),
                  pl.BlockSpec((1, d, f), wspec),
                  pl.BlockSpec((1, f, d), wspec),
                  pl.BlockSpec((1, 1, d), resident),
                  pl.BlockSpec(memory_space=pl.ANY)],
        out_specs=pl.BlockSpec((1, s, d), resident, pipeline_mode=pl.Buffered(1)),
        scratch_shapes=[pltpu.VMEM((cap, d), F32), pltpu.VMEM((cap, 1), F32), pltpu.VMEM((cap, d), BF16),
                        pltpu.SemaphoreType.DMA((1,))],
    )
    return pl.pallas_call(
        functools.partial(_moe_kernel, cap=cap),
        grid_spec=grid_spec,
        out_shape=jax.ShapeDtypeStruct((bsz, s, d), F32),
        compiler_params=_params(("arbitrary", "arbitrary"), VMEM_LIMIT),
        name="expert_choice_ffn",
    )(cnt.reshape(-1), pos.reshape(bsz, n_e, N_TOK_CHUNKS, -1), aff.reshape(bsz, n_e, N_TOK_CHUNKS, -1), h2,
      w1.astype(BF16), w3.astype(BF16), w2.astype(BF16), gate2, x1)


def kernel(x, c, w_mod, b_mod, g_norm1, w_in, conv_w, conv_b, lru_w_a, lru_b_a, lru_w_x, lru_b_x,
           lru_lambda, g_q, g_k, lambda_qk, g_attn_out, rel_bias, w_out, g_norm2, w_router, w1, w3, w2):
    bsz, s, d = x.shape
    depth = w_mod.shape[0]
    cap = max(1, EC_FACTOR * s // N_EXPERTS)
    tm = min(512, s)
    bias_tiles = _bias_tiles(rel_bias, min(ATT_TILE, s))
    for l in range(depth):
        mod = _modulation(c, w_mod[l], b_mod[l])
        shift1, scale1, gate1, shift2, scale2, gate2 = [m.reshape(bsz, 1, d) for m in jnp.split(mod, 6, axis=-1)]
        x_lru, gz, qn, kn, vt = _in_projection(x, scale1, shift1, g_norm1[l], w_in[l], g_q[l], g_k[l], tm)
        y_lru = _rg_lru(x_lru, gz, conv_w[l], conv_b[l], lru_w_a[l], lru_b_a[l], lru_w_x[l], lru_b_x[l],
                        lru_lambda[l])
        lam_init = 0.8 - 0.6 * math.exp(-0.3 * l)
        y_att = _diff_attention(qn, kn, vt, rel_bias, bias_tiles, lambda_qk[l], g_attn_out[l], lam_init)
        x1, h2, aff = _out_projection(y_lru, y_att, x, gate1, scale2, shift2, g_norm2[l], w_out[l],
                                      w_router[l], tm)
        pos, cnt = _routing(aff, cap)
        x = _moe(cnt[:, :, :N_TOK_CHUNKS], pos, aff, h2, w1[l], w3[l], w2[l], gate2, x1, cap)
    return x
```

```python
import functools
import math

import jax
import jax.numpy as jnp
from jax import lax
from jax.experimental import pallas as pl
from jax.experimental.pallas import tpu as pltpu

F32 = jnp.float32
BF16 = jnp.bfloat16

D_MODEL = 1024
D_LRU = 512
LRU_BLOCK = 64
LRU_C = 8.0
CONV_W = 4
N_HEADS = 4
HEAD_DV = 128
HEAD_DK = 64
D_ATT = N_HEADS * HEAD_DV
N_BUCKETS = 32
N_EXPERTS = 16
EC_FACTOR = 2
EPS = 1e-6
LOG2E = math.log2(math.e)

IN_PROJ_ROWS = 1024
LRU_HALF = 256
SCAN_LANES = 8
ATT_TILE = 256
PV_ONES = 16
N_TOK_CHUNKS = 16
N_CAP_CHUNKS = 4
N_SCATTER_CHUNKS = 2
VMEM_LIMIT = 56 * 1024 * 1024

NT_DIMS = (((1,), (1,)), ((), ()))
TN_DIMS = (((0,), (0,)), ((), ()))


def _sigmoid(x):
    return 1.0 / (1.0 + jnp.exp(-x))


def _params(sem, vmem=None, flags=None):
    return pltpu.CompilerParams(dimension_semantics=sem, vmem_limit_bytes=vmem, flags=flags)


def _mod_kernel(c_ref, w_ref, b_ref, o_ref):
    c = c_ref[...]
    o_ref[...] = jnp.dot((c * _sigmoid(c)).astype(BF16), w_ref[...].astype(BF16),
                         preferred_element_type=F32) + b_ref[...]


def _modulation(c, w_mod, b_mod):
    bsz, d = c.shape
    n = w_mod.shape[1]
    return pl.pallas_call(
        _mod_kernel,
        grid=(n // d,),
        in_specs=[pl.BlockSpec((bsz, d), lambda j: (0, 0)),
                  pl.BlockSpec((d, d), lambda j: (0, j)),
                  pl.BlockSpec((1, d), lambda j: (0, j))],
        out_specs=pl.BlockSpec((bsz, d), lambda j: (0, j)),
        out_shape=jax.ShapeDtypeStruct((bsz, n), F32),
        compiler_params=_params(("arbitrary",)),
        name="adaln_mod",
    )(c, w_mod, b_mod.reshape(1, n))


def _bias_kernel(tab_ref, o_ref):
    h = pl.program_id(0)
    t = o_ref.shape[-1]
    key = lax.broadcasted_iota(jnp.int32, (t, t), 0)
    qry = lax.broadcasted_iota(jnp.int32, (t, t), 1)
    half = N_BUCKETS // 2
    max_exact = half // 2
    for d in range(5):
        if d == 0:
            o_ref[0, d] = jnp.full((t, t), tab_ref[half - 1, h] * LOG2E, F32)
        elif d == 4:
            o_ref[0, d] = jnp.full((t, t), tab_ref[N_BUCKETS - 1, h] * LOG2E, F32)
        else:
            rel = (d - 2) * t + key - qry
            n = jnp.abs(rel)
            n2 = n * n
            large = jnp.full((t, t), max_exact, jnp.int32)
            for k in range(1, half - max_exact):
                large = large + jnp.where(n2 >= (max_exact * max_exact) * (2 ** k), 1, 0)
            idx = jnp.where(n < max_exact, n, large) + jnp.where(rel > 0, half, 0)
            val = jnp.zeros((t, t), F32)
            for j in range(N_BUCKETS):
                val = jnp.where(idx == j, tab_ref[j, h] * LOG2E, val)
            o_ref[0, d] = val


def _bias_tiles(rel_bias, t):
    return pl.pallas_call(
        _bias_kernel,
        grid=(N_HEADS,),
        in_specs=[pl.BlockSpec(memory_space=pltpu.SMEM)],
        out_specs=pl.BlockSpec((1, 5, t, t), lambda h: (h, 0, 0, 0)),
        out_shape=jax.ShapeDtypeStruct((N_HEADS, 5, t, t), F32),
        compiler_params=_params(("arbitrary",)),
        name="t5_bias_tiles",
    )(rel_bias)


def _inproj_kernel(x_ref, sc_ref, sh_ref, g1_ref, w_ref, wvt_ref, mseg_ref, gq_ref, gk_ref,
                   xl_ref, gz_ref, q_ref, k_ref, vt_ref):
    x = x_ref[0]
    ms = jnp.mean(x * x, axis=-1, keepdims=True)
    h = (x * lax.rsqrt(ms + EPS) * g1_ref[...]) * (1.0 + sc_ref[0]) + sh_ref[0]
    hb = h.astype(BF16)

    def proj(lo, width):
        return jnp.dot(hb, w_ref[:, lo:lo + width], preferred_element_type=F32)

    def qk_norm(t, g):
        ss = jnp.dot((t * t).astype(BF16), mseg_ref[...], preferred_element_type=F32)
        return t * lax.rsqrt(ss * (1.0 / HEAD_DK) + EPS) * g

    xl_ref[0] = proj(0, D_LRU)
    z = proj(D_LRU, D_LRU)
    cdf = 0.5 * (1.0 + jnp.tanh(math.sqrt(2.0 / math.pi) * (z + 0.044715 * (z * z * z))))
    gz_ref[0] = (z * cdf).astype(BF16)
    q_ref[0] = (qk_norm(proj(2 * D_LRU, D_ATT), gq_ref[...]) * (HEAD_DK ** -0.5 * LOG2E)).astype(BF16)
    k_ref[0] = qk_norm(proj(2 * D_LRU + D_ATT, D_ATT), gk_ref[...]).astype(BF16)
    vt_ref[0] = lax.dot_general(wvt_ref[...], hb, NT_DIMS,
                                preferred_element_type=F32).astype(BF16)


def _in_projection(x, scale1, shift1, g_norm1, w_in, g_q, g_k, tm):
    bsz, s, d = x.shape
    n = w_in.shape[1] - D_ATT
    w_main = w_in[:, :n].astype(BF16)
    w_vt = w_in[:, n:].T.astype(BF16)
    seg = jnp.arange(D_ATT, dtype=jnp.int32) // HEAD_DK
    mseg = (seg[:, None] == seg[None, :]).astype(BF16)
    n_sub = D_ATT // HEAD_DK
    row = lambda b, i: (b, i, 0)
    vec = lambda b, i: (b, 0, 0)
    full = lambda b, i: (0, 0)
    out_block = pl.BlockSpec((1, tm, D_LRU), row)
    return pl.pallas_call(
        _inproj_kernel,
        grid=(bsz, s // tm),
        in_specs=[pl.BlockSpec((1, tm, d), row),
                  pl.BlockSpec((1, 1, d), vec),
                  pl.BlockSpec((1, 1, d), vec),
                  pl.BlockSpec((1, d), full),
                  pl.BlockSpec((d, n), full),
                  pl.BlockSpec((D_ATT, d), full),
                  pl.BlockSpec((D_ATT, D_ATT), full),
                  pl.BlockSpec((1, D_ATT), full),
                  pl.BlockSpec((1, D_ATT), full)],
        out_specs=[out_block] * 4 + [pl.BlockSpec((1, D_ATT, tm), lambda b, i: (b, 0, i))],
        out_shape=[jax.ShapeDtypeStruct((bsz, s, D_LRU), F32)]
                  + [jax.ShapeDtypeStruct((bsz, s, D_LRU), BF16)] * 3
                  + [jax.ShapeDtypeStruct((bsz, D_ATT, s), BF16)],
        compiler_params=_params(("parallel", "parallel"), VMEM_LIMIT),
        name="norm1_in_proj",
    )(x, scale1, shift1, g_norm1.reshape(1, d), w_main, w_vt, mseg,
      jnp.tile(g_q, n_sub).reshape(1, D_ATT), jnp.tile(g_k, n_sub).reshape(1, D_ATT))


def _lru_kernel(x_ref, gz_ref, cw_ref, cb_ref, wa_ref, wx_ref, ba_ref, bx_ref, lam_ref, y_ref,
                xpad, a_f, u_f, a_b, u_b, *, tc):
    s = x_ref.shape[1]
    c = x_ref.shape[2]
    n_chunks = s // tc
    n_slab = c // 128
    pitch = a_f.shape[1] // n_chunks
    zeros8 = jnp.zeros((8, c), F32)
    xpad[0:8, :] = zeros8
    xpad[s + 8:s + 16, :] = zeros8

    def fill(ci, carry):
        t0 = pl.multiple_of(ci * tc, tc)
        xpad[pl.ds(t0 + 8, tc), :] = x_ref[0, pl.ds(t0, tc), :]
        return carry

    lax.fori_loop(0, n_chunks, fill, 0)

    cw = cw_ref[...]
    cb = cb_ref[...]
    decay = []
    for d in range(2):
        lam = lam_ref[d]
        softplus_neg = jnp.maximum(-lam, 0.0) + jnp.log(1.0 + jnp.exp(-jnp.abs(lam)))
        decay.append(-LRU_C * softplus_neg)
    a_scr = (a_f, a_b)
    u_scr = (u_f, u_b)

    def gates(ci, carry):
        t0 = pl.multiple_of(ci * tc, tc)
        xw = xpad[pl.ds(t0, tc + 16), :]
        xc = (cw[0:1] * pltpu.roll(xw, 2, 0)[8:8 + tc]
              + cw[1:2] * pltpu.roll(xw, 1, 0)[8:8 + tc]
              + cw[2:3] * xw[8:8 + tc]
              + cw[3:4] * pltpu.roll(xw, tc + 15, 0)[8:8 + tc]) + cb
        xcb = xc.astype(BF16)
        for d in range(2):
            r = _sigmoid(jnp.dot(xcb, wa_ref[d], preferred_element_type=F32) + ba_ref[d])
            i = _sigmoid(jnp.dot(xcb, wx_ref[d], preferred_element_type=F32) + bx_ref[d])
            a = jnp.exp(r * decay[d])
            u = jnp.sqrt(1.0 - a * a) * (i * xc)
            r0 = pl.multiple_of(ci * pitch, 8)
            for sl in range(n_slab):
                a_scr[d][sl, pl.ds(r0, tc), :] = a[:, sl * 128:(sl + 1) * 128]
                u_scr[d][sl, pl.ds(r0, tc), :] = u[:, sl * 128:(sl + 1) * 128]
        return carry

    lax.fori_loop(0, n_chunks, gates, 0)

    def rows(tt):
        return pl.ds(tt, n_chunks, stride=pitch)

    def step(tt, carry):
        out = []
        for d in range(2):
            t_loc = tt if d == 0 else tc - 1 - tt
            for sl in range(n_slab):
                h, p = carry[len(out)]
                a8 = a_scr[d][sl, rows(t_loc), :]
                h = a8 * h + u_scr[d][sl, rows(t_loc), :]
                p = a8 * p
                u_scr[d][sl, rows(t_loc), :] = h
                a_scr[d][sl, rows(t_loc), :] = p
                out.append((h, p))
        return tuple(out)

    init = (jnp.zeros((n_chunks, 128), F32), jnp.ones((n_chunks, 128), F32))
    lax.fori_loop(0, tc, step, (init,) * (2 * n_slab), unroll=4)

    for sl in range(n_slab):
        lanes = slice(sl * 128, (sl + 1) * 128)
        h_end, p_end = u_f[sl, rows(tc - 1), :], a_f[sl, rows(tc - 1), :]
        h_beg, p_beg = u_b[sl, rows(0), :], a_b[sl, rows(0), :]
        carry_f = [jnp.zeros((1, 128), F32)]
        for r in range(1, n_chunks):
            carry_f.append(p_end[r - 1:r] * carry_f[-1] + h_end[r - 1:r])
        carry_b = [jnp.zeros((1, 128), F32)]
        for r in range(n_chunks - 2, -1, -1):
            carry_b.insert(0, p_beg[r + 1:r + 2] * carry_b[0] + h_beg[r + 1:r + 2])
        for r in range(n_chunks):
            blk = slice(r * pitch, r * pitch + tc)
            hsum = (u_f[sl, blk, :] + a_f[sl, blk, :] * carry_f[r]
                    + u_b[sl, blk, :] + a_b[sl, blk, :] * carry_b[r])
            y_ref[0, r * tc:(r + 1) * tc, lanes] = (hsum * gz_ref[0, r * tc:(r + 1) * tc, lanes].astype(F32)).astype(BF16)


def _block_diag(w, half):
    n_dir, n_blocks, blk, _ = w.shape
    per = half // blk
    n_half = n_blocks // per
    w = w.reshape(n_dir, n_half, per, blk, blk)
    eye = jnp.eye(per, dtype=w.dtype)
    out = w[:, :, :, :, None, :] * eye[None, None, :, None, :, None]
    return out.reshape(n_dir, n_half, half, half)


def _rg_lru(x_lru, gz, conv_w, conv_b, w_a, b_a, w_x, b_x, lam):
    bsz, s, c = x_lru.shape
    half = LRU_HALF
    tc = s // SCAN_LANES
    pitch = tc + 8 if (tc // 8) % 2 == 0 else tc + 16
    n_half = c // half
    wa = _block_diag(w_a, half).astype(BF16)
    wx = _block_diag(w_x, half).astype(BF16)
    seq = lambda b, p: (b, 0, p)
    chan = lambda b, p: (0, p)
    dirchan = lambda b, p: (0, 0, p)
    blk = lambda b, p: (0, p, 0, 0)
    return pl.pallas_call(
        functools.partial(_lru_kernel, tc=tc),
        grid=(bsz, n_half),
        in_specs=[pl.BlockSpec((1, s, half), seq),
                  pl.BlockSpec((1, s, half), seq),
                  pl.BlockSpec((CONV_W, half), chan),
                  pl.BlockSpec((1, half), chan),
                  pl.BlockSpec((2, None, half, half), blk),
                  pl.BlockSpec((2, None, half, half), blk),
                  pl.BlockSpec((2, 1, half), dirchan),
                  pl.BlockSpec((2, 1, half), dirchan),
                  pl.BlockSpec((2, 1, half), dirchan)],
        out_specs=pl.BlockSpec((1, s, half), seq),
        out_shape=jax.ShapeDtypeStruct((bsz, s, c), BF16),
        scratch_shapes=[pltpu.VMEM((s + 16, half), F32)]
                       + [pltpu.VMEM((half // 128, SCAN_LANES * pitch, 128), F32)] * 4,
        compiler_params=_params(("parallel", "parallel"), VMEM_LIMIT),
        name="rg_lru",
    )(x_lru, gz, conv_w.reshape(CONV_W, c), conv_b.reshape(1, c), wa, wx,
      b_a.reshape(2, 1, c), b_x.reshape(2, 1, c), lam.reshape(2, 1, c))


def _attn_kernel(tab_ref, q_ref, k_ref, vt_ref, bias_ref, lq_ref, go_ref, o_ref,
                 s_even, s_odd, m_even, m_odd, o_even, o_odd, p_scr, *, lam_init):
    h = pl.program_id(1)
    t = bias_ref.shape[-1]
    n_k = k_ref.shape[1] // t
    n_q = q_ref.shape[1] // t
    chunk = 2
    pv_chunk = 2
    n_c = n_k // chunk
    c_left = tab_ref[N_BUCKETS // 2 - 1, h] * LOG2E
    c_right = tab_ref[N_BUCKETS - 1, h] * LOG2E

    s_odd[...] = jnp.zeros_like(s_odd)
    m_odd[...] = jnp.zeros_like(m_odd)
    o_odd[...] = jnp.ones_like(o_odd)
    even = dict(s=s_even, m=m_even, o=o_even)
    odd = dict(s=s_odd, m=m_odd, o=o_odd)
    ones_rows = jnp.ones((PV_ONES, pv_chunk * t), BF16)

    def fold(x, op):
        parts = [x[r * 8:(r + 1) * 8, :] for r in range(t // 8)]
        acc = parts[:2]
        for r in range(2, len(parts)):
            acc[r % 2] = op(acc[r % 2], parts[r])
        return op(acc[0], acc[1])

    def far_bias(j, tile):
        return jnp.where(j < tile, c_left, c_right)

    def is_far(j, tile):
        return jnp.abs(j - tile) >= 2

    def finish(prev, tile):
        outs = [prev['o'][u, :HEAD_DV, :] / prev['o'][u, HEAD_DV:HEAD_DV + 1, :] for u in range(2)]
        lq = lq_ref[...]
        lam = (jnp.exp(jnp.sum(lq[0:1] * lq[1:2], axis=-1, keepdims=True))
               - jnp.exp(jnp.sum(lq[2:3] * lq[3:4], axis=-1, keepdims=True)) + lam_init)
        o = (outs[0] - lam * outs[1]).T
        ms = jnp.mean(o * o, axis=-1, keepdims=True)
        o_ref[0, pl.ds(pl.multiple_of(tile * t, t), t), :] = (
            (o * lax.rsqrt(ms + EPS) * go_ref[...]) * (1.0 - lam_init)).astype(BF16)

    def step(i, cur, prev, scores=True, softmax=True):
        tile_a = i
        tile_b = jnp.maximum(i - 1, 0)
        finish(prev, jnp.maximum(i - 2, 0))
        s_cur, s_prev = cur['s'], prev['s']
        if softmax:
            m_fin = [prev['m'][u] for u in range(2)]
            for dj in (-1, 0, 1):
                j = tile_b + dj
                valid = jnp.logical_and(j >= 0, j < n_k)
                jc = jnp.clip(j, 0, n_k - 1)
                bias = jnp.where(valid, bias_ref[0, dj + 2], 0.0)
                for u in range(2):
                    sc = s_prev[u, jc] + bias
                    s_prev[u, jc] = sc
                    m_fin[u] = jnp.maximum(m_fin[u], fold(sc, jnp.maximum) + jnp.where(valid, 0.0, -jnp.inf))
            m_b = [jnp.max(m_fin[u], axis=0, keepdims=True) for u in range(2)]
        m_acc = [jnp.full((8, t), -jnp.inf, F32) for _ in range(2)]
        if scores:
            q = q_ref[0, pl.ds(pl.multiple_of(tile_a * t, t), t), :]
            lane = lax.broadcasted_iota(jnp.int32, q.shape, 1)
            zero = jnp.zeros_like(q)
            q_sub = (jnp.where(lane < HEAD_DK, q, zero), jnp.where(lane >= HEAD_DK, q, zero))
        for c in range(n_c):
            rows = slice(c * chunk * t, (c + 1) * chunk * t)
            for u in range(2 if scores else 0):
                sc_all = lax.dot_general(k_ref[0, rows, :], q_sub[u], NT_DIMS, preferred_element_type=F32)
                for jj in range(chunk):
                    j = c * chunk + jj
                    sc = sc_all[jj * t:(jj + 1) * t, :]
                    s_cur[u, j] = sc
                    m_acc[u] = jnp.maximum(m_acc[u], fold(sc, jnp.maximum)
                                           + jnp.where(is_far(j, tile_a), far_bias(j, tile_a), -jnp.inf))
            for u in range(2 if softmax else 0):
                for jj in range(chunk):
                    j = c * chunk + jj
                    off = m_b[u] - jnp.where(is_far(j, tile_b), far_bias(j, tile_b), 0.0)
                    p = jnp.exp2(s_prev[u, j] - off)
                    p_scr[u, j * t:(j + 1) * t, :] = p.astype(BF16)
                if ((c + 1) * chunk) % pv_chunk == 0:
                    keys = slice(((c + 1) * chunk - pv_chunk) * t, (c + 1) * chunk * t)
                    part = jnp.dot(jnp.concatenate([vt_ref[0, :, keys], ones_rows], axis=0), p_scr[u, keys, :],
                                   preferred_element_type=F32)
                    if (c + 1) * chunk == pv_chunk:
                        cur['o'][u] = part
                    else:
                        cur['o'][u] += part

        for u in range(2 if scores else 0):
            cur['m'][u] = m_acc[u]

    def pair(pi, carry):
        @pl.when(pi >= 0)
        def _():
            step(2 * pi, even, odd)

        @pl.when(pi < n_q)
        def _():
            step(2 * pi + 1, odd, even)

        return carry

    lax.fori_loop(0, n_q // 2, pair, 0)
    step(n_q, even, odd, scores=False)
    finish(even, n_q - 1)


def _diff_attention(qn, kn, vt, rel_bias, bias_tiles, lambda_qk, g_o, lam_init):
    bsz, s, _ = qn.shape
    t = bias_tiles.shape[-1]
    n_k = s // t
    assert (s // t) % 2 == 0, "the query-tile pipeline advances two tiles per loop trip"
    scores = pltpu.VMEM((2, n_k, t, t), F32)
    maxima = pltpu.VMEM((2, 8, t), F32)
    pv_acc = pltpu.VMEM((2, HEAD_DV + PV_ONES, t), F32)
    seq = lambda b, h: (b, 0, h)
    return pl.pallas_call(
        functools.partial(_attn_kernel, lam_init=lam_init),
        grid=(bsz, N_HEADS),
        in_specs=[pl.BlockSpec(memory_space=pltpu.SMEM),
                  pl.BlockSpec((1, s, HEAD_DV), seq),
                  pl.BlockSpec((1, s, HEAD_DV), seq),
                  pl.BlockSpec((1, HEAD_DV, s), lambda b, h: (b, h, 0)),
                  pl.BlockSpec((1, 5, t, t), lambda b, h: (h, 0, 0, 0)),
                  pl.BlockSpec((4, HEAD_DK), lambda b, h: (0, 0)),
                  pl.BlockSpec((1, HEAD_DV), lambda b, h: (0, 0))],
        out_specs=pl.BlockSpec((1, s, HEAD_DV), seq),
        out_shape=jax.ShapeDtypeStruct((bsz, s, D_ATT), BF16),
        scratch_shapes=[scores, scores, maxima, maxima, pv_acc, pv_acc, pltpu.VMEM((2, s, t), BF16)],
        compiler_params=_params(("parallel", "parallel"), VMEM_LIMIT),
        name="diff_attention",
    )(rel_bias, qn, kn, vt, bias_tiles, lambda_qk, g_o.reshape(1, HEAD_DV))


def _outproj_kernel(yl_ref, ya_ref, x_ref, gate_ref, sc_ref, sh_ref, g2_ref, wo_ref, wr_ref,
                    x1_ref, h2_ref, aff_ref):
    mix = (jnp.dot(yl_ref[0], wo_ref[0:D_LRU, :], preferred_element_type=F32)
           + jnp.dot(ya_ref[0], wo_ref[D_LRU:D_LRU + D_ATT, :], preferred_element_type=F32))
    x1 = x_ref[0] + gate_ref[0] * mix
    x1_ref[0] = x1
    ms = jnp.mean(x1 * x1, axis=-1, keepdims=True)
    h2 = (x1 * lax.rsqrt(ms + EPS) * g2_ref[...]) * (1.0 + sc_ref[0]) + sh_ref[0]
    h2b = h2.astype(BF16)
    h2_ref[0] = h2b
    logits = lax.dot_general(wr_ref[...], h2b, NT_DIMS, preferred_element_type=F32)
    ex = jnp.exp(logits - jnp.max(logits, axis=0, keepdims=True))
    aff_ref[0] = ex / jnp.sum(ex, axis=0, keepdims=True)


def _out_projection(y_lru, y_att, x, gate1, scale2, shift2, g_norm2, w_out, w_router, tm):
    bsz, s, d = x.shape
    row = lambda b, i: (b, i, 0)
    vec = lambda b, i: (b, 0, 0)
    full = lambda b, i: (0, 0)
    return pl.pallas_call(
        _outproj_kernel,
        grid=(bsz, s // tm),
        in_specs=[pl.BlockSpec((1, tm, D_LRU), row),
                  pl.BlockSpec((1, tm, D_ATT), row),
                  pl.BlockSpec((1, tm, d), row),
                  pl.BlockSpec((1, 1, d), vec),
                  pl.BlockSpec((1, 1, d), vec),
                  pl.BlockSpec((1, 1, d), vec),
                  pl.BlockSpec((1, d), full),
                  pl.BlockSpec((D_LRU + D_ATT, d), full),
                  pl.BlockSpec((N_EXPERTS, d), full)],
        out_specs=[pl.BlockSpec((1, tm, d), row),
                   pl.BlockSpec((1, tm, d), row),
                   pl.BlockSpec((1, N_EXPERTS, tm), lambda b, i: (b, 0, i))],
        out_shape=[jax.ShapeDtypeStruct((bsz, s, d), F32),
                   jax.ShapeDtypeStruct((bsz, s, d), BF16),
                   jax.ShapeDtypeStruct((bsz, N_EXPERTS, s), F32)],
        compiler_params=_params(("parallel", "parallel"), VMEM_LIMIT),
        name="out_proj_norm2_router",
    )(y_lru, y_att, x, gate1, scale2, shift2, g_norm2.reshape(1, d), w_out.astype(BF16),
      w_router.T.astype(BF16))


def _route_kernel(aff_ref, pos_ref, cnt_ref, *, cap, n_tok_chunks):
    aff = aff_ref[0]
    n_e, s = aff.shape
    bits = lax.bitcast_convert_type(aff, jnp.int32)
    capf = float(cap)

    def count(mask):
        return jnp.sum(jnp.where(mask, 1.0, 0.0), axis=-1, keepdims=True)

    tau = jnp.zeros((n_e, 1), jnp.int32)
    for bit in range(30, -1, -1):
        cand = tau | (1 << bit)
        tau = jnp.where(count(bits >= cand) >= capf, cand, tau)
    gt = bits > tau
    eq = bits == tau
    need = capf - count(gt)

    blk = min(256, s // n_tok_chunks)
    r = lax.broadcasted_iota(jnp.int32, (blk, blk), 0)
    cidx = lax.broadcasted_iota(jnp.int32, (blk, blk), 1)
    upper = jnp.where(r < cidx, 1.0, 0.0).astype(BF16)

    def prefix_blocks(mask):
        off = jnp.zeros((n_e, 1), F32)
        pieces, offs = [], []
        for k in range(s // blk):
            mb = jnp.where(mask[:, k * blk:(k + 1) * blk], 1.0, 0.0)
            offs.append(off)
            pieces.append(jnp.dot(mb.astype(BF16), upper, preferred_element_type=F32) + off)
            off = off + jnp.sum(mb, axis=-1, keepdims=True)
        return pieces, offs

    eq_rank, _ = prefix_blocks(eq)
    sel_blocks = []
    for k in range(s // blk):
        sl = slice(k * blk, (k + 1) * blk)
        sel_blocks.append(jnp.logical_or(gt[:, sl], jnp.logical_and(eq[:, sl], eq_rank[k] < need)))
    sel = jnp.concatenate(sel_blocks, axis=1)
    slot, offs = prefix_blocks(sel)
    for k in range(s // blk):
        pos_ref[0, :, k * blk:(k + 1) * blk] = jnp.where(sel_blocks[k], slot[k], -1.0).astype(jnp.int32)

    lane = lax.broadcasted_iota(jnp.int32, (n_e, 128), 1)
    cnt = jnp.zeros((n_e, 128), F32)
    per = (s // n_tok_chunks) // blk
    for j in range(n_tok_chunks):
        cnt = jnp.where(lane == j, offs[j * per], cnt)
    cnt_ref[0] = cnt.astype(jnp.int32)


def _routing(aff, cap):
    bsz, n_e, s = aff.shape
    return pl.pallas_call(
        functools.partial(_route_kernel, cap=cap, n_tok_chunks=N_TOK_CHUNKS),
        grid=(bsz,),
        in_specs=[pl.BlockSpec((1, n_e, s), lambda b: (b, 0, 0))],
        out_specs=[pl.BlockSpec((1, n_e, s), lambda b: (b, 0, 0)),
                   pl.BlockSpec((1, n_e, 128), lambda b: (b, 0, 0))],
        out_shape=[jax.ShapeDtypeStruct((bsz, n_e, s), jnp.int32),
                   jax.ShapeDtypeStruct((bsz, n_e, 128), jnp.int32)],
        compiler_params=_params(("parallel",)),
        name="expert_choice_routing",
    )(aff)


def _moe_kernel(cnt_ref, pos_ref, aff_ref, h2_ref, w1_ref, w3_ref, w2_ref, gate_ref, x1_hbm, out_ref,
                xe_scr, g_scr, y_scr, x1_sem, *, cap):
    b = pl.program_id(0)
    e = pl.program_id(1)
    n_e = pl.num_programs(1)
    s = h2_ref.shape[1]
    tchunk = s // N_TOK_CHUNKS

    def residual_copy():
        return pltpu.make_async_copy(x1_hbm.at[b], out_ref.at[0], x1_sem.at[0])

    @pl.when(e == 0)
    def _():
        residual_copy().start()

    xe_scr[...] = jnp.zeros_like(xe_scr)
    g_scr[...] = jnp.zeros_like(g_scr)
    base = (b * n_e + e) * N_TOK_CHUNKS

    def walk(cchunk, visit):
        n_cap = cap // cchunk
        slot_iota = lax.broadcasted_iota(jnp.int32, (cchunk, tchunk), 0)
        j = jnp.int32(0)
        i = jnp.int32(0)
        for _ in range(N_TOK_CHUNKS + n_cap - 1):
            jc = jnp.minimum(j, N_TOK_CHUNKS - 1)
            ic = jnp.minimum(i, n_cap - 1)
            c_lo = cnt_ref[base + jc]
            c_hi = jnp.where(jc + 1 < N_TOK_CHUNKS, cnt_ref[base + jnp.minimum(jc + 1, N_TOK_CHUNKS - 1)], cap)
            s_hi = (ic + 1) * cchunk
            valid = jnp.logical_and(jnp.maximum(c_lo, ic * cchunk) < jnp.minimum(c_hi, s_hi),
                                    jnp.logical_and(j < N_TOK_CHUNKS, i < n_cap))
            slot0 = jnp.where(valid, ic * cchunk, -2 * cap)
            sel = pos_ref[0, 0, pl.ds(jc, 1), :] == slot_iota + slot0
            visit(sel, jc, pl.ds(pl.multiple_of(ic * cchunk, cchunk), cchunk),
                  pl.ds(pl.multiple_of(jc * tchunk, tchunk), tchunk))
            j = j + jnp.where(c_hi <= s_hi, 1, 0)
            i = i + jnp.where(s_hi <= c_hi, 1, 0)

    def gather(sel, jc, rows, toks):
        xe_scr[rows, :] += jnp.dot(jnp.where(sel, 1.0, 0.0).astype(BF16), h2_ref[0, toks, :],
                                   preferred_element_type=F32)
        g_scr[rows, :] += jnp.sum(jnp.where(sel, aff_ref[0, 0, pl.ds(jc, 1), :], 0.0), axis=-1, keepdims=True)

    walk(cap // N_CAP_CHUNKS, gather)

    xe = xe_scr[...].astype(BF16)
    a = jnp.dot(xe, w1_ref[0], preferred_element_type=F32)
    gate = jnp.dot(xe, w3_ref[0], preferred_element_type=F32)
    hmid = ((a * _sigmoid(a)) * gate).astype(BF16)
    y = jnp.dot(hmid, w2_ref[0], preferred_element_type=F32) * g_scr[...] * gate_ref[0]
    y_scr[...] = y.astype(BF16)

    @pl.when(e == 0)
    def _():
        residual_copy().wait()

    def scatter(sel, jc, rows, toks):
        out_ref[0, toks, :] += lax.dot_general(jnp.where(sel, 1.0, 0.0).astype(BF16), y_scr[rows, :],
                                               TN_DIMS, preferred_element_type=F32)

    walk(cap // N_SCATTER_CHUNKS, scatter)


def _moe(cnt, pos, aff, h2, w1, w3, w2, gate2, x1, cap):
    bsz, s, d = h2.shape
    n_e = w1.shape[0]
    f = w1.shape[2]
    tok_row = lambda b, e, c: (b, e, 0, 0)
    wspec = lambda b, e, c: (e, 0, 0)
    resident = lambda b, e, c: (b, 0, 0)
    grid_spec = pltpu.PrefetchScalarGridSpec(
        num_scalar_prefetch=1,
        grid=(bsz, n_e),
        in_specs=[pl.BlockSpec((1, 1, N_TOK_CHUNKS, s // N_TOK_CHUNKS), tok_row),
                  pl.BlockSpec((1, 1, N_TOK_CHUNKS, s // N_TOK_CHUNKS), tok_row),
                  pl.BlockSpec((1, s, d), resident, pipeline_mode=pl.Buffered(1)),
                  pl.BlockSpec((1, d, f), wspec),
                  pl.BlockSpec((1, d, f), wspec),
                  pl.BlockSpec((1, f, d), wspec),
                  pl.BlockSpec((1, 1, d), resident),
                  pl.BlockSpec(memory_space=pl.ANY)],
        out_specs=pl.BlockSpec((1, s, d), resident, pipeline_mode=pl.Buffered(1)),
        scratch_shapes=[pltpu.VMEM((cap, d), F32), pltpu.VMEM((cap, 1), F32), pltpu.VMEM((cap, d), BF16),
                        pltpu.SemaphoreType.DMA((1,))],
    )
    return pl.pallas_call(
        functools.partial(_moe_kernel, cap=cap),
        grid_spec=grid_spec,
        out_shape=jax.ShapeDtypeStruct((bsz, s, d), F32),
        compiler_params=_params(("arbitrary", "arbitrary"), VMEM_LIMIT),
        name="expert_choice_ffn",
    )(cnt.reshape(-1), pos.reshape(bsz, n_e, N_TOK_CHUNKS, -1), aff.reshape(bsz, n_e, N_TOK_CHUNKS, -1), h2,
      w1.astype(BF16), w3.astype(BF16), w2.astype(BF16), gate2, x1)


def kernel(x, c, w_mod, b_mod, g_norm1, w_in, conv_w, conv_b, lru_w_a, lru_b_a, lru_w_x, lru_b_x,
           lru_lambda, g_q, g_k, lambda_qk, g_attn_out, rel_bias, w_out, g_norm2, w_router, w1, w3, w2):
    bsz, s, d = x.shape
    depth = w_mod.shape[0]
    cap = max(1, EC_FACTOR * s // N_EXPERTS)
    tm = min(512, s)
    bias_tiles = _bias_tiles(rel_bias, min(ATT_TILE, s))
    for l in range(depth):
        mod = _modulation(c, w_mod[l], b_mod[l])
        shift1, scale1, gate1, shift2, scale2, gate2 = [m.reshape(bsz, 1, d) for m in jnp.split(mod, 6, axis=-1)]
        x_lru, gz, qn, kn, vt = _in_projection(x, scale1, shift1, g_norm1[l], w_in[l], g_q[l], g_k[l],
                                               min(IN_PROJ_ROWS, s))
        y_lru = _rg_lru(x_lru, gz, conv_w[l], conv_b[l], lru_w_a[l], lru_b_a[l], lru_w_x[l], lru_b_x[l],
                        lru_lambda[l])
        lam_init = 0.8 - 0.6 * math.exp(-0.3 * l)
        y_att = _diff_attention(qn, kn, vt, rel_bias, bias_tiles, lambda_qk[l], g_attn_out[l], lam_init)
        x1, h2, aff = _out_projection(y_lru, y_att, x, gate1, scale2, shift2, g_norm2[l], w_out[l],
                                      w_router[l], tm)
        pos, cnt = _routing(aff, cap)
        x = _moe(cnt[:, :, :N_TOK_CHUNKS], pos, aff, h2, w1[l], w3[l], w2[l], gate2, x1, cap)
    return x
```

```python
import functools
import math

import jax
import jax.numpy as jnp
from jax import lax
from jax.experimental import pallas as pl
from jax.experimental.pallas import tpu as pltpu

F32 = jnp.float32
BF16 = jnp.bfloat16

D_MODEL = 1024
D_LRU = 512
LRU_BLOCK = 64
LRU_C = 8.0
CONV_W = 4
N_HEADS = 4
HEAD_DV = 128
HEAD_DK = 64
D_ATT = N_HEADS * HEAD_DV
N_BUCKETS = 32
N_EXPERTS = 16
EC_FACTOR = 2
EPS = 1e-6
LOG2E = math.log2(math.e)

IN_PROJ_ROWS = 1024
OUT_PROJ_ROWS = 1024
LRU_HALF = 256
SCAN_LANES = 8
ATT_TILE = 256
PV_ONES = 16
N_TOK_CHUNKS = 8
N_CAP_CHUNKS = 4
N_SCATTER_CHUNKS = 2
VMEM_LIMIT = 56 * 1024 * 1024

NT_DIMS = (((1,), (1,)), ((), ()))
TN_DIMS = (((0,), (0,)), ((), ()))


def _sigmoid(x):
    return 1.0 / (1.0 + jnp.exp(-x))


def _params(sem, vmem=None, flags=None):
    return pltpu.CompilerParams(dimension_semantics=sem, vmem_limit_bytes=vmem, flags=flags)


def _mod_kernel(c_ref, w_ref, b_ref, o_ref):
    c = c_ref[...]
    o_ref[...] = jnp.dot((c * _sigmoid(c)).astype(BF16), w_ref[...].astype(BF16),
                         preferred_element_type=F32) + b_ref[...]


def _modulation(c, w_mod, b_mod):
    bsz, d = c.shape
    n = w_mod.shape[1]
    return pl.pallas_call(
        _mod_kernel,
        grid=(n // d,),
        in_specs=[pl.BlockSpec((bsz, d), lambda j: (0, 0)),
                  pl.BlockSpec((d, d), lambda j: (0, j)),
                  pl.BlockSpec((1, d), lambda j: (0, j))],
        out_specs=pl.BlockSpec((bsz, d), lambda j: (0, j)),
        out_shape=jax.ShapeDtypeStruct((bsz, n), F32),
        compiler_params=_params(("arbitrary",)),
        name="adaln_mod",
    )(c, w_mod, b_mod.reshape(1, n))


def _bias_kernel(tab_ref, o_ref):
    h = pl.program_id(0)
    t = o_ref.shape[-1]
    key = lax.broadcasted_iota(jnp.int32, (t, t), 0)
    qry = lax.broadcasted_iota(jnp.int32, (t, t), 1)
    half = N_BUCKETS // 2
    max_exact = half // 2
    for d in range(5):
        if d == 0:
            o_ref[0, d] = jnp.full((t, t), tab_ref[half - 1, h] * LOG2E, F32)
        elif d == 4:
            o_ref[0, d] = jnp.full((t, t), tab_ref[N_BUCKETS - 1, h] * LOG2E, F32)
        else:
            rel = (d - 2) * t + key - qry
            n = jnp.abs(rel)
            n2 = n * n
            large = jnp.full((t, t), max_exact, jnp.int32)
            for k in range(1, half - max_exact):
                large = large + jnp.where(n2 >= (max_exact * max_exact) * (2 ** k), 1, 0)
            idx = jnp.where(n < max_exact, n, large) + jnp.where(rel > 0, half, 0)
            val = jnp.zeros((t, t), F32)
            for j in range(N_BUCKETS):
                val = jnp.where(idx == j, tab_ref[j, h] * LOG2E, val)
            o_ref[0, d] = val


def _bias_tiles(rel_bias, t):
    return pl.pallas_call(
        _bias_kernel,
        grid=(N_HEADS,),
        in_specs=[pl.BlockSpec(memory_space=pltpu.SMEM)],
        out_specs=pl.BlockSpec((1, 5, t, t), lambda h: (h, 0, 0, 0)),
        out_shape=jax.ShapeDtypeStruct((N_HEADS, 5, t, t), F32),
        compiler_params=_params(("arbitrary",)),
        name="t5_bias_tiles",
    )(rel_bias)


def _inproj_kernel(x_ref, sc_ref, sh_ref, g1_ref, w_ref, wvt_ref, mseg_ref, gq_ref, gk_ref,
                   xl_ref, gz_ref, q_ref, k_ref, vt_ref):
    x = x_ref[0]
    ms = jnp.mean(x * x, axis=-1, keepdims=True)
    h = (x * lax.rsqrt(ms + EPS) * g1_ref[...]) * (1.0 + sc_ref[0]) + sh_ref[0]
    hb = h.astype(BF16)

    def proj(lo, width):
        return jnp.dot(hb, w_ref[:, lo:lo + width], preferred_element_type=F32)

    def qk_norm(t, g):
        ss = jnp.dot((t * t).astype(BF16), mseg_ref[...], preferred_element_type=F32)
        return t * lax.rsqrt(ss * (1.0 / HEAD_DK) + EPS) * g

    xl_ref[0] = proj(0, D_LRU)
    z = proj(D_LRU, D_LRU)
    cdf = 0.5 * (1.0 + jnp.tanh(math.sqrt(2.0 / math.pi) * (z + 0.044715 * (z * z * z))))
    gz_ref[0] = (z * cdf).astype(BF16)
    q_ref[0] = (qk_norm(proj(2 * D_LRU, D_ATT), gq_ref[...]) * (HEAD_DK ** -0.5 * LOG2E)).astype(BF16)
    k_ref[0] = qk_norm(proj(2 * D_LRU + D_ATT, D_ATT), gk_ref[...]).astype(BF16)
    vt_ref[0] = lax.dot_general(wvt_ref[...], hb, NT_DIMS,
                                preferred_element_type=F32).astype(BF16)


def _in_projection(x, scale1, shift1, g_norm1, w_in, g_q, g_k, tm):
    bsz, s, d = x.shape
    n = w_in.shape[1] - D_ATT
    w_main = w_in[:, :n].astype(BF16)
    w_vt = w_in[:, n:].T.astype(BF16)
    seg = jnp.arange(D_ATT, dtype=jnp.int32) // HEAD_DK
    mseg = (seg[:, None] == seg[None, :]).astype(BF16)
    n_sub = D_ATT // HEAD_DK
    row = lambda b, i: (b, i, 0)
    vec = lambda b, i: (b, 0, 0)
    full = lambda b, i: (0, 0)
    out_block = pl.BlockSpec((1, tm, D_LRU), row)
    return pl.pallas_call(
        _inproj_kernel,
        grid=(bsz, s // tm),
        in_specs=[pl.BlockSpec((1, tm, d), row),
                  pl.BlockSpec((1, 1, d), vec),
                  pl.BlockSpec((1, 1, d), vec),
                  pl.BlockSpec((1, d), full),
                  pl.BlockSpec((d, n), full),
                  pl.BlockSpec((D_ATT, d), full),
                  pl.BlockSpec((D_ATT, D_ATT), full),
                  pl.BlockSpec((1, D_ATT), full),
                  pl.BlockSpec((1, D_ATT), full)],
        out_specs=[out_block] * 4 + [pl.BlockSpec((1, D_ATT, tm), lambda b, i: (b, 0, i))],
        out_shape=[jax.ShapeDtypeStruct((bsz, s, D_LRU), F32)]
                  + [jax.ShapeDtypeStruct((bsz, s, D_LRU), BF16)] * 3
                  + [jax.ShapeDtypeStruct((bsz, D_ATT, s), BF16)],
        compiler_params=_params(("parallel", "parallel"), VMEM_LIMIT),
        name="norm1_in_proj",
    )(x, scale1, shift1, g_norm1.reshape(1, d), w_main, w_vt, mseg,
      jnp.tile(g_q, n_sub).reshape(1, D_ATT), jnp.tile(g_k, n_sub).reshape(1, D_ATT))


def _lru_kernel(x_ref, gz_ref, cw_ref, cb_ref, wa_ref, wx_ref, ba_ref, bx_ref, lam_ref, y_ref,
                xpad, a_f, u_f, a_b, u_b, *, tc):
    s = x_ref.shape[1]
    c = x_ref.shape[2]
    n_chunks = s // tc
    n_slab = c // 128
    pitch = a_f.shape[1] // n_chunks
    zeros8 = jnp.zeros((8, c), F32)
    xpad[0:8, :] = zeros8
    xpad[s + 8:s + 16, :] = zeros8

    def fill(ci, carry):
        t0 = pl.multiple_of(ci * tc, tc)
        xpad[pl.ds(t0 + 8, tc), :] = x_ref[0, pl.ds(t0, tc), :]
        return carry

    lax.fori_loop(0, n_chunks, fill, 0)

    cw = cw_ref[...]
    cb = cb_ref[...]
    decay = []
    for d in range(2):
        lam = lam_ref[d]
        softplus_neg = jnp.maximum(-lam, 0.0) + jnp.log(1.0 + jnp.exp(-jnp.abs(lam)))
        decay.append(-LRU_C * softplus_neg)
    a_scr = (a_f, a_b)
    u_scr = (u_f, u_b)

    def gates(ci, carry):
        t0 = pl.multiple_of(ci * tc, tc)
        xw = xpad[pl.ds(t0, tc + 16), :]
        xc = (cw[0:1] * pltpu.roll(xw, 2, 0)[8:8 + tc]
              + cw[1:2] * pltpu.roll(xw, 1, 0)[8:8 + tc]
              + cw[2:3] * xw[8:8 + tc]
              + cw[3:4] * pltpu.roll(xw, tc + 15, 0)[8:8 + tc]) + cb
        xcb = xc.astype(BF16)
        for d in range(2):
            r = _sigmoid(jnp.dot(xcb, wa_ref[d], preferred_element_type=F32) + ba_ref[d])
            i = _sigmoid(jnp.dot(xcb, wx_ref[d], preferred_element_type=F32) + bx_ref[d])
            a = jnp.exp(r * decay[d])
            u = jnp.sqrt(1.0 - a * a) * (i * xc)
            r0 = pl.multiple_of(ci * pitch, 8)
            for sl in range(n_slab):
                a_scr[d][sl, pl.ds(r0, tc), :] = a[:, sl * 128:(sl + 1) * 128]
                u_scr[d][sl, pl.ds(r0, tc), :] = u[:, sl * 128:(sl + 1) * 128]
        return carry

    lax.fori_loop(0, n_chunks, gates, 0)

    def rows(tt):
        return pl.ds(tt, n_chunks, stride=pitch)

    def step(tt, carry):
        out = []
        for d in range(2):
            t_loc = tt if d == 0 else tc - 1 - tt
            for sl in range(n_slab):
                h, p = carry[len(out)]
                a8 = a_scr[d][sl, rows(t_loc), :]
                h = a8 * h + u_scr[d][sl, rows(t_loc), :]
                p = a8 * p
                u_scr[d][sl, rows(t_loc), :] = h
                a_scr[d][sl, rows(t_loc), :] = p
                out.append((h, p))
        return tuple(out)

    init = (jnp.zeros((n_chunks, 128), F32), jnp.ones((n_chunks, 128), F32))
    lax.fori_loop(0, tc, step, (init,) * (2 * n_slab), unroll=4)

    for sl in range(n_slab):
        lanes = slice(sl * 128, (sl + 1) * 128)
        h_end, p_end = u_f[sl, rows(tc - 1), :], a_f[sl, rows(tc - 1), :]
        h_beg, p_beg = u_b[sl, rows(0), :], a_b[sl, rows(0), :]
        carry_f = [jnp.zeros((1, 128), F32)]
        for r in range(1, n_chunks):
            carry_f.append(p_end[r - 1:r] * carry_f[-1] + h_end[r - 1:r])
        carry_b = [jnp.zeros((1, 128), F32)]
        for r in range(n_chunks - 2, -1, -1):
            carry_b.insert(0, p_beg[r + 1:r + 2] * carry_b[0] + h_beg[r + 1:r + 2])
        for r in range(n_chunks):
            blk = slice(r * pitch, r * pitch + tc)
            hsum = (u_f[sl, blk, :] + a_f[sl, blk, :] * carry_f[r]
                    + u_b[sl, blk, :] + a_b[sl, blk, :] * carry_b[r])
            y_ref[0, r * tc:(r + 1) * tc, lanes] = (hsum * gz_ref[0, r * tc:(r + 1) * tc, lanes].astype(F32)).astype(BF16)


def _block_diag(w, half):
    n_dir, n_blocks, blk, _ = w.shape
    per = half // blk
    n_half = n_blocks // per
    w = w.reshape(n_dir, n_half, per, blk, blk)
    eye = jnp.eye(per, dtype=w.dtype)
    out = w[:, :, :, :, None, :] * eye[None, None, :, None, :, None]
    return out.reshape(n_dir, n_half, half, half)


def _rg_lru(x_lru, gz, conv_w, conv_b, w_a, b_a, w_x, b_x, lam):
    bsz, s, c = x_lru.shape
    half = LRU_HALF
    tc = s // SCAN_LANES
    pitch = tc + 8 if (tc // 8) % 2 == 0 else tc + 16
    n_half = c // half
    wa = _block_diag(w_a, half).astype(BF16)
    wx = _block_diag(w_x, half).astype(BF16)
    seq = lambda b, p: (b, 0, p)
    chan = lambda b, p: (0, p)
    dirchan = lambda b, p: (0, 0, p)
    blk = lambda b, p: (0, p, 0, 0)
    return pl.pallas_call(
        functools.partial(_lru_kernel, tc=tc),
        grid=(bsz, n_half),
        in_specs=[pl.BlockSpec((1, s, half), seq),
                  pl.BlockSpec((1, s, half), seq),
                  pl.BlockSpec((CONV_W, half), chan),
                  pl.BlockSpec((1, half), chan),
                  pl.BlockSpec((2, None, half, half), blk),
                  pl.BlockSpec((2, None, half, half), blk),
                  pl.BlockSpec((2, 1, half), dirchan),
                  pl.BlockSpec((2, 1, half), dirchan),
                  pl.BlockSpec((2, 1, half), dirchan)],
        out_specs=pl.BlockSpec((1, s, half), seq),
        out_shape=jax.ShapeDtypeStruct((bsz, s, c), BF16),
        scratch_shapes=[pltpu.VMEM((s + 16, half), F32)]
                       + [pltpu.VMEM((half // 128, SCAN_LANES * pitch, 128), F32)] * 4,
        compiler_params=_params(("parallel", "parallel"), VMEM_LIMIT),
        name="rg_lru",
    )(x_lru, gz, conv_w.reshape(CONV_W, c), conv_b.reshape(1, c), wa, wx,
      b_a.reshape(2, 1, c), b_x.reshape(2, 1, c), lam.reshape(2, 1, c))


def _attn_kernel(tab_ref, q_ref, k_ref, vt_ref, bias_ref, lq_ref, go_ref, o_ref,
                 s_even, s_odd, m_even, m_odd, o_even, o_odd, p_scr, *, lam_init):
    h = pl.program_id(1)
    t = bias_ref.shape[-1]
    n_k = k_ref.shape[1] // t
    n_q = q_ref.shape[1] // t
    chunk = 2
    pv_chunk = 2
    n_c = n_k // chunk
    c_left = tab_ref[N_BUCKETS // 2 - 1, h] * LOG2E
    c_right = tab_ref[N_BUCKETS - 1, h] * LOG2E

    s_odd[...] = jnp.zeros_like(s_odd)
    m_odd[...] = jnp.zeros_like(m_odd)
    o_odd[...] = jnp.ones_like(o_odd)
    even = dict(s=s_even, m=m_even, o=o_even)
    odd = dict(s=s_odd, m=m_odd, o=o_odd)
    ones_rows = jnp.ones((PV_ONES, pv_chunk * t), BF16)

    def fold(x, op):
        parts = [x[r * 8:(r + 1) * 8, :] for r in range(t // 8)]
        acc = parts[:2]
        for r in range(2, len(parts)):
            acc[r % 2] = op(acc[r % 2], parts[r])
        return op(acc[0], acc[1])

    def far_bias(j, tile):
        return jnp.where(j < tile, c_left, c_right)

    def is_far(j, tile):
        return jnp.abs(j - tile) >= 2

    def finish(prev, tile):
        outs = [prev['o'][u, :HEAD_DV, :] / prev['o'][u, HEAD_DV:HEAD_DV + 1, :] for u in range(2)]
        lq = lq_ref[...]
        lam = (jnp.exp(jnp.sum(lq[0:1] * lq[1:2], axis=-1, keepdims=True))
               - jnp.exp(jnp.sum(lq[2:3] * lq[3:4], axis=-1, keepdims=True)) + lam_init)
        o = (outs[0] - lam * outs[1]).T
        ms = jnp.mean(o * o, axis=-1, keepdims=True)
        o_ref[0, pl.ds(pl.multiple_of(tile * t, t), t), :] = (
            (o * lax.rsqrt(ms + EPS) * go_ref[...]) * (1.0 - lam_init)).astype(BF16)

    def step(i, cur, prev, scores=True, softmax=True):
        tile_a = i
        tile_b = jnp.maximum(i - 1, 0)
        finish(prev, jnp.maximum(i - 2, 0))
        s_cur, s_prev = cur['s'], prev['s']
        if softmax:
            m_fin = [prev['m'][u] for u in range(2)]
            for dj in (-1, 0, 1):
                j = tile_b + dj
                valid = jnp.logical_and(j >= 0, j < n_k)
                jc = jnp.clip(j, 0, n_k - 1)
                bias = jnp.where(valid, bias_ref[0, dj + 2], 0.0)
                for u in range(2):
                    m_fin[u] = jnp.maximum(m_fin[u], fold(s_prev[u, jc] + bias, jnp.maximum)
                                           + jnp.where(valid, 0.0, -jnp.inf))
            m_b = [jnp.max(m_fin[u], axis=0, keepdims=True) for u in range(2)]
        m_acc = [jnp.full((8, t), -jnp.inf, F32) for _ in range(2)]
        if scores:
            q = q_ref[0, pl.ds(pl.multiple_of(tile_a * t, t), t), :]
            lane = lax.broadcasted_iota(jnp.int32, q.shape, 1)
            zero = jnp.zeros_like(q)
            q_sub = (jnp.where(lane < HEAD_DK, q, zero), jnp.where(lane >= HEAD_DK, q, zero))
        for c in range(n_c):
            rows = slice(c * chunk * t, (c + 1) * chunk * t)
            for u in range(2 if scores else 0):
                sc_all = lax.dot_general(k_ref[0, rows, :], q_sub[u], NT_DIMS, preferred_element_type=F32)
                for jj in range(chunk):
                    j = c * chunk + jj
                    sc = sc_all[jj * t:(jj + 1) * t, :]
                    s_cur[u, j] = sc
                    m_acc[u] = jnp.maximum(m_acc[u], fold(sc, jnp.maximum)
                                           + jnp.where(is_far(j, tile_a), far_bias(j, tile_a), -jnp.inf))
            for u in range(2 if softmax else 0):
                for jj in range(chunk):
                    j = c * chunk + jj
                    p = jnp.exp2(s_prev[u, j] + bias_ref[0, jnp.clip(j - tile_b, -2, 2) + 2] - m_b[u])
                    p_scr[u, j * t:(j + 1) * t, :] = p.astype(BF16)
                if ((c + 1) * chunk) % pv_chunk == 0:
                    keys = slice(((c + 1) * chunk - pv_chunk) * t, (c + 1) * chunk * t)
                    part = jnp.dot(jnp.concatenate([vt_ref[0, :, keys], ones_rows], axis=0), p_scr[u, keys, :],
                                   preferred_element_type=F32)
                    if (c + 1) * chunk == pv_chunk:
                        cur['o'][u] = part
                    else:
                        cur['o'][u] += part

        for u in range(2 if scores else 0):
            cur['m'][u] = m_acc[u]

    def pair(pi, carry):
        @pl.when(pi >= 0)
        def _():
            step(2 * pi, even, odd)

        @pl.when(pi < n_q)
        def _():
            step(2 * pi + 1, odd, even)

        return carry

    lax.fori_loop(0, n_q // 2, pair, 0)
    step(n_q, even, odd, scores=False)
    finish(even, n_q - 1)


def _diff_attention(qn, kn, vt, rel_bias, bias_tiles, lambda_qk, g_o, lam_init):
    bsz, s, _ = qn.shape
    t = bias_tiles.shape[-1]
    n_k = s // t
    assert (s // t) % 2 == 0, "the query-tile pipeline advances two tiles per loop trip"
    scores = pltpu.VMEM((2, n_k, t, t), F32)
    maxima = pltpu.VMEM((2, 8, t), F32)
    pv_acc = pltpu.VMEM((2, HEAD_DV + PV_ONES, t), F32)
    seq = lambda b, h: (b, 0, h)
    return pl.pallas_call(
        functools.partial(_attn_kernel, lam_init=lam_init),
        grid=(bsz, N_HEADS),
        in_specs=[pl.BlockSpec(memory_space=pltpu.SMEM),
                  pl.BlockSpec((1, s, HEAD_DV), seq),
                  pl.BlockSpec((1, s, HEAD_DV), seq),
                  pl.BlockSpec((1, HEAD_DV, s), lambda b, h: (b, h, 0)),
                  pl.BlockSpec((1, 5, t, t), lambda b, h: (h, 0, 0, 0)),
                  pl.BlockSpec((4, HEAD_DK), lambda b, h: (0, 0)),
                  pl.BlockSpec((1, HEAD_DV), lambda b, h: (0, 0))],
        out_specs=pl.BlockSpec((1, s, HEAD_DV), seq),
        out_shape=jax.ShapeDtypeStruct((bsz, s, D_ATT), BF16),
        scratch_shapes=[scores, scores, maxima, maxima, pv_acc, pv_acc, pltpu.VMEM((2, s, t), BF16)],
        compiler_params=_params(("parallel", "parallel"), VMEM_LIMIT),
        name="diff_attention",
    )(rel_bias, qn, kn, vt, bias_tiles, lambda_qk, g_o.reshape(1, HEAD_DV))


def _outproj_kernel(yl_ref, ya_ref, x_ref, gate_ref, sc_ref, sh_ref, g2_ref, wo_ref, wr_ref,
                    x1_ref, h2_ref, aff_ref):
    mix = (jnp.dot(yl_ref[0], wo_ref[0:D_LRU, :], preferred_element_type=F32)
           + jnp.dot(ya_ref[0], wo_ref[D_LRU:D_LRU + D_ATT, :], preferred_element_type=F32))
    x1 = x_ref[0] + gate_ref[0] * mix
    x1_ref[0] = x1
    ms = jnp.mean(x1 * x1, axis=-1, keepdims=True)
    h2 = (x1 * lax.rsqrt(ms + EPS) * g2_ref[...]) * (1.0 + sc_ref[0]) + sh_ref[0]
    h2b = h2.astype(BF16)
    h2_ref[0] = h2b
    logits = lax.dot_general(wr_ref[...], h2b, NT_DIMS, preferred_element_type=F32)
    ex = jnp.exp(logits - jnp.max(logits, axis=0, keepdims=True))
    aff_ref[0] = ex / jnp.sum(ex, axis=0, keepdims=True)


def _out_projection(y_lru, y_att, x, gate1, scale2, shift2, g_norm2, w_out, w_router, tm):
    bsz, s, d = x.shape
    row = lambda b, i: (b, i, 0)
    vec = lambda b, i: (b, 0, 0)
    full = lambda b, i: (0, 0)
    return pl.pallas_call(
        _outproj_kernel,
        grid=(bsz, s // tm),
        in_specs=[pl.BlockSpec((1, tm, D_LRU), row),
                  pl.BlockSpec((1, tm, D_ATT), row),
                  pl.BlockSpec((1, tm, d), row),
                  pl.BlockSpec((1, 1, d), vec),
                  pl.BlockSpec((1, 1, d), vec),
                  pl.BlockSpec((1, 1, d), vec),
                  pl.BlockSpec((1, d), full),
                  pl.BlockSpec((D_LRU + D_ATT, d), full),
                  pl.BlockSpec((N_EXPERTS, d), full)],
        out_specs=[pl.BlockSpec((1, tm, d), row),
                   pl.BlockSpec((1, tm, d), row),
                   pl.BlockSpec((1, N_EXPERTS, tm), lambda b, i: (b, 0, i))],
        out_shape=[jax.ShapeDtypeStruct((bsz, s, d), F32),
                   jax.ShapeDtypeStruct((bsz, s, d), BF16),
                   jax.ShapeDtypeStruct((bsz, N_EXPERTS, s), F32)],
        compiler_params=_params(("parallel", "parallel"), VMEM_LIMIT),
        name="out_proj_norm2_router",
    )(y_lru, y_att, x, gate1, scale2, shift2, g_norm2.reshape(1, d), w_out.astype(BF16),
      w_router.T.astype(BF16))


def _route_kernel(aff_ref, pos_ref, cnt_ref, *, cap, n_tok_chunks):
    aff = aff_ref[0]
    n_e, s = aff.shape
    bits = lax.bitcast_convert_type(aff, jnp.int32)
    capf = float(cap)

    def count(mask):
        return jnp.sum(jnp.where(mask, 1.0, 0.0), axis=-1, keepdims=True)

    tau = jnp.zeros((n_e, 1), jnp.int32)
    for bit in range(30, -1, -1):
        cand = tau | (1 << bit)
        tau = jnp.where(count(bits >= cand) >= capf, cand, tau)
    gt = bits > tau
    eq = bits == tau
    need = capf - count(gt)

    blk = min(256, s // n_tok_chunks)
    r = lax.broadcasted_iota(jnp.int32, (blk, blk), 0)
    cidx = lax.broadcasted_iota(jnp.int32, (blk, blk), 1)
    upper = jnp.where(r < cidx, 1.0, 0.0).astype(BF16)

    def prefix_blocks(mask):
        off = jnp.zeros((n_e, 1), F32)
        pieces, offs = [], []
        for k in range(s // blk):
            mb = jnp.where(mask[:, k * blk:(k + 1) * blk], 1.0, 0.0)
            offs.append(off)
            pieces.append(jnp.dot(mb.astype(BF16), upper, preferred_element_type=F32) + off)
            off = off + jnp.sum(mb, axis=-1, keepdims=True)
        return pieces, offs

    eq_rank, _ = prefix_blocks(eq)
    sel_blocks = []
    for k in range(s // blk):
        sl = slice(k * blk, (k + 1) * blk)
        sel_blocks.append(jnp.logical_or(gt[:, sl], jnp.logical_and(eq[:, sl], eq_rank[k] < need)))
    sel = jnp.concatenate(sel_blocks, axis=1)
    slot, offs = prefix_blocks(sel)
    for k in range(s // blk):
        pos_ref[0, :, k * blk:(k + 1) * blk] = jnp.where(sel_blocks[k], slot[k], -1.0).astype(jnp.int32)

    lane = lax.broadcasted_iota(jnp.int32, (n_e, 128), 1)
    cnt = jnp.zeros((n_e, 128), F32)
    per = (s // n_tok_chunks) // blk
    for j in range(n_tok_chunks):
        cnt = jnp.where(lane == j, offs[j * per], cnt)
    cnt_ref[0] = cnt.astype(jnp.int32)


def _routing(aff, cap):
    bsz, n_e, s = aff.shape
    return pl.pallas_call(
        functools.partial(_route_kernel, cap=cap, n_tok_chunks=N_TOK_CHUNKS),
        grid=(bsz,),
        in_specs=[pl.BlockSpec((1, n_e, s), lambda b: (b, 0, 0))],
        out_specs=[pl.BlockSpec((1, n_e, s), lambda b: (b, 0, 0)),
                   pl.BlockSpec((1, n_e, 128), lambda b: (b, 0, 0))],
        out_shape=[jax.ShapeDtypeStruct((bsz, n_e, s), jnp.int32),
                   jax.ShapeDtypeStruct((bsz, n_e, 128), jnp.int32)],
        compiler_params=_params(("parallel",)),
        name="expert_choice_routing",
    )(aff)


def _moe_kernel(cnt_ref, pos_ref, aff_ref, h2_ref, w1_ref, w3_ref, w2_ref, gate_ref, x1_hbm, out_ref,
                xe_scr, g_scr, y_scr, x1_sem, *, cap):
    b = pl.program_id(0)
    e = pl.program_id(1)
    n_e = pl.num_programs(1)
    s = h2_ref.shape[1]
    tchunk = s // N_TOK_CHUNKS

    def residual_copy():
        return pltpu.make_async_copy(x1_hbm.at[b], out_ref.at[0], x1_sem.at[0])

    @pl.when(e == 0)
    def _():
        residual_copy().start()

    xe_scr[...] = jnp.zeros_like(xe_scr)
    g_scr[...] = jnp.zeros_like(g_scr)
    base = (b * n_e + e) * N_TOK_CHUNKS

    def walk(cchunk, visit):
        n_cap = cap // cchunk
        slot_iota = lax.broadcasted_iota(jnp.int32, (cchunk, tchunk), 0)
        j = jnp.int32(0)
        i = jnp.int32(0)
        for _ in range(N_TOK_CHUNKS + n_cap - 1):
            jc = jnp.minimum(j, N_TOK_CHUNKS - 1)
            ic = jnp.minimum(i, n_cap - 1)
            c_lo = cnt_ref[base + jc]
            c_hi = jnp.where(jc + 1 < N_TOK_CHUNKS, cnt_ref[base + jnp.minimum(jc + 1, N_TOK_CHUNKS - 1)], cap)
            s_hi = (ic + 1) * cchunk
            valid = jnp.logical_and(jnp.maximum(c_lo, ic * cchunk) < jnp.minimum(c_hi, s_hi),
                                    jnp.logical_and(j < N_TOK_CHUNKS, i < n_cap))
            slot0 = jnp.where(valid, ic * cchunk, -2 * cap)
            sel = pos_ref[0, 0, pl.ds(jc, 1), :] == slot_iota + slot0
            visit(sel, jc, pl.ds(pl.multiple_of(ic * cchunk, cchunk), cchunk),
                  pl.ds(pl.multiple_of(jc * tchunk, tchunk), tchunk))
            j = j + jnp.where(c_hi <= s_hi, 1, 0)
            i = i + jnp.where(s_hi <= c_hi, 1, 0)

    def gather(sel, jc, rows, toks):
        xe_scr[rows, :] += jnp.dot(jnp.where(sel, 1.0, 0.0).astype(BF16), h2_ref[0, toks, :],
                                   preferred_element_type=F32)
        g_scr[rows, :] += jnp.sum(jnp.where(sel, aff_ref[0, 0, pl.ds(jc, 1), :], 0.0), axis=-1, keepdims=True)

    walk(cap // N_CAP_CHUNKS, gather)

    xe = xe_scr[...].astype(BF16)
    a = jnp.dot(xe, w1_ref[0], preferred_element_type=F32)
    gate = jnp.dot(xe, w3_ref[0], preferred_element_type=F32)
    hmid = ((a * _sigmoid(a)) * gate).astype(BF16)
    y = jnp.dot(hmid, w2_ref[0], preferred_element_type=F32) * g_scr[...] * gate_ref[0]
    y_scr[...] = y.astype(BF16)

    @pl.when(e == 0)
    def _():
        residual_copy().wait()

    def scatter(sel, jc, rows, toks):
        out_ref[0, toks, :] += lax.dot_general(jnp.where(sel, 1.0, 0.0).astype(BF16), y_scr[rows, :],
                                               TN_DIMS, preferred_element_type=F32)

    walk(cap // N_SCATTER_CHUNKS, scatter)


def _moe(cnt, pos, aff, h2, w1, w3, w2, gate2, x1, cap):
    bsz, s, d = h2.shape
    n_e = w1.shape[0]
    f = w1.shape[2]
    tok_row = lambda b, e, c: (b, e, 0, 0)
    wspec = lambda b, e, c: (e, 0, 0)
    resident = lambda b, e, c: (b, 0, 0)
    grid_spec = pltpu.PrefetchScalarGridSpec(
        num_scalar_prefetch=1,
        grid=(bsz, n_e),
        in_specs=[pl.BlockSpec((1, 1, N_TOK_CHUNKS, s // N_TOK_CHUNKS), tok_row),
                  pl.BlockSpec((1, 1, N_TOK_CHUNKS, s // N_TOK_CHUNKS), tok_row),
                  pl.BlockSpec((1, s, d), resident, pipeline_mode=pl.Buffered(1)),
                  pl.BlockSpec((1, d, f), wspec),
                  pl.BlockSpec((1, d, f), wspec),
                  pl.BlockSpec((1, f, d), wspec),
                  pl.BlockSpec((1, 1, d), resident),
                  pl.BlockSpec(memory_space=pl.ANY)],
        out_specs=pl.BlockSpec((1, s, d), resident, pipeline_mode=pl.Buffered(1)),
        scratch_shapes=[pltpu.VMEM((cap, d), F32), pltpu.VMEM((cap, 1), F32), pltpu.VMEM((cap, d), BF16),
                        pltpu.SemaphoreType.DMA((1,))],
    )
    return pl.pallas_call(
        functools.partial(_moe_kernel, cap=cap),
        grid_spec=grid_spec,
        out_shape=jax.ShapeDtypeStruct((bsz, s, d), F32),
        compiler_params=_params(("arbitrary", "arbitrary"), VMEM_LIMIT),
        name="expert_choice_ffn",
    )(cnt.reshape(-1), pos.reshape(bsz, n_e, N_TOK_CHUNKS, -1), aff.reshape(bsz, n_e, N_TOK_CHUNKS, -1), h2,
      w1.astype(BF16), w3.astype(BF16), w2.astype(BF16), gate2, x1)


def kernel(x, c, w_mod, b_mod, g_norm1, w_in, conv_w, conv_b, lru_w_a, lru_b_a, lru_w_x, lru_b_x,
           lru_lambda, g_q, g_k, lambda_qk, g_attn_out, rel_bias, w_out, g_norm2, w_router, w1, w3, w2):
    bsz, s, d = x.shape
    depth = w_mod.shape[0]
    cap = max(1, EC_FACTOR * s // N_EXPERTS)
    bias_tiles = _bias_tiles(rel_bias, min(ATT_TILE, s))
    for l in range(depth):
        mod = _modulation(c, w_mod[l], b_mod[l])
        shift1, scale1, gate1, shift2, scale2, gate2 = [m.reshape(bsz, 1, d) for m in jnp.split(mod, 6, axis=-1)]
        x_lru, gz, qn, kn, vt = _in_projection(x, scale1, shift1, g_norm1[l], w_in[l], g_q[l], g_k[l],
                                               min(IN_PROJ_ROWS, s))
        y_lru = _rg_lru(x_lru, gz, conv_w[l], conv_b[l], lru_w_a[l], lru_b_a[l], lru_w_x[l], lru_b_x[l],
                        lru_lambda[l])
        lam_init = 0.8 - 0.6 * math.exp(-0.3 * l)
        y_att = _diff_attention(qn, kn, vt, rel_bias, bias_tiles, lambda_qk[l], g_attn_out[l], lam_init)
        x1, h2, aff = _out_projection(y_lru, y_att, x, gate1, scale2, shift2, g_norm2[l], w_out[l],
                                      w_router[l], min(OUT_PROJ_ROWS, s))
        pos, cnt = _routing(aff, cap)
        x = _moe(cnt[:, :, :N_TOK_CHUNKS], pos, aff, h2, w1[l], w3[l], w2[l], gate2, x1, cap)
    return x
```

```python
import functools
import math

import jax
import jax.numpy as jnp
from jax import lax
from jax.experimental import pallas as pl
from jax.experimental.pallas import tpu as pltpu

F32 = jnp.float32
BF16 = jnp.bfloat16

D_MODEL = 1024
D_LRU = 512
LRU_BLOCK = 64
LRU_C = 8.0
CONV_W = 4
N_HEADS = 4
HEAD_DV = 128
HEAD_DK = 64
D_ATT = N_HEADS * HEAD_DV
N_BUCKETS = 32
N_EXPERTS = 16
EC_FACTOR = 2
EPS = 1e-6
LOG2E = math.log2(math.e)

IN_PROJ_ROWS = 1024
OUT_PROJ_ROWS = 1024
LRU_HALF = 256
SCAN_LANES = 8
ATT_TILE = 256
PV_ONES = 16
ATT_WIDTH = 1
N_TOK_CHUNKS = 8
N_CAP_CHUNKS = 4
N_SCATTER_CHUNKS = 2
VMEM_LIMIT = 56 * 1024 * 1024

NT_DIMS = (((1,), (1,)), ((), ()))
TN_DIMS = (((0,), (0,)), ((), ()))


def _sigmoid(x):
    return 1.0 / (1.0 + jnp.exp(-x))


def _params(sem, vmem=None, flags=None):
    return pltpu.CompilerParams(dimension_semantics=sem, vmem_limit_bytes=vmem, flags=flags)


def _mod_kernel(c_ref, w_ref, b_ref, o_ref):
    c = c_ref[...]
    o_ref[...] = jnp.dot((c * _sigmoid(c)).astype(BF16), w_ref[...].astype(BF16),
                         preferred_element_type=F32) + b_ref[...]


def _modulation(c, w_mod, b_mod):
    bsz, d = c.shape
    n = w_mod.shape[1]
    return pl.pallas_call(
        _mod_kernel,
        grid=(n // d,),
        in_specs=[pl.BlockSpec((bsz, d), lambda j: (0, 0)),
                  pl.BlockSpec((d, d), lambda j: (0, j)),
                  pl.BlockSpec((1, d), lambda j: (0, j))],
        out_specs=pl.BlockSpec((bsz, d), lambda j: (0, j)),
        out_shape=jax.ShapeDtypeStruct((bsz, n), F32),
        compiler_params=_params(("arbitrary",)),
        name="adaln_mod",
    )(c, w_mod, b_mod.reshape(1, n))


def _bias_kernel(tab_ref, o_ref):
    h = pl.program_id(0)
    t = o_ref.shape[-1]
    key = lax.broadcasted_iota(jnp.int32, (t, t), 0)
    qry = lax.broadcasted_iota(jnp.int32, (t, t), 1)
    half = N_BUCKETS // 2
    max_exact = half // 2
    for d in range(5):
        if d == 0:
            o_ref[0, d] = jnp.full((t, t), tab_ref[half - 1, h] * LOG2E, F32)
        elif d == 4:
            o_ref[0, d] = jnp.full((t, t), tab_ref[N_BUCKETS - 1, h] * LOG2E, F32)
        else:
            rel = (d - 2) * t + key - qry
            n = jnp.abs(rel)
            n2 = n * n
            large = jnp.full((t, t), max_exact, jnp.int32)
            for k in range(1, half - max_exact):
                large = large + jnp.where(n2 >= (max_exact * max_exact) * (2 ** k), 1, 0)
            idx = jnp.where(n < max_exact, n, large) + jnp.where(rel > 0, half, 0)
            val = jnp.zeros((t, t), F32)
            for j in range(N_BUCKETS):
                val = jnp.where(idx == j, tab_ref[j, h] * LOG2E, val)
            o_ref[0, d] = val


def _bias_tiles(rel_bias, t):
    return pl.pallas_call(
        _bias_kernel,
        grid=(N_HEADS,),
        in_specs=[pl.BlockSpec(memory_space=pltpu.SMEM)],
        out_specs=pl.BlockSpec((1, 5, t, t), lambda h: (h, 0, 0, 0)),
        out_shape=jax.ShapeDtypeStruct((N_HEADS, 5, t, t), F32),
        compiler_params=_params(("arbitrary",)),
        name="t5_bias_tiles",
    )(rel_bias)


def _inproj_kernel(x_ref, sc_ref, sh_ref, g1_ref, w_ref, wvt_ref, mseg_ref, gq_ref, gk_ref,
                   xl_ref, gz_ref, q_ref, k_ref, vt_ref):
    x = x_ref[0]
    ms = jnp.mean(x * x, axis=-1, keepdims=True)
    h = (x * lax.rsqrt(ms + EPS) * g1_ref[...]) * (1.0 + sc_ref[0]) + sh_ref[0]
    hb = h.astype(BF16)

    def proj(lo, width):
        return jnp.dot(hb, w_ref[:, lo:lo + width], preferred_element_type=F32)

    def qk_norm(t, g):
        ss = jnp.dot((t * t).astype(BF16), mseg_ref[...], preferred_element_type=F32)
        return t * lax.rsqrt(ss * (1.0 / HEAD_DK) + EPS) * g

    xl_ref[0] = proj(0, D_LRU)
    z = proj(D_LRU, D_LRU)
    cdf = 0.5 * (1.0 + jnp.tanh(math.sqrt(2.0 / math.pi) * (z + 0.044715 * (z * z * z))))
    gz_ref[0] = (z * cdf).astype(BF16)
    q_ref[0] = (qk_norm(proj(2 * D_LRU, D_ATT), gq_ref[...]) * (HEAD_DK ** -0.5 * LOG2E)).astype(BF16)
    k_ref[0] = qk_norm(proj(2 * D_LRU + D_ATT, D_ATT), gk_ref[...]).astype(BF16)
    vt_ref[0] = lax.dot_general(wvt_ref[...], hb, NT_DIMS,
                                preferred_element_type=F32).astype(BF16)


def _in_projection(x, scale1, shift1, g_norm1, w_in, g_q, g_k, tm):
    bsz, s, d = x.shape
    n = w_in.shape[1] - D_ATT
    w_main = w_in[:, :n].astype(BF16)
    w_vt = w_in[:, n:].T.astype(BF16)
    seg = jnp.arange(D_ATT, dtype=jnp.int32) // HEAD_DK
    mseg = (seg[:, None] == seg[None, :]).astype(BF16)
    n_sub = D_ATT // HEAD_DK
    row = lambda b, i: (b, i, 0)
    vec = lambda b, i: (b, 0, 0)
    full = lambda b, i: (0, 0)
    out_block = pl.BlockSpec((1, tm, D_LRU), row)
    return pl.pallas_call(
        _inproj_kernel,
        grid=(bsz, s // tm),
        in_specs=[pl.BlockSpec((1, tm, d), row),
                  pl.BlockSpec((1, 1, d), vec),
                  pl.BlockSpec((1, 1, d), vec),
                  pl.BlockSpec((1, d), full),
                  pl.BlockSpec((d, n), full),
                  pl.BlockSpec((D_ATT, d), full),
                  pl.BlockSpec((D_ATT, D_ATT), full),
                  pl.BlockSpec((1, D_ATT), full),
                  pl.BlockSpec((1, D_ATT), full)],
        out_specs=[out_block] * 4 + [pl.BlockSpec((1, D_ATT, tm), lambda b, i: (b, 0, i))],
        out_shape=[jax.ShapeDtypeStruct((bsz, s, D_LRU), F32)]
                  + [jax.ShapeDtypeStruct((bsz, s, D_LRU), BF16)] * 3
                  + [jax.ShapeDtypeStruct((bsz, D_ATT, s), BF16)],
        compiler_params=_params(("parallel", "parallel"), VMEM_LIMIT),
        name="norm1_in_proj",
    )(x, scale1, shift1, g_norm1.reshape(1, d), w_main, w_vt, mseg,
      jnp.tile(g_q, n_sub).reshape(1, D_ATT), jnp.tile(g_k, n_sub).reshape(1, D_ATT))


def _lru_kernel(x_ref, gz_ref, cw_ref, cb_ref, wa_ref, wx_ref, ba_ref, bx_ref, lam_ref, y_ref,
                xpad, a_f, u_f, a_b, u_b, *, tc):
    s = x_ref.shape[1]
    c = x_ref.shape[2]
    n_chunks = s // tc
    n_slab = c // 128
    pitch = a_f.shape[1] // n_chunks
    zeros8 = jnp.zeros((8, c), F32)
    xpad[0:8, :] = zeros8
    xpad[s + 8:s + 16, :] = zeros8

    def fill(ci, carry):
        t0 = pl.multiple_of(ci * tc, tc)
        xpad[pl.ds(t0 + 8, tc), :] = x_ref[0, pl.ds(t0, tc), :]
        return carry

    lax.fori_loop(0, n_chunks, fill, 0)

    cw = cw_ref[...]
    cb = cb_ref[...]
    decay = []
    for d in range(2):
        lam = lam_ref[d]
        softplus_neg = jnp.maximum(-lam, 0.0) + jnp.log(1.0 + jnp.exp(-jnp.abs(lam)))
        decay.append(-LRU_C * softplus_neg)
    a_scr = (a_f, a_b)
    u_scr = (u_f, u_b)

    def gates(ci, carry):
        t0 = pl.multiple_of(ci * tc, tc)
        xw = xpad[pl.ds(t0, tc + 16), :]
        xc = (cw[0:1] * pltpu.roll(xw, 2, 0)[8:8 + tc]
              + cw[1:2] * pltpu.roll(xw, 1, 0)[8:8 + tc]
              + cw[2:3] * xw[8:8 + tc]
              + cw[3:4] * pltpu.roll(xw, tc + 15, 0)[8:8 + tc]) + cb
        xcb = xc.astype(BF16)
        for d in range(2):
            r = _sigmoid(jnp.dot(xcb, wa_ref[d], preferred_element_type=F32) + ba_ref[d])
            i = _sigmoid(jnp.dot(xcb, wx_ref[d], preferred_element_type=F32) + bx_ref[d])
            a = jnp.exp(r * decay[d])
            u = jnp.sqrt(1.0 - a * a) * (i * xc)
            r0 = pl.multiple_of(ci * pitch, 8)
            for sl in range(n_slab):
                a_scr[d][sl, pl.ds(r0, tc), :] = a[:, sl * 128:(sl + 1) * 128]
                u_scr[d][sl, pl.ds(r0, tc), :] = u[:, sl * 128:(sl + 1) * 128]
        return carry

    lax.fori_loop(0, n_chunks, gates, 0)

    def rows(tt):
        return pl.ds(tt, n_chunks, stride=pitch)

    def step(tt, carry):
        out = []
        for d in range(2):
            t_loc = tt if d == 0 else tc - 1 - tt
            for sl in range(n_slab):
                h, p = carry[len(out)]
                a8 = a_scr[d][sl, rows(t_loc), :]
                h = a8 * h + u_scr[d][sl, rows(t_loc), :]
                p = a8 * p
                u_scr[d][sl, rows(t_loc), :] = h
                a_scr[d][sl, rows(t_loc), :] = p
                out.append((h, p))
        return tuple(out)

    init = (jnp.zeros((n_chunks, 128), F32), jnp.ones((n_chunks, 128), F32))
    lax.fori_loop(0, tc, step, (init,) * (2 * n_slab), unroll=4)

    for sl in range(n_slab):
        lanes = slice(sl * 128, (sl + 1) * 128)
        h_end, p_end = u_f[sl, rows(tc - 1), :], a_f[sl, rows(tc - 1), :]
        h_beg, p_beg = u_b[sl, rows(0), :], a_b[sl, rows(0), :]
        carry_f = [jnp.zeros((1, 128), F32)]
        for r in range(1, n_chunks):
            carry_f.append(p_end[r - 1:r] * carry_f[-1] + h_end[r - 1:r])
        carry_b = [jnp.zeros((1, 128), F32)]
        for r in range(n_chunks - 2, -1, -1):
            carry_b.insert(0, p_beg[r + 1:r + 2] * carry_b[0] + h_beg[r + 1:r + 2])
        for r in range(n_chunks):
            blk = slice(r * pitch, r * pitch + tc)
            hsum = (u_f[sl, blk, :] + a_f[sl, blk, :] * carry_f[r]
                    + u_b[sl, blk, :] + a_b[sl, blk, :] * carry_b[r])
            y_ref[0, r * tc:(r + 1) * tc, lanes] = (hsum * gz_ref[0, r * tc:(r + 1) * tc, lanes].astype(F32)).astype(BF16)


def _block_diag(w, half):
    n_dir, n_blocks, blk, _ = w.shape
    per = half // blk
    n_half = n_blocks // per
    w = w.reshape(n_dir, n_half, per, blk, blk)
    eye = jnp.eye(per, dtype=w.dtype)
    out = w[:, :, :, :, None, :] * eye[None, None, :, None, :, None]
    return out.reshape(n_dir, n_half, half, half)


def _rg_lru(x_lru, gz, conv_w, conv_b, w_a, b_a, w_x, b_x, lam):
    bsz, s, c = x_lru.shape
    half = LRU_HALF
    tc = s // SCAN_LANES
    pitch = tc + 8 if (tc // 8) % 2 == 0 else tc + 16
    n_half = c // half
    wa = _block_diag(w_a, half).astype(BF16)
    wx = _block_diag(w_x, half).astype(BF16)
    seq = lambda b, p: (b, 0, p)
    chan = lambda b, p: (0, p)
    dirchan = lambda b, p: (0, 0, p)
    blk = lambda b, p: (0, p, 0, 0)
    return pl.pallas_call(
        functools.partial(_lru_kernel, tc=tc),
        grid=(bsz, n_half),
        in_specs=[pl.BlockSpec((1, s, half), seq),
                  pl.BlockSpec((1, s, half), seq),
                  pl.BlockSpec((CONV_W, half), chan),
                  pl.BlockSpec((1, half), chan),
                  pl.BlockSpec((2, None, half, half), blk),
                  pl.BlockSpec((2, None, half, half), blk),
                  pl.BlockSpec((2, 1, half), dirchan),
                  pl.BlockSpec((2, 1, half), dirchan),
                  pl.BlockSpec((2, 1, half), dirchan)],
        out_specs=pl.BlockSpec((1, s, half), seq),
        out_shape=jax.ShapeDtypeStruct((bsz, s, c), BF16),
        scratch_shapes=[pltpu.VMEM((s + 16, half), F32)]
                       + [pltpu.VMEM((half // 128, SCAN_LANES * pitch, 128), F32)] * 4,
        compiler_params=_params(("parallel", "parallel"), VMEM_LIMIT),
        name="rg_lru",
    )(x_lru, gz, conv_w.reshape(CONV_W, c), conv_b.reshape(1, c), wa, wx,
      b_a.reshape(2, 1, c), b_x.reshape(2, 1, c), lam.reshape(2, 1, c))


def _attn_kernel(q_ref, k_ref, vt_ref, bias_ref, lq_ref, go_ref, o_ref,
                 s_even, s_odd, m_even, m_odd, o_even, o_odd, p_scr, *, lam_init):
    t = bias_ref.shape[-1]
    width = s_even.shape[0]
    n_k = k_ref.shape[1] // t
    n_q = q_ref.shape[1] // t
    n_steps = n_q // width
    subs = [(w, u) for w in range(width) for u in range(2)]
    chunk = 2
    pv_chunk = 2
    n_c = n_k // chunk

    s_odd[...] = jnp.zeros_like(s_odd)
    m_odd[...] = jnp.zeros_like(m_odd)
    o_odd[...] = jnp.ones_like(o_odd)
    even = dict(s=s_even, m=m_even, o=o_even)
    odd = dict(s=s_odd, m=m_odd, o=o_odd)
    ones_rows = jnp.ones((PV_ONES, pv_chunk * t), BF16)

    def fold(x, op):
        parts = [x[r * 8:(r + 1) * 8, :] for r in range(t // 8)]
        acc = parts[:2]
        for r in range(2, len(parts)):
            acc[r % 2] = op(acc[r % 2], parts[r])
        return op(acc[0], acc[1])

    def bias_tile(j, tile):
        return bias_ref[0, jnp.clip(j - tile, -2, 2) + 2]

    def finish(prev, w, tile):
        outs = [prev['o'][w, u, :HEAD_DV, :] / prev['o'][w, u, HEAD_DV:HEAD_DV + 1, :] for u in range(2)]
        lq = lq_ref[...]
        lam = (jnp.exp(jnp.sum(lq[0:1] * lq[1:2], axis=-1, keepdims=True))
               - jnp.exp(jnp.sum(lq[2:3] * lq[3:4], axis=-1, keepdims=True)) + lam_init)
        o = (outs[0] - lam * outs[1]).T
        ms = jnp.mean(o * o, axis=-1, keepdims=True)
        o_ref[0, pl.ds(pl.multiple_of(tile * t, t), t), :] = (
            (o * lax.rsqrt(ms + EPS) * go_ref[...]) * (1.0 - lam_init)).astype(BF16)

    def step(i, cur, prev, scores=True, softmax=True):
        tiles_a = [i * width + w for w in range(width)]
        for w in range(width):
            finish(prev, w, jnp.maximum((i - 2) * width + w, 0))
        s_cur, s_prev = cur['s'], prev['s']
        m_b = {wu: jnp.max(prev['m'][wu], axis=0, keepdims=True) for wu in subs} if softmax else {}
        m_acc = {wu: jnp.full((8, t), -jnp.inf, F32) for wu in subs}
        q_sub = {}
        if scores:
            for w in range(width):
                q = q_ref[0, pl.ds(pl.multiple_of(tiles_a[w] * t, t), t), :]
                lane = lax.broadcasted_iota(jnp.int32, q.shape, 1)
                zero = jnp.zeros_like(q)
                q_sub[w, 0] = jnp.where(lane < HEAD_DK, q, zero)
                q_sub[w, 1] = jnp.where(lane >= HEAD_DK, q, zero)
        for c in range(n_c):
            rows = slice(c * chunk * t, (c + 1) * chunk * t)
            for w, u in (subs if scores else []):
                sc_all = lax.dot_general(k_ref[0, rows, :], q_sub[w, u], NT_DIMS, preferred_element_type=F32)
                for jj in range(chunk):
                    j = c * chunk + jj
                    sc = sc_all[jj * t:(jj + 1) * t, :] + bias_tile(j, tiles_a[w])
                    s_cur[w, u, j] = sc
                    m_acc[w, u] = jnp.maximum(m_acc[w, u], fold(sc, jnp.maximum))
            for w, u in (subs if softmax else []):
                for jj in range(chunk):
                    j = c * chunk + jj
                    p = jnp.exp2(s_prev[w, u, j] - m_b[w, u])
                    p_scr[w, u, j * t:(j + 1) * t, :] = p.astype(BF16)
                if ((c + 1) * chunk) % pv_chunk == 0:
                    keys = slice(((c + 1) * chunk - pv_chunk) * t, (c + 1) * chunk * t)
                    part = jnp.dot(jnp.concatenate([vt_ref[0, :, keys], ones_rows], axis=0), p_scr[w, u, keys, :],
                                   preferred_element_type=F32)
                    if (c + 1) * chunk == pv_chunk:
                        cur['o'][w, u] = part
                    else:
                        cur['o'][w, u] += part

        if scores:
            for w, u in subs:
                cur['m'][w, u] = m_acc[w, u]

    def pair(pi, carry):
        @pl.when(pi >= 0)
        def _():
            step(2 * pi, even, odd)

        @pl.when(pi < n_steps)
        def _():
            step(2 * pi + 1, odd, even)

        return carry

    lax.fori_loop(0, n_steps // 2, pair, 0)
    step(n_steps, even, odd, scores=False)
    for w in range(width):
        finish(even, w, n_q - width + w)


def _diff_attention(qn, kn, vt, bias_tiles, lambda_qk, g_o, lam_init):
    bsz, s, _ = qn.shape
    t = bias_tiles.shape[-1]
    n_k = s // t
    width = ATT_WIDTH
    assert (s // t) % (2 * width) == 0, "the query-tile pipeline advances two groups of tiles per loop trip"
    scores = pltpu.VMEM((width, 2, n_k, t, t), F32)
    maxima = pltpu.VMEM((width, 2, 8, t), F32)
    pv_acc = pltpu.VMEM((width, 2, HEAD_DV + PV_ONES, t), F32)
    seq = lambda b, h: (b, 0, h)
    return pl.pallas_call(
        functools.partial(_attn_kernel, lam_init=lam_init),
        grid=(bsz, N_HEADS),
        in_specs=[pl.BlockSpec((1, s, HEAD_DV), seq),
                  pl.BlockSpec((1, s, HEAD_DV), seq),
                  pl.BlockSpec((1, HEAD_DV, s), lambda b, h: (b, h, 0)),
                  pl.BlockSpec((1, 5, t, t), lambda b, h: (h, 0, 0, 0)),
                  pl.BlockSpec((4, HEAD_DK), lambda b, h: (0, 0)),
                  pl.BlockSpec((1, HEAD_DV), lambda b, h: (0, 0))],
        out_specs=pl.BlockSpec((1, s, HEAD_DV), seq),
        out_shape=jax.ShapeDtypeStruct((bsz, s, D_ATT), BF16),
        scratch_shapes=[scores, scores, maxima, maxima, pv_acc, pv_acc, pltpu.VMEM((width, 2, s, t), BF16)],
        compiler_params=_params(("parallel", "parallel"), VMEM_LIMIT),
        name="diff_attention",
    )(qn, kn, vt, bias_tiles, lambda_qk, g_o.reshape(1, HEAD_DV))


def _outproj_kernel(yl_ref, ya_ref, x_ref, gate_ref, sc_ref, sh_ref, g2_ref, wo_ref, wr_ref,
                    x1_ref, h2_ref, aff_ref):
    mix = (jnp.dot(yl_ref[0], wo_ref[0:D_LRU, :], preferred_element_type=F32)
           + jnp.dot(ya_ref[0], wo_ref[D_LRU:D_LRU + D_ATT, :], preferred_element_type=F32))
    x1 = x_ref[0] + gate_ref[0] * mix
    x1_ref[0] = x1
    ms = jnp.mean(x1 * x1, axis=-1, keepdims=True)
    h2 = (x1 * lax.rsqrt(ms + EPS) * g2_ref[...]) * (1.0 + sc_ref[0]) + sh_ref[0]
    h2b = h2.astype(BF16)
    h2_ref[0] = h2b
    logits = lax.dot_general(wr_ref[...], h2b, NT_DIMS, preferred_element_type=F32)
    ex = jnp.exp(logits - jnp.max(logits, axis=0, keepdims=True))
    aff_ref[0] = ex / jnp.sum(ex, axis=0, keepdims=True)


def _out_projection(y_lru, y_att, x, gate1, scale2, shift2, g_norm2, w_out, w_router, tm):
    bsz, s, d = x.shape
    row = lambda b, i: (b, i, 0)
    vec = lambda b, i: (b, 0, 0)
    full = lambda b, i: (0, 0)
    return pl.pallas_call(
        _outproj_kernel,
        grid=(bsz, s // tm),
        in_specs=[pl.BlockSpec((1, tm, D_LRU), row),
                  pl.BlockSpec((1, tm, D_ATT), row),
                  pl.BlockSpec((1, tm, d), row),
                  pl.BlockSpec((1, 1, d), vec),
                  pl.BlockSpec((1, 1, d), vec),
                  pl.BlockSpec((1, 1, d), vec),
                  pl.BlockSpec((1, d), full),
                  pl.BlockSpec((D_LRU + D_ATT, d), full),
                  pl.BlockSpec((N_EXPERTS, d), full)],
        out_specs=[pl.BlockSpec((1, tm, d), row),
                   pl.BlockSpec((1, tm, d), row),
                   pl.BlockSpec((1, N_EXPERTS, tm), lambda b, i: (b, 0, i))],
        out_shape=[jax.ShapeDtypeStruct((bsz, s, d), F32),
                   jax.ShapeDtypeStruct((bsz, s, d), BF16),
                   jax.ShapeDtypeStruct((bsz, N_EXPERTS, s), F32)],
        compiler_params=_params(("parallel", "parallel"), VMEM_LIMIT),
        name="out_proj_norm2_router",
    )(y_lru, y_att, x, gate1, scale2, shift2, g_norm2.reshape(1, d), w_out.astype(BF16),
      w_router.T.astype(BF16))


def _route_kernel(aff_ref, pos_ref, cnt_ref, *, cap, n_tok_chunks):
    aff = aff_ref[0]
    n_e, s = aff.shape
    bits = lax.bitcast_convert_type(aff, jnp.int32)
    capf = float(cap)

    def count(mask):
        return jnp.sum(jnp.where(mask, 1.0, 0.0), axis=-1, keepdims=True)

    tau = jnp.zeros((n_e, 1), jnp.int32)
    for bit in range(30, -1, -1):
        cand = tau | (1 << bit)
        tau = jnp.where(count(bits >= cand) >= capf, cand, tau)
    gt = bits > tau
    eq = bits == tau
    need = capf - count(gt)

    blk = min(256, s // n_tok_chunks)
    r = lax.broadcasted_iota(jnp.int32, (blk, blk), 0)
    cidx = lax.broadcasted_iota(jnp.int32, (blk, blk), 1)
    upper = jnp.where(r < cidx, 1.0, 0.0).astype(BF16)

    def prefix_blocks(mask):
        off = jnp.zeros((n_e, 1), F32)
        pieces, offs = [], []
        for k in range(s // blk):
            mb = jnp.where(mask[:, k * blk:(k + 1) * blk], 1.0, 0.0)
            offs.append(off)
            pieces.append(jnp.dot(mb.astype(BF16), upper, preferred_element_type=F32) + off)
            off = off + jnp.sum(mb, axis=-1, keepdims=True)
        return pieces, offs

    eq_rank, _ = prefix_blocks(eq)
    sel_blocks = []
    for k in range(s // blk):
        sl = slice(k * blk, (k + 1) * blk)
        sel_blocks.append(jnp.logical_or(gt[:, sl], jnp.logical_and(eq[:, sl], eq_rank[k] < need)))
    sel = jnp.concatenate(sel_blocks, axis=1)
    slot, offs = prefix_blocks(sel)
    for k in range(s // blk):
        pos_ref[0, :, k * blk:(k + 1) * blk] = jnp.where(sel_blocks[k], slot[k], -1.0).astype(jnp.int32)

    lane = lax.broadcasted_iota(jnp.int32, (n_e, 128), 1)
    cnt = jnp.zeros((n_e, 128), F32)
    per = (s // n_tok_chunks) // blk
    for j in range(n_tok_chunks):
        cnt = jnp.where(lane == j, offs[j * per], cnt)
    cnt_ref[0] = cnt.astype(jnp.int32)


def _routing(aff, cap):
    bsz, n_e, s = aff.shape
    return pl.pallas_call(
        functools.partial(_route_kernel, cap=cap, n_tok_chunks=N_TOK_CHUNKS),
        grid=(bsz,),
        in_specs=[pl.BlockSpec((1, n_e, s), lambda b: (b, 0, 0))],
        out_specs=[pl.BlockSpec((1, n_e, s), lambda b: (b, 0, 0)),
                   pl.BlockSpec((1, n_e, 128), lambda b: (b, 0, 0))],
        out_shape=[jax.ShapeDtypeStruct((bsz, n_e, s), jnp.int32),
                   jax.ShapeDtypeStruct((bsz, n_e, 128), jnp.int32)],
        compiler_params=_params(("parallel",)),
        name="expert_choice_routing",
    )(aff)


def _moe_kernel(cnt_ref, pos_ref, aff_ref, h2_ref, w1_ref, w3_ref, w2_ref, gate_ref, x1_hbm, out_ref,
                xe_scr, g_scr, y_scr, x1_sem, *, cap):
    b = pl.program_id(0)
    e = pl.program_id(1)
    n_e = pl.num_programs(1)
    s = h2_ref.shape[1]
    tchunk = s // N_TOK_CHUNKS

    def residual_copy():
        return pltpu.make_async_copy(x1_hbm.at[b], out_ref.at[0], x1_sem.at[0])

    @pl.when(e == 0)
    def _():
        residual_copy().start()

    xe_scr[...] = jnp.zeros_like(xe_scr)
    g_scr[...] = jnp.zeros_like(g_scr)
    base = (b * n_e + e) * N_TOK_CHUNKS

    def walk(cchunk, visit):
        n_cap = cap // cchunk
        slot_iota = lax.broadcasted_iota(jnp.int32, (cchunk, tchunk), 0)
        j = jnp.int32(0)
        i = jnp.int32(0)
        for _ in range(N_TOK_CHUNKS + n_cap - 1):
            jc = jnp.minimum(j, N_TOK_CHUNKS - 1)
            ic = jnp.minimum(i, n_cap - 1)
            c_lo = cnt_ref[base + jc]
            c_hi = jnp.where(jc + 1 < N_TOK_CHUNKS, cnt_ref[base + jnp.minimum(jc + 1, N_TOK_CHUNKS - 1)], cap)
            s_hi = (ic + 1) * cchunk
            valid = jnp.logical_and(jnp.maximum(c_lo, ic * cchunk) < jnp.minimum(c_hi, s_hi),
                                    jnp.logical_and(j < N_TOK_CHUNKS, i < n_cap))
            slot0 = jnp.where(valid, ic * cchunk, -2 * cap)
            sel = pos_ref[0, 0, pl.ds(jc, 1), :] == slot_iota + slot0
            visit(sel, jc, pl.ds(pl.multiple_of(ic * cchunk, cchunk), cchunk),
                  pl.ds(pl.multiple_of(jc * tchunk, tchunk), tchunk))
            j = j + jnp.where(c_hi <= s_hi, 1, 0)
            i = i + jnp.where(s_hi <= c_hi, 1, 0)

    def gather(sel, jc, rows, toks):
        xe_scr[rows, :] += jnp.dot(jnp.where(sel, 1.0, 0.0).astype(BF16), h2_ref[0, toks, :],
                                   preferred_element_type=F32)
        g_scr[rows, :] += jnp.sum(jnp.where(sel, aff_ref[0, 0, pl.ds(jc, 1), :], 0.0), axis=-1, keepdims=True)

    walk(cap // N_CAP_CHUNKS, gather)

    xe = xe_scr[...].astype(BF16)
    a = jnp.dot(xe, w1_ref[0], preferred_element_type=F32)
    gate = jnp.dot(xe, w3_ref[0], preferred_element_type=F32)
    hmid = ((a * _sigmoid(a)) * gate).astype(BF16)
    y = jnp.dot(hmid, w2_ref[0], preferred_element_type=F32) * g_scr[...] * gate_ref[0]
    y_scr[...] = y.astype(BF16)

    @pl.when(e == 0)
    def _():
        residual_copy().wait()

    def scatter(sel, jc, rows, toks):
        out_ref[0, toks, :] += lax.dot_general(jnp.where(sel, 1.0, 0.0).astype(BF16), y_scr[rows, :],
                                               TN_DIMS, preferred_element_type=F32)

    walk(cap // N_SCATTER_CHUNKS, scatter)


def _moe(cnt, pos, aff, h2, w1, w3, w2, gate2, x1, cap):
    bsz, s, d = h2.shape
    n_e = w1.shape[0]
    f = w1.shape[2]
    tok_row = lambda b, e, c: (b, e, 0, 0)
    wspec = lambda b, e, c: (e, 0, 0)
    resident = lambda b, e, c: (b, 0, 0)
    grid_spec = pltpu.PrefetchScalarGridSpec(
        num_scalar_prefetch=1,
        grid=(bsz, n_e),
        in_specs=[pl.BlockSpec((1, 1, N_TOK_CHUNKS, s // N_TOK_CHUNKS), tok_row),
                  pl.BlockSpec((1, 1, N_TOK_CHUNKS, s // N_TOK_CHUNKS), tok_row),
                  pl.BlockSpec((1, s, d), resident, pipeline_mode=pl.Buffered(1)),
                  pl.BlockSpec((1, d, f), wspec),
                  pl.BlockSpec((1, d, f), wspec),
                  pl.BlockSpec((1, f, d), wspec),
                  pl.BlockSpec((1, 1, d), resident),
                  pl.BlockSpec(memory_space=pl.ANY)],
        out_specs=pl.BlockSpec((1, s, d), resident, pipeline_mode=pl.Buffered(1)),
        scratch_shapes=[pltpu.VMEM((cap, d), F32), pltpu.VMEM((cap, 1), F32), pltpu.VMEM((cap, d), BF16),
                        pltpu.SemaphoreType.DMA((1,))],
    )
    return pl.pallas_call(
        functools.partial(_moe_kernel, cap=cap),
        grid_spec=grid_spec,
        out_shape=jax.ShapeDtypeStruct((bsz, s, d), F32),
        compiler_params=_params(("arbitrary", "arbitrary"), VMEM_LIMIT),
        name="expert_choice_ffn",
    )(cnt.reshape(-1), pos.reshape(bsz, n_e, N_TOK_CHUNKS, -1), aff.reshape(bsz, n_e, N_TOK_CHUNKS, -1), h2,
      w1.astype(BF16), w3.astype(BF16), w2.astype(BF16), gate2, x1)


def kernel(x, c, w_mod, b_mod, g_norm1, w_in, conv_w, conv_b, lru_w_a, lru_b_a, lru_w_x, lru_b_x,
           lru_lambda, g_q, g_k, lambda_qk, g_attn_out, rel_bias, w_out, g_norm2, w_router, w1, w3, w2):
    bsz, s, d = x.shape
    depth = w_mod.shape[0]
    cap = max(1, EC_FACTOR * s // N_EXPERTS)
    bias_tiles = _bias_tiles(rel_bias, min(ATT_TILE, s))
    for l in range(depth):
        mod = _modulation(c, w_mod[l], b_mod[l])
        shift1, scale1, gate1, shift2, scale2, gate2 = [m.reshape(bsz, 1, d) for m in jnp.split(mod, 6, axis=-1)]
        x_lru, gz, qn, kn, vt = _in_projection(x, scale1, shift1, g_norm1[l], w_in[l], g_q[l], g_k[l],
                                               min(IN_PROJ_ROWS, s))
        y_lru = _rg_lru(x_lru, gz, conv_w[l], conv_b[l], lru_w_a[l], lru_b_a[l], lru_w_x[l], lru_b_x[l],
                        lru_lambda[l])
        lam_init = 0.8 - 0.6 * math.exp(-0.3 * l)
        y_att = _diff_attention(qn, kn, vt, bias_tiles, lambda_qk[l], g_attn_out[l], lam_init)
        x1, h2, aff = _out_projection(y_lru, y_att, x, gate1, scale2, shift2, g_norm2[l], w_out[l],
                                      w_router[l], min(OUT_PROJ_ROWS, s))
        pos, cnt = _routing(aff, cap)
        x = _moe(cnt[:, :, :N_TOK_CHUNKS], pos, aff, h2, w1[l], w3[l], w2[l], gate2, x1, cap)
    return x
```

```python
import functools
import math

import jax
import jax.numpy as jnp
from jax import lax
from jax.experimental import pallas as pl
from jax.experimental.pallas import tpu as pltpu

F32 = jnp.float32
BF16 = jnp.bfloat16

D_MODEL = 1024
D_LRU = 512
LRU_BLOCK = 64
LRU_C = 8.0
CONV_W = 4
N_HEADS = 4
HEAD_DV = 128
HEAD_DK = 64
D_ATT = N_HEADS * HEAD_DV
N_BUCKETS = 32
N_EXPERTS = 16
EC_FACTOR = 2
EPS = 1e-6
F32_TINY = 2.0 ** -126
LOG2E = math.log2(math.e)

IN_PROJ_ROWS = 1024
OUT_PROJ_ROWS = 1024
LRU_HALF = 256
SCAN_LANES = 8
ATT_TILE = 256
PV_ONES = 16
ATT_WIDTH = 1
ROUTE_GROUP = 8
N_TOK_CHUNKS = 8
N_CAP_CHUNKS = 4
N_SCATTER_CHUNKS = 2
VMEM_LIMIT = 56 * 1024 * 1024

NT_DIMS = (((1,), (1,)), ((), ()))
TN_DIMS = (((0,), (0,)), ((), ()))


def _sigmoid(x):
    return 1.0 / (1.0 + jnp.exp(-x))


def _params(sem, vmem=None, flags=None):
    return pltpu.CompilerParams(dimension_semantics=sem, vmem_limit_bytes=vmem, flags=flags)


def _mod_kernel(c_ref, w_ref, b_ref, o_ref):
    c = c_ref[...]
    o_ref[...] = jnp.dot((c * _sigmoid(c)).astype(BF16), w_ref[...].astype(BF16),
                         preferred_element_type=F32) + b_ref[...]


def _modulation(c, w_mod, b_mod):
    bsz, d = c.shape
    n = w_mod.shape[1]
    return pl.pallas_call(
        _mod_kernel,
        grid=(n // d,),
        in_specs=[pl.BlockSpec((bsz, d), lambda j: (0, 0)),
                  pl.BlockSpec((d, d), lambda j: (0, j)),
                  pl.BlockSpec((1, d), lambda j: (0, j))],
        out_specs=pl.BlockSpec((bsz, d), lambda j: (0, j)),
        out_shape=jax.ShapeDtypeStruct((bsz, n), F32),
        compiler_params=_params(("arbitrary",)),
        name="adaln_mod",
    )(c, w_mod, b_mod.reshape(1, n))


def _bias_kernel(tab_ref, o_ref):
    h = pl.program_id(0)
    t = o_ref.shape[-1]
    key = lax.broadcasted_iota(jnp.int32, (t, t), 0)
    qry = lax.broadcasted_iota(jnp.int32, (t, t), 1)
    half = N_BUCKETS // 2
    max_exact = half // 2
    for d in range(5):
        if d == 0:
            o_ref[0, d] = jnp.full((t, t), tab_ref[half - 1, h] * LOG2E, F32)
        elif d == 4:
            o_ref[0, d] = jnp.full((t, t), tab_ref[N_BUCKETS - 1, h] * LOG2E, F32)
        else:
            rel = (d - 2) * t + key - qry
            n = jnp.abs(rel)
            n2 = n * n
            large = jnp.full((t, t), max_exact, jnp.int32)
            for k in range(1, half - max_exact):
                large = large + jnp.where(n2 >= (max_exact * max_exact) * (2 ** k), 1, 0)
            idx = jnp.where(n < max_exact, n, large) + jnp.where(rel > 0, half, 0)
            val = jnp.zeros((t, t), F32)
            for j in range(N_BUCKETS):
                val = jnp.where(idx == j, tab_ref[j, h] * LOG2E, val)
            o_ref[0, d] = val


def _bias_tiles(rel_bias, t):
    return pl.pallas_call(
        _bias_kernel,
        grid=(N_HEADS,),
        in_specs=[pl.BlockSpec(memory_space=pltpu.SMEM)],
        out_specs=pl.BlockSpec((1, 5, t, t), lambda h: (h, 0, 0, 0)),
        out_shape=jax.ShapeDtypeStruct((N_HEADS, 5, t, t), F32),
        compiler_params=_params(("arbitrary",)),
        name="t5_bias_tiles",
    )(rel_bias)


def _inproj_kernel(x_ref, sc_ref, sh_ref, g1_ref, w_ref, wvt_ref, mseg_ref, gq_ref, gk_ref,
                   xl_ref, gz_ref, q_ref, k_ref, vt_ref):
    x = x_ref[0]
    ms = jnp.mean(x * x, axis=-1, keepdims=True)
    h = (x * lax.rsqrt(ms + EPS) * g1_ref[...]) * (1.0 + sc_ref[0]) + sh_ref[0]
    hb = h.astype(BF16)

    def proj(lo, width):
        return jnp.dot(hb, w_ref[:, lo:lo + width], preferred_element_type=F32)

    def qk_norm(t, g):
        ss = jnp.dot((t * t).astype(BF16), mseg_ref[...], preferred_element_type=F32)
        return t * lax.rsqrt(ss * (1.0 / HEAD_DK) + EPS) * g

    xl_ref[0] = proj(0, D_LRU)
    z = proj(D_LRU, D_LRU)
    cdf = 0.5 * (1.0 + jnp.tanh(math.sqrt(2.0 / math.pi) * (z + 0.044715 * (z * z * z))))
    gz_ref[0] = (z * cdf).astype(BF16)
    q_ref[0] = (qk_norm(proj(2 * D_LRU, D_ATT), gq_ref[...]) * (HEAD_DK ** -0.5 * LOG2E)).astype(BF16)
    k_ref[0] = qk_norm(proj(2 * D_LRU + D_ATT, D_ATT), gk_ref[...]).astype(BF16)
    vt_ref[0] = lax.dot_general(wvt_ref[...], hb, NT_DIMS,
                                preferred_element_type=F32).astype(BF16)


def _in_projection(x, scale1, shift1, g_norm1, w_in, g_q, g_k, tm):
    bsz, s, d = x.shape
    n = w_in.shape[1] - D_ATT
    w_main = w_in[:, :n].astype(BF16)
    w_vt = w_in[:, n:].T.astype(BF16)
    seg = jnp.arange(D_ATT, dtype=jnp.int32) // HEAD_DK
    mseg = (seg[:, None] == seg[None, :]).astype(BF16)
    n_sub = D_ATT // HEAD_DK
    row = lambda b, i: (b, i, 0)
    vec = lambda b, i: (b, 0, 0)
    full = lambda b, i: (0, 0)
    out_block = pl.BlockSpec((1, tm, D_LRU), row)
    return pl.pallas_call(
        _inproj_kernel,
        grid=(bsz, s // tm),
        in_specs=[pl.BlockSpec((1, tm, d), row),
                  pl.BlockSpec((1, 1, d), vec),
                  pl.BlockSpec((1, 1, d), vec),
                  pl.BlockSpec((1, d), full),
                  pl.BlockSpec((d, n), full),
                  pl.BlockSpec((D_ATT, d), full),
                  pl.BlockSpec((D_ATT, D_ATT), full),
                  pl.BlockSpec((1, D_ATT), full),
                  pl.BlockSpec((1, D_ATT), full)],
        out_specs=[out_block] * 4 + [pl.BlockSpec((1, D_ATT, tm), lambda b, i: (b, 0, i))],
        out_shape=[jax.ShapeDtypeStruct((bsz, s, D_LRU), F32)]
                  + [jax.ShapeDtypeStruct((bsz, s, D_LRU), BF16)] * 3
                  + [jax.ShapeDtypeStruct((bsz, D_ATT, s), BF16)],
        compiler_params=_params(("parallel", "parallel"), VMEM_LIMIT),
        name="norm1_in_proj",
    )(x, scale1, shift1, g_norm1.reshape(1, d), w_main, w_vt, mseg,
      jnp.tile(g_q, n_sub).reshape(1, D_ATT), jnp.tile(g_k, n_sub).reshape(1, D_ATT))


def _lru_kernel(x_ref, gz_ref, cw_ref, cb_ref, wa_ref, wx_ref, ba_ref, bx_ref, lam_ref, y_ref,
                xpad, a_f, u_f, a_b, u_b, *, tc):
    s = x_ref.shape[1]
    c = x_ref.shape[2]
    n_chunks = s // tc
    n_slab = c // 128
    pitch = a_f.shape[1] // n_chunks
    zeros8 = jnp.zeros((8, c), F32)
    xpad[0:8, :] = zeros8
    xpad[s + 8:s + 16, :] = zeros8

    def fill(ci, carry):
        t0 = pl.multiple_of(ci * tc, tc)
        xpad[pl.ds(t0 + 8, tc), :] = x_ref[0, pl.ds(t0, tc), :]
        return carry

    lax.fori_loop(0, n_chunks, fill, 0)

    cw = cw_ref[...]
    cb = cb_ref[...]
    decay = []
    for d in range(2):
        lam = lam_ref[d]
        softplus_neg = jnp.maximum(-lam, 0.0) + jnp.log(1.0 + jnp.exp(-jnp.abs(lam)))
        decay.append((-LRU_C * LOG2E) * softplus_neg)
    a_scr = (a_f, a_b)
    u_scr = (u_f, u_b)

    def gates(ci, carry):
        t0 = pl.multiple_of(ci * tc, tc)
        xw = xpad[pl.ds(t0, tc + 16), :]
        xc = (cw[0:1] * pltpu.roll(xw, 2, 0)[8:8 + tc]
              + cw[1:2] * pltpu.roll(xw, 1, 0)[8:8 + tc]
              + cw[2:3] * xw[8:8 + tc]
              + cw[3:4] * pltpu.roll(xw, tc + 15, 0)[8:8 + tc]) + cb
        xcb = xc.astype(BF16)
        for d in range(2):
            r = _sigmoid(jnp.dot(xcb, wa_ref[d], preferred_element_type=F32) + ba_ref[d])
            i = _sigmoid(jnp.dot(xcb, wx_ref[d], preferred_element_type=F32) + bx_ref[d])
            a = jnp.exp2(r * decay[d])
            v = 1.0 - a * a
            u = (v * lax.rsqrt(jnp.maximum(v, F32_TINY))) * (i * xc)
            r0 = pl.multiple_of(ci * pitch, 8)
            for sl in range(n_slab):
                a_scr[d][sl, pl.ds(r0, tc), :] = a[:, sl * 128:(sl + 1) * 128]
                u_scr[d][sl, pl.ds(r0, tc), :] = u[:, sl * 128:(sl + 1) * 128]
        return carry

    lax.fori_loop(0, n_chunks, gates, 0, unroll=2)

    def rows(tt):
        return pl.ds(tt, n_chunks, stride=pitch)

    def step(tt, carry):
        out = []
        for d in range(2):
            t_loc = tt if d == 0 else tc - 1 - tt
            for sl in range(n_slab):
                h, p = carry[len(out)]
                a8 = a_scr[d][sl, rows(t_loc), :]
                h = a8 * h + u_scr[d][sl, rows(t_loc), :]
                p = a8 * p
                u_scr[d][sl, rows(t_loc), :] = h
                a_scr[d][sl, rows(t_loc), :] = p
                out.append((h, p))
        return tuple(out)

    init = (jnp.zeros((n_chunks, 128), F32), jnp.ones((n_chunks, 128), F32))
    lax.fori_loop(0, tc, step, (init,) * (2 * n_slab), unroll=4)

    for sl in range(n_slab):
        lanes = slice(sl * 128, (sl + 1) * 128)
        h_end, p_end = u_f[sl, rows(tc - 1), :], a_f[sl, rows(tc - 1), :]
        h_beg, p_beg = u_b[sl, rows(0), :], a_b[sl, rows(0), :]
        carry_f = [jnp.zeros((1, 128), F32)]
        for r in range(1, n_chunks):
            carry_f.append(p_end[r - 1:r] * carry_f[-1] + h_end[r - 1:r])
        carry_b = [jnp.zeros((1, 128), F32)]
        for r in range(n_chunks - 2, -1, -1):
            carry_b.insert(0, p_beg[r + 1:r + 2] * carry_b[0] + h_beg[r + 1:r + 2])
        for r in range(n_chunks):
            blk = slice(r * pitch, r * pitch + tc)
            hsum = (u_f[sl, blk, :] + a_f[sl, blk, :] * carry_f[r]
                    + u_b[sl, blk, :] + a_b[sl, blk, :] * carry_b[r])
            y_ref[0, r * tc:(r + 1) * tc, lanes] = (hsum * gz_ref[0, r * tc:(r + 1) * tc, lanes].astype(F32)).astype(BF16)


def _block_diag(w, half):
    n_dir, n_blocks, blk, _ = w.shape
    per = half // blk
    n_half = n_blocks // per
    w = w.reshape(n_dir, n_half, per, blk, blk)
    eye = jnp.eye(per, dtype=w.dtype)
    out = w[:, :, :, :, None, :] * eye[None, None, :, None, :, None]
    return out.reshape(n_dir, n_half, half, half)


def _rg_lru(x_lru, gz, conv_w, conv_b, w_a, b_a, w_x, b_x, lam):
    bsz, s, c = x_lru.shape
    half = LRU_HALF
    tc = s // SCAN_LANES
    pitch = tc + 8 if (tc // 8) % 2 == 0 else tc + 16
    n_half = c // half
    wa = _block_diag(w_a, half).astype(BF16)
    wx = _block_diag(w_x, half).astype(BF16)
    seq = lambda b, p: (b, 0, p)
    chan = lambda b, p: (0, p)
    dirchan = lambda b, p: (0, 0, p)
    blk = lambda b, p: (0, p, 0, 0)
    return pl.pallas_call(
        functools.partial(_lru_kernel, tc=tc),
        grid=(bsz, n_half),
        in_specs=[pl.BlockSpec((1, s, half), seq),
                  pl.BlockSpec((1, s, half), seq),
                  pl.BlockSpec((CONV_W, half), chan),
                  pl.BlockSpec((1, half), chan),
                  pl.BlockSpec((2, None, half, half), blk),
                  pl.BlockSpec((2, None, half, half), blk),
                  pl.BlockSpec((2, 1, half), dirchan),
                  pl.BlockSpec((2, 1, half), dirchan),
                  pl.BlockSpec((2, 1, half), dirchan)],
        out_specs=pl.BlockSpec((1, s, half), seq),
        out_shape=jax.ShapeDtypeStruct((bsz, s, c), BF16),
        scratch_shapes=[pltpu.VMEM((s + 16, half), F32)]
                       + [pltpu.VMEM((half // 128, SCAN_LANES * pitch, 128), F32)] * 4,
        compiler_params=_params(("parallel", "parallel"), VMEM_LIMIT),
        name="rg_lru",
    )(x_lru, gz, conv_w.reshape(CONV_W, c), conv_b.reshape(1, c), wa, wx,
      b_a.reshape(2, 1, c), b_x.reshape(2, 1, c), lam.reshape(2, 1, c))


def _attn_kernel(q_ref, k_ref, vt_ref, bias_ref, lq_ref, go_ref, o_ref,
                 s_even, s_odd, m_even, m_odd, o_even, o_odd, p_scr, *, lam_init):
    t = bias_ref.shape[-1]
    width = s_even.shape[0]
    n_k = k_ref.shape[1] // t
    n_q = q_ref.shape[1] // t
    n_steps = n_q // width
    subs = [(w, u) for w in range(width) for u in range(2)]
    chunk = 2
    pv_chunk = 2
    n_c = n_k // chunk

    s_odd[...] = jnp.zeros_like(s_odd)
    m_odd[...] = jnp.zeros_like(m_odd)
    o_odd[...] = jnp.ones_like(o_odd)
    even = dict(s=s_even, m=m_even, o=o_even)
    odd = dict(s=s_odd, m=m_odd, o=o_odd)
    ones_rows = jnp.ones((PV_ONES, pv_chunk * t), BF16)

    def fold(x, op):
        parts = [x[r * 8:(r + 1) * 8, :] for r in range(t // 8)]
        acc = parts[:2]
        for r in range(2, len(parts)):
            acc[r % 2] = op(acc[r % 2], parts[r])
        return op(acc[0], acc[1])

    def bias_tile(j, tile):
        return bias_ref[0, jnp.clip(j - tile, -2, 2) + 2]

    def finish(prev, w, tile):
        outs = [prev['o'][w, u, :HEAD_DV, :] / prev['o'][w, u, HEAD_DV:HEAD_DV + 1, :] for u in range(2)]
        lq = lq_ref[...]
        lam = (jnp.exp(jnp.sum(lq[0:1] * lq[1:2], axis=-1, keepdims=True))
               - jnp.exp(jnp.sum(lq[2:3] * lq[3:4], axis=-1, keepdims=True)) + lam_init)
        o = (outs[0] - lam * outs[1]).T
        ms = jnp.mean(o * o, axis=-1, keepdims=True)
        o_ref[0, pl.ds(pl.multiple_of(tile * t, t), t), :] = (
            (o * lax.rsqrt(ms + EPS) * go_ref[...]) * (1.0 - lam_init)).astype(BF16)

    def step(i, cur, prev, scores=True, softmax=True):
        tiles_a = [i * width + w for w in range(width)]
        for w in range(width):
            finish(prev, w, jnp.maximum((i - 2) * width + w, 0))
        s_cur, s_prev = cur['s'], prev['s']
        m_b = {wu: jnp.max(prev['m'][wu], axis=0, keepdims=True) for wu in subs} if softmax else {}
        m_acc = {wu: jnp.full((8, t), -jnp.inf, F32) for wu in subs}
        q_sub = {}
        if scores:
            for w in range(width):
                q = q_ref[0, pl.ds(pl.multiple_of(tiles_a[w] * t, t), t), :]
                lane = lax.broadcasted_iota(jnp.int32, q.shape, 1)
                zero = jnp.zeros_like(q)
                q_sub[w, 0] = jnp.where(lane < HEAD_DK, q, zero)
                q_sub[w, 1] = jnp.where(lane >= HEAD_DK, q, zero)
        for c in range(n_c):
            rows = slice(c * chunk * t, (c + 1) * chunk * t)
            for w, u in (subs if scores else []):
                sc_all = lax.dot_general(k_ref[0, rows, :], q_sub[w, u], NT_DIMS, preferred_element_type=F32)
                for jj in range(chunk):
                    j = c * chunk + jj
                    sc = sc_all[jj * t:(jj + 1) * t, :] + bias_tile(j, tiles_a[w])
                    s_cur[w, u, j] = sc
                    m_acc[w, u] = jnp.maximum(m_acc[w, u], fold(sc, jnp.maximum))
            for w, u in (subs if softmax else []):
                for jj in range(chunk):
                    j = c * chunk + jj
                    p = jnp.exp2(s_prev[w, u, j] - m_b[w, u])
                    p_scr[w, u, j * t:(j + 1) * t, :] = p.astype(BF16)
                if ((c + 1) * chunk) % pv_chunk == 0:
                    keys = slice(((c + 1) * chunk - pv_chunk) * t, (c + 1) * chunk * t)
                    part = jnp.dot(jnp.concatenate([vt_ref[0, :, keys], ones_rows], axis=0), p_scr[w, u, keys, :],
                                   preferred_element_type=F32)
                    if (c + 1) * chunk == pv_chunk:
                        cur['o'][w, u] = part
                    else:
                        cur['o'][w, u] += part

        if scores:
            for w, u in subs:
                cur['m'][w, u] = m_acc[w, u]

    def pair(pi, carry):
        @pl.when(pi >= 0)
        def _():
            step(2 * pi, even, odd)

        @pl.when(pi < n_steps)
        def _():
            step(2 * pi + 1, odd, even)

        return carry

    lax.fori_loop(0, n_steps // 2, pair, 0)
    step(n_steps, even, odd, scores=False)
    for w in range(width):
        finish(even, w, n_q - width + w)


def _diff_attention(qn, kn, vt, bias_tiles, lambda_qk, g_o, lam_init):
    bsz, s, _ = qn.shape
    t = bias_tiles.shape[-1]
    n_k = s // t
    width = ATT_WIDTH
    assert (s // t) % (2 * width) == 0, "the query-tile pipeline advances two groups of tiles per loop trip"
    scores = pltpu.VMEM((width, 2, n_k, t, t), F32)
    maxima = pltpu.VMEM((width, 2, 8, t), F32)
    pv_acc = pltpu.VMEM((width, 2, HEAD_DV + PV_ONES, t), F32)
    seq = lambda b, h: (b, 0, h)
    return pl.pallas_call(
        functools.partial(_attn_kernel, lam_init=lam_init),
        grid=(bsz, N_HEADS),
        in_specs=[pl.BlockSpec((1, s, HEAD_DV), seq),
                  pl.BlockSpec((1, s, HEAD_DV), seq),
                  pl.BlockSpec((1, HEAD_DV, s), lambda b, h: (b, h, 0)),
                  pl.BlockSpec((1, 5, t, t), lambda b, h: (h, 0, 0, 0)),
                  pl.BlockSpec((4, HEAD_DK), lambda b, h: (0, 0)),
                  pl.BlockSpec((1, HEAD_DV), lambda b, h: (0, 0))],
        out_specs=pl.BlockSpec((1, s, HEAD_DV), seq),
        out_shape=jax.ShapeDtypeStruct((bsz, s, D_ATT), BF16),
        scratch_shapes=[scores, scores, maxima, maxima, pv_acc, pv_acc, pltpu.VMEM((width, 2, s, t), BF16)],
        compiler_params=_params(("parallel", "parallel"), VMEM_LIMIT),
        name="diff_attention",
    )(qn, kn, vt, bias_tiles, lambda_qk, g_o.reshape(1, HEAD_DV))


def _outproj_kernel(yl_ref, ya_ref, x_ref, gate_ref, sc_ref, sh_ref, g2_ref, wo_ref, wr_ref,
                    x1_ref, h2_ref, aff_ref):
    mix = (jnp.dot(yl_ref[0], wo_ref[0:D_LRU, :], preferred_element_type=F32)
           + jnp.dot(ya_ref[0], wo_ref[D_LRU:D_LRU + D_ATT, :], preferred_element_type=F32))
    x1 = x_ref[0] + gate_ref[0] * mix
    x1_ref[0] = x1
    ms = jnp.mean(x1 * x1, axis=-1, keepdims=True)
    h2 = (x1 * lax.rsqrt(ms + EPS) * g2_ref[...]) * (1.0 + sc_ref[0]) + sh_ref[0]
    h2b = h2.astype(BF16)
    h2_ref[0] = h2b
    logits = lax.dot_general(wr_ref[...], h2b, NT_DIMS, preferred_element_type=F32)
    ex = jnp.exp(logits - jnp.max(logits, axis=0, keepdims=True))
    aff_ref[0] = ex / jnp.sum(ex, axis=0, keepdims=True)


def _out_projection(y_lru, y_att, x, gate1, scale2, shift2, g_norm2, w_out, w_router, tm):
    bsz, s, d = x.shape
    row = lambda b, i: (b, i, 0)
    vec = lambda b, i: (b, 0, 0)
    full = lambda b, i: (0, 0)
    return pl.pallas_call(
        _outproj_kernel,
        grid=(bsz, s // tm),
        in_specs=[pl.BlockSpec((1, tm, D_LRU), row),
                  pl.BlockSpec((1, tm, D_ATT), row),
                  pl.BlockSpec((1, tm, d), row),
                  pl.BlockSpec((1, 1, d), vec),
                  pl.BlockSpec((1, 1, d), vec),
                  pl.BlockSpec((1, 1, d), vec),
                  pl.BlockSpec((1, d), full),
                  pl.BlockSpec((D_LRU + D_ATT, d), full),
                  pl.BlockSpec((N_EXPERTS, d), full)],
        out_specs=[pl.BlockSpec((1, tm, d), row),
                   pl.BlockSpec((1, tm, d), row),
                   pl.BlockSpec((1, N_EXPERTS, tm), lambda b, i: (b, 0, i))],
        out_shape=[jax.ShapeDtypeStruct((bsz, s, d), F32),
                   jax.ShapeDtypeStruct((bsz, s, d), BF16),
                   jax.ShapeDtypeStruct((bsz, N_EXPERTS, s), F32)],
        compiler_params=_params(("parallel", "parallel"), VMEM_LIMIT),
        name="out_proj_norm2_router",
    )(y_lru, y_att, x, gate1, scale2, shift2, g_norm2.reshape(1, d), w_out.astype(BF16),
      w_router.T.astype(BF16))


def _route_kernel(aff_ref, pos_ref, cnt_ref, *, cap, n_tok_chunks):
    aff = aff_ref[0]
    n_e, s = aff.shape
    bits = lax.bitcast_convert_type(aff, jnp.int32)
    capf = float(cap)

    def count(mask):
        return jnp.sum(jnp.where(mask, 1.0, 0.0), axis=-1, keepdims=True)

    tau = jnp.zeros((n_e, 1), jnp.int32)
    for bit in range(30, -1, -1):
        cand = tau | (1 << bit)
        tau = jnp.where(count(bits >= cand) >= capf, cand, tau)
    gt = bits > tau
    eq = bits == tau
    need = capf - count(gt)

    blk = min(256, s // n_tok_chunks)
    r = lax.broadcasted_iota(jnp.int32, (blk, blk), 0)
    cidx = lax.broadcasted_iota(jnp.int32, (blk, blk), 1)
    upper = jnp.where(r < cidx, 1.0, 0.0).astype(BF16)

    def prefix_blocks(mask):
        off = jnp.zeros((n_e, 1), F32)
        pieces, offs = [], []
        for k in range(s // blk):
            mb = jnp.where(mask[:, k * blk:(k + 1) * blk], 1.0, 0.0)
            offs.append(off)
            pieces.append(jnp.dot(mb.astype(BF16), upper, preferred_element_type=F32) + off)
            off = off + jnp.sum(mb, axis=-1, keepdims=True)
        return pieces, offs

    eq_rank, _ = prefix_blocks(eq)
    sel_blocks = []
    for k in range(s // blk):
        sl = slice(k * blk, (k + 1) * blk)
        sel_blocks.append(jnp.logical_or(gt[:, sl], jnp.logical_and(eq[:, sl], eq_rank[k] < need)))
    sel = jnp.concatenate(sel_blocks, axis=1)
    slot, offs = prefix_blocks(sel)
    for k in range(s // blk):
        pos_ref[0, :, k * blk:(k + 1) * blk] = jnp.where(sel_blocks[k], slot[k], -1.0).astype(jnp.int32)

    lane = lax.broadcasted_iota(jnp.int32, (n_e, 128), 1)
    cnt = jnp.zeros((n_e, 128), F32)
    per = (s // n_tok_chunks) // blk
    for j in range(n_tok_chunks):
        cnt = jnp.where(lane == j, offs[j * per], cnt)
    cnt_ref[0] = cnt.astype(jnp.int32)


def _routing(aff, cap):
    bsz, n_e, s = aff.shape
    group = math.gcd(bsz, ROUTE_GROUP)
    rows = group * n_e
    pos, cnt = pl.pallas_call(
        functools.partial(_route_kernel, cap=cap, n_tok_chunks=N_TOK_CHUNKS),
        grid=(bsz // group,),
        in_specs=[pl.BlockSpec((1, rows, s), lambda b: (b, 0, 0))],
        out_specs=[pl.BlockSpec((1, rows, s), lambda b: (b, 0, 0)),
                   pl.BlockSpec((1, rows, 128), lambda b: (b, 0, 0))],
        out_shape=[jax.ShapeDtypeStruct((bsz // group, rows, s), jnp.int32),
                   jax.ShapeDtypeStruct((bsz // group, rows, 128), jnp.int32)],
        compiler_params=_params(("parallel",)),
        name="expert_choice_routing",
    )(aff.reshape(bsz // group, rows, s))
    return pos.reshape(bsz, n_e, s), cnt.reshape(bsz, n_e, 128)


def _moe_kernel(cnt_ref, pos_ref, aff_ref, h2_ref, w1_ref, w3_ref, w2_ref, gate_ref, x1_hbm, out_ref,
                xe_scr, g_scr, y_scr, x1_sem, *, cap):
    b = pl.program_id(0)
    e = pl.program_id(1)
    n_e = pl.num_programs(1)
    s = h2_ref.shape[1]
    tchunk = s // N_TOK_CHUNKS

    def residual_copy():
        return pltpu.make_async_copy(x1_hbm.at[b], out_ref.at[0], x1_sem.at[0])

    @pl.when(e == 0)
    def _():
        residual_copy().start()

    xe_scr[...] = jnp.zeros_like(xe_scr)
    g_scr[...] = jnp.zeros_like(g_scr)
    base = (b * n_e + e) * N_TOK_CHUNKS

    def walk(cchunk, visit):
        n_cap = cap // cchunk
        slot_iota = lax.broadcasted_iota(jnp.int32, (cchunk, tchunk), 0)
        j = jnp.int32(0)
        i = jnp.int32(0)
        for _ in range(N_TOK_CHUNKS + n_cap - 1):
            jc = jnp.minimum(j, N_TOK_CHUNKS - 1)
            ic = jnp.minimum(i, n_cap - 1)
            c_lo = cnt_ref[base + jc]
            c_hi = jnp.where(jc + 1 < N_TOK_CHUNKS, cnt_ref[base + jnp.minimum(jc + 1, N_TOK_CHUNKS - 1)], cap)
            s_hi = (ic + 1) * cchunk
            valid = jnp.logical_and(jnp.maximum(c_lo, ic * cchunk) < jnp.minimum(c_hi, s_hi),
                                    jnp.logical_and(j < N_TOK_CHUNKS, i < n_cap))
            slot0 = jnp.where(valid, ic * cchunk, -2 * cap)
            sel = pos_ref[0, 0, pl.ds(jc, 1), :] == slot_iota + slot0
            visit(sel, jc, pl.ds(pl.multiple_of(ic * cchunk, cchunk), cchunk),
                  pl.ds(pl.multiple_of(jc * tchunk, tchunk), tchunk))
            j = j + jnp.where(c_hi <= s_hi, 1, 0)
            i = i + jnp.where(s_hi <= c_hi, 1, 0)

    def gather(sel, jc, rows, toks):
        xe_scr[rows, :] += jnp.dot(jnp.where(sel, 1.0, 0.0).astype(BF16), h2_ref[0, toks, :],
                                   preferred_element_type=F32)
        g_scr[rows, :] += jnp.sum(jnp.where(sel, aff_ref[0, 0, pl.ds(jc, 1), :], 0.0), axis=-1, keepdims=True)

    walk(cap // N_CAP_CHUNKS, gather)

    xe = xe_scr[...].astype(BF16)
    a = jnp.dot(xe, w1_ref[0], preferred_element_type=F32)
    gate = jnp.dot(xe, w3_ref[0], preferred_element_type=F32)
    hmid = ((a * _sigmoid(a)) * gate).astype(BF16)
    y = jnp.dot(hmid, w2_ref[0], preferred_element_type=F32) * g_scr[...] * gate_ref[0]
    y_scr[...] = y.astype(BF16)

    @pl.when(e == 0)
    def _():
        residual_copy().wait()

    def scatter(sel, jc, rows, toks):
        out_ref[0, toks, :] += lax.dot_general(jnp.where(sel, 1.0, 0.0).astype(BF16), y_scr[rows, :],
                                               TN_DIMS, preferred_element_type=F32)

    walk(cap // N_SCATTER_CHUNKS, scatter)


def _moe(cnt, pos, aff, h2, w1, w3, w2, gate2, x1, cap):
    bsz, s, d = h2.shape
    n_e = w1.shape[0]
    f = w1.shape[2]
    tok_row = lambda b, e, c: (b, e, 0, 0)
    wspec = lambda b, e, c: (e, 0, 0)
    resident = lambda b, e, c: (b, 0, 0)
    grid_spec = pltpu.PrefetchScalarGridSpec(
        num_scalar_prefetch=1,
        grid=(bsz, n_e),
        in_specs=[pl.BlockSpec((1, 1, N_TOK_CHUNKS, s // N_TOK_CHUNKS), tok_row),
                  pl.BlockSpec((1, 1, N_TOK_CHUNKS, s // N_TOK_CHUNKS), tok_row),
                  pl.BlockSpec((1, s, d), resident, pipeline_mode=pl.Buffered(1)),
                  pl.BlockSpec((1, d, f), wspec),
                  pl.BlockSpec((1, d, f), wspec),
                  pl.BlockSpec((1, f, d), wspec),
                  pl.BlockSpec((1, 1, d), resident),
                  pl.BlockSpec(memory_space=pl.ANY)],
        out_specs=pl.BlockSpec((1, s, d), resident, pipeline_mode=pl.Buffered(1)),
        scratch_shapes=[pltpu.VMEM((cap, d), F32), pltpu.VMEM((cap, 1), F32), pltpu.VMEM((cap, d), BF16),
                        pltpu.SemaphoreType.DMA((1,))],
    )
    return pl.pallas_call(
        functools.partial(_moe_kernel, cap=cap),
        grid_spec=grid_spec,
        out_shape=jax.ShapeDtypeStruct((bsz, s, d), F32),
        compiler_params=_params(("arbitrary", "arbitrary"), VMEM_LIMIT),
        name="expert_choice_ffn",
    )(cnt.reshape(-1), pos.reshape(bsz, n_e, N_TOK_CHUNKS, -1), aff.reshape(bsz, n_e, N_TOK_CHUNKS, -1), h2,
      w1.astype(BF16), w3.astype(BF16), w2.astype(BF16), gate2, x1)


def kernel(x, c, w_mod, b_mod, g_norm1, w_in, conv_w, conv_b, lru_w_a, lru_b_a, lru_w_x, lru_b_x,
           lru_lambda, g_q, g_k, lambda_qk, g_attn_out, rel_bias, w_out, g_norm2, w_router, w1, w3, w2):
    bsz, s, d = x.shape
    depth = w_mod.shape[0]
    cap = max(1, EC_FACTOR * s // N_EXPERTS)
    bias_tiles = _bias_tiles(rel_bias, min(ATT_TILE, s))
    for l in range(depth):
        mod = _modulation(c, w_mod[l], b_mod[l])
        shift1, scale1, gate1, shift2, scale2, gate2 = [m.reshape(bsz, 1, d) for m in jnp.split(mod, 6, axis=-1)]
        x_lru, gz, qn, kn, vt = _in_projection(x, scale1, shift1, g_norm1[l], w_in[l], g_q[l], g_k[l],
                                               min(IN_PROJ_ROWS, s))
        y_lru = _rg_lru(x_lru, gz, conv_w[l], conv_b[l], lru_w_a[l], lru_b_a[l], lru_w_x[l], lru_b_x[l],
                        lru_lambda[l])
        lam_init = 0.8 - 0.6 * math.exp(-0.3 * l)
        y_att = _diff_attention(qn, kn, vt, bias_tiles, lambda_qk[l], g_attn_out[l], lam_init)
        x1, h2, aff = _out_projection(y_lru, y_att, x, gate1, scale2, shift2, g_norm2[l], w_out[l],
                                      w_router[l], min(OUT_PROJ_ROWS, s))
        pos, cnt = _routing(aff, cap)
        x = _moe(cnt[:, :, :N_TOK_CHUNKS], pos, aff, h2, w1[l], w3[l], w2[l], gate2, x1, cap)
    return x
```

```python
import functools
import math

import jax
import jax.numpy as jnp
from jax import lax
from jax.experimental import pallas as pl
from jax.experimental.pallas import tpu as pltpu

F32 = jnp.float32
BF16 = jnp.bfloat16

D_MODEL = 1024
D_LRU = 512
LRU_BLOCK = 64
LRU_C = 8.0
CONV_W = 4
N_HEADS = 4
HEAD_DV = 128
HEAD_DK = 64
D_ATT = N_HEADS * HEAD_DV
N_BUCKETS = 32
N_EXPERTS = 16
EC_FACTOR = 2
EPS = 1e-6
F32_TINY = 2.0 ** -126
LOG2E = math.log2(math.e)

IN_PROJ_ROWS = 1024
OUT_PROJ_ROWS = 1024
LRU_HALF = 256
SCAN_LANES = 8
ATT_TILE = 256
PV_ONES = 16
ATT_WIDTH = 1
ROUTE_GROUP = 8
N_TOK_CHUNKS = 8
N_CAP_CHUNKS = 4
N_SCATTER_CHUNKS = 2
VMEM_LIMIT = 56 * 1024 * 1024

NT_DIMS = (((1,), (1,)), ((), ()))
TN_DIMS = (((0,), (0,)), ((), ()))


def _sigmoid(x):
    return 1.0 / (1.0 + jnp.exp(-x))


def _params(sem, vmem=None):
    return pltpu.CompilerParams(dimension_semantics=sem, vmem_limit_bytes=vmem)


def _mod_kernel(c_ref, w_ref, b_ref, o_ref):
    c = c_ref[...]
    o_ref[...] = jnp.dot((c * _sigmoid(c)).astype(BF16), w_ref[...].astype(BF16),
                         preferred_element_type=F32) + b_ref[...]


def _modulation(c, w_mod, b_mod):
    bsz, d = c.shape
    n = w_mod.shape[1]
    return pl.pallas_call(
        _mod_kernel,
        grid=(n // d,),
        in_specs=[pl.BlockSpec((bsz, d), lambda j: (0, 0)),
                  pl.BlockSpec((d, d), lambda j: (0, j)),
                  pl.BlockSpec((1, d), lambda j: (0, j))],
        out_specs=pl.BlockSpec((bsz, d), lambda j: (0, j)),
        out_shape=jax.ShapeDtypeStruct((bsz, n), F32),
        compiler_params=_params(("arbitrary",)),
        name="adaln_mod",
    )(c, w_mod, b_mod.reshape(1, n))


def _bias_kernel(tab_ref, o_ref):
    h = pl.program_id(0)
    t = o_ref.shape[-1]
    key = lax.broadcasted_iota(jnp.int32, (t, t), 0)
    qry = lax.broadcasted_iota(jnp.int32, (t, t), 1)
    half = N_BUCKETS // 2
    max_exact = half // 2
    for d in range(5):
        if d == 0:
            o_ref[0, d] = jnp.full((t, t), tab_ref[half - 1, h] * LOG2E, F32)
        elif d == 4:
            o_ref[0, d] = jnp.full((t, t), tab_ref[N_BUCKETS - 1, h] * LOG2E, F32)
        else:
            rel = (d - 2) * t + key - qry
            n = jnp.abs(rel)
            n2 = n * n
            large = jnp.full((t, t), max_exact, jnp.int32)
            for k in range(1, half - max_exact):
                large = large + jnp.where(n2 >= (max_exact * max_exact) * (2 ** k), 1, 0)
            idx = jnp.where(n < max_exact, n, large) + jnp.where(rel > 0, half, 0)
            val = jnp.zeros((t, t), F32)
            for j in range(N_BUCKETS):
                val = jnp.where(idx == j, tab_ref[j, h] * LOG2E, val)
            o_ref[0, d] = val


def _bias_tiles(rel_bias, t):
    return pl.pallas_call(
        _bias_kernel,
        grid=(N_HEADS,),
        in_specs=[pl.BlockSpec(memory_space=pltpu.SMEM)],
        out_specs=pl.BlockSpec((1, 5, t, t), lambda h: (h, 0, 0, 0)),
        out_shape=jax.ShapeDtypeStruct((N_HEADS, 5, t, t), F32),
        compiler_params=_params(("arbitrary",)),
        name="t5_bias_tiles",
    )(rel_bias)


def _inproj_kernel(x_ref, sc_ref, sh_ref, g1_ref, w_ref, wvt_ref, mseg_ref, gq_ref, gk_ref,
                   xl_ref, gz_ref, q_ref, k_ref, vt_ref):
    x = x_ref[0]
    ms = jnp.mean(x * x, axis=-1, keepdims=True)
    h = (x * lax.rsqrt(ms + EPS) * g1_ref[...]) * (1.0 + sc_ref[0]) + sh_ref[0]
    hb = h.astype(BF16)

    def proj(lo, width):
        return jnp.dot(hb, w_ref[:, lo:lo + width], preferred_element_type=F32)

    def qk_norm(t, g):
        ss = jnp.dot((t * t).astype(BF16), mseg_ref[...], preferred_element_type=F32)
        return t * lax.rsqrt(ss * (1.0 / HEAD_DK) + EPS) * g

    xl_ref[0] = proj(0, D_LRU)
    z = proj(D_LRU, D_LRU)
    cdf = 0.5 * (1.0 + jnp.tanh(math.sqrt(2.0 / math.pi) * (z + 0.044715 * (z * z * z))))
    gz_ref[0] = (z * cdf).astype(BF16)
    q_ref[0] = (qk_norm(proj(2 * D_LRU, D_ATT), gq_ref[...]) * (HEAD_DK ** -0.5 * LOG2E)).astype(BF16)
    k_ref[0] = qk_norm(proj(2 * D_LRU + D_ATT, D_ATT), gk_ref[...]).astype(BF16)
    vt_ref[0] = lax.dot_general(wvt_ref[...], hb, NT_DIMS,
                                preferred_element_type=F32).astype(BF16)


def _in_projection(x, scale1, shift1, g_norm1, w_in, g_q, g_k, tm):
    bsz, s, d = x.shape
    n = w_in.shape[1] - D_ATT
    w_main = w_in[:, :n].astype(BF16)
    w_vt = w_in[:, n:].T.astype(BF16)
    seg = jnp.arange(D_ATT, dtype=jnp.int32) // HEAD_DK
    mseg = (seg[:, None] == seg[None, :]).astype(BF16)
    n_sub = D_ATT // HEAD_DK
    row = lambda b, i: (b, i, 0)
    vec = lambda b, i: (b, 0, 0)
    full = lambda b, i: (0, 0)
    out_block = pl.BlockSpec((1, tm, D_LRU), row)
    return pl.pallas_call(
        _inproj_kernel,
        grid=(bsz, s // tm),
        in_specs=[pl.BlockSpec((1, tm, d), row),
                  pl.BlockSpec((1, 1, d), vec),
                  pl.BlockSpec((1, 1, d), vec),
                  pl.BlockSpec((1, d), full),
                  pl.BlockSpec((d, n), full),
                  pl.BlockSpec((D_ATT, d), full),
                  pl.BlockSpec((D_ATT, D_ATT), full),
                  pl.BlockSpec((1, D_ATT), full),
                  pl.BlockSpec((1, D_ATT), full)],
        out_specs=[out_block] * 4 + [pl.BlockSpec((1, D_ATT, tm), lambda b, i: (b, 0, i))],
        out_shape=[jax.ShapeDtypeStruct((bsz, s, D_LRU), F32)]
                  + [jax.ShapeDtypeStruct((bsz, s, D_LRU), BF16)] * 3
                  + [jax.ShapeDtypeStruct((bsz, D_ATT, s), BF16)],
        compiler_params=_params(("parallel", "parallel"), VMEM_LIMIT),
        name="norm1_in_proj",
    )(x, scale1, shift1, g_norm1.reshape(1, d), w_main, w_vt, mseg,
      jnp.tile(g_q, n_sub).reshape(1, D_ATT), jnp.tile(g_k, n_sub).reshape(1, D_ATT))


def _lru_kernel(x_ref, gz_ref, cw_ref, cb_ref, wa_ref, wx_ref, ba_ref, bx_ref, lam_ref, y_ref,
                xpad, a_f, u_f, a_b, u_b, *, tc):
    s = x_ref.shape[1]
    c = x_ref.shape[2]
    n_chunks = s // tc
    n_slab = c // 128
    pitch = a_f.shape[1] // n_chunks
    zeros8 = jnp.zeros((8, c), F32)
    xpad[0:8, :] = zeros8
    xpad[s + 8:s + 16, :] = zeros8

    def fill(ci, carry):
        t0 = pl.multiple_of(ci * tc, tc)
        xpad[pl.ds(t0 + 8, tc), :] = x_ref[0, pl.ds(t0, tc), :]
        return carry

    lax.fori_loop(0, n_chunks, fill, 0)

    cw = cw_ref[...]
    cb = cb_ref[...]
    decay = []
    for d in range(2):
        lam = lam_ref[d]
        softplus_neg = jnp.maximum(-lam, 0.0) + jnp.log(1.0 + jnp.exp(-jnp.abs(lam)))
        decay.append((-LRU_C * LOG2E) * softplus_neg)
    a_scr = (a_f, a_b)
    u_scr = (u_f, u_b)

    def gates(ci, carry):
        t0 = pl.multiple_of(ci * tc, tc)
        xw = xpad[pl.ds(t0, tc + 16), :]
        xc = (cw[0:1] * pltpu.roll(xw, 2, 0)[8:8 + tc]
              + cw[1:2] * pltpu.roll(xw, 1, 0)[8:8 + tc]
              + cw[2:3] * xw[8:8 + tc]
              + cw[3:4] * pltpu.roll(xw, tc + 15, 0)[8:8 + tc]) + cb
        xcb = xc.astype(BF16)
        for d in range(2):
            r = _sigmoid(jnp.dot(xcb, wa_ref[d], preferred_element_type=F32) + ba_ref[d])
            i = _sigmoid(jnp.dot(xcb, wx_ref[d], preferred_element_type=F32) + bx_ref[d])
            a = jnp.exp2(r * decay[d])
            v = 1.0 - a * a
            u = (v * lax.rsqrt(jnp.maximum(v, F32_TINY))) * (i * xc)
            r0 = pl.multiple_of(ci * pitch, 8)
            for sl in range(n_slab):
                a_scr[d][sl, pl.ds(r0, tc), :] = a[:, sl * 128:(sl + 1) * 128]
                u_scr[d][sl, pl.ds(r0, tc), :] = u[:, sl * 128:(sl + 1) * 128]
        return carry

    lax.fori_loop(0, n_chunks, gates, 0, unroll=2)

    def rows(tt):
        return pl.ds(tt, n_chunks, stride=pitch)

    def step(tt, carry):
        out = []
        for d in range(2):
            t_loc = tt if d == 0 else tc - 1 - tt
            for sl in range(n_slab):
                h, p = carry[len(out)]
                a8 = a_scr[d][sl, rows(t_loc), :]
                h = a8 * h + u_scr[d][sl, rows(t_loc), :]
                p = a8 * p
                u_scr[d][sl, rows(t_loc), :] = h
                a_scr[d][sl, rows(t_loc), :] = p
                out.append((h, p))
        return tuple(out)

    init = (jnp.zeros((n_chunks, 128), F32), jnp.ones((n_chunks, 128), F32))
    lax.fori_loop(0, tc, step, (init,) * (2 * n_slab), unroll=4)

    for sl in range(n_slab):
        lanes = slice(sl * 128, (sl + 1) * 128)
        h_end, p_end = u_f[sl, rows(tc - 1), :], a_f[sl, rows(tc - 1), :]
        h_beg, p_beg = u_b[sl, rows(0), :], a_b[sl, rows(0), :]
        carry_f = [jnp.zeros((1, 128), F32)]
        for r in range(1, n_chunks):
            carry_f.append(p_end[r - 1:r] * carry_f[-1] + h_end[r - 1:r])
        carry_b = [jnp.zeros((1, 128), F32)]
        for r in range(n_chunks - 2, -1, -1):
            carry_b.insert(0, p_beg[r + 1:r + 2] * carry_b[0] + h_beg[r + 1:r + 2])
        for r in range(n_chunks):
            blk = slice(r * pitch, r * pitch + tc)
            hsum = (u_f[sl, blk, :] + a_f[sl, blk, :] * carry_f[r]
                    + u_b[sl, blk, :] + a_b[sl, blk, :] * carry_b[r])
            y_ref[0, r * tc:(r + 1) * tc, lanes] = (hsum * gz_ref[0, r * tc:(r + 1) * tc, lanes].astype(F32)).astype(BF16)


def _block_diag(w, half):
    n_dir, n_blocks, blk, _ = w.shape
    per = half // blk
    n_half = n_blocks // per
    w = w.reshape(n_dir, n_half, per, blk, blk)
    eye = jnp.eye(per, dtype=w.dtype)
    out = w[:, :, :, :, None, :] * eye[None, None, :, None, :, None]
    return out.reshape(n_dir, n_half, half, half)


def _rg_lru(x_lru, gz, conv_w, conv_b, w_a, b_a, w_x, b_x, lam):
    bsz, s, c = x_lru.shape
    half = LRU_HALF
    tc = s // SCAN_LANES
    pitch = tc + 8 if (tc // 8) % 2 == 0 else tc + 16
    n_half = c // half
    wa = _block_diag(w_a, half).astype(BF16)
    wx = _block_diag(w_x, half).astype(BF16)
    seq = lambda b, p: (b, 0, p)
    chan = lambda b, p: (0, p)
    dirchan = lambda b, p: (0, 0, p)
    blk = lambda b, p: (0, p, 0, 0)
    return pl.pallas_call(
        functools.partial(_lru_kernel, tc=tc),
        grid=(bsz, n_half),
        in_specs=[pl.BlockSpec((1, s, half), seq),
                  pl.BlockSpec((1, s, half), seq),
                  pl.BlockSpec((CONV_W, half), chan),
                  pl.BlockSpec((1, half), chan),
                  pl.BlockSpec((2, None, half, half), blk),
                  pl.BlockSpec((2, None, half, half), blk),
                  pl.BlockSpec((2, 1, half), dirchan),
                  pl.BlockSpec((2, 1, half), dirchan),
                  pl.BlockSpec((2, 1, half), dirchan)],
        out_specs=pl.BlockSpec((1, s, half), seq),
        out_shape=jax.ShapeDtypeStruct((bsz, s, c), BF16),
        scratch_shapes=[pltpu.VMEM((s + 16, half), F32)]
                       + [pltpu.VMEM((half // 128, SCAN_LANES * pitch, 128), F32)] * 4,
        compiler_params=_params(("parallel", "parallel"), VMEM_LIMIT),
        name="rg_lru",
    )(x_lru, gz, conv_w.reshape(CONV_W, c), conv_b.reshape(1, c), wa, wx,
      b_a.reshape(2, 1, c), b_x.reshape(2, 1, c), lam.reshape(2, 1, c))


def _attn_kernel(q_ref, k_ref, vt_ref, bias_ref, lq_ref, go_ref, w1_ref, w3_ref, w2_ref,
                 o_ref, w1b_ref, w3b_ref, w2b_ref,
                 s_even, s_odd, m_even, m_odd, o_even, o_odd, p_scr, *, lam_init):
    for w_ref, wb_ref in ((w1_ref, w1b_ref), (w3_ref, w3b_ref), (w2_ref, w2b_ref)):
        wb_ref[...] = w_ref[...].astype(BF16)

    t = bias_ref.shape[-1]
    width = s_even.shape[0]
    n_k = k_ref.shape[1] // t
    n_q = q_ref.shape[1] // t
    n_steps = n_q // width
    subs = [(w, u) for w in range(width) for u in range(2)]
    chunk = 2
    pv_chunk = 2
    n_c = n_k // chunk

    s_odd[...] = jnp.zeros_like(s_odd)
    m_odd[...] = jnp.zeros_like(m_odd)
    o_odd[...] = jnp.ones_like(o_odd)
    even = dict(s=s_even, m=m_even, o=o_even)
    odd = dict(s=s_odd, m=m_odd, o=o_odd)
    ones_rows = jnp.ones((PV_ONES, pv_chunk * t), BF16)

    def fold(x, op):
        parts = [x[r * 8:(r + 1) * 8, :] for r in range(t // 8)]
        acc = parts[:2]
        for r in range(2, len(parts)):
            acc[r % 2] = op(acc[r % 2], parts[r])
        return op(acc[0], acc[1])

    def bias_tile(j, tile):
        return bias_ref[0, jnp.clip(j - tile, -2, 2) + 2]

    def finish(prev, w, tile):
        outs = [prev['o'][w, u, :HEAD_DV, :] / prev['o'][w, u, HEAD_DV:HEAD_DV + 1, :] for u in range(2)]
        lq = lq_ref[...]
        lam = (jnp.exp(jnp.sum(lq[0:1] * lq[1:2], axis=-1, keepdims=True))
               - jnp.exp(jnp.sum(lq[2:3] * lq[3:4], axis=-1, keepdims=True)) + lam_init)
        o = (outs[0] - lam * outs[1]).T
        ms = jnp.mean(o * o, axis=-1, keepdims=True)
        o_ref[0, pl.ds(pl.multiple_of(tile * t, t), t), :] = (
            (o * lax.rsqrt(ms + EPS) * go_ref[...]) * (1.0 - lam_init)).astype(BF16)

    def step(i, cur, prev, scores=True, softmax=True):
        tiles_a = [i * width + w for w in range(width)]
        for w in range(width):
            finish(prev, w, jnp.maximum((i - 2) * width + w, 0))
        s_cur, s_prev = cur['s'], prev['s']
        m_b = {wu: jnp.max(prev['m'][wu], axis=0, keepdims=True) for wu in subs} if softmax else {}
        m_acc = {wu: jnp.full((8, t), -jnp.inf, F32) for wu in subs}
        q_sub = {}
        if scores:
            for w in range(width):
                q = q_ref[0, pl.ds(pl.multiple_of(tiles_a[w] * t, t), t), :]
                lane = lax.broadcasted_iota(jnp.int32, q.shape, 1)
                zero = jnp.zeros_like(q)
                q_sub[w, 0] = jnp.where(lane < HEAD_DK, q, zero)
                q_sub[w, 1] = jnp.where(lane >= HEAD_DK, q, zero)
        for c in range(n_c):
            rows = slice(c * chunk * t, (c + 1) * chunk * t)
            for w, u in (subs if scores else []):
                sc_all = lax.dot_general(k_ref[0, rows, :], q_sub[w, u], NT_DIMS, preferred_element_type=F32)
                for jj in range(chunk):
                    j = c * chunk + jj
                    sc = sc_all[jj * t:(jj + 1) * t, :] + bias_tile(j, tiles_a[w])
                    s_cur[w, u, j] = sc
                    m_acc[w, u] = jnp.maximum(m_acc[w, u], fold(sc, jnp.maximum))
            for w, u in (subs if softmax else []):
                for jj in range(chunk):
                    j = c * chunk + jj
                    p = jnp.exp2(s_prev[w, u, j] - m_b[w, u])
                    p_scr[w, u, j * t:(j + 1) * t, :] = p.astype(BF16)
                if ((c + 1) * chunk) % pv_chunk == 0:
                    keys = slice(((c + 1) * chunk - pv_chunk) * t, (c + 1) * chunk * t)
                    part = jnp.dot(jnp.concatenate([vt_ref[0, :, keys], ones_rows], axis=0), p_scr[w, u, keys, :],
                                   preferred_element_type=F32)
                    if (c + 1) * chunk == pv_chunk:
                        cur['o'][w, u] = part
                    else:
                        cur['o'][w, u] += part

        if scores:
            for w, u in subs:
                cur['m'][w, u] = m_acc[w, u]

    def pair(pi, carry):
        @pl.when(pi >= 0)
        def _():
            step(2 * pi, even, odd)

        @pl.when(pi < n_steps)
        def _():
            step(2 * pi + 1, odd, even)

        return carry

    lax.fori_loop(0, n_steps // 2, pair, 0)
    step(n_steps, even, odd, scores=False)
    for w in range(width):
        finish(even, w, n_q - width + w)


def _diff_attention(qn, kn, vt, bias_tiles, lambda_qk, g_o, lam_init, expert_w):
    bsz, s, _ = qn.shape
    w_rows = expert_w[0].shape[0] * expert_w[0].shape[1]
    w_cols = expert_w[0].shape[2]
    assert all(w.shape == expert_w[0].shape for w in expert_w) and w_rows % (bsz * N_HEADS * 16) == 0
    w_blk = pl.BlockSpec((w_rows // (bsz * N_HEADS), w_cols), lambda b, h: (b * N_HEADS + h, 0))
    t = bias_tiles.shape[-1]
    n_k = s // t
    width = ATT_WIDTH
    assert (s // t) % (2 * width) == 0, "the query-tile pipeline advances two groups of tiles per loop trip"
    scores = pltpu.VMEM((width, 2, n_k, t, t), F32)
    maxima = pltpu.VMEM((width, 2, 8, t), F32)
    pv_acc = pltpu.VMEM((width, 2, HEAD_DV + PV_ONES, t), F32)
    seq = lambda b, h: (b, 0, h)
    outs = pl.pallas_call(
        functools.partial(_attn_kernel, lam_init=lam_init),
        grid=(bsz, N_HEADS),
        in_specs=[pl.BlockSpec((1, s, HEAD_DV), seq),
                  pl.BlockSpec((1, s, HEAD_DV), seq),
                  pl.BlockSpec((1, HEAD_DV, s), lambda b, h: (b, h, 0)),
                  pl.BlockSpec((1, 5, t, t), lambda b, h: (h, 0, 0, 0)),
                  pl.BlockSpec((4, HEAD_DK), lambda b, h: (0, 0)),
                  pl.BlockSpec((1, HEAD_DV), lambda b, h: (0, 0)),
                  w_blk, w_blk, w_blk],
        out_specs=[pl.BlockSpec((1, s, HEAD_DV), seq), w_blk, w_blk, w_blk],
        out_shape=[jax.ShapeDtypeStruct((bsz, s, D_ATT), BF16)]
                  + [jax.ShapeDtypeStruct((w_rows, w_cols), BF16)] * 3,
        scratch_shapes=[scores, scores, maxima, maxima, pv_acc, pv_acc, pltpu.VMEM((width, 2, s, t), BF16)],
        compiler_params=_params(("parallel", "parallel"), VMEM_LIMIT),
        name="diff_attention",
    )(qn, kn, vt, bias_tiles, lambda_qk, g_o.reshape(1, HEAD_DV), *[w.reshape(w_rows, w_cols) for w in expert_w])
    return outs[0], [wb.reshape(expert_w[0].shape) for wb in outs[1:]]


def _outproj_kernel(yl_ref, ya_ref, x_ref, gate_ref, sc_ref, sh_ref, g2_ref, wo_ref, wr_ref,
                    x1_ref, h2_ref, aff_ref):
    mix = (jnp.dot(yl_ref[0], wo_ref[0:D_LRU, :], preferred_element_type=F32)
           + jnp.dot(ya_ref[0], wo_ref[D_LRU:D_LRU + D_ATT, :], preferred_element_type=F32))
    x1 = x_ref[0] + gate_ref[0] * mix
    x1_ref[0] = x1
    ms = jnp.mean(x1 * x1, axis=-1, keepdims=True)
    h2 = (x1 * lax.rsqrt(ms + EPS) * g2_ref[...]) * (1.0 + sc_ref[0]) + sh_ref[0]
    h2b = h2.astype(BF16)
    h2_ref[0] = h2b
    logits = lax.dot_general(wr_ref[...], h2b, NT_DIMS, preferred_element_type=F32)
    ex = jnp.exp(logits - jnp.max(logits, axis=0, keepdims=True))
    aff_ref[0] = ex / jnp.sum(ex, axis=0, keepdims=True)


def _out_projection(y_lru, y_att, x, gate1, scale2, shift2, g_norm2, w_out, w_router, tm):
    bsz, s, d = x.shape
    row = lambda b, i: (b, i, 0)
    vec = lambda b, i: (b, 0, 0)
    full = lambda b, i: (0, 0)
    return pl.pallas_call(
        _outproj_kernel,
        grid=(bsz, s // tm),
        in_specs=[pl.BlockSpec((1, tm, D_LRU), row),
                  pl.BlockSpec((1, tm, D_ATT), row),
                  pl.BlockSpec((1, tm, d), row),
                  pl.BlockSpec((1, 1, d), vec),
                  pl.BlockSpec((1, 1, d), vec),
                  pl.BlockSpec((1, 1, d), vec),
                  pl.BlockSpec((1, d), full),
                  pl.BlockSpec((D_LRU + D_ATT, d), full),
                  pl.BlockSpec((N_EXPERTS, d), full)],
        out_specs=[pl.BlockSpec((1, tm, d), row),
                   pl.BlockSpec((1, tm, d), row),
                   pl.BlockSpec((1, N_EXPERTS, tm), lambda b, i: (b, 0, i))],
        out_shape=[jax.ShapeDtypeStruct((bsz, s, d), F32),
                   jax.ShapeDtypeStruct((bsz, s, d), BF16),
                   jax.ShapeDtypeStruct((bsz, N_EXPERTS, s), F32)],
        compiler_params=_params(("parallel", "parallel"), VMEM_LIMIT),
        name="out_proj_norm2_router",
    )(y_lru, y_att, x, gate1, scale2, shift2, g_norm2.reshape(1, d), w_out.astype(BF16),
      w_router.T.astype(BF16))


def _route_kernel(aff_ref, pos_ref, cnt_ref, *, cap, n_tok_chunks):
    aff = aff_ref[0]
    n_e, s = aff.shape
    bits = lax.bitcast_convert_type(aff, jnp.int32)
    capf = float(cap)

    def count(mask):
        return jnp.sum(jnp.where(mask, 1.0, 0.0), axis=-1, keepdims=True)

    tau = jnp.zeros((n_e, 1), jnp.int32)
    for bit in range(30, -1, -1):
        cand = tau | (1 << bit)
        tau = jnp.where(count(bits >= cand) >= capf, cand, tau)
    gt = bits > tau
    eq = bits == tau
    need = capf - count(gt)

    blk = min(256, s // n_tok_chunks)
    r = lax.broadcasted_iota(jnp.int32, (blk, blk), 0)
    cidx = lax.broadcasted_iota(jnp.int32, (blk, blk), 1)
    upper = jnp.where(r < cidx, 1.0, 0.0).astype(BF16)

    def prefix_blocks(mask):
        off = jnp.zeros((n_e, 1), F32)
        pieces, offs = [], []
        for k in range(s // blk):
            mb = jnp.where(mask[:, k * blk:(k + 1) * blk], 1.0, 0.0)
            offs.append(off)
            pieces.append(jnp.dot(mb.astype(BF16), upper, preferred_element_type=F32) + off)
            off = off + jnp.sum(mb, axis=-1, keepdims=True)
        return pieces, offs

    eq_rank, _ = prefix_blocks(eq)
    sel_blocks = []
    for k in range(s // blk):
        sl = slice(k * blk, (k + 1) * blk)
        sel_blocks.append(jnp.logical_or(gt[:, sl], jnp.logical_and(eq[:, sl], eq_rank[k] < need)))
    sel = jnp.concatenate(sel_blocks, axis=1)
    slot, offs = prefix_blocks(sel)
    for k in range(s // blk):
        pos_ref[0, :, k * blk:(k + 1) * blk] = jnp.where(sel_blocks[k], slot[k], -1.0).astype(jnp.int32)

    lane = lax.broadcasted_iota(jnp.int32, (n_e, 128), 1)
    cnt = jnp.zeros((n_e, 128), F32)
    per = (s // n_tok_chunks) // blk
    for j in range(n_tok_chunks):
        cnt = jnp.where(lane == j, offs[j * per], cnt)
    cnt_ref[0] = cnt.astype(jnp.int32)


def _routing(aff, cap):
    bsz, n_e, s = aff.shape
    group = math.gcd(bsz, ROUTE_GROUP)
    rows = group * n_e
    pos, cnt = pl.pallas_call(
        functools.partial(_route_kernel, cap=cap, n_tok_chunks=N_TOK_CHUNKS),
        grid=(bsz // group,),
        in_specs=[pl.BlockSpec((1, rows, s), lambda b: (b, 0, 0))],
        out_specs=[pl.BlockSpec((1, rows, s), lambda b: (b, 0, 0)),
                   pl.BlockSpec((1, rows, 128), lambda b: (b, 0, 0))],
        out_shape=[jax.ShapeDtypeStruct((bsz // group, rows, s), jnp.int32),
                   jax.ShapeDtypeStruct((bsz // group, rows, 128), jnp.int32)],
        compiler_params=_params(("parallel",)),
        name="expert_choice_routing",
    )(aff.reshape(bsz // group, rows, s))
    return pos.reshape(bsz, n_e, s), cnt.reshape(bsz, n_e, 128)


def _moe_kernel(cnt_ref, pos_ref, aff_ref, h2_ref, w1_ref, w3_ref, w2_ref, gate_ref, x1_hbm, out_ref,
                xe_scr, g_scr, y_scr, x1_sem, *, cap):
    b = pl.program_id(0)
    e = pl.program_id(1)
    n_e = pl.num_programs(1)
    s = h2_ref.shape[1]
    tchunk = s // N_TOK_CHUNKS

    def residual_copy():
        return pltpu.make_async_copy(x1_hbm.at[b], out_ref.at[0], x1_sem.at[0])

    @pl.when(e == 0)
    def _():
        residual_copy().start()

    xe_scr[...] = jnp.zeros_like(xe_scr)
    g_scr[...] = jnp.zeros_like(g_scr)
    base = (b * n_e + e) * N_TOK_CHUNKS

    def walk(cchunk, visit):
        n_cap = cap // cchunk
        slot_iota = lax.broadcasted_iota(jnp.int32, (cchunk, tchunk), 0)
        j = jnp.int32(0)
        i = jnp.int32(0)
        for _ in range(N_TOK_CHUNKS + n_cap - 1):
            jc = jnp.minimum(j, N_TOK_CHUNKS - 1)
            ic = jnp.minimum(i, n_cap - 1)
            c_lo = cnt_ref[base + jc]
            c_hi = jnp.where(jc + 1 < N_TOK_CHUNKS, cnt_ref[base + jnp.minimum(jc + 1, N_TOK_CHUNKS - 1)], cap)
            s_hi = (ic + 1) * cchunk
            valid = jnp.logical_and(jnp.maximum(c_lo, ic * cchunk) < jnp.minimum(c_hi, s_hi),
                                    jnp.logical_and(j < N_TOK_CHUNKS, i < n_cap))
            slot0 = jnp.where(valid, ic * cchunk, -2 * cap)
            sel = pos_ref[0, 0, pl.ds(jc, 1), :] == slot_iota + slot0
            visit(sel, jc, pl.ds(pl.multiple_of(ic * cchunk, cchunk), cchunk),
                  pl.ds(pl.multiple_of(jc * tchunk, tchunk), tchunk))
            j = j + jnp.where(c_hi <= s_hi, 1, 0)
            i = i + jnp.where(s_hi <= c_hi, 1, 0)

    def gather(sel, jc, rows, toks):
        xe_scr[rows, :] += jnp.dot(jnp.where(sel, 1.0, 0.0).astype(BF16), h2_ref[0, toks, :],
                                   preferred_element_type=F32)
        g_scr[rows, :] += jnp.sum(jnp.where(sel, aff_ref[0, 0, pl.ds(jc, 1), :], 0.0), axis=-1, keepdims=True)

    walk(cap // N_CAP_CHUNKS, gather)

    xe = xe_scr[...].astype(BF16)
    a = jnp.dot(xe, w1_ref[0], preferred_element_type=F32)
    gate = jnp.dot(xe, w3_ref[0], preferred_element_type=F32)
    hmid = ((a * _sigmoid(a)) * gate).astype(BF16)
    y = jnp.dot(hmid, w2_ref[0], preferred_element_type=F32) * g_scr[...] * gate_ref[0]
    y_scr[...] = y.astype(BF16)

    @pl.when(e == 0)
    def _():
        residual_copy().wait()

    def scatter(sel, jc, rows, toks):
        out_ref[0, toks, :] += lax.dot_general(jnp.where(sel, 1.0, 0.0).astype(BF16), y_scr[rows, :],
                                               TN_DIMS, preferred_element_type=F32)

    walk(cap // N_SCATTER_CHUNKS, scatter)


def _moe(cnt, pos, aff, h2, w1, w3, w2, gate2, x1, cap):
    bsz, s, d = h2.shape
    n_e = w1.shape[0]
    f = w1.shape[2]
    tok_row = lambda b, e, c: (b, e, 0, 0)
    wspec = lambda b, e, c: (e, 0, 0)
    resident = lambda b, e, c: (b, 0, 0)
    grid_spec = pltpu.PrefetchScalarGridSpec(
        num_scalar_prefetch=1,
        grid=(bsz, n_e),
        in_specs=[pl.BlockSpec((1, 1, N_TOK_CHUNKS, s // N_TOK_CHUNKS), tok_row),
                  pl.BlockSpec((1, 1, N_TOK_CHUNKS, s // N_TOK_CHUNKS), tok_row),
                  pl.BlockSpec((1, s, d), resident, pipeline_mode=pl.Buffered(1)),
                  pl.BlockSpec((1, d, f), wspec),
                  pl.BlockSpec((1, d, f), wspec),
                  pl.BlockSpec((1, f, d), wspec),
                  pl.BlockSpec((1, 1, d), resident),
                  pl.BlockSpec(memory_space=pl.ANY)],
        out_specs=pl.BlockSpec((1, s, d), resident, pipeline_mode=pl.Buffered(1)),
        scratch_shapes=[pltpu.VMEM((cap, d), F32), pltpu.VMEM((cap, 1), F32), pltpu.VMEM((cap, d), BF16),
                        pltpu.SemaphoreType.DMA((1,))],
    )
    return pl.pallas_call(
        functools.partial(_moe_kernel, cap=cap),
        grid_spec=grid_spec,
        out_shape=jax.ShapeDtypeStruct((bsz, s, d), F32),
        compiler_params=_params(("arbitrary", "arbitrary"), VMEM_LIMIT),
        name="expert_choice_ffn",
    )(cnt.reshape(-1), pos.reshape(bsz, n_e, N_TOK_CHUNKS, -1), aff.reshape(bsz, n_e, N_TOK_CHUNKS, -1), h2,
      w1, w3, w2, gate2, x1)


def kernel(x, c, w_mod, b_mod, g_norm1, w_in, conv_w, conv_b, lru_w_a, lru_b_a, lru_w_x, lru_b_x,
           lru_lambda, g_q, g_k, lambda_qk, g_attn_out, rel_bias, w_out, g_norm2, w_router, w1, w3, w2):
    bsz, s, d = x.shape
    depth = w_mod.shape[0]
    cap = max(1, EC_FACTOR * s // N_EXPERTS)
    bias_tiles = _bias_tiles(rel_bias, min(ATT_TILE, s))
    for l in range(depth):
        mod = _modulation(c, w_mod[l], b_mod[l])
        shift1, scale1, gate1, shift2, scale2, gate2 = [m.reshape(bsz, 1, d) for m in jnp.split(mod, 6, axis=-1)]
        x_lru, gz, qn, kn, vt = _in_projection(x, scale1, shift1, g_norm1[l], w_in[l], g_q[l], g_k[l],
                                               min(IN_PROJ_ROWS, s))
        y_lru = _rg_lru(x_lru, gz, conv_w[l], conv_b[l], lru_w_a[l], lru_b_a[l], lru_w_x[l], lru_b_x[l],
                        lru_lambda[l])
        lam_init = 0.8 - 0.6 * math.exp(-0.3 * l)
        y_att, (w1b, w3b, w2b) = _diff_attention(qn, kn, vt, bias_tiles, lambda_qk[l], g_attn_out[l], lam_init,
                                                 (w1[l], w3[l], w2[l]))
        x1, h2, aff = _out_projection(y_lru, y_att, x, gate1, scale2, shift2, g_norm2[l], w_out[l],
                                      w_router[l], min(OUT_PROJ_ROWS, s))
        pos, cnt = _routing(aff, cap)
        x = _moe(cnt[:, :, :N_TOK_CHUNKS], pos, aff, h2, w1b, w3b, w2b, gate2, x1, cap)
    return x
```

```python
import functools
import math

import jax
import jax.numpy as jnp
from jax import lax
from jax.experimental import pallas as pl
from jax.experimental.pallas import tpu as pltpu

F32 = jnp.float32
BF16 = jnp.bfloat16

D_MODEL = 1024
D_LRU = 512
LRU_BLOCK = 64
LRU_C = 8.0
CONV_W = 4
N_HEADS = 4
HEAD_DV = 128
HEAD_DK = 64
D_ATT = N_HEADS * HEAD_DV
N_BUCKETS = 32
N_EXPERTS = 16
EC_FACTOR = 2
EPS = 1e-6
F32_TINY = 2.0 ** -126
LOG2E = math.log2(math.e)

IN_PROJ_ROWS = 1024
OUT_PROJ_ROWS = 1024
LRU_HALF = 256
SCAN_LANES = 8
ATT_TILE = 256
PV_ONES = 16
ATT_WIDTH = 1
ROUTE_GROUP = 8
N_TOK_CHUNKS = 8
N_CAP_CHUNKS = 4
N_SCATTER_CHUNKS = 2
VMEM_LIMIT = 56 * 1024 * 1024

NT_DIMS = (((1,), (1,)), ((), ()))
TN_DIMS = (((0,), (0,)), ((), ()))


def _sigmoid(x):
    return 1.0 / (1.0 + jnp.exp(-x))


def _params(sem, vmem=None):
    return pltpu.CompilerParams(dimension_semantics=sem, vmem_limit_bytes=vmem)


def _mod_kernel(c_ref, w_ref, b_ref, o_ref):
    c = c_ref[...]
    o_ref[...] = jnp.dot((c * _sigmoid(c)).astype(BF16), w_ref[...].astype(BF16),
                         preferred_element_type=F32) + b_ref[...]


def _modulation(c, w_mod, b_mod):
    bsz, d = c.shape
    n = w_mod.shape[1]
    return pl.pallas_call(
        _mod_kernel,
        grid=(n // d,),
        in_specs=[pl.BlockSpec((bsz, d), lambda j: (0, 0)),
                  pl.BlockSpec((d, d), lambda j: (0, j)),
                  pl.BlockSpec((1, d), lambda j: (0, j))],
        out_specs=pl.BlockSpec((bsz, d), lambda j: (0, j)),
        out_shape=jax.ShapeDtypeStruct((bsz, n), F32),
        compiler_params=_params(("arbitrary",)),
        name="adaln_mod",
    )(c, w_mod, b_mod.reshape(1, n))


def _bias_kernel(tab_ref, o_ref):
    h = pl.program_id(0)
    t = o_ref.shape[-1]
    key = lax.broadcasted_iota(jnp.int32, (t, t), 0)
    qry = lax.broadcasted_iota(jnp.int32, (t, t), 1)
    half = N_BUCKETS // 2
    max_exact = half // 2
    for d in range(5):
        if d == 0:
            o_ref[0, d] = jnp.full((t, t), tab_ref[half - 1, h] * LOG2E, F32)
        elif d == 4:
            o_ref[0, d] = jnp.full((t, t), tab_ref[N_BUCKETS - 1, h] * LOG2E, F32)
        else:
            rel = (d - 2) * t + key - qry
            n = jnp.abs(rel)
            n2 = n * n
            large = jnp.full((t, t), max_exact, jnp.int32)
            for k in range(1, half - max_exact):
                large = large + jnp.where(n2 >= (max_exact * max_exact) * (2 ** k), 1, 0)
            idx = jnp.where(n < max_exact, n, large) + jnp.where(rel > 0, half, 0)
            val = jnp.zeros((t, t), F32)
            for j in range(N_BUCKETS):
                val = jnp.where(idx == j, tab_ref[j, h] * LOG2E, val)
            o_ref[0, d] = val


def _bias_tiles(rel_bias, t):
    return pl.pallas_call(
        _bias_kernel,
        grid=(N_HEADS,),
        in_specs=[pl.BlockSpec(memory_space=pltpu.SMEM)],
        out_specs=pl.BlockSpec((1, 5, t, t), lambda h: (h, 0, 0, 0)),
        out_shape=jax.ShapeDtypeStruct((N_HEADS, 5, t, t), F32),
        compiler_params=_params(("arbitrary",)),
        name="t5_bias_tiles",
    )(rel_bias)


def _inproj_kernel(x_ref, sc_ref, sh_ref, g1_ref, w_ref, wvt_ref, mseg_ref, gq_ref, gk_ref,
                   xl_ref, gz_ref, q_ref, k_ref, vt_ref):
    x = x_ref[0]
    ms = jnp.mean(x * x, axis=-1, keepdims=True)
    h = (x * lax.rsqrt(ms + EPS) * g1_ref[...]) * (1.0 + sc_ref[0]) + sh_ref[0]
    hb = h.astype(BF16)

    def proj(lo, width):
        return jnp.dot(hb, w_ref[:, lo:lo + width], preferred_element_type=F32)

    def qk_norm(t, g):
        ss = jnp.dot((t * t).astype(BF16), mseg_ref[...], preferred_element_type=F32)
        return t * lax.rsqrt(ss * (1.0 / HEAD_DK) + EPS) * g

    xl_ref[0] = proj(0, D_LRU)
    z = proj(D_LRU, D_LRU)
    cdf = 0.5 * (1.0 + jnp.tanh(math.sqrt(2.0 / math.pi) * (z + 0.044715 * (z * z * z))))
    gz_ref[0] = (z * cdf).astype(BF16)
    q_ref[0] = (qk_norm(proj(2 * D_LRU, D_ATT), gq_ref[...]) * (HEAD_DK ** -0.5 * LOG2E)).astype(BF16)
    k_ref[0] = qk_norm(proj(2 * D_LRU + D_ATT, D_ATT), gk_ref[...]).astype(BF16)
    vt_ref[0] = lax.dot_general(wvt_ref[...], hb, NT_DIMS,
                                preferred_element_type=F32).astype(BF16)


def _in_projection(x, scale1, shift1, g_norm1, w_in, g_q, g_k, tm):
    bsz, s, d = x.shape
    n = w_in.shape[1] - D_ATT
    w_main = w_in[:, :n].astype(BF16)
    w_vt = w_in[:, n:].T.astype(BF16)
    seg = jnp.arange(D_ATT, dtype=jnp.int32) // HEAD_DK
    mseg = (seg[:, None] == seg[None, :]).astype(BF16)
    n_sub = D_ATT // HEAD_DK
    row = lambda b, i: (b, i, 0)
    vec = lambda b, i: (b, 0, 0)
    full = lambda b, i: (0, 0)
    out_block = pl.BlockSpec((1, tm, D_LRU), row)
    return pl.pallas_call(
        _inproj_kernel,
        grid=(bsz, s // tm),
        in_specs=[pl.BlockSpec((1, tm, d), row),
                  pl.BlockSpec((1, 1, d), vec),
                  pl.BlockSpec((1, 1, d), vec),
                  pl.BlockSpec((1, d), full),
                  pl.BlockSpec((d, n), full),
                  pl.BlockSpec((D_ATT, d), full),
                  pl.BlockSpec((D_ATT, D_ATT), full),
                  pl.BlockSpec((1, D_ATT), full),
                  pl.BlockSpec((1, D_ATT), full)],
        out_specs=[out_block] * 4 + [pl.BlockSpec((1, D_ATT, tm), lambda b, i: (b, 0, i))],
        out_shape=[jax.ShapeDtypeStruct((bsz, s, D_LRU), F32)]
                  + [jax.ShapeDtypeStruct((bsz, s, D_LRU), BF16)] * 3
                  + [jax.ShapeDtypeStruct((bsz, D_ATT, s), BF16)],
        compiler_params=_params(("parallel", "parallel"), VMEM_LIMIT),
        name="norm1_in_proj",
    )(x, scale1, shift1, g_norm1.reshape(1, d), w_main, w_vt, mseg,
      jnp.tile(g_q, n_sub).reshape(1, D_ATT), jnp.tile(g_k, n_sub).reshape(1, D_ATT))


def _lru_kernel(x_ref, gz_ref, cw_ref, cb_ref, wa_ref, wx_ref, ba_ref, bx_ref, lam_ref, y_ref,
                xpad, a_f, u_f, a_b, u_b, *, tc):
    s = x_ref.shape[1]
    c = x_ref.shape[2]
    n_chunks = s // tc
    n_slab = c // 128
    pitch = a_f.shape[1] // n_chunks
    zeros8 = jnp.zeros((8, c), F32)
    xpad[0:8, :] = zeros8
    xpad[s + 8:s + 16, :] = zeros8

    def fill(ci, carry):
        t0 = pl.multiple_of(ci * tc, tc)
        xpad[pl.ds(t0 + 8, tc), :] = x_ref[0, pl.ds(t0, tc), :]
        return carry

    lax.fori_loop(0, n_chunks, fill, 0)

    cw = cw_ref[...]
    cb = cb_ref[...]
    decay = []
    for d in range(2):
        lam = lam_ref[d]
        softplus_neg = jnp.maximum(-lam, 0.0) + jnp.log(1.0 + jnp.exp(-jnp.abs(lam)))
        decay.append((-LRU_C * LOG2E) * softplus_neg)
    a_scr = (a_f, a_b)
    u_scr = (u_f, u_b)

    def gates(ci, carry):
        t0 = pl.multiple_of(ci * tc, tc)
        xw = xpad[pl.ds(t0, tc + 16), :]
        xc = (cw[0:1] * pltpu.roll(xw, 2, 0)[8:8 + tc]
              + cw[1:2] * pltpu.roll(xw, 1, 0)[8:8 + tc]
              + cw[2:3] * xw[8:8 + tc]
              + cw[3:4] * pltpu.roll(xw, tc + 15, 0)[8:8 + tc]) + cb
        xcb = xc.astype(BF16)
        for d in range(2):
            r = _sigmoid(jnp.dot(xcb, wa_ref[d], preferred_element_type=F32) + ba_ref[d])
            i = _sigmoid(jnp.dot(xcb, wx_ref[d], preferred_element_type=F32) + bx_ref[d])
            a = jnp.exp2(r * decay[d])
            v = 1.0 - a * a
            u = (v * lax.rsqrt(jnp.maximum(v, F32_TINY))) * (i * xc)
            r0 = pl.multiple_of(ci * pitch, 8)
            for sl in range(n_slab):
                a_scr[d][sl, pl.ds(r0, tc), :] = a[:, sl * 128:(sl + 1) * 128]
                u_scr[d][sl, pl.ds(r0, tc), :] = u[:, sl * 128:(sl + 1) * 128]
        return carry

    lax.fori_loop(0, n_chunks, gates, 0, unroll=2)

    def rows(tt):
        return pl.ds(tt, n_chunks, stride=pitch)

    def step(tt, carry):
        out = []
        for d in range(2):
            t_loc = tt if d == 0 else tc - 1 - tt
            for sl in range(n_slab):
                h, p = carry[len(out)]
                a8 = a_scr[d][sl, rows(t_loc), :]
                h = a8 * h + u_scr[d][sl, rows(t_loc), :]
                p = a8 * p
                u_scr[d][sl, rows(t_loc), :] = h
                a_scr[d][sl, rows(t_loc), :] = p
                out.append((h, p))
        return tuple(out)

    init = (jnp.zeros((n_chunks, 128), F32), jnp.ones((n_chunks, 128), F32))
    lax.fori_loop(0, tc, step, (init,) * (2 * n_slab), unroll=4)

    for sl in range(n_slab):
        lanes = slice(sl * 128, (sl + 1) * 128)
        h_end, p_end = u_f[sl, rows(tc - 1), :], a_f[sl, rows(tc - 1), :]
        h_beg, p_beg = u_b[sl, rows(0), :], a_b[sl, rows(0), :]
        carry_f = [jnp.zeros((1, 128), F32)]
        for r in range(1, n_chunks):
            carry_f.append(p_end[r - 1:r] * carry_f[-1] + h_end[r - 1:r])
        carry_b = [jnp.zeros((1, 128), F32)]
        for r in range(n_chunks - 2, -1, -1):
            carry_b.insert(0, p_beg[r + 1:r + 2] * carry_b[0] + h_beg[r + 1:r + 2])
        for r in range(n_chunks):
            blk = slice(r * pitch, r * pitch + tc)
            hsum = (u_f[sl, blk, :] + a_f[sl, blk, :] * carry_f[r]
                    + u_b[sl, blk, :] + a_b[sl, blk, :] * carry_b[r])
            y_ref[0, r * tc:(r + 1) * tc, lanes] = (hsum * gz_ref[0, r * tc:(r + 1) * tc, lanes].astype(F32)).astype(BF16)


def _block_diag(w, half):
    n_dir, n_blocks, blk, _ = w.shape
    per = half // blk
    n_half = n_blocks // per
    w = w.reshape(n_dir, n_half, per, blk, blk)
    eye = jnp.eye(per, dtype=w.dtype)
    out = w[:, :, :, :, None, :] * eye[None, None, :, None, :, None]
    return out.reshape(n_dir, n_half, half, half)


def _rg_lru(x_lru, gz, conv_w, conv_b, w_a, b_a, w_x, b_x, lam):
    bsz, s, c = x_lru.shape
    half = LRU_HALF
    tc = s // SCAN_LANES
    pitch = tc + 8 if (tc // 8) % 2 == 0 else tc + 16
    n_half = c // half
    wa = _block_diag(w_a, half).astype(BF16)
    wx = _block_diag(w_x, half).astype(BF16)
    seq = lambda b, p: (b, 0, p)
    chan = lambda b, p: (0, p)
    dirchan = lambda b, p: (0, 0, p)
    blk = lambda b, p: (0, p, 0, 0)
    return pl.pallas_call(
        functools.partial(_lru_kernel, tc=tc),
        grid=(bsz, n_half),
        in_specs=[pl.BlockSpec((1, s, half), seq),
                  pl.BlockSpec((1, s, half), seq),
                  pl.BlockSpec((CONV_W, half), chan),
                  pl.BlockSpec((1, half), chan),
                  pl.BlockSpec((2, None, half, half), blk),
                  pl.BlockSpec((2, None, half, half), blk),
                  pl.BlockSpec((2, 1, half), dirchan),
                  pl.BlockSpec((2, 1, half), dirchan),
                  pl.BlockSpec((2, 1, half), dirchan)],
        out_specs=pl.BlockSpec((1, s, half), seq),
        out_shape=jax.ShapeDtypeStruct((bsz, s, c), BF16),
        scratch_shapes=[pltpu.VMEM((s + 16, half), F32)]
                       + [pltpu.VMEM((half // 128, SCAN_LANES * pitch, 128), F32)] * 4,
        compiler_params=_params(("parallel", "parallel"), VMEM_LIMIT),
        name="rg_lru",
    )(x_lru, gz, conv_w.reshape(CONV_W, c), conv_b.reshape(1, c), wa, wx,
      b_a.reshape(2, 1, c), b_x.reshape(2, 1, c), lam.reshape(2, 1, c))


def _attn_kernel(q_ref, k_ref, vt_ref, bias_ref, lq_ref, go_ref, w1_ref, w3_ref, w2_ref,
                 o_ref, w1b_ref, w3b_ref, w2b_ref,
                 s_even, s_odd, m_even, m_odd, o_even, o_odd, p_scr, *, lam_init):
    for w_ref, wb_ref in ((w1_ref, w1b_ref), (w3_ref, w3b_ref), (w2_ref, w2b_ref)):
        wb_ref[...] = w_ref[...].astype(BF16)

    t = bias_ref.shape[-1]
    width = s_even.shape[0]
    n_k = k_ref.shape[1] // t
    n_q = q_ref.shape[1] // t
    n_steps = n_q // width
    subs = [(w, u) for w in range(width) for u in range(2)]
    chunk = 2
    pv_chunk = 2
    n_c = n_k // chunk

    s_odd[...] = jnp.zeros_like(s_odd)
    m_odd[...] = jnp.zeros_like(m_odd)
    o_odd[...] = jnp.ones_like(o_odd)
    even = dict(s=s_even, m=m_even, o=o_even)
    odd = dict(s=s_odd, m=m_odd, o=o_odd)
    ones_rows = jnp.ones((PV_ONES, pv_chunk * t), BF16)

    def fold(x, op):
        parts = [x[r * 8:(r + 1) * 8, :] for r in range(t // 8)]
        acc = parts[:2]
        for r in range(2, len(parts)):
            acc[r % 2] = op(acc[r % 2], parts[r])
        return op(acc[0], acc[1])

    def bias_tile(j, tile):
        return bias_ref[0, jnp.clip(j - tile, -2, 2) + 2]

    def finish(prev, w, tile):
        outs = [prev['o'][w, u, :HEAD_DV, :] / prev['o'][w, u, HEAD_DV:HEAD_DV + 1, :] for u in range(2)]
        lq = lq_ref[...]
        lam = (jnp.exp(jnp.sum(lq[0:1] * lq[1:2], axis=-1, keepdims=True))
               - jnp.exp(jnp.sum(lq[2:3] * lq[3:4], axis=-1, keepdims=True)) + lam_init)
        o = (outs[0] - lam * outs[1]).T
        ms = jnp.mean(o * o, axis=-1, keepdims=True)
        o_ref[0, pl.ds(pl.multiple_of(tile * t, t), t), :] = (
            (o * lax.rsqrt(ms + EPS) * go_ref[...]) * (1.0 - lam_init)).astype(BF16)

    def step(i, cur, prev, scores=True, softmax=True):
        tiles_a = [i * width + w for w in range(width)]
        for w in range(width):
            finish(prev, w, jnp.maximum((i - 2) * width + w, 0))
        s_cur, s_prev = cur['s'], prev['s']
        m_b = {wu: jnp.max(prev['m'][wu], axis=0, keepdims=True) for wu in subs} if softmax else {}
        m_acc = {wu: jnp.full((8, t), -jnp.inf, F32) for wu in subs}
        q_sub = {}
        if scores:
            for w in range(width):
                q = q_ref[0, pl.ds(pl.multiple_of(tiles_a[w] * t, t), t), :]
                lane = lax.broadcasted_iota(jnp.int32, q.shape, 1)
                zero = jnp.zeros_like(q)
                q_sub[w, 0] = jnp.where(lane < HEAD_DK, q, zero)
                q_sub[w, 1] = jnp.where(lane >= HEAD_DK, q, zero)
        for c in range(n_c):
            rows = slice(c * chunk * t, (c + 1) * chunk * t)
            for w, u in (subs if scores else []):
                sc_all = lax.dot_general(k_ref[0, rows, :], q_sub[w, u], NT_DIMS, preferred_element_type=F32)
                for jj in range(chunk):
                    j = c * chunk + jj
                    sc = sc_all[jj * t:(jj + 1) * t, :] + bias_tile(j, tiles_a[w])
                    s_cur[w, u, j] = sc
                    m_acc[w, u] = jnp.maximum(m_acc[w, u], fold(sc, jnp.maximum))
            for w, u in (subs if softmax else []):
                for jj in range(chunk):
                    j = c * chunk + jj
                    p = jnp.exp2(s_prev[w, u, j] - m_b[w, u])
                    p_scr[w, u, j * t:(j + 1) * t, :] = p.astype(BF16)
                if ((c + 1) * chunk) % pv_chunk == 0:
                    keys = slice(((c + 1) * chunk - pv_chunk) * t, (c + 1) * chunk * t)
                    part = jnp.dot(jnp.concatenate([vt_ref[0, :, keys], ones_rows], axis=0), p_scr[w, u, keys, :],
                                   preferred_element_type=F32)
                    if (c + 1) * chunk == pv_chunk:
                        cur['o'][w, u] = part
                    else:
                        cur['o'][w, u] += part

        if scores:
            for w, u in subs:
                cur['m'][w, u] = m_acc[w, u]

    def pair(pi, carry):
        @pl.when(pi >= 0)
        def _():
            step(2 * pi, even, odd)

        @pl.when(pi < n_steps)
        def _():
            step(2 * pi + 1, odd, even)

        return carry

    lax.fori_loop(0, n_steps // 2, pair, 0)
    step(n_steps, even, odd, scores=False)
    for w in range(width):
        finish(even, w, n_q - width + w)


def _diff_attention(qn, kn, vt, bias_tiles, lambda_qk, g_o, lam_init, expert_w):
    bsz, s, _ = qn.shape
    w_rows = expert_w[0].shape[0] * expert_w[0].shape[1]
    w_cols = expert_w[0].shape[2]
    assert all(w.shape == expert_w[0].shape for w in expert_w) and w_rows % (bsz * N_HEADS * 16) == 0
    w_blk = pl.BlockSpec((w_rows // (bsz * N_HEADS), w_cols), lambda b, h: (b * N_HEADS + h, 0))
    t = bias_tiles.shape[-1]
    n_k = s // t
    width = ATT_WIDTH
    assert (s // t) % (2 * width) == 0, "the query-tile pipeline advances two groups of tiles per loop trip"
    scores = pltpu.VMEM((width, 2, n_k, t, t), F32)
    maxima = pltpu.VMEM((width, 2, 8, t), F32)
    pv_acc = pltpu.VMEM((width, 2, HEAD_DV + PV_ONES, t), F32)
    seq = lambda b, h: (b, 0, h)
    outs = pl.pallas_call(
        functools.partial(_attn_kernel, lam_init=lam_init),
        grid=(bsz, N_HEADS),
        in_specs=[pl.BlockSpec((1, s, HEAD_DV), seq),
                  pl.BlockSpec((1, s, HEAD_DV), seq),
                  pl.BlockSpec((1, HEAD_DV, s), lambda b, h: (b, h, 0)),
                  pl.BlockSpec((1, 5, t, t), lambda b, h: (h, 0, 0, 0)),
                  pl.BlockSpec((4, HEAD_DK), lambda b, h: (0, 0)),
                  pl.BlockSpec((1, HEAD_DV), lambda b, h: (0, 0)),
                  w_blk, w_blk, w_blk],
        out_specs=[pl.BlockSpec((1, s, HEAD_DV), seq), w_blk, w_blk, w_blk],
        out_shape=[jax.ShapeDtypeStruct((bsz, s, D_ATT), BF16)]
                  + [jax.ShapeDtypeStruct((w_rows, w_cols), BF16)] * 3,
        scratch_shapes=[scores, scores, maxima, maxima, pv_acc, pv_acc, pltpu.VMEM((width, 2, s, t), BF16)],
        compiler_params=_params(("parallel", "parallel"), VMEM_LIMIT),
        name="diff_attention",
    )(qn, kn, vt, bias_tiles, lambda_qk, g_o.reshape(1, HEAD_DV), *[w.reshape(w_rows, w_cols) for w in expert_w])
    return outs[0], [wb.reshape(expert_w[0].shape) for wb in outs[1:]]


def _outproj_kernel(yl_ref, ya_ref, x_ref, gate_ref, sc_ref, sh_ref, g2_ref, wo_ref, wr_ref,
                    x1_ref, h2_ref, aff_ref):
    mix = (jnp.dot(yl_ref[0], wo_ref[0:D_LRU, :], preferred_element_type=F32)
           + jnp.dot(ya_ref[0], wo_ref[D_LRU:D_LRU + D_ATT, :], preferred_element_type=F32))
    x1 = x_ref[0] + gate_ref[0] * mix
    x1_ref[0] = x1
    ms = jnp.mean(x1 * x1, axis=-1, keepdims=True)
    h2 = (x1 * lax.rsqrt(ms + EPS) * g2_ref[...]) * (1.0 + sc_ref[0]) + sh_ref[0]
    h2b = h2.astype(BF16)
    h2_ref[0] = h2b
    logits = lax.dot_general(wr_ref[...], h2b, NT_DIMS, preferred_element_type=F32)
    ex = jnp.exp(logits - jnp.max(logits, axis=0, keepdims=True))
    aff_ref[0] = ex / jnp.sum(ex, axis=0, keepdims=True)


def _out_projection(y_lru, y_att, x, gate1, scale2, shift2, g_norm2, w_out, w_router, tm):
    bsz, s, d = x.shape
    row = lambda b, i: (b, i, 0)
    vec = lambda b, i: (b, 0, 0)
    full = lambda b, i: (0, 0)
    return pl.pallas_call(
        _outproj_kernel,
        grid=(bsz, s // tm),
        in_specs=[pl.BlockSpec((1, tm, D_LRU), row),
                  pl.BlockSpec((1, tm, D_ATT), row),
                  pl.BlockSpec((1, tm, d), row),
                  pl.BlockSpec((1, 1, d), vec),
                  pl.BlockSpec((1, 1, d), vec),
                  pl.BlockSpec((1, 1, d), vec),
                  pl.BlockSpec((1, d), full),
                  pl.BlockSpec((D_LRU + D_ATT, d), full),
                  pl.BlockSpec((N_EXPERTS, d), full)],
        out_specs=[pl.BlockSpec((1, tm, d), row),
                   pl.BlockSpec((1, tm, d), row),
                   pl.BlockSpec((1, N_EXPERTS, tm), lambda b, i: (b, 0, i))],
        out_shape=[jax.ShapeDtypeStruct((bsz, s, d), F32),
                   jax.ShapeDtypeStruct((bsz, s, d), BF16),
                   jax.ShapeDtypeStruct((bsz, N_EXPERTS, s), F32)],
        compiler_params=_params(("parallel", "parallel"), VMEM_LIMIT),
        name="out_proj_norm2_router",
    )(y_lru, y_att, x, gate1, scale2, shift2, g_norm2.reshape(1, d), w_out.astype(BF16),
      w_router.T.astype(BF16))


def _route_kernel(aff_ref, pos_ref, cnt_ref, *, cap, n_tok_chunks):
    aff = aff_ref[0]
    n_e, s = aff.shape
    bits = lax.bitcast_convert_type(aff, jnp.int32)
    capf = float(cap)

    def count(mask):
        return jnp.sum(jnp.where(mask, 1.0, 0.0), axis=-1, keepdims=True)

    tau = jnp.zeros((n_e, 1), jnp.int32)
    for bit in range(30, -1, -1):
        cand = tau | (1 << bit)
        tau = jnp.where(count(bits >= cand) >= capf, cand, tau)
    gt = bits > tau
    eq = bits == tau
    need = capf - count(gt)

    blk = min(256, s // n_tok_chunks)
    r = lax.broadcasted_iota(jnp.int32, (blk, blk), 0)
    cidx = lax.broadcasted_iota(jnp.int32, (blk, blk), 1)
    upper = jnp.where(r < cidx, 1.0, 0.0).astype(BF16)

    def prefix_blocks(mask):
        off = jnp.zeros((n_e, 1), F32)
        pieces, offs = [], []
        for k in range(s // blk):
            mb = jnp.where(mask[:, k * blk:(k + 1) * blk], 1.0, 0.0)
            offs.append(off)
            pieces.append(jnp.dot(mb.astype(BF16), upper, preferred_element_type=F32) + off)
            off = off + jnp.sum(mb, axis=-1, keepdims=True)
        return pieces, offs

    eq_rank, _ = prefix_blocks(eq)
    sel_blocks = []
    for k in range(s // blk):
        sl = slice(k * blk, (k + 1) * blk)
        sel_blocks.append(jnp.logical_or(gt[:, sl], jnp.logical_and(eq[:, sl], eq_rank[k] < need)))
    sel = jnp.concatenate(sel_blocks, axis=1)
    slot, offs = prefix_blocks(sel)
    for k in range(s // blk):
        pos_ref[0, :, k * blk:(k + 1) * blk] = jnp.where(sel_blocks[k], slot[k], -1.0).astype(jnp.int32)

    lane = lax.broadcasted_iota(jnp.int32, (n_e, 128), 1)
    cnt = jnp.zeros((n_e, 128), F32)
    per = (s // n_tok_chunks) // blk
    for j in range(n_tok_chunks):
        cnt = jnp.where(lane == j, offs[j * per], cnt)
    cnt_ref[0] = cnt.astype(jnp.int32)


def _routing(aff, cap):
    bsz, n_e, s = aff.shape
    group = math.gcd(bsz, ROUTE_GROUP)
    rows = group * n_e
    pos, cnt = pl.pallas_call(
        functools.partial(_route_kernel, cap=cap, n_tok_chunks=N_TOK_CHUNKS),
        grid=(bsz // group,),
        in_specs=[pl.BlockSpec((1, rows, s), lambda b: (b, 0, 0))],
        out_specs=[pl.BlockSpec((1, rows, s), lambda b: (b, 0, 0)),
                   pl.BlockSpec((1, rows, 128), lambda b: (b, 0, 0))],
        out_shape=[jax.ShapeDtypeStruct((bsz // group, rows, s), jnp.int32),
                   jax.ShapeDtypeStruct((bsz // group, rows, 128), jnp.int32)],
        compiler_params=_params(("parallel",)),
        name="expert_choice_routing",
    )(aff.reshape(bsz // group, rows, s))
    return pos.reshape(bsz, n_e, s), cnt.reshape(bsz, n_e, 128)


def _moe_kernel(cnt_ref, pos_ref, aff_ref, h2_ref, w1_ref, w3_ref, w2_ref, gate_ref, x1_hbm, out_ref,
                xe_scr, g_scr, y_scr, x1_sem, *, cap):
    b = pl.program_id(0)
    e = pl.program_id(1)
    n_e = pl.num_programs(1)
    s = h2_ref.shape[1]
    tchunk = s // N_TOK_CHUNKS

    def residual_copy():
        return pltpu.make_async_copy(x1_hbm.at[b], out_ref.at[0], x1_sem.at[0])

    @pl.when(e == 0)
    def _():
        residual_copy().start()

    xe_scr[...] = jnp.zeros_like(xe_scr)
    g_scr[...] = jnp.zeros_like(g_scr)
    base = (b * n_e + e) * N_TOK_CHUNKS

    def walk(cchunk, visit):
        n_cap = cap // cchunk
        slot_iota = lax.broadcasted_iota(jnp.int32, (cchunk, tchunk), 0)
        j = jnp.int32(0)
        i = jnp.int32(0)
        for _ in range(N_TOK_CHUNKS + n_cap - 1):
            jc = jnp.minimum(j, N_TOK_CHUNKS - 1)
            ic = jnp.minimum(i, n_cap - 1)
            c_lo = cnt_ref[base + jc]
            c_hi = jnp.where(jc + 1 < N_TOK_CHUNKS, cnt_ref[base + jnp.minimum(jc + 1, N_TOK_CHUNKS - 1)], cap)
            s_hi = (ic + 1) * cchunk
            valid = jnp.logical_and(jnp.maximum(c_lo, ic * cchunk) < jnp.minimum(c_hi, s_hi),
                                    jnp.logical_and(j < N_TOK_CHUNKS, i < n_cap))
            slot0 = jnp.where(valid, ic * cchunk, -2 * cap)
            sel = pos_ref[0, 0, pl.ds(jc, 1), :] == slot_iota + slot0
            visit(sel, jc, pl.ds(pl.multiple_of(ic * cchunk, cchunk), cchunk),
                  pl.ds(pl.multiple_of(jc * tchunk, tchunk), tchunk))
            j = j + jnp.where(c_hi <= s_hi, 1, 0)
            i = i + jnp.where(s_hi <= c_hi, 1, 0)

    def gather(sel, jc, rows, toks):
        xe_scr[rows, :] += jnp.dot(jnp.where(sel, 1.0, 0.0).astype(BF16), h2_ref[0, toks, :],
                                   preferred_element_type=F32)
        g_scr[rows, :] += jnp.sum(jnp.where(sel, aff_ref[0, 0, pl.ds(jc, 1), :], 0.0), axis=-1, keepdims=True)

    walk(cap // N_CAP_CHUNKS, gather)

    xe = xe_scr[...].astype(BF16)
    a = jnp.dot(xe, w1_ref[0], preferred_element_type=F32)
    gate = jnp.dot(xe, w3_ref[0], preferred_element_type=F32)
    hmid = ((a * _sigmoid(a)) * gate).astype(BF16)
    y = jnp.dot(hmid, w2_ref[0], preferred_element_type=F32) * g_scr[...] * gate_ref[0]
    y_scr[...] = y.astype(BF16)

    @pl.when(e == 0)
    def _():
        residual_copy().wait()

    def scatter(sel, jc, rows, toks):
        out_ref[0, toks, :] += lax.dot_general(jnp.where(sel, 1.0, 0.0).astype(BF16), y_scr[rows, :],
                                               TN_DIMS, preferred_element_type=F32)

    walk(cap // N_SCATTER_CHUNKS, scatter)


def _moe(cnt, pos, aff, h2, w1, w3, w2, gate2, x1, cap):
    bsz, s, d = h2.shape
    n_e = w1.shape[0]
    f = w1.shape[2]
    tok_row = lambda b, e, c: (b, e, 0, 0)
    wspec = lambda b, e, c: (e, 0, 0)
    resident = lambda b, e, c: (b, 0, 0)
    grid_spec = pltpu.PrefetchScalarGridSpec(
        num_scalar_prefetch=1,
        grid=(bsz, n_e),
        in_specs=[pl.BlockSpec((1, 1, N_TOK_CHUNKS, s // N_TOK_CHUNKS), tok_row),
                  pl.BlockSpec((1, 1, N_TOK_CHUNKS, s // N_TOK_CHUNKS), tok_row),
                  pl.BlockSpec((1, s, d), resident),
                  pl.BlockSpec((1, d, f), wspec),
                  pl.BlockSpec((1, d, f), wspec),
                  pl.BlockSpec((1, f, d), wspec),
                  pl.BlockSpec((1, 1, d), resident),
                  pl.BlockSpec(memory_space=pl.ANY)],
        out_specs=pl.BlockSpec((1, s, d), resident, pipeline_mode=pl.Buffered(1)),
        scratch_shapes=[pltpu.VMEM((cap, d), F32), pltpu.VMEM((cap, 1), F32), pltpu.VMEM((cap, d), BF16),
                        pltpu.SemaphoreType.DMA((1,))],
    )
    return pl.pallas_call(
        functools.partial(_moe_kernel, cap=cap),
        grid_spec=grid_spec,
        out_shape=jax.ShapeDtypeStruct((bsz, s, d), F32),
        compiler_params=_params(("arbitrary", "arbitrary"), VMEM_LIMIT),
        name="expert_choice_ffn",
    )(cnt.reshape(-1), pos.reshape(bsz, n_e, N_TOK_CHUNKS, -1), aff.reshape(bsz, n_e, N_TOK_CHUNKS, -1), h2,
      w1, w3, w2, gate2, x1)


def kernel(x, c, w_mod, b_mod, g_norm1, w_in, conv_w, conv_b, lru_w_a, lru_b_a, lru_w_x, lru_b_x,
           lru_lambda, g_q, g_k, lambda_qk, g_attn_out, rel_bias, w_out, g_norm2, w_router, w1, w3, w2):
    bsz, s, d = x.shape
    depth = w_mod.shape[0]
    cap = max(1, EC_FACTOR * s // N_EXPERTS)
    bias_tiles = _bias_tiles(rel_bias, min(ATT_TILE, s))
    for l in range(depth):
        mod = _modulation(c, w_mod[l], b_mod[l])
        shift1, scale1, gate1, shift2, scale2, gate2 = [m.reshape(bsz, 1, d) for m in jnp.split(mod, 6, axis=-1)]
        x_lru, gz, qn, kn, vt = _in_projection(x, scale1, shift1, g_norm1[l], w_in[l], g_q[l], g_k[l],
                                               min(IN_PROJ_ROWS, s))
        y_lru = _rg_lru(x_lru, gz, conv_w[l], conv_b[l], lru_w_a[l], lru_b_a[l], lru_w_x[l], lru_b_x[l],
                        lru_lambda[l])
        lam_init = 0.8 - 0.6 * math.exp(-0.3 * l)
        y_att, (w1b, w3b, w2b) = _diff_attention(qn, kn, vt, bias_tiles, lambda_qk[l], g_attn_out[l], lam_init,
                                                 (w1[l], w3[l], w2[l]))
        x1, h2, aff = _out_projection(y_lru, y_att, x, gate1, scale2, shift2, g_norm2[l], w_out[l],
                                      w_router[l], min(OUT_PROJ_ROWS, s))
        pos, cnt = _routing(aff, cap)
        x = _moe(cnt[:, :, :N_TOK_CHUNKS], pos, aff, h2, w1b, w3b, w2b, gate2, x1, cap)
    return x
```

```python
import functools
import math

import jax
import jax.numpy as jnp
from jax import lax
from jax.experimental import pallas as pl
from jax.experimental.pallas import tpu as pltpu

F32 = jnp.float32
BF16 = jnp.bfloat16

D_MODEL = 1024
D_LRU = 512
LRU_BLOCK = 64
LRU_C = 8.0
CONV_W = 4
N_HEADS = 4
HEAD_DV = 128
HEAD_DK = 64
D_ATT = N_HEADS * HEAD_DV
N_BUCKETS = 32
N_EXPERTS = 16
EC_FACTOR = 2
EPS = 1e-6
F32_TINY = 2.0 ** -126
LOG2E = math.log2(math.e)

IN_PROJ_ROWS = 1024
OUT_PROJ_ROWS = 1024
LRU_HALF = 256
SCAN_LANES = 8
ATT_TILE = 256
PV_ONES = 16
ATT_WIDTH = 1
ROUTE_GROUP = 8
N_TOK_CHUNKS = 16
N_CAP_CHUNKS = 4
COMBINE_WINDOW = 64
VMEM_LIMIT = 56 * 1024 * 1024

NT_DIMS = (((1,), (1,)), ((), ()))
TN_DIMS = (((0,), (0,)), ((), ()))


def _sigmoid(x):
    return 1.0 / (1.0 + jnp.exp(-x))


def _params(sem, vmem=None):
    return pltpu.CompilerParams(dimension_semantics=sem, vmem_limit_bytes=vmem)


def _mod_kernel(c_ref, w_ref, b_ref, o_ref):
    c = c_ref[...]
    o_ref[...] = jnp.dot((c * _sigmoid(c)).astype(BF16), w_ref[...].astype(BF16),
                         preferred_element_type=F32) + b_ref[...]


def _modulation(c, w_mod, b_mod):
    bsz, d = c.shape
    n = w_mod.shape[1]
    return pl.pallas_call(
        _mod_kernel,
        grid=(n // d,),
        in_specs=[pl.BlockSpec((bsz, d), lambda j: (0, 0)),
                  pl.BlockSpec((d, d), lambda j: (0, j)),
                  pl.BlockSpec((1, d), lambda j: (0, j))],
        out_specs=pl.BlockSpec((bsz, d), lambda j: (0, j)),
        out_shape=jax.ShapeDtypeStruct((bsz, n), F32),
        compiler_params=_params(("arbitrary",)),
        name="adaln_mod",
    )(c, w_mod, b_mod.reshape(1, n))


def _bias_kernel(tab_ref, o_ref):
    h = pl.program_id(0)
    t = o_ref.shape[-1]
    key = lax.broadcasted_iota(jnp.int32, (t, t), 0)
    qry = lax.broadcasted_iota(jnp.int32, (t, t), 1)
    half = N_BUCKETS // 2
    max_exact = half // 2
    for d in range(5):
        if d == 0:
            o_ref[0, d] = jnp.full((t, t), tab_ref[half - 1, h] * LOG2E, F32)
        elif d == 4:
            o_ref[0, d] = jnp.full((t, t), tab_ref[N_BUCKETS - 1, h] * LOG2E, F32)
        else:
            rel = (d - 2) * t + key - qry
            n = jnp.abs(rel)
            n2 = n * n
            large = jnp.full((t, t), max_exact, jnp.int32)
            for k in range(1, half - max_exact):
                large = large + jnp.where(n2 >= (max_exact * max_exact) * (2 ** k), 1, 0)
            idx = jnp.where(n < max_exact, n, large) + jnp.where(rel > 0, half, 0)
            val = jnp.zeros((t, t), F32)
            for j in range(N_BUCKETS):
                val = jnp.where(idx == j, tab_ref[j, h] * LOG2E, val)
            o_ref[0, d] = val


def _bias_tiles(rel_bias, t):
    return pl.pallas_call(
        _bias_kernel,
        grid=(N_HEADS,),
        in_specs=[pl.BlockSpec(memory_space=pltpu.SMEM)],
        out_specs=pl.BlockSpec((1, 5, t, t), lambda h: (h, 0, 0, 0)),
        out_shape=jax.ShapeDtypeStruct((N_HEADS, 5, t, t), F32),
        compiler_params=_params(("arbitrary",)),
        name="t5_bias_tiles",
    )(rel_bias)


def _inproj_kernel(x_ref, sc_ref, sh_ref, g1_ref, w_ref, wvt_ref, mseg_ref, gq_ref, gk_ref,
                   xl_ref, gz_ref, q_ref, k_ref, vt_ref):
    x = x_ref[0]
    ms = jnp.mean(x * x, axis=-1, keepdims=True)
    h = (x * lax.rsqrt(ms + EPS) * g1_ref[...]) * (1.0 + sc_ref[0]) + sh_ref[0]
    hb = h.astype(BF16)

    def proj(lo, width):
        return jnp.dot(hb, w_ref[:, lo:lo + width], preferred_element_type=F32)

    def qk_norm(t, g):
        ss = jnp.dot((t * t).astype(BF16), mseg_ref[...], preferred_element_type=F32)
        return t * lax.rsqrt(ss * (1.0 / HEAD_DK) + EPS) * g

    xl_ref[0] = proj(0, D_LRU)
    z = proj(D_LRU, D_LRU)
    cdf = 0.5 * (1.0 + jnp.tanh(math.sqrt(2.0 / math.pi) * (z + 0.044715 * (z * z * z))))
    gz_ref[0] = (z * cdf).astype(BF16)
    q_ref[0] = (qk_norm(proj(2 * D_LRU, D_ATT), gq_ref[...]) * (HEAD_DK ** -0.5 * LOG2E)).astype(BF16)
    k_ref[0] = qk_norm(proj(2 * D_LRU + D_ATT, D_ATT), gk_ref[...]).astype(BF16)
    vt_ref[0] = lax.dot_general(wvt_ref[...], hb, NT_DIMS,
                                preferred_element_type=F32).astype(BF16)


def _in_projection(x, scale1, shift1, g_norm1, w_in, g_q, g_k, tm):
    bsz, s, d = x.shape
    n = w_in.shape[1] - D_ATT
    w_main = w_in[:, :n].astype(BF16)
    w_vt = w_in[:, n:].T.astype(BF16)
    seg = jnp.arange(D_ATT, dtype=jnp.int32) // HEAD_DK
    mseg = (seg[:, None] == seg[None, :]).astype(BF16)
    n_sub = D_ATT // HEAD_DK
    row = lambda b, i: (b, i, 0)
    vec = lambda b, i: (b, 0, 0)
    full = lambda b, i: (0, 0)
    out_block = pl.BlockSpec((1, tm, D_LRU), row)
    return pl.pallas_call(
        _inproj_kernel,
        grid=(bsz, s // tm),
        in_specs=[pl.BlockSpec((1, tm, d), row),
                  pl.BlockSpec((1, 1, d), vec),
                  pl.BlockSpec((1, 1, d), vec),
                  pl.BlockSpec((1, d), full),
                  pl.BlockSpec((d, n), full),
                  pl.BlockSpec((D_ATT, d), full),
                  pl.BlockSpec((D_ATT, D_ATT), full),
                  pl.BlockSpec((1, D_ATT), full),
                  pl.BlockSpec((1, D_ATT), full)],
        out_specs=[out_block] * 4 + [pl.BlockSpec((1, D_ATT, tm), lambda b, i: (b, 0, i))],
        out_shape=[jax.ShapeDtypeStruct((bsz, s, D_LRU), F32)]
                  + [jax.ShapeDtypeStruct((bsz, s, D_LRU), BF16)] * 3
                  + [jax.ShapeDtypeStruct((bsz, D_ATT, s), BF16)],
        compiler_params=_params(("parallel", "parallel"), VMEM_LIMIT),
        name="norm1_in_proj",
    )(x, scale1, shift1, g_norm1.reshape(1, d), w_main, w_vt, mseg,
      jnp.tile(g_q, n_sub).reshape(1, D_ATT), jnp.tile(g_k, n_sub).reshape(1, D_ATT))


def _lru_kernel(x_ref, gz_ref, cw_ref, cb_ref, wa_ref, wx_ref, ba_ref, bx_ref, lam_ref, y_ref,
                xpad, a_f, u_f, a_b, u_b, *, tc):
    s = x_ref.shape[1]
    c = x_ref.shape[2]
    n_chunks = s // tc
    n_slab = c // 128
    pitch = a_f.shape[1] // n_chunks
    zeros8 = jnp.zeros((8, c), F32)
    xpad[0:8, :] = zeros8
    xpad[s + 8:s + 16, :] = zeros8

    def fill(ci, carry):
        t0 = pl.multiple_of(ci * tc, tc)
        xpad[pl.ds(t0 + 8, tc), :] = x_ref[0, pl.ds(t0, tc), :]
        return carry

    lax.fori_loop(0, n_chunks, fill, 0)

    cw = cw_ref[...]
    cb = cb_ref[...]
    decay = []
    for d in range(2):
        lam = lam_ref[d]
        softplus_neg = jnp.maximum(-lam, 0.0) + jnp.log(1.0 + jnp.exp(-jnp.abs(lam)))
        decay.append((-LRU_C * LOG2E) * softplus_neg)
    a_scr = (a_f, a_b)
    u_scr = (u_f, u_b)

    def gates(ci, carry):
        t0 = pl.multiple_of(ci * tc, tc)
        xw = xpad[pl.ds(t0, tc + 16), :]
        xc = (cw[0:1] * pltpu.roll(xw, 2, 0)[8:8 + tc]
              + cw[1:2] * pltpu.roll(xw, 1, 0)[8:8 + tc]
              + cw[2:3] * xw[8:8 + tc]
              + cw[3:4] * pltpu.roll(xw, tc + 15, 0)[8:8 + tc]) + cb
        xcb = xc.astype(BF16)
        for d in range(2):
            r = _sigmoid(jnp.dot(xcb, wa_ref[d], preferred_element_type=F32) + ba_ref[d])
            i = _sigmoid(jnp.dot(xcb, wx_ref[d], preferred_element_type=F32) + bx_ref[d])
            a = jnp.exp2(r * decay[d])
            v = 1.0 - a * a
            u = (v * lax.rsqrt(jnp.maximum(v, F32_TINY))) * (i * xc)
            r0 = pl.multiple_of(ci * pitch, 8)
            for sl in range(n_slab):
                a_scr[d][sl, pl.ds(r0, tc), :] = a[:, sl * 128:(sl + 1) * 128]
                u_scr[d][sl, pl.ds(r0, tc), :] = u[:, sl * 128:(sl + 1) * 128]
        return carry

    lax.fori_loop(0, n_chunks, gates, 0, unroll=2)

    def rows(tt):
        return pl.ds(tt, n_chunks, stride=pitch)

    def step(tt, carry):
        out = []
        for d in range(2):
            t_loc = tt if d == 0 else tc - 1 - tt
            for sl in range(n_slab):
                h, p = carry[len(out)]
                a8 = a_scr[d][sl, rows(t_loc), :]
                h = a8 * h + u_scr[d][sl, rows(t_loc), :]
                p = a8 * p
                u_scr[d][sl, rows(t_loc), :] = h
                a_scr[d][sl, rows(t_loc), :] = p
                out.append((h, p))
        return tuple(out)

    init = (jnp.zeros((n_chunks, 128), F32), jnp.ones((n_chunks, 128), F32))
    lax.fori_loop(0, tc, step, (init,) * (2 * n_slab), unroll=4)

    for sl in range(n_slab):
        lanes = slice(sl * 128, (sl + 1) * 128)
        h_end, p_end = u_f[sl, rows(tc - 1), :], a_f[sl, rows(tc - 1), :]
        h_beg, p_beg = u_b[sl, rows(0), :], a_b[sl, rows(0), :]
        carry_f = [jnp.zeros((1, 128), F32)]
        for r in range(1, n_chunks):
            carry_f.append(p_end[r - 1:r] * carry_f[-1] + h_end[r - 1:r])
        carry_b = [jnp.zeros((1, 128), F32)]
        for r in range(n_chunks - 2, -1, -1):
            carry_b.insert(0, p_beg[r + 1:r + 2] * carry_b[0] + h_beg[r + 1:r + 2])
        for r in range(n_chunks):
            blk = slice(r * pitch, r * pitch + tc)
            hsum = (u_f[sl, blk, :] + a_f[sl, blk, :] * carry_f[r]
                    + u_b[sl, blk, :] + a_b[sl, blk, :] * carry_b[r])
            y_ref[0, r * tc:(r + 1) * tc, lanes] = (hsum * gz_ref[0, r * tc:(r + 1) * tc, lanes].astype(F32)).astype(BF16)


def _block_diag(w, half):
    n_dir, n_blocks, blk, _ = w.shape
    per = half // blk
    n_half = n_blocks // per
    w = w.reshape(n_dir, n_half, per, blk, blk)
    eye = jnp.eye(per, dtype=w.dtype)
    out = w[:, :, :, :, None, :] * eye[None, None, :, None, :, None]
    return out.reshape(n_dir, n_half, half, half)


def _rg_lru(x_lru, gz, conv_w, conv_b, w_a, b_a, w_x, b_x, lam):
    bsz, s, c = x_lru.shape
    half = LRU_HALF
    tc = s // SCAN_LANES
    pitch = tc + 8 if (tc // 8) % 2 == 0 else tc + 16
    n_half = c // half
    wa = _block_diag(w_a, half).astype(BF16)
    wx = _block_diag(w_x, half).astype(BF16)
    seq = lambda b, p: (b, 0, p)
    chan = lambda b, p: (0, p)
    dirchan = lambda b, p: (0, 0, p)
    blk = lambda b, p: (0, p, 0, 0)
    return pl.pallas_call(
        functools.partial(_lru_kernel, tc=tc),
        grid=(bsz, n_half),
        in_specs=[pl.BlockSpec((1, s, half), seq),
                  pl.BlockSpec((1, s, half), seq),
                  pl.BlockSpec((CONV_W, half), chan),
                  pl.BlockSpec((1, half), chan),
                  pl.BlockSpec((2, None, half, half), blk),
                  pl.BlockSpec((2, None, half, half), blk),
                  pl.BlockSpec((2, 1, half), dirchan),
                  pl.BlockSpec((2, 1, half), dirchan),
                  pl.BlockSpec((2, 1, half), dirchan)],
        out_specs=pl.BlockSpec((1, s, half), seq),
        out_shape=jax.ShapeDtypeStruct((bsz, s, c), BF16),
        scratch_shapes=[pltpu.VMEM((s + 16, half), F32)]
                       + [pltpu.VMEM((half // 128, SCAN_LANES * pitch, 128), F32)] * 4,
        compiler_params=_params(("parallel", "parallel"), VMEM_LIMIT),
        name="rg_lru",
    )(x_lru, gz, conv_w.reshape(CONV_W, c), conv_b.reshape(1, c), wa, wx,
      b_a.reshape(2, 1, c), b_x.reshape(2, 1, c), lam.reshape(2, 1, c))


def _attn_kernel(q_ref, k_ref, vt_ref, bias_ref, lq_ref, go_ref, w1_ref, w3_ref, w2_ref,
                 o_ref, w1b_ref, w3b_ref, w2b_ref,
                 s_even, s_odd, m_even, m_odd, o_even, o_odd, p_scr, *, lam_init):
    for w_ref, wb_ref in ((w1_ref, w1b_ref), (w3_ref, w3b_ref), (w2_ref, w2b_ref)):
        wb_ref[...] = w_ref[...].astype(BF16)

    t = bias_ref.shape[-1]
    width = s_even.shape[0]
    n_k = k_ref.shape[1] // t
    n_q = q_ref.shape[1] // t
    n_steps = n_q // width
    subs = [(w, u) for w in range(width) for u in range(2)]
    chunk = 2
    pv_chunk = 2
    n_c = n_k // chunk

    s_odd[...] = jnp.zeros_like(s_odd)
    m_odd[...] = jnp.zeros_like(m_odd)
    o_odd[...] = jnp.ones_like(o_odd)
    even = dict(s=s_even, m=m_even, o=o_even)
    odd = dict(s=s_odd, m=m_odd, o=o_odd)
    ones_rows = jnp.ones((PV_ONES, pv_chunk * t), BF16)

    def fold(x, op):
        parts = [x[r * 8:(r + 1) * 8, :] for r in range(t // 8)]
        acc = parts[:2]
        for r in range(2, len(parts)):
            acc[r % 2] = op(acc[r % 2], parts[r])
        return op(acc[0], acc[1])

    def bias_tile(j, tile):
        return bias_ref[0, jnp.clip(j - tile, -2, 2) + 2]

    def finish(prev, w, tile):
        outs = [prev['o'][w, u, :HEAD_DV, :] / prev['o'][w, u, HEAD_DV:HEAD_DV + 1, :] for u in range(2)]
        lq = lq_ref[...]
        lam = (jnp.exp(jnp.sum(lq[0:1] * lq[1:2], axis=-1, keepdims=True))
               - jnp.exp(jnp.sum(lq[2:3] * lq[3:4], axis=-1, keepdims=True)) + lam_init)
        o = (outs[0] - lam * outs[1]).T
        ms = jnp.mean(o * o, axis=-1, keepdims=True)
        o_ref[0, pl.ds(pl.multiple_of(tile * t, t), t), :] = (
            (o * lax.rsqrt(ms + EPS) * go_ref[...]) * (1.0 - lam_init)).astype(BF16)

    def step(i, cur, prev, scores=True, softmax=True):
        tiles_a = [i * width + w for w in range(width)]
        for w in range(width):
            finish(prev, w, jnp.maximum((i - 2) * width + w, 0))
        s_cur, s_prev = cur['s'], prev['s']
        m_b = {wu: jnp.max(prev['m'][wu], axis=0, keepdims=True) for wu in subs} if softmax else {}
        m_acc = {wu: jnp.full((8, t), -jnp.inf, F32) for wu in subs}
        q_sub = {}
        if scores:
            for w in range(width):
                q = q_ref[0, pl.ds(pl.multiple_of(tiles_a[w] * t, t), t), :]
                lane = lax.broadcasted_iota(jnp.int32, q.shape, 1)
                zero = jnp.zeros_like(q)
                q_sub[w, 0] = jnp.where(lane < HEAD_DK, q, zero)
                q_sub[w, 1] = jnp.where(lane >= HEAD_DK, q, zero)
        for c in range(n_c):
            rows = slice(c * chunk * t, (c + 1) * chunk * t)
            for w, u in (subs if scores else []):
                sc_all = lax.dot_general(k_ref[0, rows, :], q_sub[w, u], NT_DIMS, preferred_element_type=F32)
                for jj in range(chunk):
                    j = c * chunk + jj
                    sc = sc_all[jj * t:(jj + 1) * t, :] + bias_tile(j, tiles_a[w])
                    s_cur[w, u, j] = sc
                    m_acc[w, u] = jnp.maximum(m_acc[w, u], fold(sc, jnp.maximum))
            for w, u in (subs if softmax else []):
                for jj in range(chunk):
                    j = c * chunk + jj
                    p = jnp.exp2(s_prev[w, u, j] - m_b[w, u])
                    p_scr[w, u, j * t:(j + 1) * t, :] = p.astype(BF16)
                if ((c + 1) * chunk) % pv_chunk == 0:
                    keys = slice(((c + 1) * chunk - pv_chunk) * t, (c + 1) * chunk * t)
                    part = jnp.dot(jnp.concatenate([vt_ref[0, :, keys], ones_rows], axis=0), p_scr[w, u, keys, :],
                                   preferred_element_type=F32)
                    if (c + 1) * chunk == pv_chunk:
                        cur['o'][w, u] = part
                    else:
                        cur['o'][w, u] += part

        if scores:
            for w, u in subs:
                cur['m'][w, u] = m_acc[w, u]

    def pair(pi, carry):
        @pl.when(pi >= 0)
        def _():
            step(2 * pi, even, odd)

        @pl.when(pi < n_steps)
        def _():
            step(2 * pi + 1, odd, even)

        return carry

    lax.fori_loop(0, n_steps // 2, pair, 0)
    step(n_steps, even, odd, scores=False)
    for w in range(width):
        finish(even, w, n_q - width + w)


def _diff_attention(qn, kn, vt, bias_tiles, lambda_qk, g_o, lam_init, expert_w):
    bsz, s, _ = qn.shape
    w_rows = expert_w[0].shape[0] * expert_w[0].shape[1]
    w_cols = expert_w[0].shape[2]
    assert all(w.shape == expert_w[0].shape for w in expert_w) and w_rows % (bsz * N_HEADS * 16) == 0
    w_blk = pl.BlockSpec((w_rows // (bsz * N_HEADS), w_cols), lambda b, h: (b * N_HEADS + h, 0))
    t = bias_tiles.shape[-1]
    n_k = s // t
    width = ATT_WIDTH
    assert (s // t) % (2 * width) == 0, "the query-tile pipeline advances two groups of tiles per loop trip"
    scores = pltpu.VMEM((width, 2, n_k, t, t), F32)
    maxima = pltpu.VMEM((width, 2, 8, t), F32)
    pv_acc = pltpu.VMEM((width, 2, HEAD_DV + PV_ONES, t), F32)
    seq = lambda b, h: (b, 0, h)
    outs = pl.pallas_call(
        functools.partial(_attn_kernel, lam_init=lam_init),
        grid=(bsz, N_HEADS),
        in_specs=[pl.BlockSpec((1, s, HEAD_DV), seq),
                  pl.BlockSpec((1, s, HEAD_DV), seq),
                  pl.BlockSpec((1, HEAD_DV, s), lambda b, h: (b, h, 0)),
                  pl.BlockSpec((1, 5, t, t), lambda b, h: (h, 0, 0, 0)),
                  pl.BlockSpec((4, HEAD_DK), lambda b, h: (0, 0)),
                  pl.BlockSpec((1, HEAD_DV), lambda b, h: (0, 0)),
                  w_blk, w_blk, w_blk],
        out_specs=[pl.BlockSpec((1, s, HEAD_DV), seq), w_blk, w_blk, w_blk],
        out_shape=[jax.ShapeDtypeStruct((bsz, s, D_ATT), BF16)]
                  + [jax.ShapeDtypeStruct((w_rows, w_cols), BF16)] * 3,
        scratch_shapes=[scores, scores, maxima, maxima, pv_acc, pv_acc, pltpu.VMEM((width, 2, s, t), BF16)],
        compiler_params=_params(("parallel", "parallel"), VMEM_LIMIT),
        name="diff_attention",
    )(qn, kn, vt, bias_tiles, lambda_qk, g_o.reshape(1, HEAD_DV), *[w.reshape(w_rows, w_cols) for w in expert_w])
    return outs[0], [wb.reshape(expert_w[0].shape) for wb in outs[1:]]


def _outproj_kernel(yl_ref, ya_ref, x_ref, gate_ref, sc_ref, sh_ref, g2_ref, wo_ref, wr_ref,
                    x1_ref, h2_ref, aff_ref):
    mix = (jnp.dot(yl_ref[0], wo_ref[0:D_LRU, :], preferred_element_type=F32)
           + jnp.dot(ya_ref[0], wo_ref[D_LRU:D_LRU + D_ATT, :], preferred_element_type=F32))
    x1 = x_ref[0] + gate_ref[0] * mix
    x1_ref[0] = x1
    ms = jnp.mean(x1 * x1, axis=-1, keepdims=True)
    h2 = (x1 * lax.rsqrt(ms + EPS) * g2_ref[...]) * (1.0 + sc_ref[0]) + sh_ref[0]
    h2b = h2.astype(BF16)
    h2_ref[0] = h2b
    logits = lax.dot_general(wr_ref[...], h2b, NT_DIMS, preferred_element_type=F32)
    ex = jnp.exp(logits - jnp.max(logits, axis=0, keepdims=True))
    aff_ref[0] = ex / jnp.sum(ex, axis=0, keepdims=True)


def _out_projection(y_lru, y_att, x, gate1, scale2, shift2, g_norm2, w_out, w_router, tm):
    bsz, s, d = x.shape
    row = lambda b, i: (b, i, 0)
    vec = lambda b, i: (b, 0, 0)
    full = lambda b, i: (0, 0)
    return pl.pallas_call(
        _outproj_kernel,
        grid=(bsz, s // tm),
        in_specs=[pl.BlockSpec((1, tm, D_LRU), row),
                  pl.BlockSpec((1, tm, D_ATT), row),
                  pl.BlockSpec((1, tm, d), row),
                  pl.BlockSpec((1, 1, d), vec),
                  pl.BlockSpec((1, 1, d), vec),
                  pl.BlockSpec((1, 1, d), vec),
                  pl.BlockSpec((1, d), full),
                  pl.BlockSpec((D_LRU + D_ATT, d), full),
                  pl.BlockSpec((N_EXPERTS, d), full)],
        out_specs=[pl.BlockSpec((1, tm, d), row),
                   pl.BlockSpec((1, tm, d), row),
                   pl.BlockSpec((1, N_EXPERTS, tm), lambda b, i: (b, 0, i))],
        out_shape=[jax.ShapeDtypeStruct((bsz, s, d), F32),
                   jax.ShapeDtypeStruct((bsz, s, d), BF16),
                   jax.ShapeDtypeStruct((bsz, N_EXPERTS, s), F32)],
        compiler_params=_params(("parallel", "parallel"), VMEM_LIMIT),
        name="out_proj_norm2_router",
    )(y_lru, y_att, x, gate1, scale2, shift2, g_norm2.reshape(1, d), w_out.astype(BF16),
      w_router.T.astype(BF16))


def _route_kernel(aff_ref, pos_ref, cnt_ref, *, cap, n_tok_chunks):
    aff = aff_ref[0]
    n_e, s = aff.shape
    bits = lax.bitcast_convert_type(aff, jnp.int32)
    capf = float(cap)

    def count(mask):
        return jnp.sum(jnp.where(mask, 1.0, 0.0), axis=-1, keepdims=True)

    tau = jnp.zeros((n_e, 1), jnp.int32)
    for bit in range(30, -1, -1):
        cand = tau | (1 << bit)
        tau = jnp.where(count(bits >= cand) >= capf, cand, tau)
    gt = bits > tau
    eq = bits == tau
    need = capf - count(gt)

    blk = min(256, s // n_tok_chunks)
    r = lax.broadcasted_iota(jnp.int32, (blk, blk), 0)
    cidx = lax.broadcasted_iota(jnp.int32, (blk, blk), 1)
    upper = jnp.where(r < cidx, 1.0, 0.0).astype(BF16)

    def prefix_blocks(mask):
        off = jnp.zeros((n_e, 1), F32)
        pieces, offs = [], []
        for k in range(s // blk):
            mb = jnp.where(mask[:, k * blk:(k + 1) * blk], 1.0, 0.0)
            offs.append(off)
            pieces.append(jnp.dot(mb.astype(BF16), upper, preferred_element_type=F32) + off)
            off = off + jnp.sum(mb, axis=-1, keepdims=True)
        return pieces, offs

    eq_rank, _ = prefix_blocks(eq)
    sel_blocks = []
    for k in range(s // blk):
        sl = slice(k * blk, (k + 1) * blk)
        sel_blocks.append(jnp.logical_or(gt[:, sl], jnp.logical_and(eq[:, sl], eq_rank[k] < need)))
    sel = jnp.concatenate(sel_blocks, axis=1)
    slot, offs = prefix_blocks(sel)
    for k in range(s // blk):
        pos_ref[0, :, k * blk:(k + 1) * blk] = jnp.where(sel_blocks[k], slot[k], -1.0).astype(jnp.int32)

    lane = lax.broadcasted_iota(jnp.int32, (n_e, 128), 1)
    cnt = jnp.zeros((n_e, 128), F32)
    per = (s // n_tok_chunks) // blk
    for j in range(n_tok_chunks):
        cnt = jnp.where(lane == j, offs[j * per], cnt)
    cnt_ref[0] = cnt.astype(jnp.int32)


def _routing(aff, cap):
    bsz, n_e, s = aff.shape
    group = math.gcd(bsz, ROUTE_GROUP)
    rows = group * n_e
    pos, cnt = pl.pallas_call(
        functools.partial(_route_kernel, cap=cap, n_tok_chunks=N_TOK_CHUNKS),
        grid=(bsz // group,),
        in_specs=[pl.BlockSpec((1, rows, s), lambda b: (b, 0, 0))],
        out_specs=[pl.BlockSpec((1, rows, s), lambda b: (b, 0, 0)),
                   pl.BlockSpec((1, rows, 128), lambda b: (b, 0, 0))],
        out_shape=[jax.ShapeDtypeStruct((bsz // group, rows, s), jnp.int32),
                   jax.ShapeDtypeStruct((bsz // group, rows, 128), jnp.int32)],
        compiler_params=_params(("parallel",)),
        name="expert_choice_routing",
    )(aff.reshape(bsz // group, rows, s))
    return pos.reshape(bsz, n_e, s), cnt.reshape(bsz, n_e, 128)


def _moe_kernel(cnt_ref, pos_ref, aff_ref, h2_ref, w1_ref, w3_ref, w2_ref, gate_ref, y_ref, xe_scr, g_scr, *, cap):
    b = pl.program_id(0)
    e = pl.program_id(1)
    n_e = pl.num_programs(1)
    s = h2_ref.shape[1]
    tchunk = s // N_TOK_CHUNKS
    cchunk = cap // N_CAP_CHUNKS
    base = (b * n_e + e) * N_TOK_CHUNKS
    xe_scr[...] = jnp.zeros_like(xe_scr)
    g_scr[...] = jnp.zeros_like(g_scr)

    slot_iota = lax.broadcasted_iota(jnp.int32, (cchunk, tchunk), 0)
    j = jnp.int32(0)
    i = jnp.int32(0)
    for _ in range(N_TOK_CHUNKS + N_CAP_CHUNKS - 1):
        jc = jnp.minimum(j, N_TOK_CHUNKS - 1)
        ic = jnp.minimum(i, N_CAP_CHUNKS - 1)
        c_lo = cnt_ref[base + jc]
        c_hi = jnp.where(jc + 1 < N_TOK_CHUNKS, cnt_ref[base + jnp.minimum(jc + 1, N_TOK_CHUNKS - 1)], cap)
        s_hi = (ic + 1) * cchunk
        valid = jnp.logical_and(jnp.maximum(c_lo, ic * cchunk) < jnp.minimum(c_hi, s_hi),
                                jnp.logical_and(j < N_TOK_CHUNKS, i < N_CAP_CHUNKS))
        slot0 = jnp.where(valid, ic * cchunk, -2 * cap)
        sel = pos_ref[0, 0, pl.ds(jc, 1), :] == slot_iota + slot0
        rows = pl.ds(pl.multiple_of(ic * cchunk, cchunk), cchunk)
        toks = pl.ds(pl.multiple_of(jc * tchunk, tchunk), tchunk)
        xe_scr[rows, :] += jnp.dot(jnp.where(sel, 1.0, 0.0).astype(BF16), h2_ref[0, toks, :],
                                   preferred_element_type=F32)
        g_scr[rows, :] += jnp.sum(jnp.where(sel, aff_ref[0, 0, pl.ds(jc, 1), :], 0.0), axis=-1, keepdims=True)
        j = j + jnp.where(c_hi <= s_hi, 1, 0)
        i = i + jnp.where(s_hi <= c_hi, 1, 0)

    xe = xe_scr[...].astype(BF16)
    a = jnp.dot(xe, w1_ref[0], preferred_element_type=F32)
    gate = jnp.dot(xe, w3_ref[0], preferred_element_type=F32)
    hmid = ((a * _sigmoid(a)) * gate).astype(BF16)
    y = jnp.dot(hmid, w2_ref[0], preferred_element_type=F32) * g_scr[...] * gate_ref[0]
    y_ref[0] = y.astype(BF16)


def _moe(cnt, pos, aff, h2, w1, w3, w2, gate2, cap):
    bsz, s, d = h2.shape
    n_e = w1.shape[0]
    f = w1.shape[2]
    tok_row = lambda b, e, c: (b, e, 0, 0)
    wspec = lambda b, e, c: (e, 0, 0)
    resident = lambda b, e, c: (b, 0, 0)
    grid_spec = pltpu.PrefetchScalarGridSpec(
        num_scalar_prefetch=1,
        grid=(bsz, n_e),
        in_specs=[pl.BlockSpec((1, 1, N_TOK_CHUNKS, s // N_TOK_CHUNKS), tok_row),
                  pl.BlockSpec((1, 1, N_TOK_CHUNKS, s // N_TOK_CHUNKS), tok_row),
                  pl.BlockSpec((1, s, d), resident),
                  pl.BlockSpec((1, d, f), wspec),
                  pl.BlockSpec((1, d, f), wspec),
                  pl.BlockSpec((1, f, d), wspec),
                  pl.BlockSpec((1, 1, d), resident)],
        out_specs=pl.BlockSpec((1, cap, d), lambda b, e, c: (b, e, 0)),
        scratch_shapes=[pltpu.VMEM((cap, d), F32), pltpu.VMEM((cap, 1), F32)],
    )
    return pl.pallas_call(
        functools.partial(_moe_kernel, cap=cap),
        grid_spec=grid_spec,
        out_shape=jax.ShapeDtypeStruct((bsz, n_e * cap, d), BF16),
        compiler_params=_params(("parallel", "parallel"), VMEM_LIMIT),
        name="expert_choice_ffn",
    )(cnt.reshape(-1), pos, aff.reshape(pos.shape), h2, w1, w3, w2, gate2)


def _combine_kernel(cnt_ref, pos_ref, y_ref, x1_ref, out_ref, *, cap):
    b = pl.program_id(0)
    j = pl.program_id(1)
    n_j = pl.num_programs(1)
    n_e = pos_ref.shape[1]
    t = pos_ref.shape[-1]
    win = COMBINE_WINDOW
    slot_iota = lax.broadcasted_iota(jnp.int32, (win, t), 0)

    def slot_range(e):
        at = (b * n_e + e) * n_j
        lo = cnt_ref[at + j]
        hi = jnp.where(j + 1 < n_j, cnt_ref[at + jnp.minimum(j + 1, n_j - 1)], cap)
        return lax.shift_left(lax.shift_right_logical(lo, 4), 4), hi

    def windows(k):
        sels, rows = [], []
        for e in range(n_e):
            start = slot_range(e)[0] + k * win
            w0 = jnp.minimum(start, cap - win)
            slots = slot_iota + w0
            sel = jnp.logical_and(pos_ref[0, e, pl.ds(j, 1), :] == slots, slots >= start)
            sels.append(jnp.where(sel, 1.0, 0.0).astype(BF16))
            rows.append(y_ref[0, pl.ds(pl.multiple_of(e * cap + w0, 16), win), :])
        return lax.dot_general(jnp.concatenate(sels, axis=0), jnp.concatenate(rows, axis=0), TN_DIMS,
                               preferred_element_type=F32)

    out_ref[0] = x1_ref[0] + windows(0)

    n_win = jnp.int32(1)
    for e in range(n_e):
        start, hi = slot_range(e)
        n_win = jnp.maximum(n_win, lax.shift_right_logical(hi - start + (win - 1), COMBINE_WINDOW.bit_length() - 1))

    def more(k, carry):
        out_ref[0] += windows(k)
        return carry

    lax.fori_loop(1, n_win, more, 0)


def _combine(cnt, pos, y, x1, cap):
    bsz, s, d = x1.shape
    n_e, n_j, t = pos.shape[1:]
    grid_spec = pltpu.PrefetchScalarGridSpec(
        num_scalar_prefetch=1,
        grid=(bsz, n_j),
        in_specs=[pl.BlockSpec((1, n_e, n_j, t), lambda b, j, c: (b, 0, 0, 0)),
                  pl.BlockSpec((1, n_e * cap, d), lambda b, j, c: (b, 0, 0), pipeline_mode=pl.Buffered(1)),
                  pl.BlockSpec((1, t, d), lambda b, j, c: (b, j, 0))],
        out_specs=pl.BlockSpec((1, t, d), lambda b, j, c: (b, j, 0)),
    )
    return pl.pallas_call(
        functools.partial(_combine_kernel, cap=cap),
        grid_spec=grid_spec,
        out_shape=jax.ShapeDtypeStruct((bsz, s, d), F32),
        compiler_params=_params(("parallel", "parallel"), VMEM_LIMIT),
        name="expert_combine",
    )(cnt.reshape(-1), pos, y, x1)


def kernel(x, c, w_mod, b_mod, g_norm1, w_in, conv_w, conv_b, lru_w_a, lru_b_a, lru_w_x, lru_b_x,
           lru_lambda, g_q, g_k, lambda_qk, g_attn_out, rel_bias, w_out, g_norm2, w_router, w1, w3, w2):
    bsz, s, d = x.shape
    depth = w_mod.shape[0]
    cap = max(1, EC_FACTOR * s // N_EXPERTS)
    bias_tiles = _bias_tiles(rel_bias, min(ATT_TILE, s))
    for l in range(depth):
        mod = _modulation(c, w_mod[l], b_mod[l])
        shift1, scale1, gate1, shift2, scale2, gate2 = [m.reshape(bsz, 1, d) for m in jnp.split(mod, 6, axis=-1)]
        x_lru, gz, qn, kn, vt = _in_projection(x, scale1, shift1, g_norm1[l], w_in[l], g_q[l], g_k[l],
                                               min(IN_PROJ_ROWS, s))
        y_lru = _rg_lru(x_lru, gz, conv_w[l], conv_b[l], lru_w_a[l], lru_b_a[l], lru_w_x[l], lru_b_x[l],
                        lru_lambda[l])
        lam_init = 0.8 - 0.6 * math.exp(-0.3 * l)
        y_att, (w1b, w3b, w2b) = _diff_attention(qn, kn, vt, bias_tiles, lambda_qk[l], g_attn_out[l], lam_init,
                                                 (w1[l], w3[l], w2[l]))
        x1, h2, aff = _out_projection(y_lru, y_att, x, gate1, scale2, shift2, g_norm2[l], w_out[l],
                                      w_router[l], min(OUT_PROJ_ROWS, s))
        pos, cnt = _routing(aff, cap)
        pos = pos.reshape(bsz, N_EXPERTS, N_TOK_CHUNKS, -1)
        cnt = cnt[:, :, :N_TOK_CHUNKS]
        y = _moe(cnt, pos, aff, h2, w1b, w3b, w2b, gate2, cap)
        x = _combine(cnt, pos, y, x1, cap)
    return x
```

```python
import functools
import math

import jax
import jax.numpy as jnp
from jax import lax
from jax.experimental import pallas as pl
from jax.experimental.pallas import tpu as pltpu

F32 = jnp.float32
BF16 = jnp.bfloat16

D_MODEL = 1024
D_LRU = 512
LRU_BLOCK = 64
LRU_C = 8.0
CONV_W = 4
N_HEADS = 4
HEAD_DV = 128
HEAD_DK = 64
D_ATT = N_HEADS * HEAD_DV
N_BUCKETS = 32
N_EXPERTS = 16
EC_FACTOR = 2
EPS = 1e-6
F32_TINY = 2.0 ** -126
LOG2E = math.log2(math.e)

IN_PROJ_ROWS = 1024
OUT_PROJ_ROWS = 1024
LRU_HALF = 256
SCAN_LANES = 8
ATT_TILE = 256
PV_ONES = 16
ATT_WIDTH = 1
ROUTE_GROUP = 8
N_TOK_CHUNKS = 16
N_CAP_CHUNKS = 4
COMBINE_WINDOW = 64
COMBINE_CHUNKS = 4
VMEM_LIMIT = 56 * 1024 * 1024

NT_DIMS = (((1,), (1,)), ((), ()))
TN_DIMS = (((0,), (0,)), ((), ()))


def _sigmoid(x):
    return 1.0 / (1.0 + jnp.exp(-x))


def _params(sem, vmem=None):
    return pltpu.CompilerParams(dimension_semantics=sem, vmem_limit_bytes=vmem)


def _mod_kernel(c_ref, w_ref, b_ref, o_ref):
    c = c_ref[...]
    o_ref[...] = jnp.dot((c * _sigmoid(c)).astype(BF16), w_ref[...].astype(BF16),
                         preferred_element_type=F32) + b_ref[...]


def _modulation(c, w_mod, b_mod):
    bsz, d = c.shape
    n = w_mod.shape[1]
    return pl.pallas_call(
        _mod_kernel,
        grid=(n // d,),
        in_specs=[pl.BlockSpec((bsz, d), lambda j: (0, 0)),
                  pl.BlockSpec((d, d), lambda j: (0, j)),
                  pl.BlockSpec((1, d), lambda j: (0, j))],
        out_specs=pl.BlockSpec((bsz, d), lambda j: (0, j)),
        out_shape=jax.ShapeDtypeStruct((bsz, n), F32),
        compiler_params=_params(("arbitrary",)),
        name="adaln_mod",
    )(c, w_mod, b_mod.reshape(1, n))


def _bias_kernel(tab_ref, o_ref):
    h = pl.program_id(0)
    t = o_ref.shape[-1]
    key = lax.broadcasted_iota(jnp.int32, (t, t), 0)
    qry = lax.broadcasted_iota(jnp.int32, (t, t), 1)
    half = N_BUCKETS // 2
    max_exact = half // 2
    for d in range(5):
        if d == 0:
            o_ref[0, d] = jnp.full((t, t), tab_ref[half - 1, h] * LOG2E, F32)
        elif d == 4:
            o_ref[0, d] = jnp.full((t, t), tab_ref[N_BUCKETS - 1, h] * LOG2E, F32)
        else:
            rel = (d - 2) * t + key - qry
            n = jnp.abs(rel)
            n2 = n * n
            large = jnp.full((t, t), max_exact, jnp.int32)
            for k in range(1, half - max_exact):
                large = large + jnp.where(n2 >= (max_exact * max_exact) * (2 ** k), 1, 0)
            idx = jnp.where(n < max_exact, n, large) + jnp.where(rel > 0, half, 0)
            val = jnp.zeros((t, t), F32)
            for j in range(N_BUCKETS):
                val = jnp.where(idx == j, tab_ref[j, h] * LOG2E, val)
            o_ref[0, d] = val


def _bias_tiles(rel_bias, t):
    return pl.pallas_call(
        _bias_kernel,
        grid=(N_HEADS,),
        in_specs=[pl.BlockSpec(memory_space=pltpu.SMEM)],
        out_specs=pl.BlockSpec((1, 5, t, t), lambda h: (h, 0, 0, 0)),
        out_shape=jax.ShapeDtypeStruct((N_HEADS, 5, t, t), F32),
        compiler_params=_params(("arbitrary",)),
        name="t5_bias_tiles",
    )(rel_bias)


def _inproj_kernel(x_ref, sc_ref, sh_ref, g1_ref, w_ref, wvt_ref, mseg_ref, gq_ref, gk_ref,
                   xl_ref, gz_ref, q_ref, k_ref, vt_ref):
    x = x_ref[0]
    ms = jnp.mean(x * x, axis=-1, keepdims=True)
    h = (x * lax.rsqrt(ms + EPS) * g1_ref[...]) * (1.0 + sc_ref[0]) + sh_ref[0]
    hb = h.astype(BF16)

    def proj(lo, width):
        return jnp.dot(hb, w_ref[:, lo:lo + width], preferred_element_type=F32)

    def qk_norm(t, g):
        ss = jnp.dot((t * t).astype(BF16), mseg_ref[...], preferred_element_type=F32)
        return t * lax.rsqrt(ss * (1.0 / HEAD_DK) + EPS) * g

    xl_ref[0] = proj(0, D_LRU)
    z = proj(D_LRU, D_LRU)
    cdf = 0.5 * (1.0 + jnp.tanh(math.sqrt(2.0 / math.pi) * (z + 0.044715 * (z * z * z))))
    gz_ref[0] = (z * cdf).astype(BF16)
    q_ref[0] = (qk_norm(proj(2 * D_LRU, D_ATT), gq_ref[...]) * (HEAD_DK ** -0.5 * LOG2E)).astype(BF16)
    k_ref[0] = qk_norm(proj(2 * D_LRU + D_ATT, D_ATT), gk_ref[...]).astype(BF16)
    vt_ref[0] = lax.dot_general(wvt_ref[...], hb, NT_DIMS,
                                preferred_element_type=F32).astype(BF16)


def _in_projection(x, scale1, shift1, g_norm1, w_in, g_q, g_k, tm):
    bsz, s, d = x.shape
    n = w_in.shape[1] - D_ATT
    w_main = w_in[:, :n].astype(BF16)
    w_vt = w_in[:, n:].T.astype(BF16)
    seg = jnp.arange(D_ATT, dtype=jnp.int32) // HEAD_DK
    mseg = (seg[:, None] == seg[None, :]).astype(BF16)
    n_sub = D_ATT // HEAD_DK
    row = lambda b, i: (b, i, 0)
    vec = lambda b, i: (b, 0, 0)
    full = lambda b, i: (0, 0)
    out_block = pl.BlockSpec((1, tm, D_LRU), row)
    return pl.pallas_call(
        _inproj_kernel,
        grid=(bsz, s // tm),
        in_specs=[pl.BlockSpec((1, tm, d), row),
                  pl.BlockSpec((1, 1, d), vec),
                  pl.BlockSpec((1, 1, d), vec),
                  pl.BlockSpec((1, d), full),
                  pl.BlockSpec((d, n), full),
                  pl.BlockSpec((D_ATT, d), full),
                  pl.BlockSpec((D_ATT, D_ATT), full),
                  pl.BlockSpec((1, D_ATT), full),
                  pl.BlockSpec((1, D_ATT), full)],
        out_specs=[out_block] * 4 + [pl.BlockSpec((1, D_ATT, tm), lambda b, i: (b, 0, i))],
        out_shape=[jax.ShapeDtypeStruct((bsz, s, D_LRU), F32)]
                  + [jax.ShapeDtypeStruct((bsz, s, D_LRU), BF16)] * 3
                  + [jax.ShapeDtypeStruct((bsz, D_ATT, s), BF16)],
        compiler_params=_params(("parallel", "parallel"), VMEM_LIMIT),
        name="norm1_in_proj",
    )(x, scale1, shift1, g_norm1.reshape(1, d), w_main, w_vt, mseg,
      jnp.tile(g_q, n_sub).reshape(1, D_ATT), jnp.tile(g_k, n_sub).reshape(1, D_ATT))


def _lru_kernel(x_ref, gz_ref, cw_ref, cb_ref, wa_ref, wx_ref, ba_ref, bx_ref, lam_ref, y_ref,
                xpad, a_f, u_f, a_b, u_b, *, tc):
    s = x_ref.shape[1]
    c = x_ref.shape[2]
    n_chunks = s // tc
    n_slab = c // 128
    pitch = a_f.shape[1] // n_chunks
    zeros8 = jnp.zeros((8, c), F32)
    xpad[0:8, :] = zeros8
    xpad[s + 8:s + 16, :] = zeros8

    def fill(ci, carry):
        t0 = pl.multiple_of(ci * tc, tc)
        xpad[pl.ds(t0 + 8, tc), :] = x_ref[0, pl.ds(t0, tc), :]
        return carry

    lax.fori_loop(0, n_chunks, fill, 0)

    cw = cw_ref[...]
    cb = cb_ref[...]
    decay = []
    for d in range(2):
        lam = lam_ref[d]
        softplus_neg = jnp.maximum(-lam, 0.0) + jnp.log(1.0 + jnp.exp(-jnp.abs(lam)))
        decay.append((-LRU_C * LOG2E) * softplus_neg)
    a_scr = (a_f, a_b)
    u_scr = (u_f, u_b)

    def gates(ci, carry):
        t0 = pl.multiple_of(ci * tc, tc)
        xw = xpad[pl.ds(t0, tc + 16), :]
        xc = (cw[0:1] * pltpu.roll(xw, 2, 0)[8:8 + tc]
              + cw[1:2] * pltpu.roll(xw, 1, 0)[8:8 + tc]
              + cw[2:3] * xw[8:8 + tc]
              + cw[3:4] * pltpu.roll(xw, tc + 15, 0)[8:8 + tc]) + cb
        xcb = xc.astype(BF16)
        for d in range(2):
            r = _sigmoid(jnp.dot(xcb, wa_ref[d], preferred_element_type=F32) + ba_ref[d])
            i = _sigmoid(jnp.dot(xcb, wx_ref[d], preferred_element_type=F32) + bx_ref[d])
            a = jnp.exp2(r * decay[d])
            v = 1.0 - a * a
            u = (v * lax.rsqrt(jnp.maximum(v, F32_TINY))) * (i * xc)
            r0 = pl.multiple_of(ci * pitch, 8)
            for sl in range(n_slab):
                a_scr[d][sl, pl.ds(r0, tc), :] = a[:, sl * 128:(sl + 1) * 128]
                u_scr[d][sl, pl.ds(r0, tc), :] = u[:, sl * 128:(sl + 1) * 128]
        return carry

    lax.fori_loop(0, n_chunks, gates, 0, unroll=2)

    def rows(tt):
        return pl.ds(tt, n_chunks, stride=pitch)

    def step(tt, carry):
        out = []
        for d in range(2):
            t_loc = tt if d == 0 else tc - 1 - tt
            for sl in range(n_slab):
                h, p = carry[len(out)]
                a8 = a_scr[d][sl, rows(t_loc), :]
                h = a8 * h + u_scr[d][sl, rows(t_loc), :]
                p = a8 * p
                u_scr[d][sl, rows(t_loc), :] = h
                a_scr[d][sl, rows(t_loc), :] = p
                out.append((h, p))
        return tuple(out)

    init = (jnp.zeros((n_chunks, 128), F32), jnp.ones((n_chunks, 128), F32))
    lax.fori_loop(0, tc, step, (init,) * (2 * n_slab), unroll=4)

    for sl in range(n_slab):
        lanes = slice(sl * 128, (sl + 1) * 128)
        h_end, p_end = u_f[sl, rows(tc - 1), :], a_f[sl, rows(tc - 1), :]
        h_beg, p_beg = u_b[sl, rows(0), :], a_b[sl, rows(0), :]
        carry_f = [jnp.zeros((1, 128), F32)]
        for r in range(1, n_chunks):
            carry_f.append(p_end[r - 1:r] * carry_f[-1] + h_end[r - 1:r])
        carry_b = [jnp.zeros((1, 128), F32)]
        for r in range(n_chunks - 2, -1, -1):
            carry_b.insert(0, p_beg[r + 1:r + 2] * carry_b[0] + h_beg[r + 1:r + 2])
        for r in range(n_chunks):
            blk = slice(r * pitch, r * pitch + tc)
            hsum = (u_f[sl, blk, :] + a_f[sl, blk, :] * carry_f[r]
                    + u_b[sl, blk, :] + a_b[sl, blk, :] * carry_b[r])
            y_ref[0, r * tc:(r + 1) * tc, lanes] = (hsum * gz_ref[0, r * tc:(r + 1) * tc, lanes].astype(F32)).astype(BF16)


def _block_diag(w, half):
    n_dir, n_blocks, blk, _ = w.shape
    per = half // blk
    n_half = n_blocks // per
    w = w.reshape(n_dir, n_half, per, blk, blk)
    eye = jnp.eye(per, dtype=w.dtype)
    out = w[:, :, :, :, None, :] * eye[None, None, :, None, :, None]
    return out.reshape(n_dir, n_half, half, half)


def _rg_lru(x_lru, gz, conv_w, conv_b, w_a, b_a, w_x, b_x, lam):
    bsz, s, c = x_lru.shape
    half = LRU_HALF
    tc = s // SCAN_LANES
    pitch = tc + 8 if (tc // 8) % 2 == 0 else tc + 16
    n_half = c // half
    wa = _block_diag(w_a, half).astype(BF16)
    wx = _block_diag(w_x, half).astype(BF16)
    seq = lambda b, p: (b, 0, p)
    chan = lambda b, p: (0, p)
    dirchan = lambda b, p: (0, 0, p)
    blk = lambda b, p: (0, p, 0, 0)
    return pl.pallas_call(
        functools.partial(_lru_kernel, tc=tc),
        grid=(bsz, n_half),
        in_specs=[pl.BlockSpec((1, s, half), seq),
                  pl.BlockSpec((1, s, half), seq),
                  pl.BlockSpec((CONV_W, half), chan),
                  pl.BlockSpec((1, half), chan),
                  pl.BlockSpec((2, None, half, half), blk),
                  pl.BlockSpec((2, None, half, half), blk),
                  pl.BlockSpec((2, 1, half), dirchan),
                  pl.BlockSpec((2, 1, half), dirchan),
                  pl.BlockSpec((2, 1, half), dirchan)],
        out_specs=pl.BlockSpec((1, s, half), seq),
        out_shape=jax.ShapeDtypeStruct((bsz, s, c), BF16),
        scratch_shapes=[pltpu.VMEM((s + 16, half), F32)]
                       + [pltpu.VMEM((half // 128, SCAN_LANES * pitch, 128), F32)] * 4,
        compiler_params=_params(("parallel", "parallel"), VMEM_LIMIT),
        name="rg_lru",
    )(x_lru, gz, conv_w.reshape(CONV_W, c), conv_b.reshape(1, c), wa, wx,
      b_a.reshape(2, 1, c), b_x.reshape(2, 1, c), lam.reshape(2, 1, c))


def _attn_kernel(q_ref, k_ref, vt_ref, bias_ref, lq_ref, go_ref, w1_ref, w3_ref, w2_ref,
                 o_ref, w1b_ref, w3b_ref, w2b_ref,
                 s_even, s_odd, m_even, m_odd, o_even, o_odd, p_scr, *, lam_init):
    for w_ref, wb_ref in ((w1_ref, w1b_ref), (w3_ref, w3b_ref), (w2_ref, w2b_ref)):
        wb_ref[...] = w_ref[...].astype(BF16)

    t = bias_ref.shape[-1]
    width = s_even.shape[0]
    n_k = k_ref.shape[1] // t
    n_q = q_ref.shape[1] // t
    n_steps = n_q // width
    subs = [(w, u) for w in range(width) for u in range(2)]
    chunk = 2
    pv_chunk = 2
    n_c = n_k // chunk

    s_odd[...] = jnp.zeros_like(s_odd)
    m_odd[...] = jnp.zeros_like(m_odd)
    o_odd[...] = jnp.ones_like(o_odd)
    even = dict(s=s_even, m=m_even, o=o_even)
    odd = dict(s=s_odd, m=m_odd, o=o_odd)
    ones_rows = jnp.ones((PV_ONES, pv_chunk * t), BF16)

    def fold(x, op):
        parts = [x[r * 8:(r + 1) * 8, :] for r in range(t // 8)]
        acc = parts[:2]
        for r in range(2, len(parts)):
            acc[r % 2] = op(acc[r % 2], parts[r])
        return op(acc[0], acc[1])

    def bias_tile(j, tile):
        return bias_ref[0, jnp.clip(j - tile, -2, 2) + 2]

    def finish(prev, w, tile):
        outs = [prev['o'][w, u, :HEAD_DV, :] / prev['o'][w, u, HEAD_DV:HEAD_DV + 1, :] for u in range(2)]
        lq = lq_ref[...]
        lam = (jnp.exp(jnp.sum(lq[0:1] * lq[1:2], axis=-1, keepdims=True))
               - jnp.exp(jnp.sum(lq[2:3] * lq[3:4], axis=-1, keepdims=True)) + lam_init)
        o = (outs[0] - lam * outs[1]).T
        ms = jnp.mean(o * o, axis=-1, keepdims=True)
        o_ref[0, pl.ds(pl.multiple_of(tile * t, t), t), :] = (
            (o * lax.rsqrt(ms + EPS) * go_ref[...]) * (1.0 - lam_init)).astype(BF16)

    def step(i, cur, prev, scores=True, softmax=True):
        tiles_a = [i * width + w for w in range(width)]
        for w in range(width):
            finish(prev, w, jnp.maximum((i - 2) * width + w, 0))
        s_cur, s_prev = cur['s'], prev['s']
        m_b = {wu: jnp.max(prev['m'][wu], axis=0, keepdims=True) for wu in subs} if softmax else {}
        m_acc = {wu: jnp.full((8, t), -jnp.inf, F32) for wu in subs}
        q_sub = {}
        if scores:
            for w in range(width):
                q = q_ref[0, pl.ds(pl.multiple_of(tiles_a[w] * t, t), t), :]
                lane = lax.broadcasted_iota(jnp.int32, q.shape, 1)
                zero = jnp.zeros_like(q)
                q_sub[w, 0] = jnp.where(lane < HEAD_DK, q, zero)
                q_sub[w, 1] = jnp.where(lane >= HEAD_DK, q, zero)
        for c in range(n_c):
            rows = slice(c * chunk * t, (c + 1) * chunk * t)
            for w, u in (subs if scores else []):
                sc_all = lax.dot_general(k_ref[0, rows, :], q_sub[w, u], NT_DIMS, preferred_element_type=F32)
                for jj in range(chunk):
                    j = c * chunk + jj
                    sc = sc_all[jj * t:(jj + 1) * t, :] + bias_tile(j, tiles_a[w])
                    s_cur[w, u, j] = sc
                    m_acc[w, u] = jnp.maximum(m_acc[w, u], fold(sc, jnp.maximum))
            for w, u in (subs if softmax else []):
                for jj in range(chunk):
                    j = c * chunk + jj
                    p = jnp.exp2(s_prev[w, u, j] - m_b[w, u])
                    p_scr[w, u, j * t:(j + 1) * t, :] = p.astype(BF16)
                if ((c + 1) * chunk) % pv_chunk == 0:
                    keys = slice(((c + 1) * chunk - pv_chunk) * t, (c + 1) * chunk * t)
                    part = jnp.dot(jnp.concatenate([vt_ref[0, :, keys], ones_rows], axis=0), p_scr[w, u, keys, :],
                                   preferred_element_type=F32)
                    if (c + 1) * chunk == pv_chunk:
                        cur['o'][w, u] = part
                    else:
                        cur['o'][w, u] += part

        if scores:
            for w, u in subs:
                cur['m'][w, u] = m_acc[w, u]

    def pair(pi, carry):
        @pl.when(pi >= 0)
        def _():
            step(2 * pi, even, odd)

        @pl.when(pi < n_steps)
        def _():
            step(2 * pi + 1, odd, even)

        return carry

    lax.fori_loop(0, n_steps // 2, pair, 0)
    step(n_steps, even, odd, scores=False)
    for w in range(width):
        finish(even, w, n_q - width + w)


def _diff_attention(qn, kn, vt, bias_tiles, lambda_qk, g_o, lam_init, expert_w):
    bsz, s, _ = qn.shape
    w_rows = expert_w[0].shape[0] * expert_w[0].shape[1]
    w_cols = expert_w[0].shape[2]
    assert all(w.shape == expert_w[0].shape for w in expert_w) and w_rows % (bsz * N_HEADS * 16) == 0
    w_blk = pl.BlockSpec((w_rows // (bsz * N_HEADS), w_cols), lambda b, h: (b * N_HEADS + h, 0))
    t = bias_tiles.shape[-1]
    n_k = s // t
    width = ATT_WIDTH
    assert (s // t) % (2 * width) == 0, "the query-tile pipeline advances two groups of tiles per loop trip"
    scores = pltpu.VMEM((width, 2, n_k, t, t), F32)
    maxima = pltpu.VMEM((width, 2, 8, t), F32)
    pv_acc = pltpu.VMEM((width, 2, HEAD_DV + PV_ONES, t), F32)
    seq = lambda b, h: (b, 0, h)
    outs = pl.pallas_call(
        functools.partial(_attn_kernel, lam_init=lam_init),
        grid=(bsz, N_HEADS),
        in_specs=[pl.BlockSpec((1, s, HEAD_DV), seq),
                  pl.BlockSpec((1, s, HEAD_DV), seq),
                  pl.BlockSpec((1, HEAD_DV, s), lambda b, h: (b, h, 0)),
                  pl.BlockSpec((1, 5, t, t), lambda b, h: (h, 0, 0, 0)),
                  pl.BlockSpec((4, HEAD_DK), lambda b, h: (0, 0)),
                  pl.BlockSpec((1, HEAD_DV), lambda b, h: (0, 0)),
                  w_blk, w_blk, w_blk],
        out_specs=[pl.BlockSpec((1, s, HEAD_DV), seq), w_blk, w_blk, w_blk],
        out_shape=[jax.ShapeDtypeStruct((bsz, s, D_ATT), BF16)]
                  + [jax.ShapeDtypeStruct((w_rows, w_cols), BF16)] * 3,
        scratch_shapes=[scores, scores, maxima, maxima, pv_acc, pv_acc, pltpu.VMEM((width, 2, s, t), BF16)],
        compiler_params=_params(("parallel", "parallel"), VMEM_LIMIT),
        name="diff_attention",
    )(qn, kn, vt, bias_tiles, lambda_qk, g_o.reshape(1, HEAD_DV), *[w.reshape(w_rows, w_cols) for w in expert_w])
    return outs[0], [wb.reshape(expert_w[0].shape) for wb in outs[1:]]


def _outproj_kernel(yl_ref, ya_ref, x_ref, gate_ref, sc_ref, sh_ref, g2_ref, wo_ref, wr_ref,
                    x1_ref, h2_ref, aff_ref):
    mix = (jnp.dot(yl_ref[0], wo_ref[0:D_LRU, :], preferred_element_type=F32)
           + jnp.dot(ya_ref[0], wo_ref[D_LRU:D_LRU + D_ATT, :], preferred_element_type=F32))
    x1 = x_ref[0] + gate_ref[0] * mix
    x1_ref[0] = x1
    ms = jnp.mean(x1 * x1, axis=-1, keepdims=True)
    h2 = (x1 * lax.rsqrt(ms + EPS) * g2_ref[...]) * (1.0 + sc_ref[0]) + sh_ref[0]
    h2b = h2.astype(BF16)
    h2_ref[0] = h2b
    logits = lax.dot_general(wr_ref[...], h2b, NT_DIMS, preferred_element_type=F32)
    ex = jnp.exp(logits - jnp.max(logits, axis=0, keepdims=True))
    aff_ref[0] = ex / jnp.sum(ex, axis=0, keepdims=True)


def _out_projection(y_lru, y_att, x, gate1, scale2, shift2, g_norm2, w_out, w_router, tm):
    bsz, s, d = x.shape
    row = lambda b, i: (b, i, 0)
    vec = lambda b, i: (b, 0, 0)
    full = lambda b, i: (0, 0)
    return pl.pallas_call(
        _outproj_kernel,
        grid=(bsz, s // tm),
        in_specs=[pl.BlockSpec((1, tm, D_LRU), row),
                  pl.BlockSpec((1, tm, D_ATT), row),
                  pl.BlockSpec((1, tm, d), row),
                  pl.BlockSpec((1, 1, d), vec),
                  pl.BlockSpec((1, 1, d), vec),
                  pl.BlockSpec((1, 1, d), vec),
                  pl.BlockSpec((1, d), full),
                  pl.BlockSpec((D_LRU + D_ATT, d), full),
                  pl.BlockSpec((N_EXPERTS, d), full)],
        out_specs=[pl.BlockSpec((1, tm, d), row),
                   pl.BlockSpec((1, tm, d), row),
                   pl.BlockSpec((1, N_EXPERTS, tm), lambda b, i: (b, 0, i))],
        out_shape=[jax.ShapeDtypeStruct((bsz, s, d), F32),
                   jax.ShapeDtypeStruct((bsz, s, d), BF16),
                   jax.ShapeDtypeStruct((bsz, N_EXPERTS, s), F32)],
        compiler_params=_params(("parallel", "parallel"), VMEM_LIMIT),
        name="out_proj_norm2_router",
    )(y_lru, y_att, x, gate1, scale2, shift2, g_norm2.reshape(1, d), w_out.astype(BF16),
      w_router.T.astype(BF16))


def _route_kernel(aff_ref, pos_ref, cnt_ref, *, cap, n_tok_chunks):
    aff = aff_ref[0]
    n_e, s = aff.shape
    bits = lax.bitcast_convert_type(aff, jnp.int32)
    capf = float(cap)

    def count(mask):
        return jnp.sum(jnp.where(mask, 1.0, 0.0), axis=-1, keepdims=True)

    tau = jnp.zeros((n_e, 1), jnp.int32)
    for bit in range(30, -1, -1):
        cand = tau | (1 << bit)
        tau = jnp.where(count(bits >= cand) >= capf, cand, tau)
    gt = bits > tau
    eq = bits == tau
    need = capf - count(gt)

    blk = min(256, s // n_tok_chunks)
    r = lax.broadcasted_iota(jnp.int32, (blk, blk), 0)
    cidx = lax.broadcasted_iota(jnp.int32, (blk, blk), 1)
    upper = jnp.where(r < cidx, 1.0, 0.0).astype(BF16)

    def prefix_blocks(mask):
        off = jnp.zeros((n_e, 1), F32)
        pieces, offs = [], []
        for k in range(s // blk):
            mb = jnp.where(mask[:, k * blk:(k + 1) * blk], 1.0, 0.0)
            offs.append(off)
            pieces.append(jnp.dot(mb.astype(BF16), upper, preferred_element_type=F32) + off)
            off = off + jnp.sum(mb, axis=-1, keepdims=True)
        return pieces, offs

    eq_rank, _ = prefix_blocks(eq)
    sel_blocks = []
    for k in range(s // blk):
        sl = slice(k * blk, (k + 1) * blk)
        sel_blocks.append(jnp.logical_or(gt[:, sl], jnp.logical_and(eq[:, sl], eq_rank[k] < need)))
    sel = jnp.concatenate(sel_blocks, axis=1)
    slot, offs = prefix_blocks(sel)
    for k in range(s // blk):
        pos_ref[0, :, k * blk:(k + 1) * blk] = jnp.where(sel_blocks[k], slot[k], -1.0).astype(jnp.int32)

    lane = lax.broadcasted_iota(jnp.int32, (n_e, 128), 1)
    cnt = jnp.zeros((n_e, 128), F32)
    per = (s // n_tok_chunks) // blk
    for j in range(n_tok_chunks):
        cnt = jnp.where(lane == j, offs[j * per], cnt)
    cnt_ref[0] = cnt.astype(jnp.int32)


def _routing(aff, cap):
    bsz, n_e, s = aff.shape
    group = math.gcd(bsz, ROUTE_GROUP)
    rows = group * n_e
    pos, cnt = pl.pallas_call(
        functools.partial(_route_kernel, cap=cap, n_tok_chunks=N_TOK_CHUNKS),
        grid=(bsz // group,),
        in_specs=[pl.BlockSpec((1, rows, s), lambda b: (b, 0, 0))],
        out_specs=[pl.BlockSpec((1, rows, s), lambda b: (b, 0, 0)),
                   pl.BlockSpec((1, rows, 128), lambda b: (b, 0, 0))],
        out_shape=[jax.ShapeDtypeStruct((bsz // group, rows, s), jnp.int32),
                   jax.ShapeDtypeStruct((bsz // group, rows, 128), jnp.int32)],
        compiler_params=_params(("parallel",)),
        name="expert_choice_routing",
    )(aff.reshape(bsz // group, rows, s))
    return pos.reshape(bsz, n_e, s), cnt.reshape(bsz, n_e, 128)


def _moe_kernel(cnt_ref, pos_ref, aff_ref, h2_ref, w1_ref, w3_ref, w2_ref, gate_ref, y_ref, xe_scr, g_scr, *, cap):
    b = pl.program_id(0)
    e = pl.program_id(1)
    n_e = pl.num_programs(1)
    s = h2_ref.shape[1]
    tchunk = s // N_TOK_CHUNKS
    cchunk = cap // N_CAP_CHUNKS
    base = (b * n_e + e) * N_TOK_CHUNKS
    xe_scr[...] = jnp.zeros_like(xe_scr)
    g_scr[...] = jnp.zeros_like(g_scr)

    slot_iota = lax.broadcasted_iota(jnp.int32, (cchunk, tchunk), 0)
    j = jnp.int32(0)
    i = jnp.int32(0)
    for _ in range(N_TOK_CHUNKS + N_CAP_CHUNKS - 1):
        jc = jnp.minimum(j, N_TOK_CHUNKS - 1)
        ic = jnp.minimum(i, N_CAP_CHUNKS - 1)
        c_lo = cnt_ref[base + jc]
        c_hi = jnp.where(jc + 1 < N_TOK_CHUNKS, cnt_ref[base + jnp.minimum(jc + 1, N_TOK_CHUNKS - 1)], cap)
        s_hi = (ic + 1) * cchunk
        valid = jnp.logical_and(jnp.maximum(c_lo, ic * cchunk) < jnp.minimum(c_hi, s_hi),
                                jnp.logical_and(j < N_TOK_CHUNKS, i < N_CAP_CHUNKS))
        slot0 = jnp.where(valid, ic * cchunk, -2 * cap)
        sel = pos_ref[0, 0, pl.ds(jc, 1), :] == slot_iota + slot0
        rows = pl.ds(pl.multiple_of(ic * cchunk, cchunk), cchunk)
        toks = pl.ds(pl.multiple_of(jc * tchunk, tchunk), tchunk)
        xe_scr[rows, :] += jnp.dot(jnp.where(sel, 1.0, 0.0).astype(BF16), h2_ref[0, toks, :],
                                   preferred_element_type=F32)
        g_scr[rows, :] += jnp.sum(jnp.where(sel, aff_ref[0, 0, pl.ds(jc, 1), :], 0.0), axis=-1, keepdims=True)
        j = j + jnp.where(c_hi <= s_hi, 1, 0)
        i = i + jnp.where(s_hi <= c_hi, 1, 0)

    xe = xe_scr[...].astype(BF16)
    a = jnp.dot(xe, w1_ref[0], preferred_element_type=F32)
    gate = jnp.dot(xe, w3_ref[0], preferred_element_type=F32)
    hmid = ((a * _sigmoid(a)) * gate).astype(BF16)
    y = jnp.dot(hmid, w2_ref[0], preferred_element_type=F32) * g_scr[...] * gate_ref[0]
    y_ref[0] = y.astype(BF16)


def _moe(cnt, pos, aff, h2, w1, w3, w2, gate2, cap):
    bsz, s, d = h2.shape
    n_e = w1.shape[0]
    f = w1.shape[2]
    tok_row = lambda b, e, c: (b, e, 0, 0)
    wspec = lambda b, e, c: (e, 0, 0)
    resident = lambda b, e, c: (b, 0, 0)
    grid_spec = pltpu.PrefetchScalarGridSpec(
        num_scalar_prefetch=1,
        grid=(bsz, n_e),
        in_specs=[pl.BlockSpec((1, 1, N_TOK_CHUNKS, s // N_TOK_CHUNKS), tok_row),
                  pl.BlockSpec((1, 1, N_TOK_CHUNKS, s // N_TOK_CHUNKS), tok_row),
                  pl.BlockSpec((1, s, d), resident),
                  pl.BlockSpec((1, d, f), wspec),
                  pl.BlockSpec((1, d, f), wspec),
                  pl.BlockSpec((1, f, d), wspec),
                  pl.BlockSpec((1, 1, d), resident)],
        out_specs=pl.BlockSpec((1, cap, d), lambda b, e, c: (b, e, 0)),
        scratch_shapes=[pltpu.VMEM((cap, d), F32), pltpu.VMEM((cap, 1), F32)],
    )
    return pl.pallas_call(
        functools.partial(_moe_kernel, cap=cap),
        grid_spec=grid_spec,
        out_shape=jax.ShapeDtypeStruct((bsz, n_e * cap, d), BF16),
        compiler_params=_params(("parallel", "parallel"), VMEM_LIMIT),
        name="expert_choice_ffn",
    )(cnt.reshape(-1), pos, aff.reshape(pos.shape), h2, w1, w3, w2, gate2)


def _combine_kernel(cnt_ref, pos_ref, y_ref, x1_ref, out_ref, *, cap):
    b = pl.program_id(0)
    n_e, n_j, t = pos_ref.shape[1:]
    win = COMBINE_WINDOW
    slot_iota = lax.broadcasted_iota(jnp.int32, (win, t), 0)

    def chunk(jj, carry):
        j = pl.program_id(1) * COMBINE_CHUNKS + jj
        toks = pl.ds(pl.multiple_of(jj * t, t), t)

        def slot_range(e):
            at = (b * n_e + e) * n_j
            lo = cnt_ref[at + j]
            hi = jnp.where(j + 1 < n_j, cnt_ref[at + jnp.minimum(j + 1, n_j - 1)], cap)
            return lax.shift_left(lax.shift_right_logical(lo, 4), 4), hi

        def windows(k):
            sels, rows = [], []
            for e in range(n_e):
                start = slot_range(e)[0] + k * win
                w0 = jnp.minimum(start, cap - win)
                slots = slot_iota + w0
                sel = jnp.logical_and(pos_ref[0, e, pl.ds(j, 1), :] == slots, slots >= start)
                sels.append(jnp.where(sel, 1.0, 0.0).astype(BF16))
                rows.append(y_ref[0, pl.ds(pl.multiple_of(e * cap + w0, 16), win), :])
            return lax.dot_general(jnp.concatenate(sels, axis=0), jnp.concatenate(rows, axis=0), TN_DIMS,
                                   preferred_element_type=F32)

        out_ref[0, toks, :] = x1_ref[0, toks, :] + windows(0)

        n_win = jnp.int32(1)
        for e in range(n_e):
            start, hi = slot_range(e)
            n_win = jnp.maximum(n_win, lax.shift_right_logical(hi - start + (win - 1),
                                                               COMBINE_WINDOW.bit_length() - 1))

        def more(k, inner):
            out_ref[0, toks, :] += windows(k)
            return inner

        lax.fori_loop(1, n_win, more, 0)
        return carry

    lax.fori_loop(0, COMBINE_CHUNKS, chunk, 0)


def _combine(cnt, pos, y, x1, cap):
    bsz, s, d = x1.shape
    n_e, n_j, t = pos.shape[1:]
    grid_spec = pltpu.PrefetchScalarGridSpec(
        num_scalar_prefetch=1,
        grid=(bsz, n_j // COMBINE_CHUNKS),
        in_specs=[pl.BlockSpec((1, n_e, n_j, t), lambda b, j, c: (b, 0, 0, 0)),
                  pl.BlockSpec((1, n_e * cap, d), lambda b, j, c: (b, 0, 0)),
                  pl.BlockSpec((1, COMBINE_CHUNKS * t, d), lambda b, j, c: (b, j, 0))],
        out_specs=pl.BlockSpec((1, COMBINE_CHUNKS * t, d), lambda b, j, c: (b, j, 0)),
    )
    return pl.pallas_call(
        functools.partial(_combine_kernel, cap=cap),
        grid_spec=grid_spec,
        out_shape=jax.ShapeDtypeStruct((bsz, s, d), F32),
        compiler_params=_params(("parallel", "parallel"), VMEM_LIMIT),
        name="expert_combine",
    )(cnt.reshape(-1), pos, y, x1)


def kernel(x, c, w_mod, b_mod, g_norm1, w_in, conv_w, conv_b, lru_w_a, lru_b_a, lru_w_x, lru_b_x,
           lru_lambda, g_q, g_k, lambda_qk, g_attn_out, rel_bias, w_out, g_norm2, w_router, w1, w3, w2):
    bsz, s, d = x.shape
    depth = w_mod.shape[0]
    cap = max(1, EC_FACTOR * s // N_EXPERTS)
    bias_tiles = _bias_tiles(rel_bias, min(ATT_TILE, s))
    for l in range(depth):
        mod = _modulation(c, w_mod[l], b_mod[l])
        shift1, scale1, gate1, shift2, scale2, gate2 = [m.reshape(bsz, 1, d) for m in jnp.split(mod, 6, axis=-1)]
        x_lru, gz, qn, kn, vt = _in_projection(x, scale1, shift1, g_norm1[l], w_in[l], g_q[l], g_k[l],
                                               min(IN_PROJ_ROWS, s))
        y_lru = _rg_lru(x_lru, gz, conv_w[l], conv_b[l], lru_w_a[l], lru_b_a[l], lru_w_x[l], lru_b_x[l],
                        lru_lambda[l])
        lam_init = 0.8 - 0.6 * math.exp(-0.3 * l)
        y_att, (w1b, w3b, w2b) = _diff_attention(qn, kn, vt, bias_tiles, lambda_qk[l], g_attn_out[l], lam_init,
                                                 (w1[l], w3[l], w2[l]))
        x1, h2, aff = _out_projection(y_lru, y_att, x, gate1, scale2, shift2, g_norm2[l], w_out[l],
                                      w_router[l], min(OUT_PROJ_ROWS, s))
        pos, cnt = _routing(aff, cap)
        pos = pos.reshape(bsz, N_EXPERTS, N_TOK_CHUNKS, -1)
        cnt = cnt[:, :, :N_TOK_CHUNKS]
        y = _moe(cnt, pos, aff, h2, w1b, w3b, w2b, gate2, cap)
        x = _combine(cnt, pos, y, x1, cap)
    return x
```

```python
import functools
import math

import jax
import jax.numpy as jnp
from jax import lax
from jax.experimental import pallas as pl
from jax.experimental.pallas import tpu as pltpu

F32 = jnp.float32
BF16 = jnp.bfloat16

D_MODEL = 1024
D_LRU = 512
LRU_BLOCK = 64
LRU_C = 8.0
CONV_W = 4
N_HEADS = 4
HEAD_DV = 128
HEAD_DK = 64
D_ATT = N_HEADS * HEAD_DV
N_BUCKETS = 32
N_EXPERTS = 16
EC_FACTOR = 2
EPS = 1e-6
F32_TINY = 2.0 ** -126
LOG2E = math.log2(math.e)

IN_PROJ_ROWS = 1024
OUT_PROJ_ROWS = 1024
LRU_HALF = 256
SCAN_LANES = 8
ATT_TILE = 256
PV_ONES = 16
ATT_WIDTH = 1
ROUTE_GROUP = 8
N_TOK_CHUNKS = 16
N_CAP_CHUNKS = 4
MOE_WINDOW = 64
MOE_CHUNKS = 4
VMEM_LIMIT = 56 * 1024 * 1024

NT_DIMS = (((1,), (1,)), ((), ()))
TN_DIMS = (((0,), (0,)), ((), ()))


def _sigmoid(x):
    return 1.0 / (1.0 + jnp.exp(-x))


def _params(sem, vmem=None):
    return pltpu.CompilerParams(dimension_semantics=sem, vmem_limit_bytes=vmem)


def _mod_kernel(c_ref, w_ref, b_ref, o_ref):
    c = c_ref[...]
    o_ref[...] = jnp.dot((c * _sigmoid(c)).astype(BF16), w_ref[...].astype(BF16),
                         preferred_element_type=F32) + b_ref[...]


def _modulation(c, w_mod, b_mod):
    bsz, d = c.shape
    n = w_mod.shape[1]
    return pl.pallas_call(
        _mod_kernel,
        grid=(n // d,),
        in_specs=[pl.BlockSpec((bsz, d), lambda j: (0, 0)),
                  pl.BlockSpec((d, d), lambda j: (0, j)),
                  pl.BlockSpec((1, d), lambda j: (0, j))],
        out_specs=pl.BlockSpec((bsz, d), lambda j: (0, j)),
        out_shape=jax.ShapeDtypeStruct((bsz, n), F32),
        compiler_params=_params(("arbitrary",)),
        name="adaln_mod",
    )(c, w_mod, b_mod.reshape(1, n))


def _bias_kernel(tab_ref, o_ref):
    h = pl.program_id(0)
    t = o_ref.shape[-1]
    key = lax.broadcasted_iota(jnp.int32, (t, t), 0)
    qry = lax.broadcasted_iota(jnp.int32, (t, t), 1)
    half = N_BUCKETS // 2
    max_exact = half // 2
    for d in range(5):
        if d == 0:
            o_ref[0, d] = jnp.full((t, t), tab_ref[half - 1, h] * LOG2E, F32)
        elif d == 4:
            o_ref[0, d] = jnp.full((t, t), tab_ref[N_BUCKETS - 1, h] * LOG2E, F32)
        else:
            rel = (d - 2) * t + key - qry
            n = jnp.abs(rel)
            n2 = n * n
            large = jnp.full((t, t), max_exact, jnp.int32)
            for k in range(1, half - max_exact):
                large = large + jnp.where(n2 >= (max_exact * max_exact) * (2 ** k), 1, 0)
            idx = jnp.where(n < max_exact, n, large) + jnp.where(rel > 0, half, 0)
            val = jnp.zeros((t, t), F32)
            for j in range(N_BUCKETS):
                val = jnp.where(idx == j, tab_ref[j, h] * LOG2E, val)
            o_ref[0, d] = val


def _bias_tiles(rel_bias, t):
    return pl.pallas_call(
        _bias_kernel,
        grid=(N_HEADS,),
        in_specs=[pl.BlockSpec(memory_space=pltpu.SMEM)],
        out_specs=pl.BlockSpec((1, 5, t, t), lambda h: (h, 0, 0, 0)),
        out_shape=jax.ShapeDtypeStruct((N_HEADS, 5, t, t), F32),
        compiler_params=_params(("arbitrary",)),
        name="t5_bias_tiles",
    )(rel_bias)


def _inproj_kernel(x_ref, sc_ref, sh_ref, g1_ref, w_ref, wvt_ref, mseg_ref, gq_ref, gk_ref,
                   xl_ref, gz_ref, q_ref, k_ref, vt_ref):
    x = x_ref[0]
    ms = jnp.mean(x * x, axis=-1, keepdims=True)
    h = (x * lax.rsqrt(ms + EPS) * g1_ref[...]) * (1.0 + sc_ref[0]) + sh_ref[0]
    hb = h.astype(BF16)

    def proj(lo, width):
        return jnp.dot(hb, w_ref[:, lo:lo + width], preferred_element_type=F32)

    def qk_norm(t, g):
        ss = jnp.dot((t * t).astype(BF16), mseg_ref[...], preferred_element_type=F32)
        return t * lax.rsqrt(ss * (1.0 / HEAD_DK) + EPS) * g

    xl_ref[0] = proj(0, D_LRU)
    z = proj(D_LRU, D_LRU)
    cdf = 0.5 * (1.0 + jnp.tanh(math.sqrt(2.0 / math.pi) * (z + 0.044715 * (z * z * z))))
    gz_ref[0] = (z * cdf).astype(BF16)
    q_ref[0] = (qk_norm(proj(2 * D_LRU, D_ATT), gq_ref[...]) * (HEAD_DK ** -0.5 * LOG2E)).astype(BF16)
    k_ref[0] = qk_norm(proj(2 * D_LRU + D_ATT, D_ATT), gk_ref[...]).astype(BF16)
    vt_ref[0] = lax.dot_general(wvt_ref[...], hb, NT_DIMS,
                                preferred_element_type=F32).astype(BF16)


def _in_projection(x, scale1, shift1, g_norm1, w_in, g_q, g_k, tm):
    bsz, s, d = x.shape
    n = w_in.shape[1] - D_ATT
    w_main = w_in[:, :n].astype(BF16)
    w_vt = w_in[:, n:].T.astype(BF16)
    seg = jnp.arange(D_ATT, dtype=jnp.int32) // HEAD_DK
    mseg = (seg[:, None] == seg[None, :]).astype(BF16)
    n_sub = D_ATT // HEAD_DK
    row = lambda b, i: (b, i, 0)
    vec = lambda b, i: (b, 0, 0)
    full = lambda b, i: (0, 0)
    out_block = pl.BlockSpec((1, tm, D_LRU), row)
    return pl.pallas_call(
        _inproj_kernel,
        grid=(bsz, s // tm),
        in_specs=[pl.BlockSpec((1, tm, d), row),
                  pl.BlockSpec((1, 1, d), vec),
                  pl.BlockSpec((1, 1, d), vec),
                  pl.BlockSpec((1, d), full),
                  pl.BlockSpec((d, n), full),
                  pl.BlockSpec((D_ATT, d), full),
                  pl.BlockSpec((D_ATT, D_ATT), full),
                  pl.BlockSpec((1, D_ATT), full),
                  pl.BlockSpec((1, D_ATT), full)],
        out_specs=[out_block] * 4 + [pl.BlockSpec((1, D_ATT, tm), lambda b, i: (b, 0, i))],
        out_shape=[jax.ShapeDtypeStruct((bsz, s, D_LRU), F32)]
                  + [jax.ShapeDtypeStruct((bsz, s, D_LRU), BF16)] * 3
                  + [jax.ShapeDtypeStruct((bsz, D_ATT, s), BF16)],
        compiler_params=_params(("parallel", "parallel"), VMEM_LIMIT),
        name="norm1_in_proj",
    )(x, scale1, shift1, g_norm1.reshape(1, d), w_main, w_vt, mseg,
      jnp.tile(g_q, n_sub).reshape(1, D_ATT), jnp.tile(g_k, n_sub).reshape(1, D_ATT))


def _lru_kernel(x_ref, gz_ref, cw_ref, cb_ref, wa_ref, wx_ref, ba_ref, bx_ref, lam_ref, y_ref,
                xpad, a_f, u_f, a_b, u_b, *, tc):
    s = x_ref.shape[1]
    c = x_ref.shape[2]
    n_chunks = s // tc
    n_slab = c // 128
    pitch = a_f.shape[1] // n_chunks
    zeros8 = jnp.zeros((8, c), F32)
    xpad[0:8, :] = zeros8
    xpad[s + 8:s + 16, :] = zeros8

    def fill(ci, carry):
        t0 = pl.multiple_of(ci * tc, tc)
        xpad[pl.ds(t0 + 8, tc), :] = x_ref[0, pl.ds(t0, tc), :]
        return carry

    lax.fori_loop(0, n_chunks, fill, 0)

    cw = cw_ref[...]
    cb = cb_ref[...]
    decay = []
    for d in range(2):
        lam = lam_ref[d]
        softplus_neg = jnp.maximum(-lam, 0.0) + jnp.log(1.0 + jnp.exp(-jnp.abs(lam)))
        decay.append((-LRU_C * LOG2E) * softplus_neg)
    a_scr = (a_f, a_b)
    u_scr = (u_f, u_b)

    def gates(ci, carry):
        t0 = pl.multiple_of(ci * tc, tc)
        xw = xpad[pl.ds(t0, tc + 16), :]
        xc = (cw[0:1] * pltpu.roll(xw, 2, 0)[8:8 + tc]
              + cw[1:2] * pltpu.roll(xw, 1, 0)[8:8 + tc]
              + cw[2:3] * xw[8:8 + tc]
              + cw[3:4] * pltpu.roll(xw, tc + 15, 0)[8:8 + tc]) + cb
        xcb = xc.astype(BF16)
        for d in range(2):
            r = _sigmoid(jnp.dot(xcb, wa_ref[d], preferred_element_type=F32) + ba_ref[d])
            i = _sigmoid(jnp.dot(xcb, wx_ref[d], preferred_element_type=F32) + bx_ref[d])
            a = jnp.exp2(r * decay[d])
            v = 1.0 - a * a
            u = (v * lax.rsqrt(jnp.maximum(v, F32_TINY))) * (i * xc)
            r0 = pl.multiple_of(ci * pitch, 8)
            for sl in range(n_slab):
                a_scr[d][sl, pl.ds(r0, tc), :] = a[:, sl * 128:(sl + 1) * 128]
                u_scr[d][sl, pl.ds(r0, tc), :] = u[:, sl * 128:(sl + 1) * 128]
        return carry

    lax.fori_loop(0, n_chunks, gates, 0, unroll=2)

    def rows(tt):
        return pl.ds(tt, n_chunks, stride=pitch)

    def step(tt, carry):
        out = []
        for d in range(2):
            t_loc = tt if d == 0 else tc - 1 - tt
            for sl in range(n_slab):
                h, p = carry[len(out)]
                a8 = a_scr[d][sl, rows(t_loc), :]
                h = a8 * h + u_scr[d][sl, rows(t_loc), :]
                p = a8 * p
                u_scr[d][sl, rows(t_loc), :] = h
                a_scr[d][sl, rows(t_loc), :] = p
                out.append((h, p))
        return tuple(out)

    init = (jnp.zeros((n_chunks, 128), F32), jnp.ones((n_chunks, 128), F32))
    lax.fori_loop(0, tc, step, (init,) * (2 * n_slab), unroll=4)

    for sl in range(n_slab):
        lanes = slice(sl * 128, (sl + 1) * 128)
        h_end, p_end = u_f[sl, rows(tc - 1), :], a_f[sl, rows(tc - 1), :]
        h_beg, p_beg = u_b[sl, rows(0), :], a_b[sl, rows(0), :]
        carry_f = [jnp.zeros((1, 128), F32)]
        for r in range(1, n_chunks):
            carry_f.append(p_end[r - 1:r] * carry_f[-1] + h_end[r - 1:r])
        carry_b = [jnp.zeros((1, 128), F32)]
        for r in range(n_chunks - 2, -1, -1):
            carry_b.insert(0, p_beg[r + 1:r + 2] * carry_b[0] + h_beg[r + 1:r + 2])
        for r in range(n_chunks):
            blk = slice(r * pitch, r * pitch + tc)
            hsum = (u_f[sl, blk, :] + a_f[sl, blk, :] * carry_f[r]
                    + u_b[sl, blk, :] + a_b[sl, blk, :] * carry_b[r])
            y_ref[0, r * tc:(r + 1) * tc, lanes] = (hsum * gz_ref[0, r * tc:(r + 1) * tc, lanes].astype(F32)).astype(BF16)


def _block_diag(w, half):
    n_dir, n_blocks, blk, _ = w.shape
    per = half // blk
    n_half = n_blocks // per
    w = w.reshape(n_dir, n_half, per, blk, blk)
    eye = jnp.eye(per, dtype=w.dtype)
    out = w[:, :, :, :, None, :] * eye[None, None, :, None, :, None]
    return out.reshape(n_dir, n_half, half, half)


def _rg_lru(x_lru, gz, conv_w, conv_b, w_a, b_a, w_x, b_x, lam):
    bsz, s, c = x_lru.shape
    half = LRU_HALF
    tc = s // SCAN_LANES
    pitch = tc + 8 if (tc // 8) % 2 == 0 else tc + 16
    n_half = c // half
    wa = _block_diag(w_a, half).astype(BF16)
    wx = _block_diag(w_x, half).astype(BF16)
    seq = lambda b, p: (b, 0, p)
    chan = lambda b, p: (0, p)
    dirchan = lambda b, p: (0, 0, p)
    blk = lambda b, p: (0, p, 0, 0)
    return pl.pallas_call(
        functools.partial(_lru_kernel, tc=tc),
        grid=(bsz, n_half),
        in_specs=[pl.BlockSpec((1, s, half), seq),
                  pl.BlockSpec((1, s, half), seq),
                  pl.BlockSpec((CONV_W, half), chan),
                  pl.BlockSpec((1, half), chan),
                  pl.BlockSpec((2, None, half, half), blk),
                  pl.BlockSpec((2, None, half, half), blk),
                  pl.BlockSpec((2, 1, half), dirchan),
                  pl.BlockSpec((2, 1, half), dirchan),
                  pl.BlockSpec((2, 1, half), dirchan)],
        out_specs=pl.BlockSpec((1, s, half), seq),
        out_shape=jax.ShapeDtypeStruct((bsz, s, c), BF16),
        scratch_shapes=[pltpu.VMEM((s + 16, half), F32)]
                       + [pltpu.VMEM((half // 128, SCAN_LANES * pitch, 128), F32)] * 4,
        compiler_params=_params(("parallel", "parallel"), VMEM_LIMIT),
        name="rg_lru",
    )(x_lru, gz, conv_w.reshape(CONV_W, c), conv_b.reshape(1, c), wa, wx,
      b_a.reshape(2, 1, c), b_x.reshape(2, 1, c), lam.reshape(2, 1, c))


def _attn_kernel(q_ref, k_ref, vt_ref, bias_ref, lq_ref, go_ref, w1_ref, w3_ref, w2_ref,
                 o_ref, w1b_ref, w3b_ref, w2b_ref,
                 s_even, s_odd, m_even, m_odd, o_even, o_odd, p_scr, *, lam_init):
    for w_ref, wb_ref in ((w1_ref, w1b_ref), (w3_ref, w3b_ref), (w2_ref, w2b_ref)):
        wb_ref[...] = w_ref[...].astype(BF16)

    t = bias_ref.shape[-1]
    width = s_even.shape[0]
    n_k = k_ref.shape[1] // t
    n_q = q_ref.shape[1] // t
    n_steps = n_q // width
    subs = [(w, u) for w in range(width) for u in range(2)]
    chunk = 2
    pv_chunk = 2
    n_c = n_k // chunk

    s_odd[...] = jnp.zeros_like(s_odd)
    m_odd[...] = jnp.zeros_like(m_odd)
    o_odd[...] = jnp.ones_like(o_odd)
    even = dict(s=s_even, m=m_even, o=o_even)
    odd = dict(s=s_odd, m=m_odd, o=o_odd)
    ones_rows = jnp.ones((PV_ONES, pv_chunk * t), BF16)

    def fold(x, op):
        parts = [x[r * 8:(r + 1) * 8, :] for r in range(t // 8)]
        acc = parts[:2]
        for r in range(2, len(parts)):
            acc[r % 2] = op(acc[r % 2], parts[r])
        return op(acc[0], acc[1])

    def bias_tile(j, tile):
        return bias_ref[0, jnp.clip(j - tile, -2, 2) + 2]

    def finish(prev, w, tile):
        outs = [prev['o'][w, u, :HEAD_DV, :] / prev['o'][w, u, HEAD_DV:HEAD_DV + 1, :] for u in range(2)]
        lq = lq_ref[...]
        lam = (jnp.exp(jnp.sum(lq[0:1] * lq[1:2], axis=-1, keepdims=True))
               - jnp.exp(jnp.sum(lq[2:3] * lq[3:4], axis=-1, keepdims=True)) + lam_init)
        o = (outs[0] - lam * outs[1]).T
        ms = jnp.mean(o * o, axis=-1, keepdims=True)
        o_ref[0, pl.ds(pl.multiple_of(tile * t, t), t), :] = (
            (o * lax.rsqrt(ms + EPS) * go_ref[...]) * (1.0 - lam_init)).astype(BF16)

    def step(i, cur, prev, scores=True, softmax=True):
        tiles_a = [i * width + w for w in range(width)]
        for w in range(width):
            finish(prev, w, jnp.maximum((i - 2) * width + w, 0))
        s_cur, s_prev = cur['s'], prev['s']
        m_b = {wu: jnp.max(prev['m'][wu], axis=0, keepdims=True) for wu in subs} if softmax else {}
        m_acc = {wu: jnp.full((8, t), -jnp.inf, F32) for wu in subs}
        q_sub = {}
        if scores:
            for w in range(width):
                q = q_ref[0, pl.ds(pl.multiple_of(tiles_a[w] * t, t), t), :]
                lane = lax.broadcasted_iota(jnp.int32, q.shape, 1)
                zero = jnp.zeros_like(q)
                q_sub[w, 0] = jnp.where(lane < HEAD_DK, q, zero)
                q_sub[w, 1] = jnp.where(lane >= HEAD_DK, q, zero)
        for c in range(n_c):
            rows = slice(c * chunk * t, (c + 1) * chunk * t)
            for w, u in (subs if scores else []):
                sc_all = lax.dot_general(k_ref[0, rows, :], q_sub[w, u], NT_DIMS, preferred_element_type=F32)
                for jj in range(chunk):
                    j = c * chunk + jj
                    sc = sc_all[jj * t:(jj + 1) * t, :] + bias_tile(j, tiles_a[w])
                    s_cur[w, u, j] = sc
                    m_acc[w, u] = jnp.maximum(m_acc[w, u], fold(sc, jnp.maximum))
            for w, u in (subs if softmax else []):
                for jj in range(chunk):
                    j = c * chunk + jj
                    p = jnp.exp2(s_prev[w, u, j] - m_b[w, u])
                    p_scr[w, u, j * t:(j + 1) * t, :] = p.astype(BF16)
                if ((c + 1) * chunk) % pv_chunk == 0:
                    keys = slice(((c + 1) * chunk - pv_chunk) * t, (c + 1) * chunk * t)
                    part = jnp.dot(jnp.concatenate([vt_ref[0, :, keys], ones_rows], axis=0), p_scr[w, u, keys, :],
                                   preferred_element_type=F32)
                    if (c + 1) * chunk == pv_chunk:
                        cur['o'][w, u] = part
                    else:
                        cur['o'][w, u] += part

        if scores:
            for w, u in subs:
                cur['m'][w, u] = m_acc[w, u]

    def pair(pi, carry):
        @pl.when(pi >= 0)
        def _():
            step(2 * pi, even, odd)

        @pl.when(pi < n_steps)
        def _():
            step(2 * pi + 1, odd, even)

        return carry

    lax.fori_loop(0, n_steps // 2, pair, 0)
    step(n_steps, even, odd, scores=False)
    for w in range(width):
        finish(even, w, n_q - width + w)


def _diff_attention(qn, kn, vt, bias_tiles, lambda_qk, g_o, lam_init, expert_w):
    bsz, s, _ = qn.shape
    w_rows = expert_w[0].shape[0] * expert_w[0].shape[1]
    w_cols = expert_w[0].shape[2]
    assert all(w.shape == expert_w[0].shape for w in expert_w) and w_rows % (bsz * N_HEADS * 16) == 0
    w_blk = pl.BlockSpec((w_rows // (bsz * N_HEADS), w_cols), lambda b, h: (b * N_HEADS + h, 0))
    t = bias_tiles.shape[-1]
    n_k = s // t
    width = ATT_WIDTH
    assert (s // t) % (2 * width) == 0, "the query-tile pipeline advances two groups of tiles per loop trip"
    scores = pltpu.VMEM((width, 2, n_k, t, t), F32)
    maxima = pltpu.VMEM((width, 2, 8, t), F32)
    pv_acc = pltpu.VMEM((width, 2, HEAD_DV + PV_ONES, t), F32)
    seq = lambda b, h: (b, 0, h)
    outs = pl.pallas_call(
        functools.partial(_attn_kernel, lam_init=lam_init),
        grid=(bsz, N_HEADS),
        in_specs=[pl.BlockSpec((1, s, HEAD_DV), seq),
                  pl.BlockSpec((1, s, HEAD_DV), seq),
                  pl.BlockSpec((1, HEAD_DV, s), lambda b, h: (b, h, 0)),
                  pl.BlockSpec((1, 5, t, t), lambda b, h: (h, 0, 0, 0)),
                  pl.BlockSpec((4, HEAD_DK), lambda b, h: (0, 0)),
                  pl.BlockSpec((1, HEAD_DV), lambda b, h: (0, 0)),
                  w_blk, w_blk, w_blk],
        out_specs=[pl.BlockSpec((1, s, HEAD_DV), seq), w_blk, w_blk, w_blk],
        out_shape=[jax.ShapeDtypeStruct((bsz, s, D_ATT), BF16)]
                  + [jax.ShapeDtypeStruct((w_rows, w_cols), BF16)] * 3,
        scratch_shapes=[scores, scores, maxima, maxima, pv_acc, pv_acc, pltpu.VMEM((width, 2, s, t), BF16)],
        compiler_params=_params(("parallel", "parallel"), VMEM_LIMIT),
        name="diff_attention",
    )(qn, kn, vt, bias_tiles, lambda_qk, g_o.reshape(1, HEAD_DV), *[w.reshape(w_rows, w_cols) for w in expert_w])
    return outs[0], [wb.reshape(expert_w[0].shape) for wb in outs[1:]]


def _outproj_kernel(yl_ref, ya_ref, x_ref, gate_ref, sc_ref, sh_ref, g2_ref, wo_ref, wr_ref,
                    x1_ref, h2_ref, aff_ref):
    mix = (jnp.dot(yl_ref[0], wo_ref[0:D_LRU, :], preferred_element_type=F32)
           + jnp.dot(ya_ref[0], wo_ref[D_LRU:D_LRU + D_ATT, :], preferred_element_type=F32))
    x1 = x_ref[0] + gate_ref[0] * mix
    x1_ref[0] = x1
    ms = jnp.mean(x1 * x1, axis=-1, keepdims=True)
    h2 = (x1 * lax.rsqrt(ms + EPS) * g2_ref[...]) * (1.0 + sc_ref[0]) + sh_ref[0]
    h2b = h2.astype(BF16)
    h2_ref[0] = h2b
    logits = lax.dot_general(wr_ref[...], h2b, NT_DIMS, preferred_element_type=F32)
    ex = jnp.exp(logits - jnp.max(logits, axis=0, keepdims=True))
    aff_ref[0] = ex / jnp.sum(ex, axis=0, keepdims=True)


def _out_projection(y_lru, y_att, x, gate1, scale2, shift2, g_norm2, w_out, w_router, tm):
    bsz, s, d = x.shape
    row = lambda b, i: (b, i, 0)
    vec = lambda b, i: (b, 0, 0)
    full = lambda b, i: (0, 0)
    return pl.pallas_call(
        _outproj_kernel,
        grid=(bsz, s // tm),
        in_specs=[pl.BlockSpec((1, tm, D_LRU), row),
                  pl.BlockSpec((1, tm, D_ATT), row),
                  pl.BlockSpec((1, tm, d), row),
                  pl.BlockSpec((1, 1, d), vec),
                  pl.BlockSpec((1, 1, d), vec),
                  pl.BlockSpec((1, 1, d), vec),
                  pl.BlockSpec((1, d), full),
                  pl.BlockSpec((D_LRU + D_ATT, d), full),
                  pl.BlockSpec((N_EXPERTS, d), full)],
        out_specs=[pl.BlockSpec((1, tm, d), row),
                   pl.BlockSpec((1, tm, d), row),
                   pl.BlockSpec((1, N_EXPERTS, tm), lambda b, i: (b, 0, i))],
        out_shape=[jax.ShapeDtypeStruct((bsz, s, d), F32),
                   jax.ShapeDtypeStruct((bsz, s, d), BF16),
                   jax.ShapeDtypeStruct((bsz, N_EXPERTS, s), F32)],
        compiler_params=_params(("parallel", "parallel"), VMEM_LIMIT),
        name="out_proj_norm2_router",
    )(y_lru, y_att, x, gate1, scale2, shift2, g_norm2.reshape(1, d), w_out.astype(BF16),
      w_router.T.astype(BF16))


def _route_kernel(aff_ref, pos_ref, cnt_ref, *, cap, n_tok_chunks):
    aff = aff_ref[0]
    n_e, s = aff.shape
    bits = lax.bitcast_convert_type(aff, jnp.int32)
    capf = float(cap)

    def count(mask):
        return jnp.sum(jnp.where(mask, 1.0, 0.0), axis=-1, keepdims=True)

    tau = jnp.zeros((n_e, 1), jnp.int32)
    for bit in range(30, -1, -1):
        cand = tau | (1 << bit)
        tau = jnp.where(count(bits >= cand) >= capf, cand, tau)
    gt = bits > tau
    eq = bits == tau
    need = capf - count(gt)

    blk = min(256, s // n_tok_chunks)
    r = lax.broadcasted_iota(jnp.int32, (blk, blk), 0)
    cidx = lax.broadcasted_iota(jnp.int32, (blk, blk), 1)
    upper = jnp.where(r < cidx, 1.0, 0.0).astype(BF16)

    def prefix_blocks(mask):
        off = jnp.zeros((n_e, 1), F32)
        pieces, offs = [], []
        for k in range(s // blk):
            mb = jnp.where(mask[:, k * blk:(k + 1) * blk], 1.0, 0.0)
            offs.append(off)
            pieces.append(jnp.dot(mb.astype(BF16), upper, preferred_element_type=F32) + off)
            off = off + jnp.sum(mb, axis=-1, keepdims=True)
        return pieces, offs

    eq_rank, _ = prefix_blocks(eq)
    sel_blocks = []
    for k in range(s // blk):
        sl = slice(k * blk, (k + 1) * blk)
        sel_blocks.append(jnp.logical_or(gt[:, sl], jnp.logical_and(eq[:, sl], eq_rank[k] < need)))
    sel = jnp.concatenate(sel_blocks, axis=1)
    slot, offs = prefix_blocks(sel)
    for k in range(s // blk):
        pos_ref[0, :, k * blk:(k + 1) * blk] = jnp.where(sel_blocks[k], slot[k], -1.0).astype(jnp.int32)

    lane = lax.broadcasted_iota(jnp.int32, (n_e, 128), 1)
    cnt = jnp.zeros((n_e, 128), F32)
    per = (s // n_tok_chunks) // blk
    for j in range(n_tok_chunks):
        cnt = jnp.where(lane == j, offs[j * per], cnt)
    cnt_ref[0] = cnt.astype(jnp.int32)


def _routing(aff, cap):
    bsz, n_e, s = aff.shape
    group = math.gcd(bsz, ROUTE_GROUP)
    rows = group * n_e
    pos, cnt = pl.pallas_call(
        functools.partial(_route_kernel, cap=cap, n_tok_chunks=N_TOK_CHUNKS),
        grid=(bsz // group,),
        in_specs=[pl.BlockSpec((1, rows, s), lambda b: (b, 0, 0))],
        out_specs=[pl.BlockSpec((1, rows, s), lambda b: (b, 0, 0)),
                   pl.BlockSpec((1, rows, 128), lambda b: (b, 0, 0))],
        out_shape=[jax.ShapeDtypeStruct((bsz // group, rows, s), jnp.int32),
                   jax.ShapeDtypeStruct((bsz // group, rows, 128), jnp.int32)],
        compiler_params=_params(("parallel",)),
        name="expert_choice_routing",
    )(aff.reshape(bsz // group, rows, s))
    return pos.reshape(bsz, n_e, s), cnt.reshape(bsz, n_e, 128)


def _slot_windows(cnt_ref, pos_ref, b, j, k, cap):
    n_e, n_j, t = pos_ref.shape[1:]
    slot_iota = lax.broadcasted_iota(jnp.int32, (MOE_WINDOW, t), 0)
    sels, firsts = [], []
    for e in range(n_e):
        lo = cnt_ref[(b * n_e + e) * n_j + j]
        start = lax.shift_left(lax.shift_right_logical(lo, 4), 4) + k * MOE_WINDOW
        w0 = jnp.minimum(start, cap - MOE_WINDOW)
        slots = slot_iota + w0
        sel = jnp.logical_and(pos_ref[0, e, pl.ds(j, 1), :] == slots, slots >= start)
        sels.append(jnp.where(sel, 1.0, 0.0).astype(BF16))
        firsts.append(pl.multiple_of(e * cap + w0, 16))
    return jnp.concatenate(sels, axis=0), firsts


def _window_passes(cnt_ref, pos_ref, b, j, cap):
    n_e, n_j, _ = pos_ref.shape[1:]
    n_win = jnp.int32(1)
    for e in range(n_e):
        at = (b * n_e + e) * n_j
        start = lax.shift_left(lax.shift_right_logical(cnt_ref[at + j], 4), 4)
        hi = jnp.where(j + 1 < n_j, cnt_ref[at + jnp.minimum(j + 1, n_j - 1)], cap)
        n_win = jnp.maximum(n_win, lax.shift_right_logical(hi - start + (MOE_WINDOW - 1), MOE_WINDOW.bit_length() - 1))
    return n_win


def _dispatch_kernel(cnt_ref, pos_ref, h2_ref, xe_ref, *, cap):
    b = pl.program_id(0)
    t = pos_ref.shape[-1]

    @pl.when(pl.program_id(1) == 0)
    def _():
        xe_ref[...] = jnp.zeros_like(xe_ref)

    def chunk(jj, carry):
        j = pl.program_id(1) * MOE_CHUNKS + jj
        toks = pl.ds(pl.multiple_of(jj * t, t), t)

        def one_pass(k, inner):
            sels, firsts = _slot_windows(cnt_ref, pos_ref, b, j, k, cap)
            rows = jnp.dot(sels, h2_ref[0, toks, :], preferred_element_type=F32).astype(BF16)
            for e, first in enumerate(firsts):
                xe_ref[0, pl.ds(first, MOE_WINDOW), :] += rows[e * MOE_WINDOW:(e + 1) * MOE_WINDOW, :]
            return inner

        one_pass(0, 0)
        lax.fori_loop(1, _window_passes(cnt_ref, pos_ref, b, j, cap), one_pass, 0)
        return carry

    lax.fori_loop(0, MOE_CHUNKS, chunk, 0)


def _dispatch(cnt, pos, h2, cap):
    bsz, s, d = h2.shape
    n_e, n_j, t = pos.shape[1:]
    grid_spec = pltpu.PrefetchScalarGridSpec(
        num_scalar_prefetch=1,
        grid=(bsz, n_j // MOE_CHUNKS),
        in_specs=[pl.BlockSpec((1, n_e, n_j, t), lambda b, j, c: (b, 0, 0, 0)),
                  pl.BlockSpec((1, MOE_CHUNKS * t, d), lambda b, j, c: (b, j, 0))],
        out_specs=pl.BlockSpec((1, n_e * cap, d), lambda b, j, c: (b, 0, 0)),
    )
    return pl.pallas_call(
        functools.partial(_dispatch_kernel, cap=cap),
        grid_spec=grid_spec,
        out_shape=jax.ShapeDtypeStruct((bsz, n_e * cap, d), BF16),
        compiler_params=_params(("parallel", "arbitrary"), VMEM_LIMIT),
        name="expert_dispatch",
    )(cnt.reshape(-1), pos, h2)


def _moe_kernel(cnt_ref, pos_ref, aff_ref, xe_ref, w1_ref, w3_ref, w2_ref, gate_ref, y_ref, g_scr, *, cap):
    b = pl.program_id(0)
    e = pl.program_id(1)
    n_e = pl.num_programs(1)
    n_j, tchunk = pos_ref.shape[2:]
    cchunk = cap // N_CAP_CHUNKS
    base = (b * n_e + e) * n_j
    g_scr[...] = jnp.zeros_like(g_scr)

    slot_iota = lax.broadcasted_iota(jnp.int32, (cchunk, tchunk), 0)
    j = jnp.int32(0)
    i = jnp.int32(0)
    for _ in range(n_j + N_CAP_CHUNKS - 1):
        jc = jnp.minimum(j, n_j - 1)
        ic = jnp.minimum(i, N_CAP_CHUNKS - 1)
        c_lo = cnt_ref[base + jc]
        c_hi = jnp.where(jc + 1 < n_j, cnt_ref[base + jnp.minimum(jc + 1, n_j - 1)], cap)
        s_hi = (ic + 1) * cchunk
        valid = jnp.logical_and(jnp.maximum(c_lo, ic * cchunk) < jnp.minimum(c_hi, s_hi),
                                jnp.logical_and(j < n_j, i < N_CAP_CHUNKS))
        slot0 = jnp.where(valid, ic * cchunk, -2 * cap)
        sel = pos_ref[0, 0, pl.ds(jc, 1), :] == slot_iota + slot0
        rows = pl.ds(pl.multiple_of(ic * cchunk, cchunk), cchunk)
        g_scr[rows, :] += jnp.sum(jnp.where(sel, aff_ref[0, 0, pl.ds(jc, 1), :], 0.0), axis=-1, keepdims=True)
        j = j + jnp.where(c_hi <= s_hi, 1, 0)
        i = i + jnp.where(s_hi <= c_hi, 1, 0)

    xe = xe_ref[0]
    a = jnp.dot(xe, w1_ref[0], preferred_element_type=F32)
    gate = jnp.dot(xe, w3_ref[0], preferred_element_type=F32)
    hmid = ((a * _sigmoid(a)) * gate).astype(BF16)
    y = jnp.dot(hmid, w2_ref[0], preferred_element_type=F32) * g_scr[...] * gate_ref[0]
    y_ref[0] = y.astype(BF16)


def _moe(cnt, pos, aff, xe, w1, w3, w2, gate2, cap):
    bsz, n_e, n_j, t = pos.shape
    d, f = w1.shape[1:]
    tok_row = lambda b, e, c: (b, e, 0, 0)
    wspec = lambda b, e, c: (e, 0, 0)
    slot_rows = pl.BlockSpec((1, cap, d), lambda b, e, c: (b, e, 0))
    grid_spec = pltpu.PrefetchScalarGridSpec(
        num_scalar_prefetch=1,
        grid=(bsz, n_e),
        in_specs=[pl.BlockSpec((1, 1, n_j, t), tok_row),
                  pl.BlockSpec((1, 1, n_j, t), tok_row),
                  slot_rows,
                  pl.BlockSpec((1, d, f), wspec),
                  pl.BlockSpec((1, d, f), wspec),
                  pl.BlockSpec((1, f, d), wspec),
                  pl.BlockSpec((1, 1, d), lambda b, e, c: (b, 0, 0))],
        out_specs=slot_rows,
        scratch_shapes=[pltpu.VMEM((cap, 1), F32)],
    )
    return pl.pallas_call(
        functools.partial(_moe_kernel, cap=cap),
        grid_spec=grid_spec,
        out_shape=jax.ShapeDtypeStruct((bsz, n_e * cap, d), BF16),
        compiler_params=_params(("parallel", "parallel"), VMEM_LIMIT),
        name="expert_choice_ffn",
    )(cnt.reshape(-1), pos, aff.reshape(pos.shape), xe, w1, w3, w2, gate2)


def _combine_kernel(cnt_ref, pos_ref, y_ref, x1_ref, out_ref, *, cap):
    b = pl.program_id(0)
    t = pos_ref.shape[-1]

    def chunk(jj, carry):
        j = pl.program_id(1) * MOE_CHUNKS + jj
        toks = pl.ds(pl.multiple_of(jj * t, t), t)

        def windows(k):
            sels, firsts = _slot_windows(cnt_ref, pos_ref, b, j, k, cap)
            rows = jnp.concatenate([y_ref[0, pl.ds(first, MOE_WINDOW), :] for first in firsts], axis=0)
            return lax.dot_general(sels, rows, TN_DIMS, preferred_element_type=F32)

        out_ref[0, toks, :] = x1_ref[0, toks, :] + windows(0)

        def more(k, inner):
            out_ref[0, toks, :] += windows(k)
            return inner

        lax.fori_loop(1, _window_passes(cnt_ref, pos_ref, b, j, cap), more, 0)
        return carry

    lax.fori_loop(0, MOE_CHUNKS, chunk, 0)


def _combine(cnt, pos, y, x1, cap):
    bsz, s, d = x1.shape
    n_e, n_j, t = pos.shape[1:]
    grid_spec = pltpu.PrefetchScalarGridSpec(
        num_scalar_prefetch=1,
        grid=(bsz, n_j // MOE_CHUNKS),
        in_specs=[pl.BlockSpec((1, n_e, n_j, t), lambda b, j, c: (b, 0, 0, 0)),
                  pl.BlockSpec((1, n_e * cap, d), lambda b, j, c: (b, 0, 0)),
                  pl.BlockSpec((1, MOE_CHUNKS * t, d), lambda b, j, c: (b, j, 0))],
        out_specs=pl.BlockSpec((1, MOE_CHUNKS * t, d), lambda b, j, c: (b, j, 0)),
    )
    return pl.pallas_call(
        functools.partial(_combine_kernel, cap=cap),
        grid_spec=grid_spec,
        out_shape=jax.ShapeDtypeStruct((bsz, s, d), F32),
        compiler_params=_params(("parallel", "parallel"), VMEM_LIMIT),
        name="expert_combine",
    )(cnt.reshape(-1), pos, y, x1)


def kernel(x, c, w_mod, b_mod, g_norm1, w_in, conv_w, conv_b, lru_w_a, lru_b_a, lru_w_x, lru_b_x,
           lru_lambda, g_q, g_k, lambda_qk, g_attn_out, rel_bias, w_out, g_norm2, w_router, w1, w3, w2):
    bsz, s, d = x.shape
    depth = w_mod.shape[0]
    cap = max(1, EC_FACTOR * s // N_EXPERTS)
    bias_tiles = _bias_tiles(rel_bias, min(ATT_TILE, s))
    for l in range(depth):
        mod = _modulation(c, w_mod[l], b_mod[l])
        shift1, scale1, gate1, shift2, scale2, gate2 = [m.reshape(bsz, 1, d) for m in jnp.split(mod, 6, axis=-1)]
        x_lru, gz, qn, kn, vt = _in_projection(x, scale1, shift1, g_norm1[l], w_in[l], g_q[l], g_k[l],
                                               min(IN_PROJ_ROWS, s))
        y_lru = _rg_lru(x_lru, gz, conv_w[l], conv_b[l], lru_w_a[l], lru_b_a[l], lru_w_x[l], lru_b_x[l],
                        lru_lambda[l])
        lam_init = 0.8 - 0.6 * math.exp(-0.3 * l)
        y_att, (w1b, w3b, w2b) = _diff_attention(qn, kn, vt, bias_tiles, lambda_qk[l], g_attn_out[l], lam_init,
                                                 (w1[l], w3[l], w2[l]))
        x1, h2, aff = _out_projection(y_lru, y_att, x, gate1, scale2, shift2, g_norm2[l], w_out[l],
                                      w_router[l], min(OUT_PROJ_ROWS, s))
        pos, cnt = _routing(aff, cap)
        pos = pos.reshape(bsz, N_EXPERTS, N_TOK_CHUNKS, -1)
        cnt = cnt[:, :, :N_TOK_CHUNKS]
        xe = _dispatch(cnt, pos, h2, cap)
        y = _moe(cnt, pos, aff, xe, w1b, w3b, w2b, gate2, cap)
        x = _combine(cnt, pos, y, x1, cap)
    return x
```

```python
import functools
import math

import jax
import jax.numpy as jnp
from jax import lax
from jax.experimental import pallas as pl
from jax.experimental.pallas import tpu as pltpu

F32 = jnp.float32
BF16 = jnp.bfloat16

D_MODEL = 1024
D_LRU = 512
LRU_BLOCK = 64
LRU_C = 8.0
CONV_W = 4
N_HEADS = 4
HEAD_DV = 128
HEAD_DK = 64
D_ATT = N_HEADS * HEAD_DV
N_BUCKETS = 32
N_EXPERTS = 16
EC_FACTOR = 2
EPS = 1e-6
F32_TINY = 2.0 ** -126
LOG2E = math.log2(math.e)

IN_PROJ_ROWS = 1024
OUT_PROJ_ROWS = 1024
LRU_HALF = 256
SCAN_LANES = 8
ATT_TILE = 256
PV_ONES = 16
ATT_WIDTH = 1
ROUTE_GROUP = 8
N_TOK_CHUNKS = 16
LANES = 128
MOE_WINDOW = 64
MOE_CHUNKS = 4
VMEM_LIMIT = 56 * 1024 * 1024

NT_DIMS = (((1,), (1,)), ((), ()))
TN_DIMS = (((0,), (0,)), ((), ()))


def _sigmoid(x):
    return 1.0 / (1.0 + jnp.exp(-x))


def _params(sem, vmem=None):
    return pltpu.CompilerParams(dimension_semantics=sem, vmem_limit_bytes=vmem)


def _mod_kernel(c_ref, w_ref, b_ref, o_ref):
    c = c_ref[...]
    o_ref[...] = jnp.dot((c * _sigmoid(c)).astype(BF16), w_ref[...].astype(BF16),
                         preferred_element_type=F32) + b_ref[...]


def _modulation(c, w_mod, b_mod):
    bsz, d = c.shape
    n = w_mod.shape[1]
    return pl.pallas_call(
        _mod_kernel,
        grid=(n // d,),
        in_specs=[pl.BlockSpec((bsz, d), lambda j: (0, 0)),
                  pl.BlockSpec((d, d), lambda j: (0, j)),
                  pl.BlockSpec((1, d), lambda j: (0, j))],
        out_specs=pl.BlockSpec((bsz, d), lambda j: (0, j)),
        out_shape=jax.ShapeDtypeStruct((bsz, n), F32),
        compiler_params=_params(("arbitrary",)),
        name="adaln_mod",
    )(c, w_mod, b_mod.reshape(1, n))


def _bias_kernel(tab_ref, o_ref):
    h = pl.program_id(0)
    t = o_ref.shape[-1]
    key = lax.broadcasted_iota(jnp.int32, (t, t), 0)
    qry = lax.broadcasted_iota(jnp.int32, (t, t), 1)
    half = N_BUCKETS // 2
    max_exact = half // 2
    for d in range(5):
        if d == 0:
            o_ref[0, d] = jnp.full((t, t), tab_ref[half - 1, h] * LOG2E, F32)
        elif d == 4:
            o_ref[0, d] = jnp.full((t, t), tab_ref[N_BUCKETS - 1, h] * LOG2E, F32)
        else:
            rel = (d - 2) * t + key - qry
            n = jnp.abs(rel)
            n2 = n * n
            large = jnp.full((t, t), max_exact, jnp.int32)
            for k in range(1, half - max_exact):
                large = large + jnp.where(n2 >= (max_exact * max_exact) * (2 ** k), 1, 0)
            idx = jnp.where(n < max_exact, n, large) + jnp.where(rel > 0, half, 0)
            val = jnp.zeros((t, t), F32)
            for j in range(N_BUCKETS):
                val = jnp.where(idx == j, tab_ref[j, h] * LOG2E, val)
            o_ref[0, d] = val


def _bias_tiles(rel_bias, t):
    return pl.pallas_call(
        _bias_kernel,
        grid=(N_HEADS,),
        in_specs=[pl.BlockSpec(memory_space=pltpu.SMEM)],
        out_specs=pl.BlockSpec((1, 5, t, t), lambda h: (h, 0, 0, 0)),
        out_shape=jax.ShapeDtypeStruct((N_HEADS, 5, t, t), F32),
        compiler_params=_params(("arbitrary",)),
        name="t5_bias_tiles",
    )(rel_bias)


def _inproj_kernel(x_ref, sc_ref, sh_ref, g1_ref, w_ref, wvt_ref, mseg_ref, gq_ref, gk_ref,
                   xl_ref, gz_ref, q_ref, k_ref, vt_ref):
    x = x_ref[0]
    ms = jnp.mean(x * x, axis=-1, keepdims=True)
    h = (x * lax.rsqrt(ms + EPS) * g1_ref[...]) * (1.0 + sc_ref[0]) + sh_ref[0]
    hb = h.astype(BF16)

    def proj(lo, width):
        return jnp.dot(hb, w_ref[:, lo:lo + width], preferred_element_type=F32)

    def qk_norm(t, g):
        ss = jnp.dot((t * t).astype(BF16), mseg_ref[...], preferred_element_type=F32)
        return t * lax.rsqrt(ss * (1.0 / HEAD_DK) + EPS) * g

    xl_ref[0] = proj(0, D_LRU)
    z = proj(D_LRU, D_LRU)
    cdf = 0.5 * (1.0 + jnp.tanh(math.sqrt(2.0 / math.pi) * (z + 0.044715 * (z * z * z))))
    gz_ref[0] = (z * cdf).astype(BF16)
    q_ref[0] = (qk_norm(proj(2 * D_LRU, D_ATT), gq_ref[...]) * (HEAD_DK ** -0.5 * LOG2E)).astype(BF16)
    k_ref[0] = qk_norm(proj(2 * D_LRU + D_ATT, D_ATT), gk_ref[...]).astype(BF16)
    vt_ref[0] = lax.dot_general(wvt_ref[...], hb, NT_DIMS,
                                preferred_element_type=F32).astype(BF16)


def _in_projection(x, scale1, shift1, g_norm1, w_in, g_q, g_k, tm):
    bsz, s, d = x.shape
    n = w_in.shape[1] - D_ATT
    w_main = w_in[:, :n].astype(BF16)
    w_vt = w_in[:, n:].T.astype(BF16)
    seg = jnp.arange(D_ATT, dtype=jnp.int32) // HEAD_DK
    mseg = (seg[:, None] == seg[None, :]).astype(BF16)
    n_sub = D_ATT // HEAD_DK
    row = lambda b, i: (b, i, 0)
    vec = lambda b, i: (b, 0, 0)
    full = lambda b, i: (0, 0)
    out_block = pl.BlockSpec((1, tm, D_LRU), row)
    return pl.pallas_call(
        _inproj_kernel,
        grid=(bsz, s // tm),
        in_specs=[pl.BlockSpec((1, tm, d), row),
                  pl.BlockSpec((1, 1, d), vec),
                  pl.BlockSpec((1, 1, d), vec),
                  pl.BlockSpec((1, d), full),
                  pl.BlockSpec((d, n), full),
                  pl.BlockSpec((D_ATT, d), full),
                  pl.BlockSpec((D_ATT, D_ATT), full),
                  pl.BlockSpec((1, D_ATT), full),
                  pl.BlockSpec((1, D_ATT), full)],
        out_specs=[out_block] * 4 + [pl.BlockSpec((1, D_ATT, tm), lambda b, i: (b, 0, i))],
        out_shape=[jax.ShapeDtypeStruct((bsz, s, D_LRU), F32)]
                  + [jax.ShapeDtypeStruct((bsz, s, D_LRU), BF16)] * 3
                  + [jax.ShapeDtypeStruct((bsz, D_ATT, s), BF16)],
        compiler_params=_params(("parallel", "parallel"), VMEM_LIMIT),
        name="norm1_in_proj",
    )(x, scale1, shift1, g_norm1.reshape(1, d), w_main, w_vt, mseg,
      jnp.tile(g_q, n_sub).reshape(1, D_ATT), jnp.tile(g_k, n_sub).reshape(1, D_ATT))


def _lru_kernel(x_ref, gz_ref, cw_ref, cb_ref, wa_ref, wx_ref, ba_ref, bx_ref, lam_ref, y_ref,
                xpad, a_f, u_f, a_b, u_b, *, tc):
    s = x_ref.shape[1]
    c = x_ref.shape[2]
    n_chunks = s // tc
    n_slab = c // 128
    pitch = a_f.shape[1] // n_chunks
    zeros8 = jnp.zeros((8, c), F32)
    xpad[0:8, :] = zeros8
    xpad[s + 8:s + 16, :] = zeros8

    def fill(ci, carry):
        t0 = pl.multiple_of(ci * tc, tc)
        xpad[pl.ds(t0 + 8, tc), :] = x_ref[0, pl.ds(t0, tc), :]
        return carry

    lax.fori_loop(0, n_chunks, fill, 0)

    cw = cw_ref[...]
    cb = cb_ref[...]
    decay = []
    for d in range(2):
        lam = lam_ref[d]
        softplus_neg = jnp.maximum(-lam, 0.0) + jnp.log(1.0 + jnp.exp(-jnp.abs(lam)))
        decay.append((-LRU_C * LOG2E) * softplus_neg)
    a_scr = (a_f, a_b)
    u_scr = (u_f, u_b)

    def gates(ci, carry):
        t0 = pl.multiple_of(ci * tc, tc)
        xw = xpad[pl.ds(t0, tc + 16), :]
        xc = (cw[0:1] * pltpu.roll(xw, 2, 0)[8:8 + tc]
              + cw[1:2] * pltpu.roll(xw, 1, 0)[8:8 + tc]
              + cw[2:3] * xw[8:8 + tc]
              + cw[3:4] * pltpu.roll(xw, tc + 15, 0)[8:8 + tc]) + cb
        xcb = xc.astype(BF16)
        for d in range(2):
            r = _sigmoid(jnp.dot(xcb, wa_ref[d], preferred_element_type=F32) + ba_ref[d])
            i = _sigmoid(jnp.dot(xcb, wx_ref[d], preferred_element_type=F32) + bx_ref[d])
            a = jnp.exp2(r * decay[d])
            v = 1.0 - a * a
            u = (v * lax.rsqrt(jnp.maximum(v, F32_TINY))) * (i * xc)
            r0 = pl.multiple_of(ci * pitch, 8)
            for sl in range(n_slab):
                a_scr[d][sl, pl.ds(r0, tc), :] = a[:, sl * 128:(sl + 1) * 128]
                u_scr[d][sl, pl.ds(r0, tc), :] = u[:, sl * 128:(sl + 1) * 128]
        return carry

    lax.fori_loop(0, n_chunks, gates, 0, unroll=2)

    def rows(tt):
        return pl.ds(tt, n_chunks, stride=pitch)

    def step(tt, carry):
        out = []
        for d in range(2):
            t_loc = tt if d == 0 else tc - 1 - tt
            for sl in range(n_slab):
                h, p = carry[len(out)]
                a8 = a_scr[d][sl, rows(t_loc), :]
                h = a8 * h + u_scr[d][sl, rows(t_loc), :]
                p = a8 * p
                u_scr[d][sl, rows(t_loc), :] = h
                a_scr[d][sl, rows(t_loc), :] = p
                out.append((h, p))
        return tuple(out)

    init = (jnp.zeros((n_chunks, 128), F32), jnp.ones((n_chunks, 128), F32))
    lax.fori_loop(0, tc, step, (init,) * (2 * n_slab), unroll=4)

    for sl in range(n_slab):
        lanes = slice(sl * 128, (sl + 1) * 128)
        h_end, p_end = u_f[sl, rows(tc - 1), :], a_f[sl, rows(tc - 1), :]
        h_beg, p_beg = u_b[sl, rows(0), :], a_b[sl, rows(0), :]
        carry_f = [jnp.zeros((1, 128), F32)]
        for r in range(1, n_chunks):
            carry_f.append(p_end[r - 1:r] * carry_f[-1] + h_end[r - 1:r])
        carry_b = [jnp.zeros((1, 128), F32)]
        for r in range(n_chunks - 2, -1, -1):
            carry_b.insert(0, p_beg[r + 1:r + 2] * carry_b[0] + h_beg[r + 1:r + 2])
        for r in range(n_chunks):
            blk = slice(r * pitch, r * pitch + tc)
            hsum = (u_f[sl, blk, :] + a_f[sl, blk, :] * carry_f[r]
                    + u_b[sl, blk, :] + a_b[sl, blk, :] * carry_b[r])
            y_ref[0, r * tc:(r + 1) * tc, lanes] = (hsum * gz_ref[0, r * tc:(r + 1) * tc, lanes].astype(F32)).astype(BF16)


def _block_diag(w, half):
    n_dir, n_blocks, blk, _ = w.shape
    per = half // blk
    n_half = n_blocks // per
    w = w.reshape(n_dir, n_half, per, blk, blk)
    eye = jnp.eye(per, dtype=w.dtype)
    out = w[:, :, :, :, None, :] * eye[None, None, :, None, :, None]
    return out.reshape(n_dir, n_half, half, half)


def _rg_lru(x_lru, gz, conv_w, conv_b, w_a, b_a, w_x, b_x, lam):
    bsz, s, c = x_lru.shape
    half = LRU_HALF
    tc = s // SCAN_LANES
    pitch = tc + 8 if (tc // 8) % 2 == 0 else tc + 16
    n_half = c // half
    wa = _block_diag(w_a, half).astype(BF16)
    wx = _block_diag(w_x, half).astype(BF16)
    seq = lambda b, p: (b, 0, p)
    chan = lambda b, p: (0, p)
    dirchan = lambda b, p: (0, 0, p)
    blk = lambda b, p: (0, p, 0, 0)
    return pl.pallas_call(
        functools.partial(_lru_kernel, tc=tc),
        grid=(bsz, n_half),
        in_specs=[pl.BlockSpec((1, s, half), seq),
                  pl.BlockSpec((1, s, half), seq),
                  pl.BlockSpec((CONV_W, half), chan),
                  pl.BlockSpec((1, half), chan),
                  pl.BlockSpec((2, None, half, half), blk),
                  pl.BlockSpec((2, None, half, half), blk),
                  pl.BlockSpec((2, 1, half), dirchan),
                  pl.BlockSpec((2, 1, half), dirchan),
                  pl.BlockSpec((2, 1, half), dirchan)],
        out_specs=pl.BlockSpec((1, s, half), seq),
        out_shape=jax.ShapeDtypeStruct((bsz, s, c), BF16),
        scratch_shapes=[pltpu.VMEM((s + 16, half), F32)]
                       + [pltpu.VMEM((half // 128, SCAN_LANES * pitch, 128), F32)] * 4,
        compiler_params=_params(("parallel", "parallel"), VMEM_LIMIT),
        name="rg_lru",
    )(x_lru, gz, conv_w.reshape(CONV_W, c), conv_b.reshape(1, c), wa, wx,
      b_a.reshape(2, 1, c), b_x.reshape(2, 1, c), lam.reshape(2, 1, c))


def _attn_kernel(q_ref, k_ref, vt_ref, bias_ref, lq_ref, go_ref, w1_ref, w3_ref, w2_ref,
                 o_ref, w1b_ref, w3b_ref, w2b_ref,
                 s_even, s_odd, m_even, m_odd, o_even, o_odd, p_scr, *, lam_init):
    for w_ref, wb_ref in ((w1_ref, w1b_ref), (w3_ref, w3b_ref), (w2_ref, w2b_ref)):
        wb_ref[...] = w_ref[...].astype(BF16)

    t = bias_ref.shape[-1]
    width = s_even.shape[0]
    n_k = k_ref.shape[1] // t
    n_q = q_ref.shape[1] // t
    n_steps = n_q // width
    subs = [(w, u) for w in range(width) for u in range(2)]
    chunk = 2
    pv_chunk = 2
    n_c = n_k // chunk

    s_odd[...] = jnp.zeros_like(s_odd)
    m_odd[...] = jnp.zeros_like(m_odd)
    o_odd[...] = jnp.ones_like(o_odd)
    even = dict(s=s_even, m=m_even, o=o_even)
    odd = dict(s=s_odd, m=m_odd, o=o_odd)
    ones_rows = jnp.ones((PV_ONES, pv_chunk * t), BF16)

    def fold(x, op):
        parts = [x[r * 8:(r + 1) * 8, :] for r in range(t // 8)]
        acc = parts[:2]
        for r in range(2, len(parts)):
            acc[r % 2] = op(acc[r % 2], parts[r])
        return op(acc[0], acc[1])

    def bias_tile(j, tile):
        return bias_ref[0, jnp.clip(j - tile, -2, 2) + 2]

    def finish(prev, w, tile):
        outs = [prev['o'][w, u, :HEAD_DV, :] / prev['o'][w, u, HEAD_DV:HEAD_DV + 1, :] for u in range(2)]
        lq = lq_ref[...]
        lam = (jnp.exp(jnp.sum(lq[0:1] * lq[1:2], axis=-1, keepdims=True))
               - jnp.exp(jnp.sum(lq[2:3] * lq[3:4], axis=-1, keepdims=True)) + lam_init)
        o = (outs[0] - lam * outs[1]).T
        ms = jnp.mean(o * o, axis=-1, keepdims=True)
        o_ref[0, pl.ds(pl.multiple_of(tile * t, t), t), :] = (
            (o * lax.rsqrt(ms + EPS) * go_ref[...]) * (1.0 - lam_init)).astype(BF16)

    def step(i, cur, prev, scores=True, softmax=True):
        tiles_a = [i * width + w for w in range(width)]
        for w in range(width):
            finish(prev, w, jnp.maximum((i - 2) * width + w, 0))
        s_cur, s_prev = cur['s'], prev['s']
        m_b = {wu: jnp.max(prev['m'][wu], axis=0, keepdims=True) for wu in subs} if softmax else {}
        m_acc = {wu: jnp.full((8, t), -jnp.inf, F32) for wu in subs}
        q_sub = {}
        if scores:
            for w in range(width):
                q = q_ref[0, pl.ds(pl.multiple_of(tiles_a[w] * t, t), t), :]
                lane = lax.broadcasted_iota(jnp.int32, q.shape, 1)
                zero = jnp.zeros_like(q)
                q_sub[w, 0] = jnp.where(lane < HEAD_DK, q, zero)
                q_sub[w, 1] = jnp.where(lane >= HEAD_DK, q, zero)
        for c in range(n_c):
            rows = slice(c * chunk * t, (c + 1) * chunk * t)
            for w, u in (subs if scores else []):
                sc_all = lax.dot_general(k_ref[0, rows, :], q_sub[w, u], NT_DIMS, preferred_element_type=F32)
                for jj in range(chunk):
                    j = c * chunk + jj
                    sc = sc_all[jj * t:(jj + 1) * t, :] + bias_tile(j, tiles_a[w])
                    s_cur[w, u, j] = sc
                    m_acc[w, u] = jnp.maximum(m_acc[w, u], fold(sc, jnp.maximum))
            for w, u in (subs if softmax else []):
                for jj in range(chunk):
                    j = c * chunk + jj
                    p = jnp.exp2(s_prev[w, u, j] - m_b[w, u])
                    p_scr[w, u, j * t:(j + 1) * t, :] = p.astype(BF16)
                if ((c + 1) * chunk) % pv_chunk == 0:
                    keys = slice(((c + 1) * chunk - pv_chunk) * t, (c + 1) * chunk * t)
                    part = jnp.dot(jnp.concatenate([vt_ref[0, :, keys], ones_rows], axis=0), p_scr[w, u, keys, :],
                                   preferred_element_type=F32)
                    if (c + 1) * chunk == pv_chunk:
                        cur['o'][w, u] = part
                    else:
                        cur['o'][w, u] += part

        if scores:
            for w, u in subs:
                cur['m'][w, u] = m_acc[w, u]

    def pair(pi, carry):
        @pl.when(pi >= 0)
        def _():
            step(2 * pi, even, odd)

        @pl.when(pi < n_steps)
        def _():
            step(2 * pi + 1, odd, even)

        return carry

    lax.fori_loop(0, n_steps // 2, pair, 0)
    step(n_steps, even, odd, scores=False)
    for w in range(width):
        finish(even, w, n_q - width + w)


def _diff_attention(qn, kn, vt, bias_tiles, lambda_qk, g_o, lam_init, expert_w):
    bsz, s, _ = qn.shape
    w_rows = expert_w[0].shape[0] * expert_w[0].shape[1]
    w_cols = expert_w[0].shape[2]
    assert all(w.shape == expert_w[0].shape for w in expert_w) and w_rows % (bsz * N_HEADS * 16) == 0
    w_blk = pl.BlockSpec((w_rows // (bsz * N_HEADS), w_cols), lambda b, h: (b * N_HEADS + h, 0))
    t = bias_tiles.shape[-1]
    n_k = s // t
    width = ATT_WIDTH
    assert (s // t) % (2 * width) == 0, "the query-tile pipeline advances two groups of tiles per loop trip"
    scores = pltpu.VMEM((width, 2, n_k, t, t), F32)
    maxima = pltpu.VMEM((width, 2, 8, t), F32)
    pv_acc = pltpu.VMEM((width, 2, HEAD_DV + PV_ONES, t), F32)
    seq = lambda b, h: (b, 0, h)
    outs = pl.pallas_call(
        functools.partial(_attn_kernel, lam_init=lam_init),
        grid=(bsz, N_HEADS),
        in_specs=[pl.BlockSpec((1, s, HEAD_DV), seq),
                  pl.BlockSpec((1, s, HEAD_DV), seq),
                  pl.BlockSpec((1, HEAD_DV, s), lambda b, h: (b, h, 0)),
                  pl.BlockSpec((1, 5, t, t), lambda b, h: (h, 0, 0, 0)),
                  pl.BlockSpec((4, HEAD_DK), lambda b, h: (0, 0)),
                  pl.BlockSpec((1, HEAD_DV), lambda b, h: (0, 0)),
                  w_blk, w_blk, w_blk],
        out_specs=[pl.BlockSpec((1, s, HEAD_DV), seq), w_blk, w_blk, w_blk],
        out_shape=[jax.ShapeDtypeStruct((bsz, s, D_ATT), BF16)]
                  + [jax.ShapeDtypeStruct((w_rows, w_cols), BF16)] * 3,
        scratch_shapes=[scores, scores, maxima, maxima, pv_acc, pv_acc, pltpu.VMEM((width, 2, s, t), BF16)],
        compiler_params=_params(("parallel", "parallel"), VMEM_LIMIT),
        name="diff_attention",
    )(qn, kn, vt, bias_tiles, lambda_qk, g_o.reshape(1, HEAD_DV), *[w.reshape(w_rows, w_cols) for w in expert_w])
    return outs[0], [wb.reshape(expert_w[0].shape) for wb in outs[1:]]


def _outproj_kernel(yl_ref, ya_ref, x_ref, gate_ref, sc_ref, sh_ref, g2_ref, wo_ref, wr_ref,
                    x1_ref, h2_ref, aff_ref):
    mix = (jnp.dot(yl_ref[0], wo_ref[0:D_LRU, :], preferred_element_type=F32)
           + jnp.dot(ya_ref[0], wo_ref[D_LRU:D_LRU + D_ATT, :], preferred_element_type=F32))
    x1 = x_ref[0] + gate_ref[0] * mix
    x1_ref[0] = x1
    ms = jnp.mean(x1 * x1, axis=-1, keepdims=True)
    h2 = (x1 * lax.rsqrt(ms + EPS) * g2_ref[...]) * (1.0 + sc_ref[0]) + sh_ref[0]
    h2b = h2.astype(BF16)
    h2_ref[0] = h2b
    logits = lax.dot_general(wr_ref[...], h2b, NT_DIMS, preferred_element_type=F32)
    ex = jnp.exp(logits - jnp.max(logits, axis=0, keepdims=True))
    aff_ref[0] = ex / jnp.sum(ex, axis=0, keepdims=True)


def _out_projection(y_lru, y_att, x, gate1, scale2, shift2, g_norm2, w_out, w_router, tm):
    bsz, s, d = x.shape
    row = lambda b, i: (b, i, 0)
    vec = lambda b, i: (b, 0, 0)
    full = lambda b, i: (0, 0)
    return pl.pallas_call(
        _outproj_kernel,
        grid=(bsz, s // tm),
        in_specs=[pl.BlockSpec((1, tm, D_LRU), row),
                  pl.BlockSpec((1, tm, D_ATT), row),
                  pl.BlockSpec((1, tm, d), row),
                  pl.BlockSpec((1, 1, d), vec),
                  pl.BlockSpec((1, 1, d), vec),
                  pl.BlockSpec((1, 1, d), vec),
                  pl.BlockSpec((1, d), full),
                  pl.BlockSpec((D_LRU + D_ATT, d), full),
                  pl.BlockSpec((N_EXPERTS, d), full)],
        out_specs=[pl.BlockSpec((1, tm, d), row),
                   pl.BlockSpec((1, tm, d), row),
                   pl.BlockSpec((1, N_EXPERTS, tm), lambda b, i: (b, 0, i))],
        out_shape=[jax.ShapeDtypeStruct((bsz, s, d), F32),
                   jax.ShapeDtypeStruct((bsz, s, d), BF16),
                   jax.ShapeDtypeStruct((bsz, N_EXPERTS, s), F32)],
        compiler_params=_params(("parallel", "parallel"), VMEM_LIMIT),
        name="out_proj_norm2_router",
    )(y_lru, y_att, x, gate1, scale2, shift2, g_norm2.reshape(1, d), w_out.astype(BF16),
      w_router.T.astype(BF16))


def _route_kernel(aff_ref, pos_ref, cnt_ref, *, cap, n_tok_chunks):
    aff = aff_ref[0]
    n_e, s = aff.shape
    bits = lax.bitcast_convert_type(aff, jnp.int32)
    capf = float(cap)

    def count(mask):
        return jnp.sum(jnp.where(mask, 1.0, 0.0), axis=-1, keepdims=True)

    tau = jnp.zeros((n_e, 1), jnp.int32)
    for bit in range(30, -1, -1):
        cand = tau | (1 << bit)
        tau = jnp.where(count(bits >= cand) >= capf, cand, tau)
    gt = bits > tau
    eq = bits == tau
    need = capf - count(gt)

    blk = min(256, s // n_tok_chunks)
    r = lax.broadcasted_iota(jnp.int32, (blk, blk), 0)
    cidx = lax.broadcasted_iota(jnp.int32, (blk, blk), 1)
    upper = jnp.where(r < cidx, 1.0, 0.0).astype(BF16)

    def prefix_blocks(mask):
        off = jnp.zeros((n_e, 1), F32)
        pieces, offs = [], []
        for k in range(s // blk):
            mb = jnp.where(mask[:, k * blk:(k + 1) * blk], 1.0, 0.0)
            offs.append(off)
            pieces.append(jnp.dot(mb.astype(BF16), upper, preferred_element_type=F32) + off)
            off = off + jnp.sum(mb, axis=-1, keepdims=True)
        return pieces, offs

    eq_rank, _ = prefix_blocks(eq)
    sel_blocks = []
    for k in range(s // blk):
        sl = slice(k * blk, (k + 1) * blk)
        sel_blocks.append(jnp.logical_or(gt[:, sl], jnp.logical_and(eq[:, sl], eq_rank[k] < need)))
    sel = jnp.concatenate(sel_blocks, axis=1)
    slot, offs = prefix_blocks(sel)
    for k in range(s // blk):
        pos_ref[0, :, k * blk:(k + 1) * blk] = jnp.where(sel_blocks[k], slot[k], -1.0).astype(jnp.int32)

    lane = lax.broadcasted_iota(jnp.int32, (n_e, 128), 1)
    cnt = jnp.zeros((n_e, 128), F32)
    per = (s // n_tok_chunks) // blk
    for j in range(n_tok_chunks):
        cnt = jnp.where(lane == j, offs[j * per], cnt)
    cnt_ref[0] = cnt.astype(jnp.int32)


def _routing(aff, cap):
    bsz, n_e, s = aff.shape
    group = math.gcd(bsz, ROUTE_GROUP)
    rows = group * n_e
    pos, cnt = pl.pallas_call(
        functools.partial(_route_kernel, cap=cap, n_tok_chunks=N_TOK_CHUNKS),
        grid=(bsz // group,),
        in_specs=[pl.BlockSpec((1, rows, s), lambda b: (b, 0, 0))],
        out_specs=[pl.BlockSpec((1, rows, s), lambda b: (b, 0, 0)),
                   pl.BlockSpec((1, rows, 128), lambda b: (b, 0, 0))],
        out_shape=[jax.ShapeDtypeStruct((bsz // group, rows, s), jnp.int32),
                   jax.ShapeDtypeStruct((bsz // group, rows, 128), jnp.int32)],
        compiler_params=_params(("parallel",)),
        name="expert_choice_routing",
    )(aff.reshape(bsz // group, rows, s))
    return pos.reshape(bsz, n_e, s), cnt.reshape(bsz, n_e, 128)


def _slot_windows(cnt_ref, pos_ref, b, j, k, cap):
    n_e, n_j, t = pos_ref.shape[1:]
    slot_iota = lax.broadcasted_iota(jnp.int32, (MOE_WINDOW, t), 0)
    sels, firsts = [], []
    for e in range(n_e):
        lo = cnt_ref[(b * n_e + e) * n_j + j]
        start = lax.shift_left(lax.shift_right_logical(lo, 4), 4) + k * MOE_WINDOW
        w0 = jnp.minimum(start, cap - MOE_WINDOW)
        slots = slot_iota + w0
        sels.append(jnp.logical_and(pos_ref[0, e, pl.ds(j, 1), :] == slots, slots >= start))
        firsts.append(pl.multiple_of(e * cap + w0, 16))
    return sels, firsts


def _stacked_one_hot(sels):
    return jnp.concatenate([jnp.where(sel, 1.0, 0.0).astype(BF16) for sel in sels], axis=0)


def _window_passes(cnt_ref, pos_ref, b, j, cap):
    n_e, n_j, _ = pos_ref.shape[1:]
    n_win = jnp.int32(1)
    for e in range(n_e):
        at = (b * n_e + e) * n_j
        start = lax.shift_left(lax.shift_right_logical(cnt_ref[at + j], 4), 4)
        hi = jnp.where(j + 1 < n_j, cnt_ref[at + jnp.minimum(j + 1, n_j - 1)], cap)
        n_win = jnp.maximum(n_win, lax.shift_right_logical(hi - start + (MOE_WINDOW - 1), MOE_WINDOW.bit_length() - 1))
    return n_win


def _dispatch_kernel(cnt_ref, pos_ref, aff_ref, h2_ref, xe_ref, g_ref, *, cap):
    b = pl.program_id(0)
    t = pos_ref.shape[-1]

    @pl.when(pl.program_id(1) == 0)
    def _():
        xe_ref[...] = jnp.zeros_like(xe_ref)
        g_ref[...] = jnp.zeros_like(g_ref)

    def chunk(jj, carry):
        j = pl.program_id(1) * MOE_CHUNKS + jj
        toks = pl.ds(pl.multiple_of(jj * t, t), t)

        def one_pass(k, inner):
            sels, firsts = _slot_windows(cnt_ref, pos_ref, b, j, k, cap)
            rows = jnp.dot(_stacked_one_hot(sels), h2_ref[0, toks, :], preferred_element_type=F32).astype(BF16)
            for e, (sel, first) in enumerate(zip(sels, firsts)):
                window = pl.ds(first, MOE_WINDOW)
                xe_ref[0, window, :] += rows[e * MOE_WINDOW:(e + 1) * MOE_WINDOW, :]
                weight = jnp.sum(jnp.where(sel, aff_ref[0, e, pl.ds(j, 1), :], 0.0), axis=-1, keepdims=True)
                g_ref[0, window, :] += jnp.broadcast_to(weight, (MOE_WINDOW, g_ref.shape[-1]))
            return inner

        one_pass(0, 0)
        lax.fori_loop(1, _window_passes(cnt_ref, pos_ref, b, j, cap), one_pass, 0)
        return carry

    lax.fori_loop(0, MOE_CHUNKS, chunk, 0)


def _dispatch(cnt, pos, aff, h2, cap):
    bsz, s, d = h2.shape
    n_e, n_j, t = pos.shape[1:]
    per_token = pl.BlockSpec((1, n_e, n_j, t), lambda b, j, c: (b, 0, 0, 0))
    grid_spec = pltpu.PrefetchScalarGridSpec(
        num_scalar_prefetch=1,
        grid=(bsz, n_j // MOE_CHUNKS),
        in_specs=[per_token, per_token,
                  pl.BlockSpec((1, MOE_CHUNKS * t, d), lambda b, j, c: (b, j, 0))],
        out_specs=[pl.BlockSpec((1, n_e * cap, d), lambda b, j, c: (b, 0, 0)),
                   pl.BlockSpec((1, n_e * cap, LANES), lambda b, j, c: (b, 0, 0))],
    )
    return pl.pallas_call(
        functools.partial(_dispatch_kernel, cap=cap),
        grid_spec=grid_spec,
        out_shape=[jax.ShapeDtypeStruct((bsz, n_e * cap, d), BF16),
                   jax.ShapeDtypeStruct((bsz, n_e * cap, LANES), F32)],
        compiler_params=_params(("parallel", "arbitrary"), VMEM_LIMIT),
        name="expert_dispatch",
    )(cnt.reshape(-1), pos, aff.reshape(pos.shape), h2)


def _moe_kernel(xe_ref, g_ref, w1_ref, w3_ref, w2_ref, gate_ref, y_ref):
    xe = xe_ref[0]
    a = jnp.dot(xe, w1_ref[0], preferred_element_type=F32)
    gate = jnp.dot(xe, w3_ref[0], preferred_element_type=F32)
    hmid = ((a * _sigmoid(a)) * gate).astype(BF16)
    y = jnp.dot(hmid, w2_ref[0], preferred_element_type=F32) * g_ref[0][:, :1] * gate_ref[0]
    y_ref[0] = y.astype(BF16)


def _moe(xe, g, w1, w3, w2, gate2, cap):
    bsz, _, d = xe.shape
    n_e, _, f = w1.shape
    wspec = lambda b, e: (e, 0, 0)
    slot_rows = lambda b, e: (b, e, 0)
    return pl.pallas_call(
        _moe_kernel,
        grid=(bsz, n_e),
        in_specs=[pl.BlockSpec((1, cap, d), slot_rows),
                  pl.BlockSpec((1, cap, g.shape[-1]), slot_rows),
                  pl.BlockSpec((1, d, f), wspec),
                  pl.BlockSpec((1, d, f), wspec),
                  pl.BlockSpec((1, f, d), wspec),
                  pl.BlockSpec((1, 1, d), lambda b, e: (b, 0, 0))],
        out_specs=pl.BlockSpec((1, cap, d), slot_rows),
        out_shape=jax.ShapeDtypeStruct((bsz, n_e * cap, d), BF16),
        compiler_params=_params(("parallel", "parallel"), VMEM_LIMIT),
        name="expert_choice_ffn",
    )(xe, g, w1, w3, w2, gate2)


def _combine_kernel(cnt_ref, pos_ref, y_ref, x1_ref, out_ref, *, cap):
    b = pl.program_id(0)
    t = pos_ref.shape[-1]

    def chunk(jj, carry):
        j = pl.program_id(1) * MOE_CHUNKS + jj
        toks = pl.ds(pl.multiple_of(jj * t, t), t)

        def windows(k):
            sels, firsts = _slot_windows(cnt_ref, pos_ref, b, j, k, cap)
            rows = jnp.concatenate([y_ref[0, pl.ds(first, MOE_WINDOW), :] for first in firsts], axis=0)
            return lax.dot_general(_stacked_one_hot(sels), rows, TN_DIMS, preferred_element_type=F32)

        out_ref[0, toks, :] = x1_ref[0, toks, :] + windows(0)

        def more(k, inner):
            out_ref[0, toks, :] += windows(k)
            return inner

        lax.fori_loop(1, _window_passes(cnt_ref, pos_ref, b, j, cap), more, 0)
        return carry

    lax.fori_loop(0, MOE_CHUNKS, chunk, 0)


def _combine(cnt, pos, y, x1, cap):
    bsz, s, d = x1.shape
    n_e, n_j, t = pos.shape[1:]
    grid_spec = pltpu.PrefetchScalarGridSpec(
        num_scalar_prefetch=1,
        grid=(bsz, n_j // MOE_CHUNKS),
        in_specs=[pl.BlockSpec((1, n_e, n_j, t), lambda b, j, c: (b, 0, 0, 0)),
                  pl.BlockSpec((1, n_e * cap, d), lambda b, j, c: (b, 0, 0)),
                  pl.BlockSpec((1, MOE_CHUNKS * t, d), lambda b, j, c: (b, j, 0))],
        out_specs=pl.BlockSpec((1, MOE_CHUNKS * t, d), lambda b, j, c: (b, j, 0)),
    )
    return pl.pallas_call(
        functools.partial(_combine_kernel, cap=cap),
        grid_spec=grid_spec,
        out_shape=jax.ShapeDtypeStruct((bsz, s, d), F32),
        compiler_params=_params(("parallel", "parallel"), VMEM_LIMIT),
        name="expert_combine",
    )(cnt.reshape(-1), pos, y, x1)


def kernel(x, c, w_mod, b_mod, g_norm1, w_in, conv_w, conv_b, lru_w_a, lru_b_a, lru_w_x, lru_b_x,
           lru_lambda, g_q, g_k, lambda_qk, g_attn_out, rel_bias, w_out, g_norm2, w_router, w1, w3, w2):
    bsz, s, d = x.shape
    depth = w_mod.shape[0]
    cap = max(1, EC_FACTOR * s // N_EXPERTS)
    bias_tiles = _bias_tiles(rel_bias, min(ATT_TILE, s))
    for l in range(depth):
        mod = _modulation(c, w_mod[l], b_mod[l])
        shift1, scale1, gate1, shift2, scale2, gate2 = [m.reshape(bsz, 1, d) for m in jnp.split(mod, 6, axis=-1)]
        x_lru, gz, qn, kn, vt = _in_projection(x, scale1, shift1, g_norm1[l], w_in[l], g_q[l], g_k[l],
                                               min(IN_PROJ_ROWS, s))
        y_lru = _rg_lru(x_lru, gz, conv_w[l], conv_b[l], lru_w_a[l], lru_b_a[l], lru_w_x[l], lru_b_x[l],
                        lru_lambda[l])
        lam_init = 0.8 - 0.6 * math.exp(-0.3 * l)
        y_att, (w1b, w3b, w2b) = _diff_attention(qn, kn, vt, bias_tiles, lambda_qk[l], g_attn_out[l], lam_init,
                                                 (w1[l], w3[l], w2[l]))
        x1, h2, aff = _out_projection(y_lru, y_att, x, gate1, scale2, shift2, g_norm2[l], w_out[l],
                                      w_router[l], min(OUT_PROJ_ROWS, s))
        pos, cnt = _routing(aff, cap)
        pos = pos.reshape(bsz, N_EXPERTS, N_TOK_CHUNKS, -1)
        cnt = cnt[:, :, :N_TOK_CHUNKS]
        xe, g = _dispatch(cnt, pos, aff, h2, cap)
        y = _moe(xe, g, w1b, w3b, w2b, gate2, cap)
        x = _combine(cnt, pos, y, x1, cap)
    return x
```

```python
import functools
import math

import jax
import jax.numpy as jnp
from jax import lax
from jax.experimental import pallas as pl
from jax.experimental.pallas import tpu as pltpu

F32 = jnp.float32
BF16 = jnp.bfloat16

D_MODEL = 1024
D_LRU = 512
LRU_BLOCK = 64
LRU_C = 8.0
CONV_W = 4
N_HEADS = 4
HEAD_DV = 128
HEAD_DK = 64
D_ATT = N_HEADS * HEAD_DV
N_BUCKETS = 32
N_EXPERTS = 16
EC_FACTOR = 2
EPS = 1e-6
F32_TINY = 2.0 ** -126
LOG2E = math.log2(math.e)

IN_PROJ_ROWS = 1024
OUT_PROJ_ROWS = 1024
LRU_HALF = 256
SCAN_LANES = 8
ATT_TILE = 256
PV_ONES = 16
ATT_WIDTH = 1
ROUTE_GROUP = 8
N_TOK_CHUNKS = 16
MXU_TILE = 256
LANES = 128
MOE_WINDOW = 64
MOE_CHUNKS = 4
VMEM_LIMIT = 56 * 1024 * 1024

NT_DIMS = (((1,), (1,)), ((), ()))
TN_DIMS = (((0,), (0,)), ((), ()))


def _sigmoid(x):
    return 1.0 / (1.0 + jnp.exp(-x))


def _params(sem, vmem=None):
    return pltpu.CompilerParams(dimension_semantics=sem, vmem_limit_bytes=vmem)


def _mod_kernel(c_ref, w_ref, b_ref, o_ref):
    c = c_ref[...]
    o_ref[...] = jnp.dot((c * _sigmoid(c)).astype(BF16), w_ref[...].astype(BF16),
                         preferred_element_type=F32) + b_ref[...]


def _modulation(c, w_mod, b_mod):
    bsz, d = c.shape
    n = w_mod.shape[1]
    return pl.pallas_call(
        _mod_kernel,
        grid=(n // d,),
        in_specs=[pl.BlockSpec((bsz, d), lambda j: (0, 0)),
                  pl.BlockSpec((d, d), lambda j: (0, j)),
                  pl.BlockSpec((1, d), lambda j: (0, j))],
        out_specs=pl.BlockSpec((bsz, d), lambda j: (0, j)),
        out_shape=jax.ShapeDtypeStruct((bsz, n), F32),
        compiler_params=_params(("arbitrary",)),
        name="adaln_mod",
    )(c, w_mod, b_mod.reshape(1, n))


def _bias_kernel(tab_ref, o_ref):
    h = pl.program_id(0)
    t = o_ref.shape[-1]
    key = lax.broadcasted_iota(jnp.int32, (t, t), 0)
    qry = lax.broadcasted_iota(jnp.int32, (t, t), 1)
    half = N_BUCKETS // 2
    max_exact = half // 2
    for d in range(5):
        if d == 0:
            o_ref[0, d] = jnp.full((t, t), tab_ref[half - 1, h] * LOG2E, F32)
        elif d == 4:
            o_ref[0, d] = jnp.full((t, t), tab_ref[N_BUCKETS - 1, h] * LOG2E, F32)
        else:
            rel = (d - 2) * t + key - qry
            n = jnp.abs(rel)
            n2 = n * n
            large = jnp.full((t, t), max_exact, jnp.int32)
            for k in range(1, half - max_exact):
                large = large + jnp.where(n2 >= (max_exact * max_exact) * (2 ** k), 1, 0)
            idx = jnp.where(n < max_exact, n, large) + jnp.where(rel > 0, half, 0)
            val = jnp.zeros((t, t), F32)
            for j in range(N_BUCKETS):
                val = jnp.where(idx == j, tab_ref[j, h] * LOG2E, val)
            o_ref[0, d] = val


def _bias_tiles(rel_bias, t):
    return pl.pallas_call(
        _bias_kernel,
        grid=(N_HEADS,),
        in_specs=[pl.BlockSpec(memory_space=pltpu.SMEM)],
        out_specs=pl.BlockSpec((1, 5, t, t), lambda h: (h, 0, 0, 0)),
        out_shape=jax.ShapeDtypeStruct((N_HEADS, 5, t, t), F32),
        compiler_params=_params(("arbitrary",)),
        name="t5_bias_tiles",
    )(rel_bias)


def _inproj_kernel(x_ref, sc_ref, sh_ref, g1_ref, w_ref, wvt_ref, mseg_ref, gq_ref, gk_ref,
                   xl_ref, gz_ref, q_ref, k_ref, vt_ref):
    x = x_ref[0]
    ms = jnp.mean(x * x, axis=-1, keepdims=True)
    h = (x * lax.rsqrt(ms + EPS) * g1_ref[...]) * (1.0 + sc_ref[0]) + sh_ref[0]
    hb = h.astype(BF16)

    def proj(lo, width):
        return jnp.dot(hb, w_ref[:, lo:lo + width], preferred_element_type=F32)

    def qk_norm(t, g):
        tt = (t * t).astype(BF16)
        m = mseg_ref.shape[0]
        ss = jnp.concatenate([jnp.dot(tt[:, c:c + m], mseg_ref[...], preferred_element_type=F32)
                              for c in range(0, tt.shape[1], m)], axis=1)
        return t * lax.rsqrt(ss * (1.0 / HEAD_DK) + EPS) * g

    xl_ref[0] = proj(0, D_LRU)
    z = proj(D_LRU, D_LRU)
    cdf = 0.5 * (1.0 + jnp.tanh(math.sqrt(2.0 / math.pi) * (z + 0.044715 * (z * z * z))))
    gz_ref[0] = (z * cdf).astype(BF16)
    q_ref[0] = (qk_norm(proj(2 * D_LRU, D_ATT), gq_ref[...]) * (HEAD_DK ** -0.5 * LOG2E)).astype(BF16)
    k_ref[0] = qk_norm(proj(2 * D_LRU + D_ATT, D_ATT), gk_ref[...]).astype(BF16)
    vt_ref[0] = lax.dot_general(wvt_ref[...], hb, NT_DIMS,
                                preferred_element_type=F32).astype(BF16)


def _in_projection(x, scale1, shift1, g_norm1, w_in, g_q, g_k, tm):
    bsz, s, d = x.shape
    n = w_in.shape[1] - D_ATT
    w_main = w_in[:, :n].astype(BF16)
    w_vt = w_in[:, n:].T.astype(BF16)
    seg = jnp.arange(MXU_TILE, dtype=jnp.int32) // HEAD_DK
    mseg = (seg[:, None] == seg[None, :]).astype(BF16)
    n_sub = D_ATT // HEAD_DK
    row = lambda b, i: (b, i, 0)
    vec = lambda b, i: (b, 0, 0)
    full = lambda b, i: (0, 0)
    out_block = pl.BlockSpec((1, tm, D_LRU), row)
    return pl.pallas_call(
        _inproj_kernel,
        grid=(bsz, s // tm),
        in_specs=[pl.BlockSpec((1, tm, d), row),
                  pl.BlockSpec((1, 1, d), vec),
                  pl.BlockSpec((1, 1, d), vec),
                  pl.BlockSpec((1, d), full),
                  pl.BlockSpec((d, n), full),
                  pl.BlockSpec((D_ATT, d), full),
                  pl.BlockSpec((MXU_TILE, MXU_TILE), full),
                  pl.BlockSpec((1, D_ATT), full),
                  pl.BlockSpec((1, D_ATT), full)],
        out_specs=[out_block] * 4 + [pl.BlockSpec((1, D_ATT, tm), lambda b, i: (b, 0, i))],
        out_shape=[jax.ShapeDtypeStruct((bsz, s, D_LRU), F32)]
                  + [jax.ShapeDtypeStruct((bsz, s, D_LRU), BF16)] * 3
                  + [jax.ShapeDtypeStruct((bsz, D_ATT, s), BF16)],
        compiler_params=_params(("parallel", "parallel"), VMEM_LIMIT),
        name="norm1_in_proj",
    )(x, scale1, shift1, g_norm1.reshape(1, d), w_main, w_vt, mseg,
      jnp.tile(g_q, n_sub).reshape(1, D_ATT), jnp.tile(g_k, n_sub).reshape(1, D_ATT))


def _lru_kernel(x_ref, gz_ref, cw_ref, cb_ref, wa_ref, wx_ref, ba_ref, bx_ref, lam_ref, y_ref,
                xpad, a_f, u_f, a_b, u_b, *, tc):
    s = x_ref.shape[1]
    c = x_ref.shape[2]
    n_chunks = s // tc
    n_slab = c // 128
    pitch = a_f.shape[1] // n_chunks
    zeros8 = jnp.zeros((8, c), F32)
    xpad[0:8, :] = zeros8
    xpad[s + 8:s + 16, :] = zeros8

    def fill(ci, carry):
        t0 = pl.multiple_of(ci * tc, tc)
        xpad[pl.ds(t0 + 8, tc), :] = x_ref[0, pl.ds(t0, tc), :]
        return carry

    lax.fori_loop(0, n_chunks, fill, 0)

    cw = cw_ref[...]
    cb = cb_ref[...]
    decay = []
    for d in range(2):
        lam = lam_ref[d]
        softplus_neg = jnp.maximum(-lam, 0.0) + jnp.log(1.0 + jnp.exp(-jnp.abs(lam)))
        decay.append((-LRU_C * LOG2E) * softplus_neg)
    a_scr = (a_f, a_b)
    u_scr = (u_f, u_b)

    def gates(ci, carry):
        t0 = pl.multiple_of(ci * tc, tc)
        xw = xpad[pl.ds(t0, tc + 16), :]
        xc = (cw[0:1] * pltpu.roll(xw, 2, 0)[8:8 + tc]
              + cw[1:2] * pltpu.roll(xw, 1, 0)[8:8 + tc]
              + cw[2:3] * xw[8:8 + tc]
              + cw[3:4] * pltpu.roll(xw, tc + 15, 0)[8:8 + tc]) + cb
        xcb = xc.astype(BF16)
        for d in range(2):
            r = _sigmoid(jnp.dot(xcb, wa_ref[d], preferred_element_type=F32) + ba_ref[d])
            i = _sigmoid(jnp.dot(xcb, wx_ref[d], preferred_element_type=F32) + bx_ref[d])
            a = jnp.exp2(r * decay[d])
            v = 1.0 - a * a
            u = (v * lax.rsqrt(jnp.maximum(v, F32_TINY))) * (i * xc)
            r0 = pl.multiple_of(ci * pitch, 8)
            for sl in range(n_slab):
                a_scr[d][sl, pl.ds(r0, tc), :] = a[:, sl * 128:(sl + 1) * 128]
                u_scr[d][sl, pl.ds(r0, tc), :] = u[:, sl * 128:(sl + 1) * 128]
        return carry

    lax.fori_loop(0, n_chunks, gates, 0, unroll=2)

    def rows(tt):
        return pl.ds(tt, n_chunks, stride=pitch)

    def step(tt, carry):
        out = []
        for d in range(2):
            t_loc = tt if d == 0 else tc - 1 - tt
            for sl in range(n_slab):
                h, p = carry[len(out)]
                a8 = a_scr[d][sl, rows(t_loc), :]
                h = a8 * h + u_scr[d][sl, rows(t_loc), :]
                p = a8 * p
                u_scr[d][sl, rows(t_loc), :] = h
                a_scr[d][sl, rows(t_loc), :] = p
                out.append((h, p))
        return tuple(out)

    init = (jnp.zeros((n_chunks, 128), F32), jnp.ones((n_chunks, 128), F32))
    lax.fori_loop(0, tc, step, (init,) * (2 * n_slab), unroll=4)

    for sl in range(n_slab):
        lanes = slice(sl * 128, (sl + 1) * 128)
        h_end, p_end = u_f[sl, rows(tc - 1), :], a_f[sl, rows(tc - 1), :]
        h_beg, p_beg = u_b[sl, rows(0), :], a_b[sl, rows(0), :]
        carry_f = [jnp.zeros((1, 128), F32)]
        for r in range(1, n_chunks):
            carry_f.append(p_end[r - 1:r] * carry_f[-1] + h_end[r - 1:r])
        carry_b = [jnp.zeros((1, 128), F32)]
        for r in range(n_chunks - 2, -1, -1):
            carry_b.insert(0, p_beg[r + 1:r + 2] * carry_b[0] + h_beg[r + 1:r + 2])
        for r in range(n_chunks):
            blk = slice(r * pitch, r * pitch + tc)
            hsum = (u_f[sl, blk, :] + a_f[sl, blk, :] * carry_f[r]
                    + u_b[sl, blk, :] + a_b[sl, blk, :] * carry_b[r])
            y_ref[0, r * tc:(r + 1) * tc, lanes] = (hsum * gz_ref[0, r * tc:(r + 1) * tc, lanes].astype(F32)).astype(BF16)


def _block_diag(w, half):
    n_dir, n_blocks, blk, _ = w.shape
    per = half // blk
    n_half = n_blocks // per
    w = w.reshape(n_dir, n_half, per, blk, blk)
    eye = jnp.eye(per, dtype=w.dtype)
    out = w[:, :, :, :, None, :] * eye[None, None, :, None, :, None]
    return out.reshape(n_dir, n_half, half, half)


def _rg_lru(x_lru, gz, conv_w, conv_b, w_a, b_a, w_x, b_x, lam):
    bsz, s, c = x_lru.shape
    half = LRU_HALF
    tc = s // SCAN_LANES
    pitch = tc + 8 if (tc // 8) % 2 == 0 else tc + 16
    n_half = c // half
    wa = _block_diag(w_a, half).astype(BF16)
    wx = _block_diag(w_x, half).astype(BF16)
    seq = lambda b, p: (b, 0, p)
    chan = lambda b, p: (0, p)
    dirchan = lambda b, p: (0, 0, p)
    blk = lambda b, p: (0, p, 0, 0)
    return pl.pallas_call(
        functools.partial(_lru_kernel, tc=tc),
        grid=(bsz, n_half),
        in_specs=[pl.BlockSpec((1, s, half), seq),
                  pl.BlockSpec((1, s, half), seq),
                  pl.BlockSpec((CONV_W, half), chan),
                  pl.BlockSpec((1, half), chan),
                  pl.BlockSpec((2, None, half, half), blk),
                  pl.BlockSpec((2, None, half, half), blk),
                  pl.BlockSpec((2, 1, half), dirchan),
                  pl.BlockSpec((2, 1, half), dirchan),
                  pl.BlockSpec((2, 1, half), dirchan)],
        out_specs=pl.BlockSpec((1, s, half), seq),
        out_shape=jax.ShapeDtypeStruct((bsz, s, c), BF16),
        scratch_shapes=[pltpu.VMEM((s + 16, half), F32)]
                       + [pltpu.VMEM((half // 128, SCAN_LANES * pitch, 128), F32)] * 4,
        compiler_params=_params(("parallel", "parallel"), VMEM_LIMIT),
        name="rg_lru",
    )(x_lru, gz, conv_w.reshape(CONV_W, c), conv_b.reshape(1, c), wa, wx,
      b_a.reshape(2, 1, c), b_x.reshape(2, 1, c), lam.reshape(2, 1, c))


def _attn_kernel(q_ref, k_ref, vt_ref, bias_ref, lq_ref, go_ref, w1_ref, w3_ref, w2_ref,
                 o_ref, w1b_ref, w3b_ref, w2b_ref,
                 s_even, s_odd, m_even, m_odd, o_even, o_odd, p_scr, *, lam_init):
    for w_ref, wb_ref in ((w1_ref, w1b_ref), (w3_ref, w3b_ref), (w2_ref, w2b_ref)):
        wb_ref[...] = w_ref[...].astype(BF16)

    t = bias_ref.shape[-1]
    width = s_even.shape[0]
    n_k = k_ref.shape[1] // t
    n_q = q_ref.shape[1] // t
    n_steps = n_q // width
    subs = [(w, u) for w in range(width) for u in range(2)]
    chunk = 2
    pv_chunk = 2
    n_c = n_k // chunk

    s_odd[...] = jnp.zeros_like(s_odd)
    m_odd[...] = jnp.zeros_like(m_odd)
    o_odd[...] = jnp.ones_like(o_odd)
    even = dict(s=s_even, m=m_even, o=o_even)
    odd = dict(s=s_odd, m=m_odd, o=o_odd)
    ones_rows = jnp.ones((PV_ONES, pv_chunk * t), BF16)

    def fold(x, op):
        parts = [x[r * 8:(r + 1) * 8, :] for r in range(t // 8)]
        acc = parts[:2]
        for r in range(2, len(parts)):
            acc[r % 2] = op(acc[r % 2], parts[r])
        return op(acc[0], acc[1])

    def bias_tile(j, tile):
        return bias_ref[0, jnp.clip(j - tile, -2, 2) + 2]

    def finish(prev, w, tile):
        outs = [prev['o'][w, u, :HEAD_DV, :] / prev['o'][w, u, HEAD_DV:HEAD_DV + 1, :] for u in range(2)]
        lq = lq_ref[...]
        lam = (jnp.exp(jnp.sum(lq[0:1] * lq[1:2], axis=-1, keepdims=True))
               - jnp.exp(jnp.sum(lq[2:3] * lq[3:4], axis=-1, keepdims=True)) + lam_init)
        o = (outs[0] - lam * outs[1]).T
        ms = jnp.mean(o * o, axis=-1, keepdims=True)
        o_ref[0, pl.ds(pl.multiple_of(tile * t, t), t), :] = (
            (o * lax.rsqrt(ms + EPS) * go_ref[...]) * (1.0 - lam_init)).astype(BF16)

    def step(i, cur, prev, scores=True, softmax=True):
        tiles_a = [i * width + w for w in range(width)]
        for w in range(width):
            finish(prev, w, jnp.maximum((i - 2) * width + w, 0))
        s_cur, s_prev = cur['s'], prev['s']
        m_b = {wu: jnp.max(prev['m'][wu], axis=0, keepdims=True) for wu in subs} if softmax else {}
        m_acc = {wu: jnp.full((8, t), -jnp.inf, F32) for wu in subs}
        q_sub = {}
        if scores:
            for w in range(width):
                q = q_ref[0, pl.ds(pl.multiple_of(tiles_a[w] * t, t), t), :]
                lane = lax.broadcasted_iota(jnp.int32, q.shape, 1)
                zero = jnp.zeros_like(q)
                q_sub[w, 0] = jnp.where(lane < HEAD_DK, q, zero)
                q_sub[w, 1] = jnp.where(lane >= HEAD_DK, q, zero)
        for c in range(n_c):
            rows = slice(c * chunk * t, (c + 1) * chunk * t)
            for w, u in (subs if scores else []):
                sc_all = lax.dot_general(k_ref[0, rows, :], q_sub[w, u], NT_DIMS, preferred_element_type=F32)
                for jj in range(chunk):
                    j = c * chunk + jj
                    sc = sc_all[jj * t:(jj + 1) * t, :] + bias_tile(j, tiles_a[w])
                    s_cur[w, u, j] = sc
                    m_acc[w, u] = jnp.maximum(m_acc[w, u], fold(sc, jnp.maximum))
            for w, u in (subs if softmax else []):
                for jj in range(chunk):
                    j = c * chunk + jj
                    p = jnp.exp2(s_prev[w, u, j] - m_b[w, u])
                    p_scr[w, u, j * t:(j + 1) * t, :] = p.astype(BF16)
                if ((c + 1) * chunk) % pv_chunk == 0:
                    keys = slice(((c + 1) * chunk - pv_chunk) * t, (c + 1) * chunk * t)
                    part = jnp.dot(jnp.concatenate([vt_ref[0, :, keys], ones_rows], axis=0), p_scr[w, u, keys, :],
                                   preferred_element_type=F32)
                    if (c + 1) * chunk == pv_chunk:
                        cur['o'][w, u] = part
                    else:
                        cur['o'][w, u] += part

        if scores:
            for w, u in subs:
                cur['m'][w, u] = m_acc[w, u]

    def pair(pi, carry):
        @pl.when(pi >= 0)
        def _():
            step(2 * pi, even, odd)

        @pl.when(pi < n_steps)
        def _():
            step(2 * pi + 1, odd, even)

        return carry

    lax.fori_loop(0, n_steps // 2, pair, 0)
    step(n_steps, even, odd, scores=False)
    for w in range(width):
        finish(even, w, n_q - width + w)


def _diff_attention(qn, kn, vt, bias_tiles, lambda_qk, g_o, lam_init, expert_w):
    bsz, s, _ = qn.shape
    w_rows = expert_w[0].shape[0] * expert_w[0].shape[1]
    w_cols = expert_w[0].shape[2]
    assert all(w.shape == expert_w[0].shape for w in expert_w) and w_rows % (bsz * N_HEADS * 16) == 0
    w_blk = pl.BlockSpec((w_rows // (bsz * N_HEADS), w_cols), lambda b, h: (b * N_HEADS + h, 0))
    t = bias_tiles.shape[-1]
    n_k = s // t
    width = ATT_WIDTH
    assert (s // t) % (2 * width) == 0, "the query-tile pipeline advances two groups of tiles per loop trip"
    scores = pltpu.VMEM((width, 2, n_k, t, t), F32)
    maxima = pltpu.VMEM((width, 2, 8, t), F32)
    pv_acc = pltpu.VMEM((width, 2, HEAD_DV + PV_ONES, t), F32)
    seq = lambda b, h: (b, 0, h)
    outs = pl.pallas_call(
        functools.partial(_attn_kernel, lam_init=lam_init),
        grid=(bsz, N_HEADS),
        in_specs=[pl.BlockSpec((1, s, HEAD_DV), seq),
                  pl.BlockSpec((1, s, HEAD_DV), seq),
                  pl.BlockSpec((1, HEAD_DV, s), lambda b, h: (b, h, 0)),
                  pl.BlockSpec((1, 5, t, t), lambda b, h: (h, 0, 0, 0)),
                  pl.BlockSpec((4, HEAD_DK), lambda b, h: (0, 0)),
                  pl.BlockSpec((1, HEAD_DV), lambda b, h: (0, 0)),
                  w_blk, w_blk, w_blk],
        out_specs=[pl.BlockSpec((1, s, HEAD_DV), seq), w_blk, w_blk, w_blk],
        out_shape=[jax.ShapeDtypeStruct((bsz, s, D_ATT), BF16)]
                  + [jax.ShapeDtypeStruct((w_rows, w_cols), BF16)] * 3,
        scratch_shapes=[scores, scores, maxima, maxima, pv_acc, pv_acc, pltpu.VMEM((width, 2, s, t), BF16)],
        compiler_params=_params(("parallel", "parallel"), VMEM_LIMIT),
        name="diff_attention",
    )(qn, kn, vt, bias_tiles, lambda_qk, g_o.reshape(1, HEAD_DV), *[w.reshape(w_rows, w_cols) for w in expert_w])
    return outs[0], [wb.reshape(expert_w[0].shape) for wb in outs[1:]]


def _outproj_kernel(yl_ref, ya_ref, x_ref, gate_ref, sc_ref, sh_ref, g2_ref, wo_ref, wr_ref,
                    x1_ref, h2_ref, aff_ref):
    mix = (jnp.dot(yl_ref[0], wo_ref[0:D_LRU, :], preferred_element_type=F32)
           + jnp.dot(ya_ref[0], wo_ref[D_LRU:D_LRU + D_ATT, :], preferred_element_type=F32))
    x1 = x_ref[0] + gate_ref[0] * mix
    x1_ref[0] = x1
    ms = jnp.mean(x1 * x1, axis=-1, keepdims=True)
    h2 = (x1 * lax.rsqrt(ms + EPS) * g2_ref[...]) * (1.0 + sc_ref[0]) + sh_ref[0]
    h2b = h2.astype(BF16)
    h2_ref[0] = h2b
    logits = lax.dot_general(wr_ref[...], h2b, NT_DIMS, preferred_element_type=F32)
    ex = jnp.exp(logits - jnp.max(logits, axis=0, keepdims=True))
    aff_ref[0] = ex / jnp.sum(ex, axis=0, keepdims=True)


def _out_projection(y_lru, y_att, x, gate1, scale2, shift2, g_norm2, w_out, w_router, tm):
    bsz, s, d = x.shape
    row = lambda b, i: (b, i, 0)
    vec = lambda b, i: (b, 0, 0)
    full = lambda b, i: (0, 0)
    return pl.pallas_call(
        _outproj_kernel,
        grid=(bsz, s // tm),
        in_specs=[pl.BlockSpec((1, tm, D_LRU), row),
                  pl.BlockSpec((1, tm, D_ATT), row),
                  pl.BlockSpec((1, tm, d), row),
                  pl.BlockSpec((1, 1, d), vec),
                  pl.BlockSpec((1, 1, d), vec),
                  pl.BlockSpec((1, 1, d), vec),
                  pl.BlockSpec((1, d), full),
                  pl.BlockSpec((D_LRU + D_ATT, d), full),
                  pl.BlockSpec((N_EXPERTS, d), full)],
        out_specs=[pl.BlockSpec((1, tm, d), row),
                   pl.BlockSpec((1, tm, d), row),
                   pl.BlockSpec((1, N_EXPERTS, tm), lambda b, i: (b, 0, i))],
        out_shape=[jax.ShapeDtypeStruct((bsz, s, d), F32),
                   jax.ShapeDtypeStruct((bsz, s, d), BF16),
                   jax.ShapeDtypeStruct((bsz, N_EXPERTS, s), F32)],
        compiler_params=_params(("parallel", "parallel"), VMEM_LIMIT),
        name="out_proj_norm2_router",
    )(y_lru, y_att, x, gate1, scale2, shift2, g_norm2.reshape(1, d), w_out.astype(BF16),
      w_router.T.astype(BF16))


def _route_kernel(aff_ref, pos_ref, cnt_ref, *, cap, n_tok_chunks):
    aff = aff_ref[0]
    n_e, s = aff.shape
    bits = lax.bitcast_convert_type(aff, jnp.int32)
    capf = float(cap)

    def count(mask):
        return jnp.sum(jnp.where(mask, 1.0, 0.0), axis=-1, keepdims=True)

    tau = jnp.zeros((n_e, 1), jnp.int32)
    for bit in range(30, -1, -1):
        cand = tau | (1 << bit)
        tau = jnp.where(count(bits >= cand) >= capf, cand, tau)
    gt = bits > tau
    eq = bits == tau
    need = capf - count(gt)

    blk = min(256, s // n_tok_chunks)
    r = lax.broadcasted_iota(jnp.int32, (blk, blk), 0)
    cidx = lax.broadcasted_iota(jnp.int32, (blk, blk), 1)
    upper = jnp.where(r < cidx, 1.0, 0.0).astype(BF16)

    def prefix_blocks(mask):
        off = jnp.zeros((n_e, 1), F32)
        pieces, offs = [], []
        for k in range(s // blk):
            mb = jnp.where(mask[:, k * blk:(k + 1) * blk], 1.0, 0.0)
            offs.append(off)
            pieces.append(jnp.dot(mb.astype(BF16), upper, preferred_element_type=F32) + off)
            off = off + jnp.sum(mb, axis=-1, keepdims=True)
        return pieces, offs

    eq_rank, _ = prefix_blocks(eq)
    sel_blocks = []
    for k in range(s // blk):
        sl = slice(k * blk, (k + 1) * blk)
        sel_blocks.append(jnp.logical_or(gt[:, sl], jnp.logical_and(eq[:, sl], eq_rank[k] < need)))
    sel = jnp.concatenate(sel_blocks, axis=1)
    slot, offs = prefix_blocks(sel)
    for k in range(s // blk):
        pos_ref[0, :, k * blk:(k + 1) * blk] = jnp.where(sel_blocks[k], slot[k], -1.0).astype(jnp.int32)

    lane = lax.broadcasted_iota(jnp.int32, (n_e, 128), 1)
    cnt = jnp.zeros((n_e, 128), F32)
    per = (s // n_tok_chunks) // blk
    for j in range(n_tok_chunks):
        cnt = jnp.where(lane == j, offs[j * per], cnt)
    cnt_ref[0] = cnt.astype(jnp.int32)


def _routing(aff, cap):
    bsz, n_e, s = aff.shape
    group = math.gcd(bsz, ROUTE_GROUP)
    rows = group * n_e
    pos, cnt = pl.pallas_call(
        functools.partial(_route_kernel, cap=cap, n_tok_chunks=N_TOK_CHUNKS),
        grid=(bsz // group,),
        in_specs=[pl.BlockSpec((1, rows, s), lambda b: (b, 0, 0))],
        out_specs=[pl.BlockSpec((1, rows, s), lambda b: (b, 0, 0)),
                   pl.BlockSpec((1, rows, 128), lambda b: (b, 0, 0))],
        out_shape=[jax.ShapeDtypeStruct((bsz // group, rows, s), jnp.int32),
                   jax.ShapeDtypeStruct((bsz // group, rows, 128), jnp.int32)],
        compiler_params=_params(("parallel",)),
        name="expert_choice_routing",
    )(aff.reshape(bsz // group, rows, s))
    return pos.reshape(bsz, n_e, s), cnt.reshape(bsz, n_e, 128)


def _chunking(pos):
    return pos.shape[1], N_TOK_CHUNKS, pos.shape[2] // N_TOK_CHUNKS


def _chunk_row(ref, e, j):
    t = _chunking(ref)[2]
    return ref[0, pl.ds(e, 1), pl.ds(pl.multiple_of(j * t, t), t)]


def _slot_windows(cnt_ref, pos_ref, b, j, k, cap):
    n_e, n_j, t = _chunking(pos_ref)
    slot_iota = lax.broadcasted_iota(jnp.int32, (MOE_WINDOW, t), 0)
    sels, firsts = [], []
    for e in range(n_e):
        lo = cnt_ref[(b * n_e + e) * n_j + j]
        start = lax.shift_left(lax.shift_right_logical(lo, 4), 4) + k * MOE_WINDOW
        w0 = jnp.minimum(start, cap - MOE_WINDOW)
        slots = slot_iota + w0
        sels.append(jnp.logical_and(_chunk_row(pos_ref, e, j) == slots, slots >= start))
        firsts.append(pl.multiple_of(e * cap + w0, 16))
    return sels, firsts


def _stacked_one_hot(sels):
    return jnp.concatenate([jnp.where(sel, 1.0, 0.0).astype(BF16) for sel in sels], axis=0)


def _window_passes(cnt_ref, pos_ref, b, j, cap):
    n_e, n_j, _ = _chunking(pos_ref)
    n_win = jnp.int32(1)
    for e in range(n_e):
        at = (b * n_e + e) * n_j
        start = lax.shift_left(lax.shift_right_logical(cnt_ref[at + j], 4), 4)
        hi = jnp.where(j + 1 < n_j, cnt_ref[at + jnp.minimum(j + 1, n_j - 1)], cap)
        n_win = jnp.maximum(n_win, lax.shift_right_logical(hi - start + (MOE_WINDOW - 1), MOE_WINDOW.bit_length() - 1))
    return n_win


def _dispatch_kernel(cnt_ref, pos_ref, aff_ref, h2_ref, xe_ref, g_ref, *, cap):
    b = pl.program_id(0)
    t = _chunking(pos_ref)[2]

    @pl.when(pl.program_id(1) == 0)
    def _():
        xe_ref[...] = jnp.zeros_like(xe_ref)
        g_ref[...] = jnp.zeros_like(g_ref)

    def chunk(jj, carry):
        j = pl.program_id(1) * MOE_CHUNKS + jj
        toks = pl.ds(pl.multiple_of(jj * t, t), t)

        def one_pass(k, inner):
            sels, firsts = _slot_windows(cnt_ref, pos_ref, b, j, k, cap)
            rows = jnp.dot(_stacked_one_hot(sels), h2_ref[0, toks, :], preferred_element_type=F32).astype(BF16)
            for e, (sel, first) in enumerate(zip(sels, firsts)):
                window = pl.ds(first, MOE_WINDOW)
                xe_ref[0, window, :] += rows[e * MOE_WINDOW:(e + 1) * MOE_WINDOW, :]
                weight = jnp.sum(jnp.where(sel, _chunk_row(aff_ref, e, j), 0.0), axis=-1, keepdims=True)
                g_ref[0, window, :] += jnp.broadcast_to(weight, (MOE_WINDOW, g_ref.shape[-1]))
            return inner

        one_pass(0, 0)
        lax.fori_loop(1, _window_passes(cnt_ref, pos_ref, b, j, cap), one_pass, 0)
        return carry

    lax.fori_loop(0, MOE_CHUNKS, chunk, 0)


def _dispatch(cnt, pos, aff, h2, cap):
    bsz, s, d = h2.shape
    n_e, n_j, t = _chunking(pos)
    per_token = pl.BlockSpec((1, n_e, s), lambda b, j, c: (b, 0, 0))
    grid_spec = pltpu.PrefetchScalarGridSpec(
        num_scalar_prefetch=1,
        grid=(bsz, n_j // MOE_CHUNKS),
        in_specs=[per_token, per_token,
                  pl.BlockSpec((1, MOE_CHUNKS * t, d), lambda b, j, c: (b, j, 0))],
        out_specs=[pl.BlockSpec((1, n_e * cap, d), lambda b, j, c: (b, 0, 0)),
                   pl.BlockSpec((1, n_e * cap, LANES), lambda b, j, c: (b, 0, 0))],
    )
    return pl.pallas_call(
        functools.partial(_dispatch_kernel, cap=cap),
        grid_spec=grid_spec,
        out_shape=[jax.ShapeDtypeStruct((bsz, n_e * cap, d), BF16),
                   jax.ShapeDtypeStruct((bsz, n_e * cap, LANES), F32)],
        compiler_params=_params(("parallel", "arbitrary"), VMEM_LIMIT),
        name="expert_dispatch",
    )(cnt.reshape(-1), pos, aff, h2)


def _moe_kernel(xe_ref, g_ref, w1_ref, w3_ref, w2_ref, gate_ref, y_ref):
    xe = xe_ref[0]
    a = jnp.dot(xe, w1_ref[0], preferred_element_type=F32)
    gate = jnp.dot(xe, w3_ref[0], preferred_element_type=F32)
    hmid = ((a * _sigmoid(a)) * gate).astype(BF16)
    y = jnp.dot(hmid, w2_ref[0], preferred_element_type=F32) * g_ref[0][:, :1] * gate_ref[0]
    y_ref[0] = y.astype(BF16)


def _moe(xe, g, w1, w3, w2, gate2, cap):
    bsz, _, d = xe.shape
    n_e, _, f = w1.shape
    wspec = lambda b, e: (e, 0, 0)
    slot_rows = lambda b, e: (b, e, 0)
    return pl.pallas_call(
        _moe_kernel,
        grid=(bsz, n_e),
        in_specs=[pl.BlockSpec((1, cap, d), slot_rows),
                  pl.BlockSpec((1, cap, g.shape[-1]), slot_rows),
                  pl.BlockSpec((1, d, f), wspec),
                  pl.BlockSpec((1, d, f), wspec),
                  pl.BlockSpec((1, f, d), wspec),
                  pl.BlockSpec((1, 1, d), lambda b, e: (b, 0, 0))],
        out_specs=pl.BlockSpec((1, cap, d), slot_rows),
        out_shape=jax.ShapeDtypeStruct((bsz, n_e * cap, d), BF16),
        compiler_params=_params(("parallel", "parallel"), VMEM_LIMIT),
        name="expert_choice_ffn",
    )(xe, g, w1, w3, w2, gate2)


def _combine_kernel(cnt_ref, pos_ref, y_ref, x1_ref, out_ref, *, cap):
    b = pl.program_id(0)
    t = _chunking(pos_ref)[2]

    def chunk(jj, carry):
        j = pl.program_id(1) * MOE_CHUNKS + jj
        toks = pl.ds(pl.multiple_of(jj * t, t), t)

        def windows(k):
            sels, firsts = _slot_windows(cnt_ref, pos_ref, b, j, k, cap)
            rows = jnp.concatenate([y_ref[0, pl.ds(first, MOE_WINDOW), :] for first in firsts], axis=0)
            return lax.dot_general(_stacked_one_hot(sels), rows, TN_DIMS, preferred_element_type=F32)

        out_ref[0, toks, :] = x1_ref[0, toks, :] + windows(0)

        def more(k, inner):
            out_ref[0, toks, :] += windows(k)
            return inner

        lax.fori_loop(1, _window_passes(cnt_ref, pos_ref, b, j, cap), more, 0)
        return carry

    lax.fori_loop(0, MOE_CHUNKS, chunk, 0)


def _combine(cnt, pos, y, x1, cap):
    bsz, s, d = x1.shape
    n_e, n_j, t = _chunking(pos)
    grid_spec = pltpu.PrefetchScalarGridSpec(
        num_scalar_prefetch=1,
        grid=(bsz, n_j // MOE_CHUNKS),
        in_specs=[pl.BlockSpec((1, n_e, s), lambda b, j, c: (b, 0, 0)),
                  pl.BlockSpec((1, n_e * cap, d), lambda b, j, c: (b, 0, 0)),
                  pl.BlockSpec((1, MOE_CHUNKS * t, d), lambda b, j, c: (b, j, 0))],
        out_specs=pl.BlockSpec((1, MOE_CHUNKS * t, d), lambda b, j, c: (b, j, 0)),
    )
    return pl.pallas_call(
        functools.partial(_combine_kernel, cap=cap),
        grid_spec=grid_spec,
        out_shape=jax.ShapeDtypeStruct((bsz, s, d), F32),
        compiler_params=_params(("parallel", "parallel"), VMEM_LIMIT),
        name="expert_combine",
    )(cnt.reshape(-1), pos, y, x1)


def kernel(x, c, w_mod, b_mod, g_norm1, w_in, conv_w, conv_b, lru_w_a, lru_b_a, lru_w_x, lru_b_x,
           lru_lambda, g_q, g_k, lambda_qk, g_attn_out, rel_bias, w_out, g_norm2, w_router, w1, w3, w2):
    bsz, s, d = x.shape
    depth = w_mod.shape[0]
    cap = max(1, EC_FACTOR * s // N_EXPERTS)
    bias_tiles = _bias_tiles(rel_bias, min(ATT_TILE, s))
    for l in range(depth):
        mod = _modulation(c, w_mod[l], b_mod[l])
        shift1, scale1, gate1, shift2, scale2, gate2 = [m.reshape(bsz, 1, d) for m in jnp.split(mod, 6, axis=-1)]
        x_lru, gz, qn, kn, vt = _in_projection(x, scale1, shift1, g_norm1[l], w_in[l], g_q[l], g_k[l],
                                               min(IN_PROJ_ROWS, s))
        y_lru = _rg_lru(x_lru, gz, conv_w[l], conv_b[l], lru_w_a[l], lru_b_a[l], lru_w_x[l], lru_b_x[l],
                        lru_lambda[l])
        lam_init = 0.8 - 0.6 * math.exp(-0.3 * l)
        y_att, (w1b, w3b, w2b) = _diff_attention(qn, kn, vt, bias_tiles, lambda_qk[l], g_attn_out[l], lam_init,
                                                 (w1[l], w3[l], w2[l]))
        x1, h2, aff = _out_projection(y_lru, y_att, x, gate1, scale2, shift2, g_norm2[l], w_out[l],
                                      w_router[l], min(OUT_PROJ_ROWS, s))
        pos, cnt = _routing(aff, cap)
        cnt = cnt[:, :, :N_TOK_CHUNKS]
        xe, g = _dispatch(cnt, pos, aff, h2, cap)
        y = _moe(xe, g, w1b, w3b, w2b, gate2, cap)
        x = _combine(cnt, pos, y, x1, cap)
    return x
```

```python
import functools
import math

import jax
import jax.numpy as jnp
from jax import lax
from jax.experimental import pallas as pl
from jax.experimental.pallas import tpu as pltpu

F32 = jnp.float32
BF16 = jnp.bfloat16

D_MODEL = 1024
D_LRU = 512
LRU_BLOCK = 64
LRU_C = 8.0
CONV_W = 4
N_HEADS = 4
HEAD_DV = 128
HEAD_DK = 64
D_ATT = N_HEADS * HEAD_DV
N_BUCKETS = 32
N_EXPERTS = 16
EC_FACTOR = 2
EPS = 1e-6
F32_TINY = 2.0 ** -126
LOG2E = math.log2(math.e)

IN_PROJ_ROWS = 1024
OUT_PROJ_ROWS = 1024
LRU_HALF = 256
SCAN_LANES = 8
ATT_TILE = 256
PV_ONES = 16
ATT_WIDTH = 1
ROUTE_GROUP = 8
N_TOK_CHUNKS = 16
MXU_TILE = 256
LANES = 128
MOE_WINDOW = 64
MOE_CHUNKS = 4
VMEM_LIMIT = 56 * 1024 * 1024

NT_DIMS = (((1,), (1,)), ((), ()))
TN_DIMS = (((0,), (0,)), ((), ()))


def _sigmoid(x):
    return 1.0 / (1.0 + jnp.exp(-x))


def _params(sem, vmem=None):
    return pltpu.CompilerParams(dimension_semantics=sem, vmem_limit_bytes=vmem)


def _mod_kernel(c_ref, w_ref, b_ref, o_ref):
    c = c_ref[...]
    o_ref[...] = jnp.dot((c * _sigmoid(c)).astype(BF16), w_ref[...].astype(BF16),
                         preferred_element_type=F32) + b_ref[...]


def _modulation(c, w_mod, b_mod):
    bsz, d = c.shape
    n = w_mod.shape[1]
    return pl.pallas_call(
        _mod_kernel,
        grid=(n // d,),
        in_specs=[pl.BlockSpec((bsz, d), lambda j: (0, 0)),
                  pl.BlockSpec((d, d), lambda j: (0, j)),
                  pl.BlockSpec((1, d), lambda j: (0, j))],
        out_specs=pl.BlockSpec((bsz, d), lambda j: (0, j)),
        out_shape=jax.ShapeDtypeStruct((bsz, n), F32),
        compiler_params=_params(("arbitrary",)),
        name="adaln_mod",
    )(c, w_mod, b_mod.reshape(1, n))


def _bias_kernel(tab_ref, o_ref):
    h = pl.program_id(0)
    t = o_ref.shape[-1]
    key = lax.broadcasted_iota(jnp.int32, (t, t), 0)
    qry = lax.broadcasted_iota(jnp.int32, (t, t), 1)
    half = N_BUCKETS // 2
    max_exact = half // 2
    for d in range(5):
        if d == 0:
            o_ref[0, d] = jnp.full((t, t), tab_ref[half - 1, h] * LOG2E, F32)
        elif d == 4:
            o_ref[0, d] = jnp.full((t, t), tab_ref[N_BUCKETS - 1, h] * LOG2E, F32)
        else:
            rel = (d - 2) * t + key - qry
            n = jnp.abs(rel)
            n2 = n * n
            large = jnp.full((t, t), max_exact, jnp.int32)
            for k in range(1, half - max_exact):
                large = large + jnp.where(n2 >= (max_exact * max_exact) * (2 ** k), 1, 0)
            idx = jnp.where(n < max_exact, n, large) + jnp.where(rel > 0, half, 0)
            val = jnp.zeros((t, t), F32)
            for j in range(N_BUCKETS):
                val = jnp.where(idx == j, tab_ref[j, h] * LOG2E, val)
            o_ref[0, d] = val


def _bias_tiles(rel_bias, t):
    return pl.pallas_call(
        _bias_kernel,
        grid=(N_HEADS,),
        in_specs=[pl.BlockSpec(memory_space=pltpu.SMEM)],
        out_specs=pl.BlockSpec((1, 5, t, t), lambda h: (h, 0, 0, 0)),
        out_shape=jax.ShapeDtypeStruct((N_HEADS, 5, t, t), F32),
        compiler_params=_params(("arbitrary",)),
        name="t5_bias_tiles",
    )(rel_bias)


def _inproj_kernel(x_ref, sc_ref, sh_ref, g1_ref, w_ref, wvt_ref, mseg_ref, gq_ref, gk_ref,
                   xl_ref, gz_ref, q_ref, k_ref, vt_ref):
    x = x_ref[0]
    ms = jnp.mean(x * x, axis=-1, keepdims=True)
    h = (x * lax.rsqrt(ms + EPS) * g1_ref[...]) * (1.0 + sc_ref[0]) + sh_ref[0]
    hb = h.astype(BF16)

    def proj(lo, width):
        return jnp.dot(hb, w_ref[:, lo:lo + width], preferred_element_type=F32)

    def qk_norm(t, g):
        tt = (t * t).astype(BF16)
        m = mseg_ref.shape[0]
        ss = jnp.concatenate([jnp.dot(tt[:, c:c + m], mseg_ref[...], preferred_element_type=F32)
                              for c in range(0, tt.shape[1], m)], axis=1)
        return t * lax.rsqrt(ss * (1.0 / HEAD_DK) + EPS) * g

    xl_ref[0] = proj(0, D_LRU)
    z = proj(D_LRU, D_LRU)
    cdf = 0.5 * (1.0 + jnp.tanh(math.sqrt(2.0 / math.pi) * (z + 0.044715 * (z * z * z))))
    gz_ref[0] = (z * cdf).astype(BF16)
    q_ref[0] = (qk_norm(proj(2 * D_LRU, D_ATT), gq_ref[...]) * (HEAD_DK ** -0.5 * LOG2E)).astype(BF16)
    k_ref[0] = qk_norm(proj(2 * D_LRU + D_ATT, D_ATT), gk_ref[...]).astype(BF16)
    vt_ref[0] = lax.dot_general(wvt_ref[...], hb, NT_DIMS,
                                preferred_element_type=F32).astype(BF16)


def _in_projection(x, scale1, shift1, g_norm1, w_in, g_q, g_k, tm):
    bsz, s, d = x.shape
    n = w_in.shape[1] - D_ATT
    w_main = w_in[:, :n].astype(BF16)
    w_vt = w_in[:, n:].T.astype(BF16)
    seg = jnp.arange(MXU_TILE, dtype=jnp.int32) // HEAD_DK
    mseg = (seg[:, None] == seg[None, :]).astype(BF16)
    n_sub = D_ATT // HEAD_DK
    row = lambda b, i: (b, i, 0)
    vec = lambda b, i: (b, 0, 0)
    full = lambda b, i: (0, 0)
    out_block = pl.BlockSpec((1, tm, D_LRU), row)
    return pl.pallas_call(
        _inproj_kernel,
        grid=(bsz, s // tm),
        in_specs=[pl.BlockSpec((1, tm, d), row),
                  pl.BlockSpec((1, 1, d), vec),
                  pl.BlockSpec((1, 1, d), vec),
                  pl.BlockSpec((1, d), full),
                  pl.BlockSpec((d, n), full),
                  pl.BlockSpec((D_ATT, d), full),
                  pl.BlockSpec((MXU_TILE, MXU_TILE), full),
                  pl.BlockSpec((1, D_ATT), full),
                  pl.BlockSpec((1, D_ATT), full)],
        out_specs=[out_block] * 4 + [pl.BlockSpec((1, D_ATT, tm), lambda b, i: (b, 0, i))],
        out_shape=[jax.ShapeDtypeStruct((bsz, s, D_LRU), F32)]
                  + [jax.ShapeDtypeStruct((bsz, s, D_LRU), BF16)] * 3
                  + [jax.ShapeDtypeStruct((bsz, D_ATT, s), BF16)],
        compiler_params=_params(("parallel", "parallel"), VMEM_LIMIT),
        name="norm1_in_proj",
    )(x, scale1, shift1, g_norm1.reshape(1, d), w_main, w_vt, mseg,
      jnp.tile(g_q, n_sub).reshape(1, D_ATT), jnp.tile(g_k, n_sub).reshape(1, D_ATT))


def _lru_kernel(x_ref, gz_ref, cw_ref, cb_ref, wa_ref, wx_ref, ba_ref, bx_ref, lam_ref, y_ref,
                xpad, a_f, u_f, a_b, u_b, *, tc):
    s = x_ref.shape[1]
    c = x_ref.shape[2]
    n_chunks = s // tc
    n_slab = c // 128
    pitch = a_f.shape[1] // n_chunks
    zeros8 = jnp.zeros((8, c), F32)
    xpad[0:8, :] = zeros8
    xpad[s + 8:s + 16, :] = zeros8

    def fill(ci, carry):
        t0 = pl.multiple_of(ci * tc, tc)
        xpad[pl.ds(t0 + 8, tc), :] = x_ref[0, pl.ds(t0, tc), :]
        return carry

    lax.fori_loop(0, n_chunks, fill, 0)

    cw = cw_ref[...]
    cb = cb_ref[...]
    decay = []
    for d in range(2):
        lam = lam_ref[d]
        softplus_neg = jnp.maximum(-lam, 0.0) + jnp.log(1.0 + jnp.exp(-jnp.abs(lam)))
        decay.append((-LRU_C * LOG2E) * softplus_neg)
    a_scr = (a_f, a_b)
    u_scr = (u_f, u_b)

    def gates(ci, carry):
        t0 = pl.multiple_of(ci * tc, tc)
        xw = xpad[pl.ds(t0, tc + 16), :]
        xc = (cw[0:1] * pltpu.roll(xw, 2, 0)[8:8 + tc]
              + cw[1:2] * pltpu.roll(xw, 1, 0)[8:8 + tc]
              + cw[2:3] * xw[8:8 + tc]
              + cw[3:4] * pltpu.roll(xw, tc + 15, 0)[8:8 + tc]) + cb
        xcb = xc.astype(BF16)
        for d in range(2):
            r = _sigmoid(jnp.dot(xcb, wa_ref[d], preferred_element_type=F32) + ba_ref[d])
            i = _sigmoid(jnp.dot(xcb, wx_ref[d], preferred_element_type=F32) + bx_ref[d])
            a = jnp.exp2(r * decay[d])
            v = 1.0 - a * a
            u = (v * lax.rsqrt(jnp.maximum(v, F32_TINY))) * (i * xc)
            r0 = pl.multiple_of(ci * pitch, 8)
            for sl in range(n_slab):
                a_scr[d][sl, pl.ds(r0, tc), :] = a[:, sl * 128:(sl + 1) * 128]
                u_scr[d][sl, pl.ds(r0, tc), :] = u[:, sl * 128:(sl + 1) * 128]
        return carry

    lax.fori_loop(0, n_chunks, gates, 0, unroll=2)

    def rows(tt):
        return pl.ds(tt, n_chunks, stride=pitch)

    def step(tt, carry):
        out = []
        for d in range(2):
            t_loc = tt if d == 0 else tc - 1 - tt
            for sl in range(n_slab):
                h, p = carry[len(out)]
                a8 = a_scr[d][sl, rows(t_loc), :]
                h = a8 * h + u_scr[d][sl, rows(t_loc), :]
                p = a8 * p
                u_scr[d][sl, rows(t_loc), :] = h
                a_scr[d][sl, rows(t_loc), :] = p
                out.append((h, p))
        return tuple(out)

    init = (jnp.zeros((n_chunks, 128), F32), jnp.ones((n_chunks, 128), F32))
    lax.fori_loop(0, tc, step, (init,) * (2 * n_slab), unroll=4)

    for sl in range(n_slab):
        lanes = slice(sl * 128, (sl + 1) * 128)
        h_end, p_end = u_f[sl, rows(tc - 1), :], a_f[sl, rows(tc - 1), :]
        h_beg, p_beg = u_b[sl, rows(0), :], a_b[sl, rows(0), :]
        carry_f = [jnp.zeros((1, 128), F32)]
        for r in range(1, n_chunks):
            carry_f.append(p_end[r - 1:r] * carry_f[-1] + h_end[r - 1:r])
        carry_b = [jnp.zeros((1, 128), F32)]
        for r in range(n_chunks - 2, -1, -1):
            carry_b.insert(0, p_beg[r + 1:r + 2] * carry_b[0] + h_beg[r + 1:r + 2])
        for r in range(n_chunks):
            blk = slice(r * pitch, r * pitch + tc)
            hsum = (u_f[sl, blk, :] + a_f[sl, blk, :] * carry_f[r]
                    + u_b[sl, blk, :] + a_b[sl, blk, :] * carry_b[r])
            y_ref[0, r * tc:(r + 1) * tc, lanes] = (hsum * gz_ref[0, r * tc:(r + 1) * tc, lanes].astype(F32)).astype(BF16)


def _block_diag(w, half):
    n_dir, n_blocks, blk, _ = w.shape
    per = half // blk
    n_half = n_blocks // per
    w = w.reshape(n_dir, n_half, per, blk, blk)
    eye = jnp.eye(per, dtype=w.dtype)
    out = w[:, :, :, :, None, :] * eye[None, None, :, None, :, None]
    return out.reshape(n_dir, n_half, half, half)


def _rg_lru(x_lru, gz, conv_w, conv_b, w_a, b_a, w_x, b_x, lam):
    bsz, s, c = x_lru.shape
    half = LRU_HALF
    tc = s // SCAN_LANES
    pitch = tc + 8 if (tc // 8) % 2 == 0 else tc + 16
    n_half = c // half
    wa = _block_diag(w_a, half).astype(BF16)
    wx = _block_diag(w_x, half).astype(BF16)
    seq = lambda b, p: (b, 0, p)
    chan = lambda b, p: (0, p)
    dirchan = lambda b, p: (0, 0, p)
    blk = lambda b, p: (0, p, 0, 0)
    return pl.pallas_call(
        functools.partial(_lru_kernel, tc=tc),
        grid=(bsz, n_half),
        in_specs=[pl.BlockSpec((1, s, half), seq),
                  pl.BlockSpec((1, s, half), seq),
                  pl.BlockSpec((CONV_W, half), chan),
                  pl.BlockSpec((1, half), chan),
                  pl.BlockSpec((2, None, half, half), blk),
                  pl.BlockSpec((2, None, half, half), blk),
                  pl.BlockSpec((2, 1, half), dirchan),
                  pl.BlockSpec((2, 1, half), dirchan),
                  pl.BlockSpec((2, 1, half), dirchan)],
        out_specs=pl.BlockSpec((1, s, half), seq),
        out_shape=jax.ShapeDtypeStruct((bsz, s, c), BF16),
        scratch_shapes=[pltpu.VMEM((s + 16, half), F32)]
                       + [pltpu.VMEM((half // 128, SCAN_LANES * pitch, 128), F32)] * 4,
        compiler_params=_params(("parallel", "parallel"), VMEM_LIMIT),
        name="rg_lru",
    )(x_lru, gz, conv_w.reshape(CONV_W, c), conv_b.reshape(1, c), wa, wx,
      b_a.reshape(2, 1, c), b_x.reshape(2, 1, c), lam.reshape(2, 1, c))


def _attn_kernel(q_ref, k_ref, vt_ref, bias_ref, lq_ref, go_ref, w1_ref, w3_ref, w2_ref,
                 o_ref, w1b_ref, w3b_ref, w2b_ref,
                 s_even, s_odd, m_even, m_odd, o_even, o_odd, p_scr, *, lam_init):
    for w_ref, wb_ref in ((w1_ref, w1b_ref), (w3_ref, w3b_ref), (w2_ref, w2b_ref)):
        wb_ref[...] = w_ref[...].astype(BF16)

    t = bias_ref.shape[-1]
    width = s_even.shape[0]
    n_k = k_ref.shape[1] // t
    n_q = q_ref.shape[1] // t
    n_steps = n_q // width
    subs = [(w, u) for w in range(width) for u in range(2)]
    chunk = 2
    pv_chunk = 2
    n_c = n_k // chunk

    even = dict(s=s_even, m=m_even, o=o_even)
    odd = dict(s=s_odd, m=m_odd, o=o_odd)
    ones_rows = jnp.ones((PV_ONES, pv_chunk * t), BF16)

    def fold(x, op):
        parts = [x[r * 8:(r + 1) * 8, :] for r in range(t // 8)]
        acc = parts[:2]
        for r in range(2, len(parts)):
            acc[r % 2] = op(acc[r % 2], parts[r])
        return op(acc[0], acc[1])

    def bias_tile(j, tile):
        return bias_ref[0, jnp.clip(j - tile, -2, 2) + 2]

    def finish(prev, w, tile):
        outs = [prev['o'][w, u, :HEAD_DV, :] / prev['o'][w, u, HEAD_DV:HEAD_DV + 1, :] for u in range(2)]
        lq = lq_ref[...]
        lam = (jnp.exp(jnp.sum(lq[0:1] * lq[1:2], axis=-1, keepdims=True))
               - jnp.exp(jnp.sum(lq[2:3] * lq[3:4], axis=-1, keepdims=True)) + lam_init)
        o = (outs[0] - lam * outs[1]).T
        ms = jnp.mean(o * o, axis=-1, keepdims=True)
        o_ref[0, pl.ds(pl.multiple_of(tile * t, t), t), :] = (
            (o * lax.rsqrt(ms + EPS) * go_ref[...]) * (1.0 - lam_init)).astype(BF16)

    def step(i, cur, prev, scores=True, softmax=True, store=True):
        tiles_a = [i * width + w for w in range(width)]
        for w in range(width if store else 0):
            finish(prev, w, jnp.maximum((i - 2) * width + w, 0))
        s_cur, s_prev = cur['s'], prev['s']
        m_b = {wu: jnp.max(prev['m'][wu], axis=0, keepdims=True) for wu in subs} if softmax else {}
        m_acc = {wu: jnp.full((8, t), -jnp.inf, F32) for wu in subs}
        q_sub = {}
        if scores:
            for w in range(width):
                q = q_ref[0, pl.ds(pl.multiple_of(tiles_a[w] * t, t), t), :]
                lane = lax.broadcasted_iota(jnp.int32, q.shape, 1)
                zero = jnp.zeros_like(q)
                q_sub[w, 0] = jnp.where(lane < HEAD_DK, q, zero)
                q_sub[w, 1] = jnp.where(lane >= HEAD_DK, q, zero)
        for c in range(n_c):
            rows = slice(c * chunk * t, (c + 1) * chunk * t)
            for w, u in (subs if scores else []):
                sc_all = lax.dot_general(k_ref[0, rows, :], q_sub[w, u], NT_DIMS, preferred_element_type=F32)
                for jj in range(chunk):
                    j = c * chunk + jj
                    sc = sc_all[jj * t:(jj + 1) * t, :] + bias_tile(j, tiles_a[w])
                    s_cur[w, u, j] = sc
                    m_acc[w, u] = jnp.maximum(m_acc[w, u], fold(sc, jnp.maximum))
            for w, u in (subs if softmax else []):
                for jj in range(chunk):
                    j = c * chunk + jj
                    p = jnp.exp2(s_prev[w, u, j] - m_b[w, u])
                    p_scr[w, u, j * t:(j + 1) * t, :] = p.astype(BF16)
                if ((c + 1) * chunk) % pv_chunk == 0:
                    keys = slice(((c + 1) * chunk - pv_chunk) * t, (c + 1) * chunk * t)
                    part = jnp.dot(jnp.concatenate([vt_ref[0, :, keys], ones_rows], axis=0), p_scr[w, u, keys, :],
                                   preferred_element_type=F32)
                    if (c + 1) * chunk == pv_chunk:
                        cur['o'][w, u] = part
                    else:
                        cur['o'][w, u] += part

        if scores:
            for w, u in subs:
                cur['m'][w, u] = m_acc[w, u]

    def pair(pi, carry, **stages):
        @pl.when(pi >= 0)
        def _():
            step(2 * pi, even, odd, **stages.get('even', {}))

        @pl.when(pi < n_steps)
        def _():
            step(2 * pi + 1, odd, even, **stages.get('odd', {}))

        return carry

    pair(jnp.minimum(pl.program_id(0), 0), 0, even=dict(softmax=False, store=False), odd=dict(store=False))
    lax.fori_loop(1, n_steps // 2, pair, 0)
    step(n_steps, even, odd, scores=False)
    for w in range(width):
        finish(even, w, n_q - width + w)


def _diff_attention(qn, kn, vt, bias_tiles, lambda_qk, g_o, lam_init, expert_w):
    bsz, s, _ = qn.shape
    w_rows = expert_w[0].shape[0] * expert_w[0].shape[1]
    w_cols = expert_w[0].shape[2]
    assert all(w.shape == expert_w[0].shape for w in expert_w) and w_rows % (bsz * N_HEADS * 16) == 0
    w_blk = pl.BlockSpec((w_rows // (bsz * N_HEADS), w_cols), lambda b, h: (b * N_HEADS + h, 0))
    t = bias_tiles.shape[-1]
    n_k = s // t
    width = ATT_WIDTH
    assert (s // t) % (2 * width) == 0, "the query-tile pipeline advances two groups of tiles per loop trip"
    scores = pltpu.VMEM((width, 2, n_k, t, t), F32)
    maxima = pltpu.VMEM((width, 2, 8, t), F32)
    pv_acc = pltpu.VMEM((width, 2, HEAD_DV + PV_ONES, t), F32)
    seq = lambda b, h: (b, 0, h)
    outs = pl.pallas_call(
        functools.partial(_attn_kernel, lam_init=lam_init),
        grid=(bsz, N_HEADS),
        in_specs=[pl.BlockSpec((1, s, HEAD_DV), seq),
                  pl.BlockSpec((1, s, HEAD_DV), seq),
                  pl.BlockSpec((1, HEAD_DV, s), lambda b, h: (b, h, 0)),
                  pl.BlockSpec((1, 5, t, t), lambda b, h: (h, 0, 0, 0)),
                  pl.BlockSpec((4, HEAD_DK), lambda b, h: (0, 0)),
                  pl.BlockSpec((1, HEAD_DV), lambda b, h: (0, 0)),
                  w_blk, w_blk, w_blk],
        out_specs=[pl.BlockSpec((1, s, HEAD_DV), seq), w_blk, w_blk, w_blk],
        out_shape=[jax.ShapeDtypeStruct((bsz, s, D_ATT), BF16)]
                  + [jax.ShapeDtypeStruct((w_rows, w_cols), BF16)] * 3,
        scratch_shapes=[scores, scores, maxima, maxima, pv_acc, pv_acc, pltpu.VMEM((width, 2, s, t), BF16)],
        compiler_params=_params(("parallel", "parallel"), VMEM_LIMIT),
        name="diff_attention",
    )(qn, kn, vt, bias_tiles, lambda_qk, g_o.reshape(1, HEAD_DV), *[w.reshape(w_rows, w_cols) for w in expert_w])
    return outs[0], [wb.reshape(expert_w[0].shape) for wb in outs[1:]]


def _outproj_kernel(yl_ref, ya_ref, x_ref, gate_ref, sc_ref, sh_ref, g2_ref, wo_ref, wr_ref,
                    x1_ref, h2_ref, aff_ref):
    mix = (jnp.dot(yl_ref[0], wo_ref[0:D_LRU, :], preferred_element_type=F32)
           + jnp.dot(ya_ref[0], wo_ref[D_LRU:D_LRU + D_ATT, :], preferred_element_type=F32))
    x1 = x_ref[0] + gate_ref[0] * mix
    x1_ref[0] = x1
    ms = jnp.mean(x1 * x1, axis=-1, keepdims=True)
    h2 = (x1 * lax.rsqrt(ms + EPS) * g2_ref[...]) * (1.0 + sc_ref[0]) + sh_ref[0]
    h2b = h2.astype(BF16)
    h2_ref[0] = h2b
    logits = lax.dot_general(wr_ref[...], h2b, NT_DIMS, preferred_element_type=F32)
    ex = jnp.exp(logits - jnp.max(logits, axis=0, keepdims=True))
    aff_ref[0] = ex / jnp.sum(ex, axis=0, keepdims=True)


def _out_projection(y_lru, y_att, x, gate1, scale2, shift2, g_norm2, w_out, w_router, tm):
    bsz, s, d = x.shape
    row = lambda b, i: (b, i, 0)
    vec = lambda b, i: (b, 0, 0)
    full = lambda b, i: (0, 0)
    return pl.pallas_call(
        _outproj_kernel,
        grid=(bsz, s // tm),
        in_specs=[pl.BlockSpec((1, tm, D_LRU), row),
                  pl.BlockSpec((1, tm, D_ATT), row),
                  pl.BlockSpec((1, tm, d), row),
                  pl.BlockSpec((1, 1, d), vec),
                  pl.BlockSpec((1, 1, d), vec),
                  pl.BlockSpec((1, 1, d), vec),
                  pl.BlockSpec((1, d), full),
                  pl.BlockSpec((D_LRU + D_ATT, d), full),
                  pl.BlockSpec((N_EXPERTS, d), full)],
        out_specs=[pl.BlockSpec((1, tm, d), row),
                   pl.BlockSpec((1, tm, d), row),
                   pl.BlockSpec((1, N_EXPERTS, tm), lambda b, i: (b, 0, i))],
        out_shape=[jax.ShapeDtypeStruct((bsz, s, d), F32),
                   jax.ShapeDtypeStruct((bsz, s, d), BF16),
                   jax.ShapeDtypeStruct((bsz, N_EXPERTS, s), F32)],
        compiler_params=_params(("parallel", "parallel"), VMEM_LIMIT),
        name="out_proj_norm2_router",
    )(y_lru, y_att, x, gate1, scale2, shift2, g_norm2.reshape(1, d), w_out.astype(BF16),
      w_router.T.astype(BF16))


def _route_kernel(aff_ref, pos_ref, cnt_ref, *, cap, n_tok_chunks):
    aff = aff_ref[0]
    n_e, s = aff.shape
    bits = lax.bitcast_convert_type(aff, jnp.int32)
    capf = float(cap)

    def count(mask):
        return jnp.sum(jnp.where(mask, 1.0, 0.0), axis=-1, keepdims=True)

    tau = jnp.zeros((n_e, 1), jnp.int32)
    for bit in range(30, -1, -1):
        cand = tau | (1 << bit)
        tau = jnp.where(count(bits >= cand) >= capf, cand, tau)
    gt = bits > tau
    eq = bits == tau
    need = capf - count(gt)

    blk = min(256, s // n_tok_chunks)
    r = lax.broadcasted_iota(jnp.int32, (blk, blk), 0)
    cidx = lax.broadcasted_iota(jnp.int32, (blk, blk), 1)
    upper = jnp.where(r < cidx, 1.0, 0.0).astype(BF16)

    def prefix_blocks(mask):
        off = jnp.zeros((n_e, 1), F32)
        pieces, offs = [], []
        for k in range(s // blk):
            mb = jnp.where(mask[:, k * blk:(k + 1) * blk], 1.0, 0.0)
            offs.append(off)
            pieces.append(jnp.dot(mb.astype(BF16), upper, preferred_element_type=F32) + off)
            off = off + jnp.sum(mb, axis=-1, keepdims=True)
        return pieces, offs

    eq_rank, _ = prefix_blocks(eq)
    sel_blocks = []
    for k in range(s // blk):
        sl = slice(k * blk, (k + 1) * blk)
        sel_blocks.append(jnp.logical_or(gt[:, sl], jnp.logical_and(eq[:, sl], eq_rank[k] < need)))
    sel = jnp.concatenate(sel_blocks, axis=1)
    slot, offs = prefix_blocks(sel)
    for k in range(s // blk):
        pos_ref[0, :, k * blk:(k + 1) * blk] = jnp.where(sel_blocks[k], slot[k], -1.0).astype(jnp.int32)

    lane = lax.broadcasted_iota(jnp.int32, (n_e, 128), 1)
    cnt = jnp.zeros((n_e, 128), F32)
    per = (s // n_tok_chunks) // blk
    for j in range(n_tok_chunks):
        cnt = jnp.where(lane == j, offs[j * per], cnt)
    cnt_ref[0] = cnt.astype(jnp.int32)


def _routing(aff, cap):
    bsz, n_e, s = aff.shape
    group = math.gcd(bsz, ROUTE_GROUP)
    rows = group * n_e
    pos, cnt = pl.pallas_call(
        functools.partial(_route_kernel, cap=cap, n_tok_chunks=N_TOK_CHUNKS),
        grid=(bsz // group,),
        in_specs=[pl.BlockSpec((1, rows, s), lambda b: (b, 0, 0))],
        out_specs=[pl.BlockSpec((1, rows, s), lambda b: (b, 0, 0)),
                   pl.BlockSpec((1, rows, 128), lambda b: (b, 0, 0))],
        out_shape=[jax.ShapeDtypeStruct((bsz // group, rows, s), jnp.int32),
                   jax.ShapeDtypeStruct((bsz // group, rows, 128), jnp.int32)],
        compiler_params=_params(("parallel",)),
        name="expert_choice_routing",
    )(aff.reshape(bsz // group, rows, s))
    return pos.reshape(bsz, n_e, s), cnt.reshape(bsz, n_e, 128)


def _chunking(pos):
    return pos.shape[1], N_TOK_CHUNKS, pos.shape[2] // N_TOK_CHUNKS


def _chunk_row(ref, e, j):
    t = _chunking(ref)[2]
    return ref[0, pl.ds(e, 1), pl.ds(pl.multiple_of(j * t, t), t)]


def _slot_windows(cnt_ref, pos_ref, b, j, k, cap):
    n_e, n_j, t = _chunking(pos_ref)
    slot_iota = lax.broadcasted_iota(jnp.int32, (MOE_WINDOW, t), 0)
    sels, firsts = [], []
    for e in range(n_e):
        lo = cnt_ref[(b * n_e + e) * n_j + j]
        start = lax.shift_left(lax.shift_right_logical(lo, 4), 4) + k * MOE_WINDOW
        w0 = jnp.minimum(start, cap - MOE_WINDOW)
        slots = slot_iota + w0
        sels.append(jnp.logical_and(_chunk_row(pos_ref, e, j) == slots, slots >= start))
        firsts.append(pl.multiple_of(e * cap + w0, 16))
    return sels, firsts


def _stacked_one_hot(sels):
    return jnp.concatenate([jnp.where(sel, 1.0, 0.0).astype(BF16) for sel in sels], axis=0)


def _window_passes(cnt_ref, pos_ref, b, j, cap):
    n_e, n_j, _ = _chunking(pos_ref)
    n_win = jnp.int32(1)
    for e in range(n_e):
        at = (b * n_e + e) * n_j
        start = lax.shift_left(lax.shift_right_logical(cnt_ref[at + j], 4), 4)
        hi = jnp.where(j + 1 < n_j, cnt_ref[at + jnp.minimum(j + 1, n_j - 1)], cap)
        n_win = jnp.maximum(n_win, lax.shift_right_logical(hi - start + (MOE_WINDOW - 1), MOE_WINDOW.bit_length() - 1))
    return n_win


def _dispatch_kernel(cnt_ref, pos_ref, aff_ref, h2_ref, xe_ref, g_ref, *, cap):
    b = pl.program_id(0)
    t = _chunking(pos_ref)[2]

    @pl.when(pl.program_id(1) == 0)
    def _():
        xe_ref[...] = jnp.zeros_like(xe_ref)
        g_ref[...] = jnp.zeros_like(g_ref)

    def chunk(jj, carry):
        j = pl.program_id(1) * MOE_CHUNKS + jj
        toks = pl.ds(pl.multiple_of(jj * t, t), t)

        def one_pass(k, inner):
            sels, firsts = _slot_windows(cnt_ref, pos_ref, b, j, k, cap)
            rows = jnp.dot(_stacked_one_hot(sels), h2_ref[0, toks, :], preferred_element_type=F32).astype(BF16)
            for e, (sel, first) in enumerate(zip(sels, firsts)):
                window = pl.ds(first, MOE_WINDOW)
                xe_ref[0, window, :] += rows[e * MOE_WINDOW:(e + 1) * MOE_WINDOW, :]
                weight = jnp.sum(jnp.where(sel, _chunk_row(aff_ref, e, j), 0.0), axis=-1, keepdims=True)
                g_ref[0, window, :] += jnp.broadcast_to(weight, (MOE_WINDOW, g_ref.shape[-1]))
            return inner

        one_pass(0, 0)
        lax.fori_loop(1, _window_passes(cnt_ref, pos_ref, b, j, cap), one_pass, 0)
        return carry

    lax.fori_loop(0, MOE_CHUNKS, chunk, 0)


def _dispatch(cnt, pos, aff, h2, cap):
    bsz, s, d = h2.shape
    n_e, n_j, t = _chunking(pos)
    per_token = pl.BlockSpec((1, n_e, s), lambda b, j, c: (b, 0, 0))
    grid_spec = pltpu.PrefetchScalarGridSpec(
        num_scalar_prefetch=1,
        grid=(bsz, n_j // MOE_CHUNKS),
        in_specs=[per_token, per_token,
                  pl.BlockSpec((1, MOE_CHUNKS * t, d), lambda b, j, c: (b, j, 0))],
        out_specs=[pl.BlockSpec((1, n_e * cap, d), lambda b, j, c: (b, 0, 0)),
                   pl.BlockSpec((1, n_e * cap, LANES), lambda b, j, c: (b, 0, 0))],
    )
    return pl.pallas_call(
        functools.partial(_dispatch_kernel, cap=cap),
        grid_spec=grid_spec,
        out_shape=[jax.ShapeDtypeStruct((bsz, n_e * cap, d), BF16),
                   jax.ShapeDtypeStruct((bsz, n_e * cap, LANES), F32)],
        compiler_params=_params(("parallel", "arbitrary"), VMEM_LIMIT),
        name="expert_dispatch",
    )(cnt.reshape(-1), pos, aff, h2)


def _moe_kernel(xe_ref, g_ref, w1_ref, w3_ref, w2_ref, gate_ref, y_ref):
    xe = xe_ref[0]
    a = jnp.dot(xe, w1_ref[0], preferred_element_type=F32)
    gate = jnp.dot(xe, w3_ref[0], preferred_element_type=F32)
    hmid = ((a * _sigmoid(a)) * gate).astype(BF16)
    y = jnp.dot(hmid, w2_ref[0], preferred_element_type=F32) * g_ref[0][:, :1] * gate_ref[0]
    y_ref[0] = y.astype(BF16)


def _moe(xe, g, w1, w3, w2, gate2, cap):
    bsz, _, d = xe.shape
    n_e, _, f = w1.shape
    wspec = lambda b, e: (e, 0, 0)
    slot_rows = lambda b, e: (b, e, 0)
    return pl.pallas_call(
        _moe_kernel,
        grid=(bsz, n_e),
        in_specs=[pl.BlockSpec((1, cap, d), slot_rows),
                  pl.BlockSpec((1, cap, g.shape[-1]), slot_rows),
                  pl.BlockSpec((1, d, f), wspec),
                  pl.BlockSpec((1, d, f), wspec),
                  pl.BlockSpec((1, f, d), wspec),
                  pl.BlockSpec((1, 1, d), lambda b, e: (b, 0, 0))],
        out_specs=pl.BlockSpec((1, cap, d), slot_rows),
        out_shape=jax.ShapeDtypeStruct((bsz, n_e * cap, d), BF16),
        compiler_params=_params(("parallel", "parallel"), VMEM_LIMIT),
        name="expert_choice_ffn",
    )(xe, g, w1, w3, w2, gate2)


def _combine_kernel(cnt_ref, pos_ref, y_ref, x1_ref, out_ref, *, cap):
    b = pl.program_id(0)
    t = _chunking(pos_ref)[2]

    def chunk(jj, carry):
        j = pl.program_id(1) * MOE_CHUNKS + jj
        toks = pl.ds(pl.multiple_of(jj * t, t), t)

        def windows(k):
            sels, firsts = _slot_windows(cnt_ref, pos_ref, b, j, k, cap)
            rows = jnp.concatenate([y_ref[0, pl.ds(first, MOE_WINDOW), :] for first in firsts], axis=0)
            return lax.dot_general(_stacked_one_hot(sels), rows, TN_DIMS, preferred_element_type=F32)

        out_ref[0, toks, :] = x1_ref[0, toks, :] + windows(0)

        def more(k, inner):
            out_ref[0, toks, :] += windows(k)
            return inner

        lax.fori_loop(1, _window_passes(cnt_ref, pos_ref, b, j, cap), more, 0)
        return carry

    lax.fori_loop(0, MOE_CHUNKS, chunk, 0)


def _combine(cnt, pos, y, x1, cap):
    bsz, s, d = x1.shape
    n_e, n_j, t = _chunking(pos)
    grid_spec = pltpu.PrefetchScalarGridSpec(
        num_scalar_prefetch=1,
        grid=(bsz, n_j // MOE_CHUNKS),
        in_specs=[pl.BlockSpec((1, n_e, s), lambda b, j, c: (b, 0, 0)),
                  pl.BlockSpec((1, n_e * cap, d), lambda b, j, c: (b, 0, 0)),
                  pl.BlockSpec((1, MOE_CHUNKS * t, d), lambda b, j, c: (b, j, 0))],
        out_specs=pl.BlockSpec((1, MOE_CHUNKS * t, d), lambda b, j, c: (b, j, 0)),
    )
    return pl.pallas_call(
        functools.partial(_combine_kernel, cap=cap),
        grid_spec=grid_spec,
        out_shape=jax.ShapeDtypeStruct((bsz, s, d), F32),
        compiler_params=_params(("parallel", "parallel"), VMEM_LIMIT),
        name="expert_combine",
    )(cnt.reshape(-1), pos, y, x1)


def kernel(x, c, w_mod, b_mod, g_norm1, w_in, conv_w, conv_b, lru_w_a, lru_b_a, lru_w_x, lru_b_x,
           lru_lambda, g_q, g_k, lambda_qk, g_attn_out, rel_bias, w_out, g_norm2, w_router, w1, w3, w2):
    bsz, s, d = x.shape
    depth = w_mod.shape[0]
    cap = max(1, EC_FACTOR * s // N_EXPERTS)
    bias_tiles = _bias_tiles(rel_bias, min(ATT_TILE, s))
    for l in range(depth):
        mod = _modulation(c, w_mod[l], b_mod[l])
        shift1, scale1, gate1, shift2, scale2, gate2 = [m.reshape(bsz, 1, d) for m in jnp.split(mod, 6, axis=-1)]
        x_lru, gz, qn, kn, vt = _in_projection(x, scale1, shift1, g_norm1[l], w_in[l], g_q[l], g_k[l],
                                               min(IN_PROJ_ROWS, s))
        y_lru = _rg_lru(x_lru, gz, conv_w[l], conv_b[l], lru_w_a[l], lru_b_a[l], lru_w_x[l], lru_b_x[l],
                        lru_lambda[l])
        lam_init = 0.8 - 0.6 * math.exp(-0.3 * l)
        y_att, (w1b, w3b, w2b) = _diff_attention(qn, kn, vt, bias_tiles, lambda_qk[l], g_attn_out[l], lam_init,
                                                 (w1[l], w3[l], w2[l]))
        x1, h2, aff = _out_projection(y_lru, y_att, x, gate1, scale2, shift2, g_norm2[l], w_out[l],
                                      w_router[l], min(OUT_PROJ_ROWS, s))
        pos, cnt = _routing(aff, cap)
        cnt = cnt[:, :, :N_TOK_CHUNKS]
        xe, g = _dispatch(cnt, pos, aff, h2, cap)
        y = _moe(xe, g, w1b, w3b, w2b, gate2, cap)
        x = _combine(cnt, pos, y, x1, cap)
    return x
```

```python
import functools
import math

import jax
import jax.numpy as jnp
from jax import lax
from jax.experimental import pallas as pl
from jax.experimental.pallas import tpu as pltpu

F32 = jnp.float32
BF16 = jnp.bfloat16

D_MODEL = 1024
D_LRU = 512
LRU_BLOCK = 64
LRU_C = 8.0
CONV_W = 4
N_HEADS = 4
HEAD_DV = 128
HEAD_DK = 64
D_ATT = N_HEADS * HEAD_DV
N_BUCKETS = 32
N_EXPERTS = 16
EC_FACTOR = 2
EPS = 1e-6
F32_TINY = 2.0 ** -126
LOG2E = math.log2(math.e)

IN_PROJ_ROWS = 1024
OUT_PROJ_ROWS = 1024
LRU_HALF = 256
SCAN_LANES = 8
ATT_TILE = 256
PV_ONES = 16
ATT_WIDTH = 1
ROUTE_GROUP = 8
N_TOK_CHUNKS = 16
MXU_TILE = 256
LANES = 128
MOE_WINDOW = 64
MOE_CHUNKS = 4
VMEM_LIMIT = 56 * 1024 * 1024

NT_DIMS = (((1,), (1,)), ((), ()))
TN_DIMS = (((0,), (0,)), ((), ()))


def _sigmoid(x):
    return 1.0 / (1.0 + jnp.exp(-x))


def _params(sem, vmem=None):
    return pltpu.CompilerParams(dimension_semantics=sem, vmem_limit_bytes=vmem)


def _mod_kernel(c_ref, w_ref, b_ref, o_ref):
    c = c_ref[...]
    o_ref[...] = jnp.dot((c * _sigmoid(c)).astype(BF16), w_ref[...].astype(BF16),
                         preferred_element_type=F32) + b_ref[...]


def _modulation(c, w_mod, b_mod):
    bsz, d = c.shape
    n = w_mod.shape[1]
    return pl.pallas_call(
        _mod_kernel,
        grid=(n // d,),
        in_specs=[pl.BlockSpec((bsz, d), lambda j: (0, 0)),
                  pl.BlockSpec((d, d), lambda j: (0, j)),
                  pl.BlockSpec((1, d), lambda j: (0, j))],
        out_specs=pl.BlockSpec((bsz, d), lambda j: (0, j)),
        out_shape=jax.ShapeDtypeStruct((bsz, n), F32),
        compiler_params=_params(("arbitrary",)),
        name="adaln_mod",
    )(c, w_mod, b_mod.reshape(1, n))


def _bias_kernel(tab_ref, o_ref):
    h = pl.program_id(0)
    t = o_ref.shape[-1]
    key = lax.broadcasted_iota(jnp.int32, (t, t), 0)
    qry = lax.broadcasted_iota(jnp.int32, (t, t), 1)
    half = N_BUCKETS // 2
    max_exact = half // 2
    for d in range(5):
        if d == 0:
            o_ref[0, d] = jnp.full((t, t), tab_ref[half - 1, h] * LOG2E, F32)
        elif d == 4:
            o_ref[0, d] = jnp.full((t, t), tab_ref[N_BUCKETS - 1, h] * LOG2E, F32)
        else:
            rel = (d - 2) * t + key - qry
            n = jnp.abs(rel)
            n2 = n * n
            large = jnp.full((t, t), max_exact, jnp.int32)
            for k in range(1, half - max_exact):
                large = large + jnp.where(n2 >= (max_exact * max_exact) * (2 ** k), 1, 0)
            idx = jnp.where(n < max_exact, n, large) + jnp.where(rel > 0, half, 0)
            val = jnp.zeros((t, t), F32)
            for j in range(N_BUCKETS):
                val = jnp.where(idx == j, tab_ref[j, h] * LOG2E, val)
            o_ref[0, d] = val


def _bias_tiles(rel_bias, t):
    return pl.pallas_call(
        _bias_kernel,
        grid=(N_HEADS,),
        in_specs=[pl.BlockSpec(memory_space=pltpu.SMEM)],
        out_specs=pl.BlockSpec((1, 5, t, t), lambda h: (h, 0, 0, 0)),
        out_shape=jax.ShapeDtypeStruct((N_HEADS, 5, t, t), F32),
        compiler_params=_params(("arbitrary",)),
        name="t5_bias_tiles",
    )(rel_bias)


def _inproj_kernel(x_ref, sc_ref, sh_ref, g1_ref, w_ref, wvt_ref, mseg_ref, gq_ref, gk_ref,
                   xl_ref, gz_ref, q_ref, k_ref, vt_ref):
    x = x_ref[0]
    ms = jnp.mean(x * x, axis=-1, keepdims=True)
    h = (x * lax.rsqrt(ms + EPS) * g1_ref[...]) * (1.0 + sc_ref[0]) + sh_ref[0]
    hb = h.astype(BF16)

    def proj(lo, width):
        return jnp.dot(hb, w_ref[:, lo:lo + width], preferred_element_type=F32)

    def qk_norm(t, g):
        tt = (t * t).astype(BF16)
        m = mseg_ref.shape[0]
        ss = jnp.concatenate([jnp.dot(tt[:, c:c + m], mseg_ref[...], preferred_element_type=F32)
                              for c in range(0, tt.shape[1], m)], axis=1)
        return t * lax.rsqrt(ss * (1.0 / HEAD_DK) + EPS) * g

    xl_ref[0] = proj(0, D_LRU)
    z = proj(D_LRU, D_LRU)
    cdf = 0.5 * (1.0 + jnp.tanh(math.sqrt(2.0 / math.pi) * (z + 0.044715 * (z * z * z))))
    gz_ref[0] = (z * cdf).astype(BF16)
    q_ref[0] = (qk_norm(proj(2 * D_LRU, D_ATT), gq_ref[...]) * (HEAD_DK ** -0.5 * LOG2E)).astype(BF16)
    k_ref[0] = qk_norm(proj(2 * D_LRU + D_ATT, D_ATT), gk_ref[...]).astype(BF16)
    vt_ref[0] = lax.dot_general(wvt_ref[...], hb, NT_DIMS,
                                preferred_element_type=F32).astype(BF16)


def _in_projection(x, scale1, shift1, g_norm1, w_in, g_q, g_k, tm):
    bsz, s, d = x.shape
    n = w_in.shape[1] - D_ATT
    w_main = w_in[:, :n].astype(BF16)
    w_vt = w_in[:, n:].T.astype(BF16)
    seg = jnp.arange(MXU_TILE, dtype=jnp.int32) // HEAD_DK
    mseg = (seg[:, None] == seg[None, :]).astype(BF16)
    n_sub = D_ATT // HEAD_DK
    row = lambda b, i: (b, i, 0)
    vec = lambda b, i: (b, 0, 0)
    full = lambda b, i: (0, 0)
    out_block = pl.BlockSpec((1, tm, D_LRU), row)
    return pl.pallas_call(
        _inproj_kernel,
        grid=(bsz, s // tm),
        in_specs=[pl.BlockSpec((1, tm, d), row),
                  pl.BlockSpec((1, 1, d), vec),
                  pl.BlockSpec((1, 1, d), vec),
                  pl.BlockSpec((1, d), full),
                  pl.BlockSpec((d, n), full),
                  pl.BlockSpec((D_ATT, d), full),
                  pl.BlockSpec((MXU_TILE, MXU_TILE), full),
                  pl.BlockSpec((1, D_ATT), full),
                  pl.BlockSpec((1, D_ATT), full)],
        out_specs=[out_block] * 4 + [pl.BlockSpec((1, D_ATT, tm), lambda b, i: (b, 0, i))],
        out_shape=[jax.ShapeDtypeStruct((bsz, s, D_LRU), F32)]
                  + [jax.ShapeDtypeStruct((bsz, s, D_LRU), BF16)] * 3
                  + [jax.ShapeDtypeStruct((bsz, D_ATT, s), BF16)],
        compiler_params=_params(("parallel", "parallel"), VMEM_LIMIT),
        name="norm1_in_proj",
    )(x, scale1, shift1, g_norm1.reshape(1, d), w_main, w_vt, mseg,
      jnp.tile(g_q, n_sub).reshape(1, D_ATT), jnp.tile(g_k, n_sub).reshape(1, D_ATT))


def _lru_kernel(x_ref, gz_ref, cw_ref, cb_ref, wa_ref, wx_ref, ba_ref, bx_ref, lam_ref, y_ref,
                xpad, a_f, u_f, a_b, u_b, *, tc):
    s = x_ref.shape[1]
    c = x_ref.shape[2]
    n_chunks = s // tc
    n_slab = c // 128
    pitch = a_f.shape[1] // n_chunks
    zeros8 = jnp.zeros((8, c), F32)
    xpad[0:8, :] = zeros8
    xpad[s + 8:s + 16, :] = zeros8

    def fill(ci, carry):
        t0 = pl.multiple_of(ci * tc, tc)
        xpad[pl.ds(t0 + 8, tc), :] = x_ref[0, pl.ds(t0, tc), :]
        return carry

    lax.fori_loop(0, n_chunks, fill, 0)

    cw = cw_ref[...]
    cb = cb_ref[...]
    decay = []
    for d in range(2):
        lam = lam_ref[d]
        softplus_neg = jnp.maximum(-lam, 0.0) + jnp.log(1.0 + jnp.exp(-jnp.abs(lam)))
        decay.append((-LRU_C * LOG2E) * softplus_neg)
    a_scr = (a_f, a_b)
    u_scr = (u_f, u_b)

    def gates(ci, carry):
        t0 = pl.multiple_of(ci * tc, tc)
        xw = xpad[pl.ds(t0, tc + 16), :]
        xc = (cw[0:1] * pltpu.roll(xw, 2, 0)[8:8 + tc]
              + cw[1:2] * pltpu.roll(xw, 1, 0)[8:8 + tc]
              + cw[2:3] * xw[8:8 + tc]
              + cw[3:4] * pltpu.roll(xw, tc + 15, 0)[8:8 + tc]) + cb
        xcb = xc.astype(BF16)
        for d in range(2):
            r = _sigmoid(jnp.dot(xcb, wa_ref[d], preferred_element_type=F32) + ba_ref[d])
            i = _sigmoid(jnp.dot(xcb, wx_ref[d], preferred_element_type=F32) + bx_ref[d])
            a = jnp.exp2(r * decay[d])
            v = 1.0 - a * a
            u = (v * lax.rsqrt(jnp.maximum(v, F32_TINY))) * (i * xc)
            r0 = pl.multiple_of(ci * pitch, 8)
            for sl in range(n_slab):
                a_scr[d][sl, pl.ds(r0, tc), :] = a[:, sl * 128:(sl + 1) * 128]
                u_scr[d][sl, pl.ds(r0, tc), :] = u[:, sl * 128:(sl + 1) * 128]
        return carry

    lax.fori_loop(0, n_chunks, gates, 0, unroll=2)

    def rows(tt):
        return pl.ds(tt, n_chunks, stride=pitch)

    def step(tt, carry):
        out = []
        for d in range(2):
            t_loc = tt if d == 0 else tc - 1 - tt
            for sl in range(n_slab):
                h, p = carry[len(out)]
                a8 = a_scr[d][sl, rows(t_loc), :]
                h = a8 * h + u_scr[d][sl, rows(t_loc), :]
                p = a8 * p
                u_scr[d][sl, rows(t_loc), :] = h
                a_scr[d][sl, rows(t_loc), :] = p
                out.append((h, p))
        return tuple(out)

    init = (jnp.zeros((n_chunks, 128), F32), jnp.ones((n_chunks, 128), F32))
    lax.fori_loop(0, tc, step, (init,) * (2 * n_slab), unroll=4)

    for sl in range(n_slab):
        lanes = slice(sl * 128, (sl + 1) * 128)
        h_end, p_end = u_f[sl, rows(tc - 1), :], a_f[sl, rows(tc - 1), :]
        h_beg, p_beg = u_b[sl, rows(0), :], a_b[sl, rows(0), :]
        carry_f = [jnp.zeros((1, 128), F32)]
        for r in range(1, n_chunks):
            carry_f.append(p_end[r - 1:r] * carry_f[-1] + h_end[r - 1:r])
        carry_b = [jnp.zeros((1, 128), F32)]
        for r in range(n_chunks - 2, -1, -1):
            carry_b.insert(0, p_beg[r + 1:r + 2] * carry_b[0] + h_beg[r + 1:r + 2])
        for r in range(n_chunks):
            blk = slice(r * pitch, r * pitch + tc)
            hsum = (u_f[sl, blk, :] + a_f[sl, blk, :] * carry_f[r]
                    + u_b[sl, blk, :] + a_b[sl, blk, :] * carry_b[r])
            y_ref[0, r * tc:(r + 1) * tc, lanes] = (hsum * gz_ref[0, r * tc:(r + 1) * tc, lanes].astype(F32)).astype(BF16)


def _block_diag(w, half):
    n_dir, n_blocks, blk, _ = w.shape
    per = half // blk
    n_half = n_blocks // per
    w = w.reshape(n_dir, n_half, per, blk, blk)
    eye = jnp.eye(per, dtype=w.dtype)
    out = w[:, :, :, :, None, :] * eye[None, None, :, None, :, None]
    return out.reshape(n_dir, n_half, half, half)


def _rg_lru(x_lru, gz, conv_w, conv_b, w_a, b_a, w_x, b_x, lam):
    bsz, s, c = x_lru.shape
    half = LRU_HALF
    tc = s // SCAN_LANES
    pitch = tc + 8 if (tc // 8) % 2 == 0 else tc + 16
    n_half = c // half
    wa = _block_diag(w_a, half).astype(BF16)
    wx = _block_diag(w_x, half).astype(BF16)
    seq = lambda b, p: (b, 0, p)
    chan = lambda b, p: (0, p)
    dirchan = lambda b, p: (0, 0, p)
    blk = lambda b, p: (0, p, 0, 0)
    return pl.pallas_call(
        functools.partial(_lru_kernel, tc=tc),
        grid=(bsz, n_half),
        in_specs=[pl.BlockSpec((1, s, half), seq),
                  pl.BlockSpec((1, s, half), seq),
                  pl.BlockSpec((CONV_W, half), chan),
                  pl.BlockSpec((1, half), chan),
                  pl.BlockSpec((2, None, half, half), blk),
                  pl.BlockSpec((2, None, half, half), blk),
                  pl.BlockSpec((2, 1, half), dirchan),
                  pl.BlockSpec((2, 1, half), dirchan),
                  pl.BlockSpec((2, 1, half), dirchan)],
        out_specs=pl.BlockSpec((1, s, half), seq),
        out_shape=jax.ShapeDtypeStruct((bsz, s, c), BF16),
        scratch_shapes=[pltpu.VMEM((s + 16, half), F32)]
                       + [pltpu.VMEM((half // 128, SCAN_LANES * pitch, 128), F32)] * 4,
        compiler_params=_params(("parallel", "parallel"), VMEM_LIMIT),
        name="rg_lru",
    )(x_lru, gz, conv_w.reshape(CONV_W, c), conv_b.reshape(1, c), wa, wx,
      b_a.reshape(2, 1, c), b_x.reshape(2, 1, c), lam.reshape(2, 1, c))


def _attn_kernel(q_ref, k_ref, vt_ref, bias_ref, lq_ref, go_ref, w1_ref, w3_ref, w2_ref,
                 o_ref, w1b_ref, w3b_ref, w2b_ref,
                 s_even, s_odd, m_even, m_odd, o_even, o_odd, p_scr, *, lam_init):
    for w_ref, wb_ref in ((w1_ref, w1b_ref), (w3_ref, w3b_ref), (w2_ref, w2b_ref)):
        wb_ref[...] = w_ref[...].astype(BF16)

    t = bias_ref.shape[-1]
    width = s_even.shape[0]
    n_k = k_ref.shape[1] // t
    n_q = q_ref.shape[1] // t
    n_steps = n_q // width
    subs = [(w, u) for w in range(width) for u in range(2)]
    chunk = 2
    pv_chunk = 2
    n_c = n_k // chunk

    even = dict(s=s_even, m=m_even, o=o_even)
    odd = dict(s=s_odd, m=m_odd, o=o_odd)
    ones_rows = jnp.ones((PV_ONES, pv_chunk * t), BF16)

    def fold(x, op):
        parts = [x[r * 8:(r + 1) * 8, :] for r in range(t // 8)]
        acc = parts[:2]
        for r in range(2, len(parts)):
            acc[r % 2] = op(acc[r % 2], parts[r])
        return op(acc[0], acc[1])

    def bias_tile(j, tile):
        return bias_ref[0, jnp.clip(j - tile, -2, 2) + 2]

    def finish(prev, w, tile):
        outs = [prev['o'][w, u, :HEAD_DV, :] / prev['o'][w, u, HEAD_DV:HEAD_DV + 1, :] for u in range(2)]
        lq = lq_ref[...]
        lam = (jnp.exp(jnp.sum(lq[0:1] * lq[1:2], axis=-1, keepdims=True))
               - jnp.exp(jnp.sum(lq[2:3] * lq[3:4], axis=-1, keepdims=True)) + lam_init)
        o = (outs[0] - lam * outs[1]).T
        ms = jnp.mean(o * o, axis=-1, keepdims=True)
        o_ref[0, pl.ds(pl.multiple_of(tile * t, t), t), :] = (
            (o * lax.rsqrt(ms + EPS) * go_ref[...]) * (1.0 - lam_init)).astype(BF16)

    def step(i, cur, prev, scores=True, softmax=True, store=True):
        tiles_a = [i * width + w for w in range(width)]
        for w in range(width if store else 0):
            finish(prev, w, jnp.maximum((i - 2) * width + w, 0))
        s_cur, s_prev = cur['s'], prev['s']
        m_b = {wu: jnp.max(prev['m'][wu], axis=0, keepdims=True) for wu in subs} if softmax else {}
        m_acc = {wu: jnp.full((8, t), -jnp.inf, F32) for wu in subs}
        q_sub = {}
        if scores:
            for w in range(width):
                q = q_ref[0, pl.ds(pl.multiple_of(tiles_a[w] * t, t), t), :]
                lane = lax.broadcasted_iota(jnp.int32, q.shape, 1)
                zero = jnp.zeros_like(q)
                q_sub[w, 0] = jnp.where(lane < HEAD_DK, q, zero)
                q_sub[w, 1] = jnp.where(lane >= HEAD_DK, q, zero)
        for c in range(n_c):
            rows = slice(c * chunk * t, (c + 1) * chunk * t)
            for w, u in (subs if scores else []):
                sc_all = lax.dot_general(k_ref[0, rows, :], q_sub[w, u], NT_DIMS, preferred_element_type=F32)
                for jj in range(chunk):
                    j = c * chunk + jj
                    sc = sc_all[jj * t:(jj + 1) * t, :] + bias_tile(j, tiles_a[w])
                    s_cur[w, u, j] = sc
                    m_acc[w, u] = jnp.maximum(m_acc[w, u], fold(sc, jnp.maximum))
            for w, u in (subs if softmax else []):
                for jj in range(chunk):
                    j = c * chunk + jj
                    p = jnp.exp2(s_prev[w, u, j] - m_b[w, u])
                    p_scr[w, u, j * t:(j + 1) * t, :] = p.astype(BF16)
                if ((c + 1) * chunk) % pv_chunk == 0:
                    keys = slice(((c + 1) * chunk - pv_chunk) * t, (c + 1) * chunk * t)
                    part = jnp.dot(jnp.concatenate([vt_ref[0, :, keys], ones_rows], axis=0), p_scr[w, u, keys, :],
                                   preferred_element_type=F32)
                    if (c + 1) * chunk == pv_chunk:
                        cur['o'][w, u] = part
                    else:
                        cur['o'][w, u] += part

        if scores:
            for w, u in subs:
                cur['m'][w, u] = m_acc[w, u]

    def pair(pi, carry, **stages):
        @pl.when(pi >= 0)
        def _():
            step(2 * pi, even, odd, **stages.get('even', {}))

        @pl.when(pi < n_steps)
        def _():
            step(2 * pi + 1, odd, even, **stages.get('odd', {}))

        return carry

    pair(jnp.minimum(pl.program_id(0), 0), 0, even=dict(softmax=False, store=False), odd=dict(store=False))
    lax.fori_loop(1, n_steps // 2, pair, 0)
    step(n_steps, even, odd, scores=False)
    for w in range(width):
        finish(even, w, n_q - width + w)


def _diff_attention(qn, kn, vt, bias_tiles, lambda_qk, g_o, lam_init, expert_w):
    bsz, s, _ = qn.shape
    w_rows = expert_w[0].shape[0] * expert_w[0].shape[1]
    w_cols = expert_w[0].shape[2]
    assert all(w.shape == expert_w[0].shape for w in expert_w) and w_rows % (bsz * N_HEADS * 16) == 0
    w_blk = pl.BlockSpec((w_rows // (bsz * N_HEADS), w_cols), lambda b, h: (b * N_HEADS + h, 0))
    t = bias_tiles.shape[-1]
    n_k = s // t
    width = ATT_WIDTH
    assert (s // t) % (2 * width) == 0, "the query-tile pipeline advances two groups of tiles per loop trip"
    scores = pltpu.VMEM((width, 2, n_k, t, t), F32)
    maxima = pltpu.VMEM((width, 2, 8, t), F32)
    pv_acc = pltpu.VMEM((width, 2, HEAD_DV + PV_ONES, t), F32)
    seq = lambda b, h: (b, 0, h)
    outs = pl.pallas_call(
        functools.partial(_attn_kernel, lam_init=lam_init),
        grid=(bsz, N_HEADS),
        in_specs=[pl.BlockSpec((1, s, HEAD_DV), seq),
                  pl.BlockSpec((1, s, HEAD_DV), seq),
                  pl.BlockSpec((1, HEAD_DV, s), lambda b, h: (b, h, 0)),
                  pl.BlockSpec((1, 5, t, t), lambda b, h: (h, 0, 0, 0)),
                  pl.BlockSpec((4, HEAD_DK), lambda b, h: (0, 0)),
                  pl.BlockSpec((1, HEAD_DV), lambda b, h: (0, 0)),
                  w_blk, w_blk, w_blk],
        out_specs=[pl.BlockSpec((1, s, HEAD_DV), seq), w_blk, w_blk, w_blk],
        out_shape=[jax.ShapeDtypeStruct((bsz, s, D_ATT), BF16)]
                  + [jax.ShapeDtypeStruct((w_rows, w_cols), BF16)] * 3,
        scratch_shapes=[scores, scores, maxima, maxima, pv_acc, pv_acc, pltpu.VMEM((width, 2, s, t), BF16)],
        compiler_params=_params(("parallel", "parallel"), VMEM_LIMIT),
        name="diff_attention",
    )(qn, kn, vt, bias_tiles, lambda_qk, g_o.reshape(1, HEAD_DV), *[w.reshape(w_rows, w_cols) for w in expert_w])
    return outs[0], [wb.reshape(expert_w[0].shape) for wb in outs[1:]]


def _outproj_kernel(yl_ref, ya_ref, x_ref, gate_ref, sc_ref, sh_ref, g2_ref, wo_ref, wr_ref,
                    x1_ref, h2_ref, aff_ref):
    mix = (jnp.dot(yl_ref[0], wo_ref[0:D_LRU, :], preferred_element_type=F32)
           + jnp.dot(ya_ref[0], wo_ref[D_LRU:D_LRU + D_ATT, :], preferred_element_type=F32))
    x1 = x_ref[0] + gate_ref[0] * mix
    x1_ref[0] = x1
    ms = jnp.mean(x1 * x1, axis=-1, keepdims=True)
    h2 = (x1 * lax.rsqrt(ms + EPS) * g2_ref[...]) * (1.0 + sc_ref[0]) + sh_ref[0]
    h2b = h2.astype(BF16)
    h2_ref[0] = h2b
    logits = lax.dot_general(wr_ref[...], h2b, NT_DIMS, preferred_element_type=F32)
    ex = jnp.exp(logits - jnp.max(logits, axis=0, keepdims=True))
    aff_ref[0] = ex / jnp.sum(ex, axis=0, keepdims=True)


def _out_projection(y_lru, y_att, x, gate1, scale2, shift2, g_norm2, w_out, w_router, tm):
    bsz, s, d = x.shape
    row = lambda b, i: (b, i, 0)
    vec = lambda b, i: (b, 0, 0)
    full = lambda b, i: (0, 0)
    return pl.pallas_call(
        _outproj_kernel,
        grid=(bsz, s // tm),
        in_specs=[pl.BlockSpec((1, tm, D_LRU), row),
                  pl.BlockSpec((1, tm, D_ATT), row),
                  pl.BlockSpec((1, tm, d), row),
                  pl.BlockSpec((1, 1, d), vec),
                  pl.BlockSpec((1, 1, d), vec),
                  pl.BlockSpec((1, 1, d), vec),
                  pl.BlockSpec((1, d), full),
                  pl.BlockSpec((D_LRU + D_ATT, d), full),
                  pl.BlockSpec((N_EXPERTS, d), full)],
        out_specs=[pl.BlockSpec((1, tm, d), row),
                   pl.BlockSpec((1, tm, d), row),
                   pl.BlockSpec((1, N_EXPERTS, tm), lambda b, i: (b, 0, i))],
        out_shape=[jax.ShapeDtypeStruct((bsz, s, d), F32),
                   jax.ShapeDtypeStruct((bsz, s, d), BF16),
                   jax.ShapeDtypeStruct((bsz, N_EXPERTS, s), F32)],
        compiler_params=_params(("parallel", "parallel"), VMEM_LIMIT),
        name="out_proj_norm2_router",
    )(y_lru, y_att, x, gate1, scale2, shift2, g_norm2.reshape(1, d), w_out.astype(BF16),
      w_router.T.astype(BF16))


def _route_kernel(aff_ref, pos_ref, cnt_ref, *, cap, n_tok_chunks):
    aff = aff_ref[0]
    n_e, s = aff.shape
    bits = lax.bitcast_convert_type(aff, jnp.int32)
    capf = float(cap)

    def count(mask):
        return jnp.sum(jnp.where(mask, 1.0, 0.0), axis=-1, keepdims=True)

    tau = jnp.zeros((n_e, 1), jnp.int32)
    for bit in range(30, -1, -1):
        cand = tau | (1 << bit)
        tau = jnp.where(count(bits >= cand) >= capf, cand, tau)
    gt = bits > tau
    eq = bits == tau
    need = capf - count(gt)

    blk = min(256, s // n_tok_chunks)
    r = lax.broadcasted_iota(jnp.int32, (blk, blk), 0)
    cidx = lax.broadcasted_iota(jnp.int32, (blk, blk), 1)
    upper = jnp.where(r < cidx, 1.0, 0.0).astype(BF16)

    def prefix_blocks(mask):
        off = jnp.zeros((n_e, 1), F32)
        pieces, offs = [], []
        for k in range(s // blk):
            mb = jnp.where(mask[:, k * blk:(k + 1) * blk], 1.0, 0.0)
            offs.append(off)
            pieces.append(jnp.dot(mb.astype(BF16), upper, preferred_element_type=F32) + off)
            off = off + jnp.sum(mb, axis=-1, keepdims=True)
        return pieces, offs

    eq_rank, _ = prefix_blocks(eq)
    sel_blocks = []
    for k in range(s // blk):
        sl = slice(k * blk, (k + 1) * blk)
        sel_blocks.append(jnp.logical_or(gt[:, sl], jnp.logical_and(eq[:, sl], eq_rank[k] < need)))
    sel = jnp.concatenate(sel_blocks, axis=1)
    slot, offs = prefix_blocks(sel)
    for k in range(s // blk):
        pos_ref[0, :, k * blk:(k + 1) * blk] = jnp.where(sel_blocks[k], slot[k], -1.0).astype(jnp.int32)

    lane = lax.broadcasted_iota(jnp.int32, (n_e, 128), 1)
    cnt = jnp.zeros((n_e, 128), F32)
    per = (s // n_tok_chunks) // blk
    for j in range(n_tok_chunks):
        cnt = jnp.where(lane == j, offs[j * per], cnt)
    cnt_ref[0] = cnt.astype(jnp.int32)


def _routing(aff, cap):
    bsz, n_e, s = aff.shape
    group = math.gcd(bsz, ROUTE_GROUP)
    rows = group * n_e
    pos, cnt = pl.pallas_call(
        functools.partial(_route_kernel, cap=cap, n_tok_chunks=N_TOK_CHUNKS),
        grid=(bsz // group,),
        in_specs=[pl.BlockSpec((1, rows, s), lambda b: (b, 0, 0))],
        out_specs=[pl.BlockSpec((1, rows, s), lambda b: (b, 0, 0)),
                   pl.BlockSpec((1, rows, 128), lambda b: (b, 0, 0))],
        out_shape=[jax.ShapeDtypeStruct((bsz // group, rows, s), jnp.int32),
                   jax.ShapeDtypeStruct((bsz // group, rows, 128), jnp.int32)],
        compiler_params=_params(("parallel",)),
        name="expert_choice_routing",
    )(aff.reshape(bsz // group, rows, s))
    return pos.reshape(bsz, n_e, s), cnt.reshape(bsz, n_e, 128)


def _chunking(pos):
    return pos.shape[1], N_TOK_CHUNKS, pos.shape[2] // N_TOK_CHUNKS


def _chunk_row(ref, e, j):
    t = _chunking(ref)[2]
    return ref[0, pl.ds(e, 1), pl.ds(pl.multiple_of(j * t, t), t)]


def _slot_windows(cnt_ref, pos_ref, b, j, k, cap):
    n_e, n_j, t = _chunking(pos_ref)
    slot_iota = lax.broadcasted_iota(jnp.int32, (MOE_WINDOW, t), 0)
    sels, firsts = [], []
    for e in range(n_e):
        lo = cnt_ref[(b * n_e + e) * n_j + j]
        start = lax.shift_left(lax.shift_right_logical(lo, 4), 4) + k * MOE_WINDOW
        w0 = jnp.minimum(start, cap - MOE_WINDOW)
        slots = slot_iota + w0
        sels.append(jnp.logical_and(_chunk_row(pos_ref, e, j) == slots, slots >= start))
        firsts.append(pl.multiple_of(e * cap + w0, 16))
    return sels, firsts


def _stacked_one_hot(sels):
    return jnp.concatenate([jnp.where(sel, 1.0, 0.0).astype(BF16) for sel in sels], axis=0)


def _window_passes(cnt_ref, pos_ref, b, j, cap):
    n_e, n_j, _ = _chunking(pos_ref)
    n_win = jnp.int32(1)
    for e in range(n_e):
        at = (b * n_e + e) * n_j
        start = lax.shift_left(lax.shift_right_logical(cnt_ref[at + j], 4), 4)
        hi = jnp.where(j + 1 < n_j, cnt_ref[at + jnp.minimum(j + 1, n_j - 1)], cap)
        n_win = jnp.maximum(n_win, lax.shift_right_logical(hi - start + (MOE_WINDOW - 1), MOE_WINDOW.bit_length() - 1))
    return n_win


def _dispatch_kernel(cnt_ref, pos_ref, aff_ref, h2_ref, xe_ref, g_ref, *, cap):
    b = pl.program_id(0)
    t = _chunking(pos_ref)[2]

    @pl.when(pl.program_id(1) == 0)
    def _():
        xe_ref[...] = jnp.zeros_like(xe_ref)
        g_ref[...] = jnp.zeros_like(g_ref)

    def chunk(jj, carry):
        j = pl.program_id(1) * MOE_CHUNKS + jj
        toks = pl.ds(pl.multiple_of(jj * t, t), t)

        def one_pass(k, inner):
            sels, firsts = _slot_windows(cnt_ref, pos_ref, b, j, k, cap)
            rows = jnp.dot(_stacked_one_hot(sels), h2_ref[0, toks, :], preferred_element_type=F32).astype(BF16)
            for e, (sel, first) in enumerate(zip(sels, firsts)):
                window = pl.ds(first, MOE_WINDOW)
                xe_ref[0, window, :] += rows[e * MOE_WINDOW:(e + 1) * MOE_WINDOW, :]
                weight = jnp.sum(jnp.where(sel, _chunk_row(aff_ref, e, j), 0.0), axis=-1, keepdims=True)
                g_ref[0, window, :] += jnp.broadcast_to(weight, (MOE_WINDOW, g_ref.shape[-1]))
            return inner

        one_pass(0, 0)
        lax.fori_loop(1, _window_passes(cnt_ref, pos_ref, b, j, cap), one_pass, 0)
        return carry

    lax.fori_loop(0, MOE_CHUNKS, chunk, 0)


def _dispatch(cnt, pos, aff, h2, cap):
    bsz, s, d = h2.shape
    n_e, n_j, t = _chunking(pos)
    per_token = pl.BlockSpec((1, n_e, s), lambda b, j, c: (b, 0, 0))
    grid_spec = pltpu.PrefetchScalarGridSpec(
        num_scalar_prefetch=1,
        grid=(bsz, n_j // MOE_CHUNKS),
        in_specs=[per_token, per_token,
                  pl.BlockSpec((1, MOE_CHUNKS * t, d), lambda b, j, c: (b, j, 0))],
        out_specs=[pl.BlockSpec((1, n_e * cap, d), lambda b, j, c: (b, 0, 0)),
                   pl.BlockSpec((1, n_e * cap, LANES), lambda b, j, c: (b, 0, 0))],
    )
    return pl.pallas_call(
        functools.partial(_dispatch_kernel, cap=cap),
        grid_spec=grid_spec,
        out_shape=[jax.ShapeDtypeStruct((bsz, n_e * cap, d), BF16),
                   jax.ShapeDtypeStruct((bsz, n_e * cap, LANES), F32)],
        compiler_params=_params(("parallel", "arbitrary"), VMEM_LIMIT),
        name="expert_dispatch",
    )(cnt.reshape(-1), pos, aff, h2)


def _moe_kernel(xe_ref, g_ref, w1_ref, w3_ref, w2_ref, gate_ref, y_ref):
    xe = xe_ref[0]
    a = jnp.dot(xe, w1_ref[0], preferred_element_type=F32)
    gate = jnp.dot(xe, w3_ref[0], preferred_element_type=F32)
    hmid = ((a * _sigmoid(a)) * gate).astype(BF16)
    y = jnp.dot(hmid, w2_ref[0], preferred_element_type=F32) * g_ref[0][:, :1] * gate_ref[0]
    y_ref[0] = y.astype(BF16)


def _moe(xe, g, w1, w3, w2, gate2, cap):
    bsz, _, d = xe.shape
    n_e, _, f = w1.shape
    wspec = lambda b, e: (e, 0, 0)
    slot_rows = lambda b, e: (b, e, 0)
    return pl.pallas_call(
        _moe_kernel,
        grid=(bsz, n_e),
        in_specs=[pl.BlockSpec((1, cap, d), slot_rows),
                  pl.BlockSpec((1, cap, g.shape[-1]), slot_rows),
                  pl.BlockSpec((1, d, f), wspec),
                  pl.BlockSpec((1, d, f), wspec),
                  pl.BlockSpec((1, f, d), wspec),
                  pl.BlockSpec((1, 1, d), lambda b, e: (b, 0, 0))],
        out_specs=pl.BlockSpec((1, cap, d), slot_rows),
        out_shape=jax.ShapeDtypeStruct((bsz, n_e * cap, d), BF16),
        compiler_params=_params(("parallel", "parallel"), VMEM_LIMIT),
        name="expert_choice_ffn",
    )(xe, g, w1, w3, w2, gate2)


def _combine_kernel(cnt_ref, pos_ref, y_hbm, x1_ref, out_ref, y_buf, y_sem, *, cap):
    b = pl.program_id(0)
    step = pl.program_id(1)
    t = _chunking(pos_ref)[2]
    n_parts = y_sem.shape[1]
    part_rows = y_buf.shape[1] // n_parts
    y_ref = y_buf.at[b % 2]

    def part(seq, q):
        rows = pl.ds(q * part_rows, part_rows)
        return pltpu.make_async_copy(y_hbm.at[seq, rows, :], y_buf.at[seq % 2, rows, :], y_sem.at[seq % 2, q])

    @pl.when(jnp.logical_and(b == 0, step == 0))
    def _():
        for q in range(n_parts):
            part(b, q).start()

    @pl.when(b + 1 < pl.num_programs(0))
    def _():
        part(b + 1, step).start()

    @pl.when(step == 0)
    def _():
        for q in range(n_parts):
            part(b, q).wait()

    def chunk(jj, carry):
        j = pl.program_id(1) * MOE_CHUNKS + jj
        toks = pl.ds(pl.multiple_of(jj * t, t), t)

        def windows(k):
            sels, firsts = _slot_windows(cnt_ref, pos_ref, b, j, k, cap)
            rows = jnp.concatenate([y_ref[pl.ds(first, MOE_WINDOW), :] for first in firsts], axis=0)
            return lax.dot_general(_stacked_one_hot(sels), rows, TN_DIMS, preferred_element_type=F32)

        out_ref[0, toks, :] = x1_ref[0, toks, :] + windows(0)

        def more(k, inner):
            out_ref[0, toks, :] += windows(k)
            return inner

        lax.fori_loop(1, _window_passes(cnt_ref, pos_ref, b, j, cap), more, 0)
        return carry

    lax.fori_loop(0, MOE_CHUNKS, chunk, 0)


def _combine(cnt, pos, y, x1, cap):
    bsz, s, d = x1.shape
    n_e, n_j, t = _chunking(pos)
    n_steps = n_j // MOE_CHUNKS
    grid_spec = pltpu.PrefetchScalarGridSpec(
        num_scalar_prefetch=1,
        grid=(bsz, n_steps),
        in_specs=[pl.BlockSpec((1, n_e, s), lambda b, j, c: (b, 0, 0)),
                  pl.BlockSpec(memory_space=pl.ANY),
                  pl.BlockSpec((1, MOE_CHUNKS * t, d), lambda b, j, c: (b, j, 0))],
        out_specs=pl.BlockSpec((1, MOE_CHUNKS * t, d), lambda b, j, c: (b, j, 0)),
        scratch_shapes=[pltpu.VMEM((2, n_e * cap, d), BF16), pltpu.SemaphoreType.DMA((2, n_steps))],
    )
    return pl.pallas_call(
        functools.partial(_combine_kernel, cap=cap),
        grid_spec=grid_spec,
        out_shape=jax.ShapeDtypeStruct((bsz, s, d), F32),
        compiler_params=_params(("arbitrary", "arbitrary"), VMEM_LIMIT),
        name="expert_combine",
    )(cnt.reshape(-1), pos, y, x1)


def kernel(x, c, w_mod, b_mod, g_norm1, w_in, conv_w, conv_b, lru_w_a, lru_b_a, lru_w_x, lru_b_x,
           lru_lambda, g_q, g_k, lambda_qk, g_attn_out, rel_bias, w_out, g_norm2, w_router, w1, w3, w2):
    bsz, s, d = x.shape
    depth = w_mod.shape[0]
    cap = max(1, EC_FACTOR * s // N_EXPERTS)
    bias_tiles = _bias_tiles(rel_bias, min(ATT_TILE, s))
    for l in range(depth):
        mod = _modulation(c, w_mod[l], b_mod[l])
        shift1, scale1, gate1, shift2, scale2, gate2 = [m.reshape(bsz, 1, d) for m in jnp.split(mod, 6, axis=-1)]
        x_lru, gz, qn, kn, vt = _in_projection(x, scale1, shift1, g_norm1[l], w_in[l], g_q[l], g_k[l],
                                               min(IN_PROJ_ROWS, s))
        y_lru = _rg_lru(x_lru, gz, conv_w[l], conv_b[l], lru_w_a[l], lru_b_a[l], lru_w_x[l], lru_b_x[l],
                        lru_lambda[l])
        lam_init = 0.8 - 0.6 * math.exp(-0.3 * l)
        y_att, (w1b, w3b, w2b) = _diff_attention(qn, kn, vt, bias_tiles, lambda_qk[l], g_attn_out[l], lam_init,
                                                 (w1[l], w3[l], w2[l]))
        x1, h2, aff = _out_projection(y_lru, y_att, x, gate1, scale2, shift2, g_norm2[l], w_out[l],
                                      w_router[l], min(OUT_PROJ_ROWS, s))
        pos, cnt = _routing(aff, cap)
        cnt = cnt[:, :, :N_TOK_CHUNKS]
        xe, g = _dispatch(cnt, pos, aff, h2, cap)
        y = _moe(xe, g, w1b, w3b, w2b, gate2, cap)
        x = _combine(cnt, pos, y, x1, cap)
    return x
```

```python
import functools
import math

import jax
import jax.numpy as jnp
from jax import lax
from jax.experimental import pallas as pl
from jax.experimental.pallas import tpu as pltpu

F32 = jnp.float32
BF16 = jnp.bfloat16

D_MODEL = 1024
D_LRU = 512
LRU_BLOCK = 64
LRU_C = 8.0
CONV_W = 4
N_HEADS = 4
HEAD_DV = 128
HEAD_DK = 64
D_ATT = N_HEADS * HEAD_DV
N_BUCKETS = 32
N_EXPERTS = 16
EC_FACTOR = 2
EPS = 1e-6
F32_TINY = 2.0 ** -126
LOG2E = math.log2(math.e)

IN_PROJ_ROWS = 1024
OUT_PROJ_ROWS = 1024
LRU_HALF = 256
SCAN_LANES = 8
ATT_TILE = 256
PV_ONES = 16
ATT_WIDTH = 1
ROUTE_GROUP = 8
N_TOK_CHUNKS = 16
MXU_TILE = 256
LANES = 128
MOE_WINDOW = 64
MOE_CHUNKS = 4
VMEM_LIMIT = 56 * 1024 * 1024

NT_DIMS = (((1,), (1,)), ((), ()))
TN_DIMS = (((0,), (0,)), ((), ()))


def _sigmoid(x):
    return 1.0 / (1.0 + jnp.exp(-x))


def _params(sem, vmem=None):
    return pltpu.CompilerParams(dimension_semantics=sem, vmem_limit_bytes=vmem)


def _mod_kernel(c_ref, w_ref, b_ref, o_ref):
    c = c_ref[...]
    o_ref[...] = jnp.dot((c * _sigmoid(c)).astype(BF16), w_ref[...].astype(BF16),
                         preferred_element_type=F32) + b_ref[...]


def _modulation(c, w_mod, b_mod):
    bsz, d = c.shape
    n = w_mod.shape[1]
    return pl.pallas_call(
        _mod_kernel,
        grid=(n // d,),
        in_specs=[pl.BlockSpec((bsz, d), lambda j: (0, 0)),
                  pl.BlockSpec((d, d), lambda j: (0, j)),
                  pl.BlockSpec((1, d), lambda j: (0, j))],
        out_specs=pl.BlockSpec((bsz, d), lambda j: (0, j)),
        out_shape=jax.ShapeDtypeStruct((bsz, n), F32),
        compiler_params=_params(("arbitrary",)),
        name="adaln_mod",
    )(c, w_mod, b_mod.reshape(1, n))


def _bias_kernel(tab_ref, o_ref):
    h = pl.program_id(0)
    t = o_ref.shape[-1]
    key = lax.broadcasted_iota(jnp.int32, (t, t), 0)
    qry = lax.broadcasted_iota(jnp.int32, (t, t), 1)
    half = N_BUCKETS // 2
    max_exact = half // 2
    for d in range(5):
        if d == 0:
            o_ref[0, d] = jnp.full((t, t), tab_ref[half - 1, h] * LOG2E, F32)
        elif d == 4:
            o_ref[0, d] = jnp.full((t, t), tab_ref[N_BUCKETS - 1, h] * LOG2E, F32)
        else:
            rel = (d - 2) * t + key - qry
            n = jnp.abs(rel)
            n2 = n * n
            large = jnp.full((t, t), max_exact, jnp.int32)
            for k in range(1, half - max_exact):
                large = large + jnp.where(n2 >= (max_exact * max_exact) * (2 ** k), 1, 0)
            idx = jnp.where(n < max_exact, n, large) + jnp.where(rel > 0, half, 0)
            val = jnp.zeros((t, t), F32)
            for j in range(N_BUCKETS):
                val = jnp.where(idx == j, tab_ref[j, h] * LOG2E, val)
            o_ref[0, d] = val


def _bias_tiles(rel_bias, t):
    return pl.pallas_call(
        _bias_kernel,
        grid=(N_HEADS,),
        in_specs=[pl.BlockSpec(memory_space=pltpu.SMEM)],
        out_specs=pl.BlockSpec((1, 5, t, t), lambda h: (h, 0, 0, 0)),
        out_shape=jax.ShapeDtypeStruct((N_HEADS, 5, t, t), F32),
        compiler_params=_params(("arbitrary",)),
        name="t5_bias_tiles",
    )(rel_bias)


def _inproj_kernel(x_ref, sc_ref, sh_ref, g1_ref, w_ref, wvt_ref, mseg_ref, gq_ref, gk_ref,
                   xl_ref, gz_ref, q_ref, k_ref, vt_ref):
    x = x_ref[0]
    ms = jnp.mean(x * x, axis=-1, keepdims=True)
    h = (x * lax.rsqrt(ms + EPS) * g1_ref[...]) * (1.0 + sc_ref[0]) + sh_ref[0]
    hb = h.astype(BF16)

    def proj(lo, width):
        return jnp.dot(hb, w_ref[:, lo:lo + width], preferred_element_type=F32)

    def qk_norm(t, g):
        tt = (t * t).astype(BF16)
        m = mseg_ref.shape[0]
        ss = jnp.concatenate([jnp.dot(tt[:, c:c + m], mseg_ref[...], preferred_element_type=F32)
                              for c in range(0, tt.shape[1], m)], axis=1)
        return t * lax.rsqrt(ss * (1.0 / HEAD_DK) + EPS) * g

    xl_ref[0] = proj(0, D_LRU)
    z = proj(D_LRU, D_LRU)
    cdf = 0.5 * (1.0 + jnp.tanh(math.sqrt(2.0 / math.pi) * (z + 0.044715 * (z * z * z))))
    gz_ref[0] = (z * cdf).astype(BF16)
    q_ref[0] = (qk_norm(proj(2 * D_LRU, D_ATT), gq_ref[...]) * (HEAD_DK ** -0.5 * LOG2E)).astype(BF16)
    k_ref[0] = qk_norm(proj(2 * D_LRU + D_ATT, D_ATT), gk_ref[...]).astype(BF16)
    vt_ref[0] = lax.dot_general(wvt_ref[...], hb, NT_DIMS,
                                preferred_element_type=F32).astype(BF16)


def _in_projection(x, scale1, shift1, g_norm1, w_in, g_q, g_k, tm):
    bsz, s, d = x.shape
    n = w_in.shape[1] - D_ATT
    w_main = w_in[:, :n].astype(BF16)
    w_vt = w_in[:, n:].T.astype(BF16)
    seg = jnp.arange(MXU_TILE, dtype=jnp.int32) // HEAD_DK
    mseg = (seg[:, None] == seg[None, :]).astype(BF16)
    n_sub = D_ATT // HEAD_DK
    row = lambda b, i: (b, i, 0)
    vec = lambda b, i: (b, 0, 0)
    full = lambda b, i: (0, 0)
    out_block = pl.BlockSpec((1, tm, D_LRU), row)
    return pl.pallas_call(
        _inproj_kernel,
        grid=(bsz, s // tm),
        in_specs=[pl.BlockSpec((1, tm, d), row),
                  pl.BlockSpec((1, 1, d), vec),
                  pl.BlockSpec((1, 1, d), vec),
                  pl.BlockSpec((1, d), full),
                  pl.BlockSpec((d, n), full),
                  pl.BlockSpec((D_ATT, d), full),
                  pl.BlockSpec((MXU_TILE, MXU_TILE), full),
                  pl.BlockSpec((1, D_ATT), full),
                  pl.BlockSpec((1, D_ATT), full)],
        out_specs=[out_block] * 4 + [pl.BlockSpec((1, D_ATT, tm), lambda b, i: (b, 0, i))],
        out_shape=[jax.ShapeDtypeStruct((bsz, s, D_LRU), F32)]
                  + [jax.ShapeDtypeStruct((bsz, s, D_LRU), BF16)] * 3
                  + [jax.ShapeDtypeStruct((bsz, D_ATT, s), BF16)],
        compiler_params=_params(("parallel", "parallel"), VMEM_LIMIT),
        name="norm1_in_proj",
    )(x, scale1, shift1, g_norm1.reshape(1, d), w_main, w_vt, mseg,
      jnp.tile(g_q, n_sub).reshape(1, D_ATT), jnp.tile(g_k, n_sub).reshape(1, D_ATT))


def _lru_kernel(x_ref, gz_ref, cw_ref, cb_ref, wa_ref, wx_ref, ba_ref, bx_ref, lam_ref, y_ref,
                xpad, a_f, u_f, a_b, u_b, *, tc):
    s = x_ref.shape[1]
    c = x_ref.shape[2]
    n_chunks = s // tc
    n_slab = c // 128
    pitch = a_f.shape[1] // n_chunks
    zeros8 = jnp.zeros((8, c), F32)
    xpad[0:8, :] = zeros8
    xpad[s + 8:s + 16, :] = zeros8

    def fill(ci, carry):
        t0 = pl.multiple_of(ci * tc, tc)
        xpad[pl.ds(t0 + 8, tc), :] = x_ref[0, pl.ds(t0, tc), :]
        return carry

    lax.fori_loop(0, n_chunks, fill, 0)

    cw = cw_ref[...]
    cb = cb_ref[...]
    decay = []
    for d in range(2):
        lam = lam_ref[d]
        softplus_neg = jnp.maximum(-lam, 0.0) + jnp.log(1.0 + jnp.exp(-jnp.abs(lam)))
        decay.append((-LRU_C * LOG2E) * softplus_neg)
    a_scr = (a_f, a_b)
    u_scr = (u_f, u_b)

    def gates(ci, carry):
        t0 = pl.multiple_of(ci * tc, tc)
        xw = xpad[pl.ds(t0, tc + 16), :]
        xc = (cw[0:1] * pltpu.roll(xw, 2, 0)[8:8 + tc]
              + cw[1:2] * pltpu.roll(xw, 1, 0)[8:8 + tc]
              + cw[2:3] * xw[8:8 + tc]
              + cw[3:4] * pltpu.roll(xw, tc + 15, 0)[8:8 + tc]) + cb
        xcb = xc.astype(BF16)
        for d in range(2):
            r = _sigmoid(jnp.dot(xcb, wa_ref[d], preferred_element_type=F32) + ba_ref[d])
            i = _sigmoid(jnp.dot(xcb, wx_ref[d], preferred_element_type=F32) + bx_ref[d])
            a = jnp.exp2(r * decay[d])
            v = 1.0 - a * a
            u = (v * lax.rsqrt(jnp.maximum(v, F32_TINY))) * (i * xc)
            r0 = pl.multiple_of(ci * pitch, 8)
            for sl in range(n_slab):
                a_scr[d][sl, pl.ds(r0, tc), :] = a[:, sl * 128:(sl + 1) * 128]
                u_scr[d][sl, pl.ds(r0, tc), :] = u[:, sl * 128:(sl + 1) * 128]
        return carry

    lax.fori_loop(0, n_chunks, gates, 0, unroll=2)

    def rows(tt):
        return pl.ds(tt, n_chunks, stride=pitch)

    def step(tt, carry):
        out = []
        for d in range(2):
            t_loc = tt if d == 0 else tc - 1 - tt
            for sl in range(n_slab):
                h, p = carry[len(out)]
                a8 = a_scr[d][sl, rows(t_loc), :]
                h = a8 * h + u_scr[d][sl, rows(t_loc), :]
                p = a8 * p
                u_scr[d][sl, rows(t_loc), :] = h
                a_scr[d][sl, rows(t_loc), :] = p
                out.append((h, p))
        return tuple(out)

    init = (jnp.zeros((n_chunks, 128), F32), jnp.ones((n_chunks, 128), F32))
    lax.fori_loop(0, tc, step, (init,) * (2 * n_slab), unroll=4)

    for sl in range(n_slab):
        lanes = slice(sl * 128, (sl + 1) * 128)
        h_end, p_end = u_f[sl, rows(tc - 1), :], a_f[sl, rows(tc - 1), :]
        h_beg, p_beg = u_b[sl, rows(0), :], a_b[sl, rows(0), :]
        carry_f = [jnp.zeros((1, 128), F32)]
        for r in range(1, n_chunks):
            carry_f.append(p_end[r - 1:r] * carry_f[-1] + h_end[r - 1:r])
        carry_b = [jnp.zeros((1, 128), F32)]
        for r in range(n_chunks - 2, -1, -1):
            carry_b.insert(0, p_beg[r + 1:r + 2] * carry_b[0] + h_beg[r + 1:r + 2])
        for r in range(n_chunks):
            blk = slice(r * pitch, r * pitch + tc)
            hsum = (u_f[sl, blk, :] + a_f[sl, blk, :] * carry_f[r]
                    + u_b[sl, blk, :] + a_b[sl, blk, :] * carry_b[r])
            y_ref[0, r * tc:(r + 1) * tc, lanes] = (hsum * gz_ref[0, r * tc:(r + 1) * tc, lanes].astype(F32)).astype(BF16)


def _block_diag(w, half):
    n_dir, n_blocks, blk, _ = w.shape
    per = half // blk
    n_half = n_blocks // per
    w = w.reshape(n_dir, n_half, per, blk, blk)
    eye = jnp.eye(per, dtype=w.dtype)
    out = w[:, :, :, :, None, :] * eye[None, None, :, None, :, None]
    return out.reshape(n_dir, n_half, half, half)


def _rg_lru(x_lru, gz, conv_w, conv_b, w_a, b_a, w_x, b_x, lam):
    bsz, s, c = x_lru.shape
    half = LRU_HALF
    tc = s // SCAN_LANES
    pitch = tc + 8 if (tc // 8) % 2 == 0 else tc + 16
    n_half = c // half
    wa = _block_diag(w_a, half).astype(BF16)
    wx = _block_diag(w_x, half).astype(BF16)
    seq = lambda b, p: (b, 0, p)
    chan = lambda b, p: (0, p)
    dirchan = lambda b, p: (0, 0, p)
    blk = lambda b, p: (0, p, 0, 0)
    return pl.pallas_call(
        functools.partial(_lru_kernel, tc=tc),
        grid=(bsz, n_half),
        in_specs=[pl.BlockSpec((1, s, half), seq),
                  pl.BlockSpec((1, s, half), seq),
                  pl.BlockSpec((CONV_W, half), chan),
                  pl.BlockSpec((1, half), chan),
                  pl.BlockSpec((2, None, half, half), blk),
                  pl.BlockSpec((2, None, half, half), blk),
                  pl.BlockSpec((2, 1, half), dirchan),
                  pl.BlockSpec((2, 1, half), dirchan),
                  pl.BlockSpec((2, 1, half), dirchan)],
        out_specs=pl.BlockSpec((1, s, half), seq),
        out_shape=jax.ShapeDtypeStruct((bsz, s, c), BF16),
        scratch_shapes=[pltpu.VMEM((s + 16, half), F32)]
                       + [pltpu.VMEM((half // 128, SCAN_LANES * pitch, 128), F32)] * 4,
        compiler_params=_params(("parallel", "parallel"), VMEM_LIMIT),
        name="rg_lru",
    )(x_lru, gz, conv_w.reshape(CONV_W, c), conv_b.reshape(1, c), wa, wx,
      b_a.reshape(2, 1, c), b_x.reshape(2, 1, c), lam.reshape(2, 1, c))


def _attn_kernel(q_ref, k_ref, vt_ref, bias_ref, lq_ref, go_ref, w1_ref, w3_ref, w2_ref,
                 o_ref, w1b_ref, w3b_ref, w2b_ref,
                 s_even, s_odd, m_even, m_odd, o_even, o_odd, p_scr, *, lam_init):
    for w_ref, wb_ref in ((w1_ref, w1b_ref), (w3_ref, w3b_ref), (w2_ref, w2b_ref)):
        wb_ref[...] = w_ref[...].astype(BF16)

    t = bias_ref.shape[-1]
    width = s_even.shape[0]
    n_k = k_ref.shape[1] // t
    n_q = q_ref.shape[1] // t
    n_steps = n_q // width
    subs = [(w, u) for w in range(width) for u in range(2)]
    chunk = 2
    pv_chunk = 2
    n_c = n_k // chunk

    even = dict(s=s_even, m=m_even, o=o_even)
    odd = dict(s=s_odd, m=m_odd, o=o_odd)
    ones_rows = jnp.ones((PV_ONES, pv_chunk * t), BF16)

    def fold(x, op):
        parts = [x[r * 8:(r + 1) * 8, :] for r in range(t // 8)]
        acc = parts[:2]
        for r in range(2, len(parts)):
            acc[r % 2] = op(acc[r % 2], parts[r])
        return op(acc[0], acc[1])

    def bias_tile(j, tile):
        return bias_ref[0, jnp.clip(j - tile, -2, 2) + 2]

    def finish(prev, w, tile):
        outs = [prev['o'][w, u, :HEAD_DV, :] / prev['o'][w, u, HEAD_DV:HEAD_DV + 1, :] for u in range(2)]
        lq = lq_ref[...]
        lam = (jnp.exp(jnp.sum(lq[0:1] * lq[1:2], axis=-1, keepdims=True))
               - jnp.exp(jnp.sum(lq[2:3] * lq[3:4], axis=-1, keepdims=True)) + lam_init)
        o = (outs[0] - lam * outs[1]).T
        ms = jnp.mean(o * o, axis=-1, keepdims=True)
        o_ref[0, pl.ds(pl.multiple_of(tile * t, t), t), :] = (
            (o * lax.rsqrt(ms + EPS) * go_ref[...]) * (1.0 - lam_init)).astype(BF16)

    def step(i, cur, prev, scores=True, softmax=True, store=True):
        tiles_a = [i * width + w for w in range(width)]
        for w in range(width if store else 0):
            finish(prev, w, jnp.maximum((i - 2) * width + w, 0))
        s_cur, s_prev = cur['s'], prev['s']
        m_b = {wu: jnp.max(prev['m'][wu], axis=0, keepdims=True) for wu in subs} if softmax else {}
        m_acc = {wu: jnp.full((8, t), -jnp.inf, F32) for wu in subs}
        q_sub = {}
        if scores:
            for w in range(width):
                q = q_ref[0, pl.ds(pl.multiple_of(tiles_a[w] * t, t), t), :]
                lane = lax.broadcasted_iota(jnp.int32, q.shape, 1)
                zero = jnp.zeros_like(q)
                q_sub[w, 0] = jnp.where(lane < HEAD_DK, q, zero)
                q_sub[w, 1] = jnp.where(lane >= HEAD_DK, q, zero)
        for c in range(n_c):
            rows = slice(c * chunk * t, (c + 1) * chunk * t)
            for w, u in (subs if scores else []):
                sc_all = lax.dot_general(k_ref[0, rows, :], q_sub[w, u], NT_DIMS, preferred_element_type=F32)
                for jj in range(chunk):
                    j = c * chunk + jj
                    sc = sc_all[jj * t:(jj + 1) * t, :] + bias_tile(j, tiles_a[w])
                    s_cur[w, u, j] = sc
                    m_acc[w, u] = jnp.maximum(m_acc[w, u], fold(sc, jnp.maximum))
            for w, u in (subs if softmax else []):
                for jj in range(chunk):
                    j = c * chunk + jj
                    p = jnp.exp2(s_prev[w, u, j] - m_b[w, u])
                    p_scr[w, u, j * t:(j + 1) * t, :] = p.astype(BF16)
                if ((c + 1) * chunk) % pv_chunk == 0:
                    keys = slice(((c + 1) * chunk - pv_chunk) * t, (c + 1) * chunk * t)
                    part = jnp.dot(jnp.concatenate([vt_ref[0, :, keys], ones_rows], axis=0), p_scr[w, u, keys, :],
                                   preferred_element_type=F32)
                    if (c + 1) * chunk == pv_chunk:
                        cur['o'][w, u] = part
                    else:
                        cur['o'][w, u] += part

        if scores:
            for w, u in subs:
                cur['m'][w, u] = m_acc[w, u]

    def pair(pi, carry, **stages):
        @pl.when(pi >= 0)
        def _():
            step(2 * pi, even, odd, **stages.get('even', {}))

        @pl.when(pi < n_steps)
        def _():
            step(2 * pi + 1, odd, even, **stages.get('odd', {}))

        return carry

    pair(jnp.minimum(pl.program_id(0), 0), 0, even=dict(softmax=False, store=False), odd=dict(store=False))
    lax.fori_loop(1, n_steps // 2, pair, 0)
    step(n_steps, even, odd, scores=False)
    for w in range(width):
        finish(even, w, n_q - width + w)


def _diff_attention(qn, kn, vt, bias_tiles, lambda_qk, g_o, lam_init, expert_w):
    bsz, s, _ = qn.shape
    w_rows = expert_w[0].shape[0] * expert_w[0].shape[1]
    w_cols = expert_w[0].shape[2]
    assert all(w.shape == expert_w[0].shape for w in expert_w) and w_rows % (bsz * N_HEADS * 16) == 0
    w_blk = pl.BlockSpec((w_rows // (bsz * N_HEADS), w_cols), lambda b, h: (b * N_HEADS + h, 0))
    t = bias_tiles.shape[-1]
    n_k = s // t
    width = ATT_WIDTH
    assert (s // t) % (2 * width) == 0, "the query-tile pipeline advances two groups of tiles per loop trip"
    scores = pltpu.VMEM((width, 2, n_k, t, t), F32)
    maxima = pltpu.VMEM((width, 2, 8, t), F32)
    pv_acc = pltpu.VMEM((width, 2, HEAD_DV + PV_ONES, t), F32)
    seq = lambda b, h: (b, 0, h)
    outs = pl.pallas_call(
        functools.partial(_attn_kernel, lam_init=lam_init),
        grid=(bsz, N_HEADS),
        in_specs=[pl.BlockSpec((1, s, HEAD_DV), seq),
                  pl.BlockSpec((1, s, HEAD_DV), seq),
                  pl.BlockSpec((1, HEAD_DV, s), lambda b, h: (b, h, 0)),
                  pl.BlockSpec((1, 5, t, t), lambda b, h: (h, 0, 0, 0)),
                  pl.BlockSpec((4, HEAD_DK), lambda b, h: (0, 0)),
                  pl.BlockSpec((1, HEAD_DV), lambda b, h: (0, 0)),
                  w_blk, w_blk, w_blk],
        out_specs=[pl.BlockSpec((1, s, HEAD_DV), seq), w_blk, w_blk, w_blk],
        out_shape=[jax.ShapeDtypeStruct((bsz, s, D_ATT), BF16)]
                  + [jax.ShapeDtypeStruct((w_rows, w_cols), BF16)] * 3,
        scratch_shapes=[scores, scores, maxima, maxima, pv_acc, pv_acc, pltpu.VMEM((width, 2, s, t), BF16)],
        compiler_params=_params(("parallel", "parallel"), VMEM_LIMIT),
        name="diff_attention",
    )(qn, kn, vt, bias_tiles, lambda_qk, g_o.reshape(1, HEAD_DV), *[w.reshape(w_rows, w_cols) for w in expert_w])
    return outs[0], [wb.reshape(expert_w[0].shape) for wb in outs[1:]]


def _outproj_kernel(yl_ref, ya_ref, x_ref, gate_ref, sc_ref, sh_ref, g2_ref, wo_ref, wr_ref,
                    x1_ref, h2_ref, aff_ref):
    mix = (jnp.dot(yl_ref[0], wo_ref[0:D_LRU, :], preferred_element_type=F32)
           + jnp.dot(ya_ref[0], wo_ref[D_LRU:D_LRU + D_ATT, :], preferred_element_type=F32))
    x1 = x_ref[0] + gate_ref[0] * mix
    x1_ref[0] = x1
    ms = jnp.mean(x1 * x1, axis=-1, keepdims=True)
    h2 = (x1 * lax.rsqrt(ms + EPS) * g2_ref[...]) * (1.0 + sc_ref[0]) + sh_ref[0]
    h2b = h2.astype(BF16)
    h2_ref[0] = h2b
    logits = lax.dot_general(wr_ref[...], h2b, NT_DIMS, preferred_element_type=F32)
    ex = jnp.exp(logits - jnp.max(logits, axis=0, keepdims=True))
    aff_ref[0] = ex / jnp.sum(ex, axis=0, keepdims=True)


def _out_projection(y_lru, y_att, x, gate1, scale2, shift2, g_norm2, w_out, w_router, tm):
    bsz, s, d = x.shape
    row = lambda b, i: (b, i, 0)
    vec = lambda b, i: (b, 0, 0)
    full = lambda b, i: (0, 0)
    return pl.pallas_call(
        _outproj_kernel,
        grid=(bsz, s // tm),
        in_specs=[pl.BlockSpec((1, tm, D_LRU), row),
                  pl.BlockSpec((1, tm, D_ATT), row),
                  pl.BlockSpec((1, tm, d), row),
                  pl.BlockSpec((1, 1, d), vec),
                  pl.BlockSpec((1, 1, d), vec),
                  pl.BlockSpec((1, 1, d), vec),
                  pl.BlockSpec((1, d), full),
                  pl.BlockSpec((D_LRU + D_ATT, d), full),
                  pl.BlockSpec((N_EXPERTS, d), full)],
        out_specs=[pl.BlockSpec((1, tm, d), row),
                   pl.BlockSpec((1, tm, d), row),
                   pl.BlockSpec((1, N_EXPERTS, tm), lambda b, i: (b, 0, i))],
        out_shape=[jax.ShapeDtypeStruct((bsz, s, d), F32),
                   jax.ShapeDtypeStruct((bsz, s, d), BF16),
                   jax.ShapeDtypeStruct((bsz, N_EXPERTS, s), F32)],
        compiler_params=_params(("parallel", "parallel"), VMEM_LIMIT),
        name="out_proj_norm2_router",
    )(y_lru, y_att, x, gate1, scale2, shift2, g_norm2.reshape(1, d), w_out.astype(BF16),
      w_router.T.astype(BF16))


def _route_kernel(aff_ref, pos_ref, cnt_ref, *, cap, n_tok_chunks):
    aff = aff_ref[0]
    n_e, s = aff.shape
    bits = lax.bitcast_convert_type(aff, jnp.int32)
    capf = float(cap)

    def count(mask):
        return jnp.sum(jnp.where(mask, 1.0, 0.0), axis=-1, keepdims=True)

    tau = jnp.zeros((n_e, 1), jnp.int32)
    for bit in range(30, -1, -1):
        cand = tau | (1 << bit)
        tau = jnp.where(count(bits >= cand) >= capf, cand, tau)
    gt = bits > tau
    eq = bits == tau
    need = capf - count(gt)

    blk = min(256, s // n_tok_chunks)
    r = lax.broadcasted_iota(jnp.int32, (blk, blk), 0)
    cidx = lax.broadcasted_iota(jnp.int32, (blk, blk), 1)
    upper = jnp.where(r < cidx, 1.0, 0.0).astype(BF16)

    def prefix_blocks(mask):
        off = jnp.zeros((n_e, 1), F32)
        pieces, offs = [], []
        for k in range(s // blk):
            mb = jnp.where(mask[:, k * blk:(k + 1) * blk], 1.0, 0.0)
            offs.append(off)
            pieces.append(jnp.dot(mb.astype(BF16), upper, preferred_element_type=F32) + off)
            off = off + jnp.sum(mb, axis=-1, keepdims=True)
        return pieces, offs

    eq_rank, _ = prefix_blocks(eq)
    sel_blocks = []
    for k in range(s // blk):
        sl = slice(k * blk, (k + 1) * blk)
        sel_blocks.append(jnp.logical_or(gt[:, sl], jnp.logical_and(eq[:, sl], eq_rank[k] < need)))
    sel = jnp.concatenate(sel_blocks, axis=1)
    slot, offs = prefix_blocks(sel)
    for k in range(s // blk):
        pos_ref[0, :, k * blk:(k + 1) * blk] = jnp.where(sel_blocks[k], slot[k], -1.0).astype(jnp.int32)

    lane = lax.broadcasted_iota(jnp.int32, (n_e, 128), 1)
    cnt = jnp.zeros((n_e, 128), F32)
    per = (s // n_tok_chunks) // blk
    for j in range(n_tok_chunks):
        cnt = jnp.where(lane == j, offs[j * per], cnt)
    cnt_ref[0] = cnt.astype(jnp.int32)


def _routing(aff, cap):
    bsz, n_e, s = aff.shape
    group = math.gcd(bsz, ROUTE_GROUP)
    rows = group * n_e
    pos, cnt = pl.pallas_call(
        functools.partial(_route_kernel, cap=cap, n_tok_chunks=N_TOK_CHUNKS),
        grid=(bsz // group,),
        in_specs=[pl.BlockSpec((1, rows, s), lambda b: (b, 0, 0))],
        out_specs=[pl.BlockSpec((1, rows, s), lambda b: (b, 0, 0)),
                   pl.BlockSpec((1, rows, 128), lambda b: (b, 0, 0))],
        out_shape=[jax.ShapeDtypeStruct((bsz // group, rows, s), jnp.int32),
                   jax.ShapeDtypeStruct((bsz // group, rows, 128), jnp.int32)],
        compiler_params=_params(("parallel",)),
        name="expert_choice_routing",
    )(aff.reshape(bsz // group, rows, s))
    return pos.reshape(bsz, n_e, s), cnt.reshape(bsz, n_e, 128)


def _chunking(pos):
    return pos.shape[1], N_TOK_CHUNKS, pos.shape[2] // N_TOK_CHUNKS


def _chunk_row(ref, e, j):
    t = _chunking(ref)[2]
    return ref[0, pl.ds(e, 1), pl.ds(pl.multiple_of(j * t, t), t)]


def _slot_windows(cnt_ref, pos_ref, b, j, k, cap):
    n_e, n_j, t = _chunking(pos_ref)
    slot_iota = lax.broadcasted_iota(jnp.int32, (MOE_WINDOW, t), 0)
    sels, firsts = [], []
    for e in range(n_e):
        lo = cnt_ref[(b * n_e + e) * n_j + j]
        start = lax.shift_left(lax.shift_right_logical(lo, 4), 4) + k * MOE_WINDOW
        w0 = jnp.minimum(start, cap - MOE_WINDOW)
        slots = slot_iota + w0
        sels.append(jnp.logical_and(_chunk_row(pos_ref, e, j) == slots, slots >= start))
        firsts.append(pl.multiple_of(e * cap + w0, 16))
    return sels, firsts


def _stacked_one_hot(sels):
    return jnp.concatenate([jnp.where(sel, 1.0, 0.0).astype(BF16) for sel in sels], axis=0)


def _window_passes(cnt_ref, pos_ref, b, j, cap):
    n_e, n_j, _ = _chunking(pos_ref)
    n_win = jnp.int32(1)
    for e in range(n_e):
        at = (b * n_e + e) * n_j
        start = lax.shift_left(lax.shift_right_logical(cnt_ref[at + j], 4), 4)
        hi = jnp.where(j + 1 < n_j, cnt_ref[at + jnp.minimum(j + 1, n_j - 1)], cap)
        n_win = jnp.maximum(n_win, lax.shift_right_logical(hi - start + (MOE_WINDOW - 1), MOE_WINDOW.bit_length() - 1))
    return n_win


def _dispatch_kernel(cnt_ref, pos_ref, aff_ref, h2_ref, xe_ref, g_ref, *, cap):
    b = pl.program_id(0)
    t = _chunking(pos_ref)[2]

    @pl.when(pl.program_id(1) == 0)
    def _():
        xe_ref[...] = jnp.zeros_like(xe_ref)
        g_ref[...] = jnp.zeros_like(g_ref)

    def chunk(jj, carry):
        j = pl.program_id(1) * MOE_CHUNKS + jj
        toks = pl.ds(pl.multiple_of(jj * t, t), t)

        def one_pass(k, inner):
            sels, firsts = _slot_windows(cnt_ref, pos_ref, b, j, k, cap)
            rows = jnp.dot(_stacked_one_hot(sels), h2_ref[0, toks, :], preferred_element_type=F32).astype(BF16)
            for e, (sel, first) in enumerate(zip(sels, firsts)):
                window = pl.ds(first, MOE_WINDOW)
                xe_ref[0, window, :] += rows[e * MOE_WINDOW:(e + 1) * MOE_WINDOW, :]
                weight = jnp.sum(jnp.where(sel, _chunk_row(aff_ref, e, j), 0.0), axis=-1, keepdims=True)
                g_ref[0, window, :] += jnp.broadcast_to(weight, (MOE_WINDOW, g_ref.shape[-1]))
            return inner

        one_pass(0, 0)
        lax.fori_loop(1, _window_passes(cnt_ref, pos_ref, b, j, cap), one_pass, 0)
        return carry

    lax.fori_loop(0, MOE_CHUNKS, chunk, 0)


def _dispatch(cnt, pos, aff, h2, cap):
    bsz, s, d = h2.shape
    n_e, n_j, t = _chunking(pos)
    per_token = pl.BlockSpec((1, n_e, s), lambda b, j, c: (b, 0, 0))
    grid_spec = pltpu.PrefetchScalarGridSpec(
        num_scalar_prefetch=1,
        grid=(bsz, n_j // MOE_CHUNKS),
        in_specs=[per_token, per_token,
                  pl.BlockSpec((1, MOE_CHUNKS * t, d), lambda b, j, c: (b, j, 0))],
        out_specs=[pl.BlockSpec((1, n_e * cap, d), lambda b, j, c: (b, 0, 0)),
                   pl.BlockSpec((1, n_e * cap, LANES), lambda b, j, c: (b, 0, 0))],
    )
    return pl.pallas_call(
        functools.partial(_dispatch_kernel, cap=cap),
        grid_spec=grid_spec,
        out_shape=[jax.ShapeDtypeStruct((bsz, n_e * cap, d), BF16),
                   jax.ShapeDtypeStruct((bsz, n_e * cap, LANES), F32)],
        compiler_params=_params(("parallel", "arbitrary"), VMEM_LIMIT),
        name="expert_dispatch",
    )(cnt.reshape(-1), pos, aff, h2)


def _moe_kernel(xe_ref, g_ref, w1_ref, w3_ref, w2_ref, gate_ref, y_ref):
    xe = xe_ref[0]
    a = jnp.dot(xe, w1_ref[0], preferred_element_type=F32)
    gate = jnp.dot(xe, w3_ref[0], preferred_element_type=F32)
    hmid = ((a * _sigmoid(a)) * gate).astype(BF16)
    y = jnp.dot(hmid, w2_ref[0], preferred_element_type=F32) * g_ref[0][:, :1] * gate_ref[0]
    y_ref[0] = y.astype(BF16)


def _moe(xe, g, w1, w3, w2, gate2, cap):
    bsz, _, d = xe.shape
    n_e, _, f = w1.shape
    wspec = lambda e, b: (e, 0, 0)
    slot_rows = lambda e, b: (b, e, 0)
    return pl.pallas_call(
        _moe_kernel,
        grid=(n_e, bsz),
        in_specs=[pl.BlockSpec((1, cap, d), slot_rows),
                  pl.BlockSpec((1, cap, g.shape[-1]), slot_rows),
                  pl.BlockSpec((1, d, f), wspec),
                  pl.BlockSpec((1, d, f), wspec),
                  pl.BlockSpec((1, f, d), wspec),
                  pl.BlockSpec((1, 1, d), lambda e, b: (b, 0, 0))],
        out_specs=pl.BlockSpec((1, cap, d), slot_rows),
        out_shape=jax.ShapeDtypeStruct((bsz, n_e * cap, d), BF16),
        compiler_params=_params(("parallel", "parallel"), VMEM_LIMIT),
        name="expert_choice_ffn",
    )(xe, g, w1, w3, w2, gate2)


def _combine_kernel(cnt_ref, pos_ref, y_hbm, x1_ref, out_ref, y_buf, y_sem, *, cap):
    b = pl.program_id(0)
    step = pl.program_id(1)
    t = _chunking(pos_ref)[2]
    n_parts = y_sem.shape[1]
    part_rows = y_buf.shape[1] // n_parts
    y_ref = y_buf.at[b % 2]

    def part(seq, q):
        rows = pl.ds(q * part_rows, part_rows)
        return pltpu.make_async_copy(y_hbm.at[seq, rows, :], y_buf.at[seq % 2, rows, :], y_sem.at[seq % 2, q])

    @pl.when(jnp.logical_and(b == 0, step == 0))
    def _():
        for q in range(n_parts):
            part(b, q).start()

    @pl.when(b + 1 < pl.num_programs(0))
    def _():
        part(b + 1, step).start()

    @pl.when(step == 0)
    def _():
        for q in range(n_parts):
            part(b, q).wait()

    def chunk(jj, carry):
        j = pl.program_id(1) * MOE_CHUNKS + jj
        toks = pl.ds(pl.multiple_of(jj * t, t), t)

        def windows(k):
            sels, firsts = _slot_windows(cnt_ref, pos_ref, b, j, k, cap)
            rows = jnp.concatenate([y_ref[pl.ds(first, MOE_WINDOW), :] for first in firsts], axis=0)
            return lax.dot_general(_stacked_one_hot(sels), rows, TN_DIMS, preferred_element_type=F32)

        out_ref[0, toks, :] = x1_ref[0, toks, :] + windows(0)

        def more(k, inner):
            out_ref[0, toks, :] += windows(k)
            return inner

        lax.fori_loop(1, _window_passes(cnt_ref, pos_ref, b, j, cap), more, 0)
        return carry

    lax.fori_loop(0, MOE_CHUNKS, chunk, 0)


def _combine(cnt, pos, y, x1, cap):
    bsz, s, d = x1.shape
    n_e, n_j, t = _chunking(pos)
    n_steps = n_j // MOE_CHUNKS
    grid_spec = pltpu.PrefetchScalarGridSpec(
        num_scalar_prefetch=1,
        grid=(bsz, n_steps),
        in_specs=[pl.BlockSpec((1, n_e, s), lambda b, j, c: (b, 0, 0)),
                  pl.BlockSpec(memory_space=pl.ANY),
                  pl.BlockSpec((1, MOE_CHUNKS * t, d), lambda b, j, c: (b, j, 0))],
        out_specs=pl.BlockSpec((1, MOE_CHUNKS * t, d), lambda b, j, c: (b, j, 0)),
        scratch_shapes=[pltpu.VMEM((2, n_e * cap, d), BF16), pltpu.SemaphoreType.DMA((2, n_steps))],
    )
    return pl.pallas_call(
        functools.partial(_combine_kernel, cap=cap),
        grid_spec=grid_spec,
        out_shape=jax.ShapeDtypeStruct((bsz, s, d), F32),
        compiler_params=_params(("arbitrary", "arbitrary"), VMEM_LIMIT),
        name="expert_combine",
    )(cnt.reshape(-1), pos, y, x1)


def kernel(x, c, w_mod, b_mod, g_norm1, w_in, conv_w, conv_b, lru_w_a, lru_b_a, lru_w_x, lru_b_x,
           lru_lambda, g_q, g_k, lambda_qk, g_attn_out, rel_bias, w_out, g_norm2, w_router, w1, w3, w2):
    bsz, s, d = x.shape
    depth = w_mod.shape[0]
    cap = max(1, EC_FACTOR * s // N_EXPERTS)
    bias_tiles = _bias_tiles(rel_bias, min(ATT_TILE, s))
    for l in range(depth):
        mod = _modulation(c, w_mod[l], b_mod[l])
        shift1, scale1, gate1, shift2, scale2, gate2 = [m.reshape(bsz, 1, d) for m in jnp.split(mod, 6, axis=-1)]
        x_lru, gz, qn, kn, vt = _in_projection(x, scale1, shift1, g_norm1[l], w_in[l], g_q[l], g_k[l],
                                               min(IN_PROJ_ROWS, s))
        y_lru = _rg_lru(x_lru, gz, conv_w[l], conv_b[l], lru_w_a[l], lru_b_a[l], lru_w_x[l], lru_b_x[l],
                        lru_lambda[l])
        lam_init = 0.8 - 0.6 * math.exp(-0.3 * l)
        y_att, (w1b, w3b, w2b) = _diff_attention(qn, kn, vt, bias_tiles, lambda_qk[l], g_attn_out[l], lam_init,
                                                 (w1[l], w3[l], w2[l]))
        x1, h2, aff = _out_projection(y_lru, y_att, x, gate1, scale2, shift2, g_norm2[l], w_out[l],
                                      w_router[l], min(OUT_PROJ_ROWS, s))
        pos, cnt = _routing(aff, cap)
        cnt = cnt[:, :, :N_TOK_CHUNKS]
        xe, g = _dispatch(cnt, pos, aff, h2, cap)
        y = _moe(xe, g, w1b, w3b, w2b, gate2, cap)
        x = _combine(cnt, pos, y, x1, cap)
    return x
```
